```python
import math
import jax
import jax.numpy as jnp
from jax import lax
import numpy as np

D_MODEL = 1024
BATCH = 8
SEQ = 8192
DEPTH = 2

CTX_LEN = 256
GRID_W = 64
SSD_HEAD_DIM = 64
SSD_HEADS = D_MODEL // SSD_HEAD_DIM
D_SSD = SSD_HEADS * SSD_HEAD_DIM
SSD_GROUPS = 2
SSD_HPG = SSD_HEADS // SSD_GROUPS
SSD_STATE = 128
SSD_CONV = 5
SSD_CHUNK = 128
XBC_DIM = D_SSD + 2 * SSD_GROUPS * SSD_STATE
NA_HEAD_DIM = 64
NA_HEADS = D_MODEL // NA_HEAD_DIM
D_NA = NA_HEADS * NA_HEAD_DIM
NA_ROWS_MAX = 8
NA_COLS = 16
ROPE_BASE = 10000.0
D_FF = 256 * ((8 * D_MODEL // 3 + 255) // 256)
N_EXPERTS = 8
TOP_K = 2
D_FF_EXPERT = 7 * D_MODEL // 2
MOE_BLOCK = 256
EPS = 1e-6
NEG_INF = -1e30
IN_SPLITS = (D_MODEL, 2 * D_MODEL, 2 * D_MODEL + D_SSD, 2 * D_MODEL + D_SSD + XBC_DIM,
             2 * D_MODEL + D_SSD + XBC_DIM + 2 * SSD_HEADS,
             2 * D_MODEL + D_SSD + XBC_DIM + 2 * SSD_HEADS + D_NA,
             2 * D_MODEL + D_SSD + XBC_DIM + 2 * SSD_HEADS + 2 * D_NA)
IN_DIM = IN_SPLITS[-1] + D_NA
F32 = jnp.float32

kernel_name = 'hybrid_ssd_natten_moe_dit'


def rms_norm(x, w):
    xf = x.astype(F32)
    y = xf * lax.rsqrt(jnp.mean(xf * xf, axis=-1, keepdims=True) + EPS)
    return (y * w.astype(F32)).astype(x.dtype)


def flip(a):
    return jnp.flip(a, axis=1)


def depthwise_conv(x, w, b):
    y = lax.conv_general_dilated(x, w[:, None, :].astype(x.dtype), window_strides=(1,),
                                 padding=[(SSD_CONV // 2, SSD_CONV // 2)],
                                 dimension_numbers=('NWC', 'WIO', 'NWC'),
                                 feature_group_count=x.shape[-1])
    return y + b.astype(x.dtype)


def ssd_inputs(xbc_raw, dt_raw, conv_w, conv_b, dt_bias):
    bsz, seqlen, _ = xbc_raw.shape
    xbc = jax.nn.silu(depthwise_conv(xbc_raw, conv_w, conv_b))
    xs, bs, cs = jnp.split(xbc, (D_SSD, D_SSD + SSD_GROUPS * SSD_STATE), axis=-1)
    xs = xs.reshape(bsz, seqlen, SSD_GROUPS, SSD_HPG, SSD_HEAD_DIM)
    bs = bs.reshape(bsz, seqlen, SSD_GROUPS, SSD_STATE)
    cs = cs.reshape(bsz, seqlen, SSD_GROUPS, SSD_STATE)
    dt = jax.nn.softplus(dt_raw.astype(F32) + dt_bias.reshape(-1).astype(F32))
    dt = dt.reshape(bsz, seqlen, 2, SSD_GROUPS, SSD_HPG)
    return xs, bs, cs, dt[:, :, 0], dt[:, :, 1]


def ssd_chunked(x, dt, a, b_in, c_in, h0, return_y=True):
    bsz, seqlen, g, r, p = x.shape
    n = b_in.shape[-1]
    nc = seqlen // SSD_CHUNK
    dtf = dt.astype(F32)
    xdt = (x.astype(F32) * dtf[..., None]).reshape(bsz, nc, SSD_CHUNK, g, r, p)
    log_a = (dtf * a.astype(F32)).reshape(bsz, nc, SSD_CHUNK, g, r)
    bc = b_in.astype(F32).reshape(bsz, nc, SSD_CHUNK, g, n)
    cc = c_in.astype(F32).reshape(bsz, nc, SSD_CHUNK, g, n)
    a_cs = jnp.cumsum(log_a, axis=2)
    a_last = a_cs[:, :, -1]
    states = jnp.einsum('bcsgn,bcsgr,bcsgrp->bcgrpn', bc, jnp.exp(a_last[:, :, None] - a_cs), xdt)

    def step(h, inp):
        s_c, d_c = inp
        return h * d_c[..., None, None] + s_c, h

    h_fin, h_in = lax.scan(step, h0.astype(F32),
                           (jnp.moveaxis(states, 1, 0), jnp.moveaxis(jnp.exp(a_last), 1, 0)))
    if not return_y:
        return None, h_fin
    h_in = jnp.moveaxis(h_in, 0, 1)
    tri = jnp.tril(jnp.ones((SSD_CHUNK, SSD_CHUNK), bool))
    seg = a_cs[:, :, :, None] - a_cs[:, :, None, :]
    decay = jnp.exp(jnp.where(tri[:, :, None, None], seg, -jnp.inf))
    cb = jnp.einsum('bcqgn,bcsgn->bcqsg', cc, bc)
    y_diag = jnp.einsum('bcqsgr,bcsgrp->bcqgrp', cb[..., None] * decay, xdt)
    y_off = jnp.einsum('bcqgn,bcgrpn->bcqgrp', cc, h_in) * jnp.exp(a_cs)[..., None]
    y = (y_diag + y_off).reshape(bsz, seqlen, g, r, p)
    return y.astype(x.dtype), h_fin


def gated_group_rmsnorm(y, z, w):
    shp = y.shape
    u = y.astype(F32) * jax.nn.silu(z.astype(F32))
    u = u.reshape(*shp[:-1], SSD_GROUPS, shp[-1] // SSD_GROUPS)
    u = u * lax.rsqrt(jnp.mean(u * u, axis=-1, keepdims=True) + EPS)
    return (u.reshape(shp) * w.astype(F32)).astype(y.dtype)


def axial_rope(x, pos_r, pos_c):
    hd = x.shape[-1]
    quarter = hd // 4
    inv = ROPE_BASE ** (-jnp.arange(quarter, dtype=F32) / quarter)

    def rot(xh, pos):
        ang = pos[:, None] * inv[None, :]
        cos = jnp.cos(ang)[None, :, None, :].astype(x.dtype)
        sin = jnp.sin(ang)[None, :, None, :].astype(x.dtype)
        x1, x2 = jnp.split(xh, 2, axis=-1)
        return jnp.concatenate([x1 * cos - x2 * sin, x1 * sin + x2 * cos], axis=-1)

    return jnp.concatenate([rot(x[..., :hd // 2], pos_r), rot(x[..., hd // 2:], pos_c)], axis=-1)


def na_latent(q, k, v, kc, vc, rpb):
    bsz, seqlen, h, hd = q.shape
    rows = seqlen // GRID_W
    kr = min(NA_ROWS_MAX, rows)
    scale = hd ** -0.5
    qg = q.reshape(bsz, rows, GRID_W, h, hd)
    kg = k.reshape(bsz, rows, GRID_W, h, hd)
    vg = v.reshape(bsz, rows, GRID_W, h, hd)
    col = jnp.arange(GRID_W)
    col_start = jnp.clip(col - NA_COLS // 2, 0, GRID_W - NA_COLS)
    col_ok = (col[None, :] >= col_start[:, None]) & (col[None, :] < col_start[:, None] + NA_COLS)
    col_idx = jnp.clip(col[None, :] - col[:, None], -(NA_COLS - 1), NA_COLS - 1) + NA_COLS - 1
    rpb_c = rpb.astype(F32)[:, :, col_idx]

    def one_row(args):
        r, q_r = args
        r0 = jnp.clip(r - kr // 2, 0, rows - kr)
        k_band = lax.dynamic_slice_in_dim(kg, r0, kr, axis=1)
        v_band = lax.dynamic_slice_in_dim(vg, r0, kr, axis=1)
        dr_idx = r0 + jnp.arange(kr) - r + NA_ROWS_MAX - 1
        bias = jnp.take(rpb_c, dr_idx, axis=1)
        bias = jnp.where(col_ok[None, None], bias, NEG_INF).transpose(0, 2, 1, 3)
        s_win = jnp.einsum('bqhd,biwhd->bhqiw', q_r, k_band).astype(F32) * scale + bias[None]
        s_ctx = jnp.einsum('bqhd,bchd->bhqc', q_r, kc).astype(F32) * scale
        s = jnp.concatenate([s_win.reshape(bsz, h, GRID_W, kr * GRID_W), s_ctx], axis=-1)
        pr = jax.nn.softmax(s, axis=-1).astype(v.dtype)
        p_win = pr[..., :kr * GRID_W].reshape(bsz, h, GRID_W, kr, GRID_W)
        p_ctx = pr[..., kr * GRID_W:]
        return (jnp.einsum('bhqiw,biwhd->bqhd', p_win, v_band)
                + jnp.einsum('bhqc,bchd->bqhd', p_ctx, vc))

    out = lax.map(one_row, (jnp.arange(rows), jnp.moveaxis(qg, 1, 0)))
    return jnp.moveaxis(out, 0, 1).reshape(bsz, seqlen, h * hd)


def ctx_attention(q, k, v):
    bsz, clen, h, hd = q.shape
    s = jnp.einsum('bqhd,bkhd->bhqk', q, k).astype(F32) * hd ** -0.5
    pr = jax.nn.softmax(s, axis=-1).astype(v.dtype)
    return jnp.einsum('bhqk,bkhd->bqhd', pr, v).reshape(bsz, clen, h * hd)


def merge_branches(g_ssd, g_na, y_ssd, y_na, w_br_ssd, w_br_na, w_out):
    mixed = jax.nn.sigmoid(g_ssd) * (y_ssd @ w_br_ssd) + jax.nn.sigmoid(g_na) * (y_na @ w_br_na)
    return mixed @ w_out


def token_mixer(a_lat, a_ctx, pos_r, pos_c, w_in, conv_w, conv_b, dt_bias, a_log, d_skip,
                ssd_norm_w, q_norm_w, k_norm_w, rpb, w_br_ssd, w_br_na, w_out, need_ctx):
    bsz, seqlen, _ = a_lat.shape
    clen = a_ctx.shape[1]
    g_ssd_l, g_na_l, z_l, xbc_l, dt_l, q_l, k_l, v_l = jnp.split(a_lat @ w_in, IN_SPLITS, axis=-1)
    g_ssd_c, g_na_c, z_c, xbc_c, dt_c, q_c, k_c, v_c = jnp.split(a_ctx @ w_in, IN_SPLITS, axis=-1)

    a_dir = -jnp.exp(a_log.astype(F32)).reshape(2, SSD_GROUPS, SSD_HPG)
    d_sum = (d_skip[0] + d_skip[1]).reshape(SSD_GROUPS, SSD_HPG)[..., None]
    xs_l, b_l, c_l, dtf_l, dtb_l = ssd_inputs(xbc_l, dt_l, conv_w, conv_b, dt_bias)
    xs_c, b_c, c_c, dtf_c, dtb_c = ssd_inputs(xbc_c, dt_c, conv_w, conv_b, dt_bias)
    h0 = jnp.zeros((bsz, SSD_GROUPS, SSD_HPG, SSD_HEAD_DIM, SSD_STATE), F32)
    yf_c, sf_c = ssd_chunked(xs_c, dtf_c, a_dir[0], b_c, c_c, h0, need_ctx)
    yb_c, sb_c = ssd_chunked(flip(xs_c), flip(dtb_c), a_dir[1], flip(b_c), flip(c_c), h0, need_ctx)
    yf_l, _ = ssd_chunked(xs_l, dtf_l, a_dir[0], b_l, c_l, sf_c)
    yb_l, _ = ssd_chunked(flip(xs_l), flip(dtb_l), a_dir[1], flip(b_l), flip(c_l), sb_c)
    y_ssd_l = (yf_l + flip(yb_l) + xs_l * d_sum.astype(xs_l.dtype)).reshape(bsz, seqlen, D_SSD)
    y_ssd_l = gated_group_rmsnorm(y_ssd_l, z_l, ssd_norm_w)

    k_c = rms_norm(k_c.reshape(bsz, clen, NA_HEADS, NA_HEAD_DIM), k_norm_w)
    v_c = v_c.reshape(bsz, clen, NA_HEADS, NA_HEAD_DIM)
    q_l = axial_rope(rms_norm(q_l.reshape(bsz, seqlen, NA_HEADS, NA_HEAD_DIM), q_norm_w), pos_r, pos_c)
    k_l = axial_rope(rms_norm(k_l.reshape(bsz, seqlen, NA_HEADS, NA_HEAD_DIM), k_norm_w), pos_r, pos_c)
    v_l = v_l.reshape(bsz, seqlen, NA_HEADS, NA_HEAD_DIM)
    y_na_l = na_latent(q_l, k_l, v_l, k_c, v_c, rpb)

    out_l = merge_branches(g_ssd_l, g_na_l, y_ssd_l, y_na_l, w_br_ssd, w_br_na, w_out)
    if not need_ctx:
        return out_l, None
    y_ssd_c = (yf_c + flip(yb_c) + xs_c * d_sum.astype(xs_c.dtype)).reshape(bsz, clen, D_SSD)
    y_ssd_c = gated_group_rmsnorm(y_ssd_c, z_c, ssd_norm_w)
    q_c = rms_norm(q_c.reshape(bsz, clen, NA_HEADS, NA_HEAD_DIM), q_norm_w)
    y_na_c = ctx_attention(q_c, k_c, v_c)
    out_c = merge_branches(g_ssd_c, g_na_c, y_ssd_c, y_na_c, w_br_ssd, w_br_na, w_out)
    return out_l, out_c


def swiglu(h, w1, w3, w2):
    return (jax.nn.silu(h @ w1) * (h @ w3)) @ w2


def moe_swiglu(h, w_router, w1, w3, w2):
    n_tok, d = h.shape
    n_assign = n_tok * TOP_K
    logits = (h @ w_router).astype(F32)
    top_v, top_i = lax.top_k(logits, TOP_K)
    gates = jax.nn.softmax(top_v, axis=-1)
    flat_e = top_i.reshape(-1)
    order = jnp.argsort(flat_e)
    sorted_e = flat_e[order]
    tok = order // TOP_K
    counts = jnp.bincount(flat_e, length=N_EXPERTS)
    padded = (counts + MOE_BLOCK - 1) // MOE_BLOCK * MOE_BLOCK
    pad_end = jnp.cumsum(padded)
    pad_start = pad_end - padded
    start = jnp.cumsum(counts) - counts
    dest = pad_start[sorted_e] + jnp.arange(n_assign) - start[sorted_e]
    n_blocks = -(-n_assign // MOE_BLOCK) + N_EXPERTS
    rows = jnp.zeros((n_blocks * MOE_BLOCK, d), h.dtype).at[dest].set(h[tok])
    block_e = jnp.minimum(jnp.searchsorted(pad_end, jnp.arange(n_blocks) * MOE_BLOCK, side='right'),
                          N_EXPERTS - 1)

    def run_block(args):
        xb, e = args
        return swiglu(xb, w1[e], w3[e], w2[e])

    y_rows = lax.map(run_block, (rows.reshape(n_blocks, MOE_BLOCK, d), block_e)).reshape(-1, d)
    contrib = y_rows[dest] * gates.reshape(-1)[order][:, None].astype(h.dtype)
    return jnp.zeros_like(h).at[tok].add(contrib)


def setup_inputs(seed: int = 0) -> dict:
    key = jax.random.key(seed)
    ks = iter(jax.random.split(key, 40))

    def nrm(shape, scale):
        return jax.random.normal(next(ks), shape, F32) * scale

    L = DEPTH
    le = (DEPTH + 1) // 2
    lo = DEPTH // 2
    D = D_MODEL
    dt0 = jnp.exp(jax.random.uniform(next(ks), (L, 2, SSD_HEADS), F32, math.log(1e-3), math.log(1e-1)))
    dt_bias = dt0 + jnp.log(-jnp.expm1(-dt0))
    a_log = jnp.log(jax.random.uniform(next(ks), (L, 2, SSD_HEADS), F32, 1.0, 16.0))
    return {
        'x': nrm((BATCH, SEQ, D), 1.0),
        'c': nrm((BATCH, D), 1.0),
        'ctx': nrm((BATCH, CTX_LEN, D), 1.0),
        'c_ctx': nrm((D,), 1.0),
        'w_mod': nrm((L, D, 6 * D), 0.5 * D ** -0.5),
        'b_mod': nrm((L, 6 * D), 0.02),
        'norm1_w': 1.0 + nrm((L, D), 0.02),
        'norm2_w': 1.0 + nrm((L, D), 0.02),
        'w_in': nrm((L, D, IN_DIM), D ** -0.5),
        'conv_w': nrm((L, SSD_CONV, XBC_DIM), SSD_CONV ** -0.5),
        'conv_b': nrm((L, XBC_DIM), 0.02),
        'dt_bias': dt_bias,
        'a_log': a_log,
        'd_skip': 1.0 + nrm((L, 2, SSD_HEADS), 0.02),
        'ssd_norm_w': 1.0 + nrm((L, D_SSD), 0.02),
        'q_norm_w': 1.0 + nrm((L, NA_HEAD_DIM), 0.02),
        'k_norm_w': 1.0 + nrm((L, NA_HEAD_DIM), 0.02),
        'rpb': nrm((L, NA_HEADS, 2 * NA_ROWS_MAX - 1, 2 * NA_COLS - 1), 0.1),
        'w_br_ssd': nrm((L, D_SSD, D), D_SSD ** -0.5),
        'w_br_na': nrm((L, D_NA, D), D_NA ** -0.5),
        'w_out': nrm((L, D, D), D ** -0.5),
        'w_ff1': nrm((le, D, D_FF), D ** -0.5),
        'w_ff3': nrm((le, D, D_FF), D ** -0.5),
        'w_ff2': nrm((le, D_FF, D), D_FF ** -0.5),
        'w_router': nrm((lo, D, N_EXPERTS), D ** -0.5),
        'w_e1': nrm((lo, N_EXPERTS, D, D_FF_EXPERT), D ** -0.5),
        'w_e3': nrm((lo, N_EXPERTS, D, D_FF_EXPERT), D ** -0.5),
        'w_e2': nrm((lo, N_EXPERTS, D_FF_EXPERT, D), D_FF_EXPERT ** -0.5),
    }


def reference(x, c, ctx, c_ctx, w_mod, b_mod, norm1_w, norm2_w, w_in, conv_w, conv_b, dt_bias,
              a_log, d_skip, ssd_norm_w, q_norm_w, k_norm_w, rpb, w_br_ssd, w_br_na, w_out,
              w_ff1, w_ff3, w_ff2, w_router, w_e1, w_e3, w_e2):
    bsz, seqlen, d = x.shape
    t = jnp.arange(seqlen)
    pos_r = (t // GRID_W).astype(F32)
    pos_c = (t % GRID_W).astype(F32)
    cond_lat = jax.nn.silu(c)
    cond_ctx = jax.nn.silu(c_ctx)[None]
    h_lat, h_ctx = x, ctx
    n_ctx_tok = ctx.shape[0] * ctx.shape[1]
    for i in range(DEPTH):
        last = i == DEPTH - 1
        sh1, sc1, g1, sh2, sc2, g2 = jnp.split((cond_lat @ w_mod[i] + b_mod[i])[:, None, :], 6, axis=-1)
        csh1, csc1, cg1, csh2, csc2, cg2 = jnp.split((cond_ctx @ w_mod[i] + b_mod[i])[:, None, :], 6, axis=-1)
        a_lat = rms_norm(h_lat, norm1_w[i]) * (1.0 + sc1) + sh1
        a_ctx = rms_norm(h_ctx, norm1_w[i]) * (1.0 + csc1) + csh1
        m_lat, m_ctx = token_mixer(a_lat, a_ctx, pos_r, pos_c, w_in[i], conv_w[i], conv_b[i], dt_bias[i],
                                   a_log[i], d_skip[i], ssd_norm_w[i], q_norm_w[i], k_norm_w[i], rpb[i],
                                   w_br_ssd[i], w_br_na[i], w_out[i], not last)
        h_lat = h_lat + g1 * m_lat
        f_lat = rms_norm(h_lat, norm2_w[i]) * (1.0 + sc2) + sh2
        if last:
            tokens = f_lat.reshape(-1, d)
        else:
            h_ctx = h_ctx + cg1 * m_ctx
            f_ctx = rms_norm(h_ctx, norm2_w[i]) * (1.0 + csc2) + csh2
            tokens = jnp.concatenate([f_ctx.reshape(-1, d), f_lat.reshape(-1, d)], axis=0)
        fi = i // 2
        if i % 2 == 0:
            f_out = swiglu(tokens, w_ff1[fi], w_ff3[fi], w_ff2[fi])
        else:
            f_out = moe_swiglu(tokens, w_router[fi], w_e1[fi], w_e3[fi], w_e2[fi])
        if last:
            h_lat = h_lat + g2 * f_out.reshape(bsz, seqlen, d)
        else:
            h_ctx = h_ctx + cg2 * f_out[:n_ctx_tok].reshape(h_ctx.shape)
            h_lat = h_lat + g2 * f_out[n_ctx_tok:].reshape(bsz, seqlen, d)
    return h_lat
```

```python
import functools
import math

import jax
import jax.numpy as jnp
from jax import lax
from jax.experimental import pallas as pl
from jax.experimental.pallas import tpu as pltpu

D_MODEL = 1024
GRID_W = 64
SSD_HEAD_DIM = 64
SSD_HEADS = D_MODEL // SSD_HEAD_DIM
D_SSD = SSD_HEADS * SSD_HEAD_DIM
SSD_GROUPS = 2
SSD_HPG = SSD_HEADS // SSD_GROUPS
SSD_STATE = 128
SSD_CONV = 5
SSD_CHUNK = 128
XBC_DIM = D_SSD + 2 * SSD_GROUPS * SSD_STATE
NA_HEAD_DIM = 64
NA_HEADS = D_MODEL // NA_HEAD_DIM
D_NA = NA_HEADS * NA_HEAD_DIM
NA_ROWS_MAX = 8
NA_COLS = 16
ROPE_BASE = 10000.0
N_EXPERTS = 8
TOP_K = 2
EPS = 1e-6
NEG_INF = -1e30
IN_SPLITS = (D_MODEL, 2 * D_MODEL, 2 * D_MODEL + D_SSD, 2 * D_MODEL + D_SSD + XBC_DIM,
             2 * D_MODEL + D_SSD + XBC_DIM + 2 * SSD_HEADS,
             2 * D_MODEL + D_SSD + XBC_DIM + 2 * SSD_HEADS + D_NA,
             2 * D_MODEL + D_SSD + XBC_DIM + 2 * SSD_HEADS + 2 * D_NA)

F32 = jnp.float32
BF16 = jnp.bfloat16
HIGHEST = lax.Precision.HIGHEST

MOD_TILE = 256
LANE = 128
VMEM_LIMIT = 48 * 1024 * 1024

COL_GS, COL_GN, COL_Z, COL_XBC, COL_Q, COL_K, COL_V = 0, 1024, 2048, 3072, 4608, 5632, 6656
MAIN_DIM = 7680
DT_PAD = LANE


def _params(sem):
    return pltpu.CompilerParams(dimension_semantics=sem, vmem_limit_bytes=VMEM_LIMIT)


def _rms_mod(x, nw, shift, scale):
    ms = jnp.mean(x * x, axis=-1, keepdims=True)
    return (x * lax.rsqrt(ms + EPS) * nw) * (1.0 + scale) + shift


def _mod_kernel(cond_ref, w_ref, b_ref, o_ref):
    c = cond_ref[...]
    s = c * jax.nn.sigmoid(c)
    o_ref[...] = jnp.dot(s, w_ref[...], precision=HIGHEST, preferred_element_type=F32) + b_ref[...]


def _modulation(cond, w_mod, b_mod):
    n_layers, d, _ = w_mod.shape
    rows = cond.shape[0]
    b4 = b_mod.reshape(n_layers, 6, 1, d)
    return pl.pallas_call(
        _mod_kernel,
        grid=(n_layers, 6),
        in_specs=[pl.BlockSpec((rows, d), lambda l, k: (0, 0)),
                  pl.BlockSpec((None, d, d), lambda l, k: (l, 0, k)),
                  pl.BlockSpec((None, None, 1, d), lambda l, k: (l, k, 0, 0))],
        out_specs=pl.BlockSpec((None, None, rows, d), lambda l, k: (l, k, 0, 0)),
        out_shape=jax.ShapeDtypeStruct((n_layers, 6, rows, d), F32),
        compiler_params=_params(("arbitrary", "arbitrary")),
        name="modulation",
    )(cond, w_mod, b4)


def _adaln_kernel(h_ref, nw_ref, modt_ref, o_ref, *, nsub):
    nw = nw_ref[...]
    for s in range(nsub):
        rows = pl.ds(s * MOD_TILE, MOD_TILE)
        x = h_ref[rows, :]
        o_ref[rows, :] = _rms_mod(x, nw, modt_ref[s, 0:1, :], modt_ref[s, 1:2, :]).astype(o_ref.dtype)


def _adaln(h, nw, modt, tm=1024):
    m, d = h.shape
    nsub = tm // MOD_TILE
    return pl.pallas_call(
        functools.partial(_adaln_kernel, nsub=nsub),
        grid=(m // tm,),
        in_specs=[pl.BlockSpec((tm, d), lambda i: (i, 0)),
                  pl.BlockSpec((1, d), lambda i: (0, 0)),
                  pl.BlockSpec((nsub, 6, d), lambda i: (i, 0, 0))],
        out_specs=pl.BlockSpec((tm, d), lambda i: (i, 0)),
        out_shape=jax.ShapeDtypeStruct((m, d), BF16),
        compiler_params=_params(("arbitrary",)),
        name="adaln",
    )(h, nw.reshape(1, d), modt)


def _matmul_kernel(a_ref, w_ref, o_ref):
    o_ref[...] = jnp.dot(a_ref[...], w_ref[...], preferred_element_type=F32).astype(o_ref.dtype)


def _matmul(a, w, out_dtype, tm, tn, name):
    m, k = a.shape
    n = w.shape[1]
    return pl.pallas_call(
        _matmul_kernel,
        grid=(n // tn, m // tm),
        in_specs=[pl.BlockSpec((tm, k), lambda j, i: (i, 0)),
                  pl.BlockSpec((k, tn), lambda j, i: (0, j))],
        out_specs=pl.BlockSpec((tm, tn), lambda j, i: (i, j)),
        out_shape=jax.ShapeDtypeStruct((m, n), out_dtype),
        compiler_params=_params(("arbitrary", "arbitrary")),
        name=name,
    )(a, w)


def _merge_kernel(gs_ref, gn_ref, ys_ref, yn_ref, h_ref, wbs_ref, wbn_ref, wo_ref, nw_ref, modt_ref,
                  ho_ref, f_ref, *, nsub):
    a = jnp.dot(ys_ref[...], wbs_ref[...], preferred_element_type=F32)
    b = jnp.dot(yn_ref[...], wbn_ref[...], preferred_element_type=F32)
    mixed = (jax.nn.sigmoid(gs_ref[...].astype(F32)) * a + jax.nn.sigmoid(gn_ref[...].astype(F32)) * b)
    m = jnp.dot(mixed.astype(BF16), wo_ref[...], preferred_element_type=F32)
    nw = nw_ref[...]
    for s in range(nsub):
        r0, r1 = s * MOD_TILE, (s + 1) * MOD_TILE
        h = h_ref[r0:r1, :] + modt_ref[s, 2:3, :] * m[r0:r1, :]
        ho_ref[r0:r1, :] = h
        f_ref[r0:r1, :] = _rms_mod(h, nw, modt_ref[s, 3:4, :], modt_ref[s, 4:5, :]).astype(f_ref.dtype)


def _merge(p_main, y_ssd, y_na, h, wbs, wbn, wo, nw2, modt, tm=512):
    m, d = h.shape
    nsub = tm // MOD_TILE
    row = lambda i: (i, 0)
    const = lambda i: (0, 0)
    return pl.pallas_call(
        functools.partial(_merge_kernel, nsub=nsub),
        grid=(m // tm,),
        in_specs=[pl.BlockSpec((tm, d), lambda i: (i, COL_GS // D_MODEL)),
                  pl.BlockSpec((tm, d), lambda i: (i, COL_GN // D_MODEL)),
                  pl.BlockSpec((tm, d), row), pl.BlockSpec((tm, d), row), pl.BlockSpec((tm, d), row),
                  pl.BlockSpec((d, d), const), pl.BlockSpec((d, d), const), pl.BlockSpec((d, d), const),
                  pl.BlockSpec((1, d), const),
                  pl.BlockSpec((nsub, 6, d), lambda i: (i, 0, 0))],
        out_specs=[pl.BlockSpec((tm, d), row), pl.BlockSpec((tm, d), row)],
        out_shape=[jax.ShapeDtypeStruct((m, d), F32), jax.ShapeDtypeStruct((m, d), BF16)],
        compiler_params=_params(("arbitrary",)),
        name="merge",
    )(p_main, p_main, y_ssd, y_na, h, wbs, wbn, wo, nw2.reshape(1, d), modt)


def _ffn_kernel(x_ref, h_ref, w1_ref, w3_ref, w2_ref, nw_ref, modt_ref, modn_ref, ho_ref, an_ref, acc_ref,
                *, nsub):
    j = pl.program_id(1)

    @pl.when(j == 0)
    def _():
        acc_ref[...] = jnp.zeros_like(acc_ref)

    x = x_ref[...]
    h1 = jnp.dot(x, w1_ref[...], preferred_element_type=F32)
    h3 = jnp.dot(x, w3_ref[...], preferred_element_type=F32)
    act = (h1 * jax.nn.sigmoid(h1) * h3).astype(BF16)
    acc_ref[...] += jnp.dot(act, w2_ref[...], preferred_element_type=F32)

    @pl.when(j == pl.num_programs(1) - 1)
    def _():
        nw = nw_ref[...]
        for s in range(nsub):
            r0, r1 = s * MOD_TILE, (s + 1) * MOD_TILE
            h = h_ref[r0:r1, :] + modt_ref[s, 5:6, :] * acc_ref[r0:r1, :]
            ho_ref[r0:r1, :] = h
            an_ref[r0:r1, :] = _rms_mod(h, nw, modn_ref[s, 0:1, :], modn_ref[s, 1:2, :]).astype(an_ref.dtype)


def _ffn(x, h, w1, w3, w2, nw_next, modt, modt_next, tm=512, tf=1408):
    m, d = h.shape
    ff = w1.shape[1]
    nsub = tm // MOD_TILE
    row = lambda i, j: (i, 0)
    return pl.pallas_call(
        functools.partial(_ffn_kernel, nsub=nsub),
        grid=(m // tm, ff // tf),
        in_specs=[pl.BlockSpec((tm, d), row), pl.BlockSpec((tm, d), row),
                  pl.BlockSpec((d, tf), lambda i, j: (0, j)), pl.BlockSpec((d, tf), lambda i, j: (0, j)),
                  pl.BlockSpec((tf, d), lambda i, j: (j, 0)),
                  pl.BlockSpec((1, d), lambda i, j: (0, 0)),
                  pl.BlockSpec((nsub, 6, d), lambda i, j: (i, 0, 0)),
                  pl.BlockSpec((nsub, 6, d), lambda i, j: (i, 0, 0))],
        out_specs=[pl.BlockSpec((tm, d), row), pl.BlockSpec((tm, d), row)],
        out_shape=[jax.ShapeDtypeStruct((m, d), F32), jax.ShapeDtypeStruct((m, d), BF16)],
        scratch_shapes=[pltpu.VMEM((tm, d), F32)],
        compiler_params=_params(("arbitrary", "arbitrary")),
        name="ffn",
    )(x, h, w1, w3, w2, nw_next.reshape(1, d), modt, modt_next)


def _fb_rms_norm(x, w):
    xf = x.astype(F32)
    y = xf * lax.rsqrt(jnp.mean(xf * xf, axis=-1, keepdims=True) + EPS)
    return (y * w.astype(F32)).astype(x.dtype)


def _fb_flip(a):
    return jnp.flip(a, axis=1)


def _fb_depthwise_conv(x, w, b):
    y = lax.conv_general_dilated(x, w[:, None, :].astype(x.dtype), window_strides=(1,),
                                 padding=[(SSD_CONV // 2, SSD_CONV // 2)],
                                 dimension_numbers=('NWC', 'WIO', 'NWC'),
                                 feature_group_count=x.shape[-1])
    return y + b.astype(x.dtype)


def _fb_ssd_inputs(xbc_raw, dt_raw, conv_w, conv_b, dt_bias):
    bsz, seqlen, _ = xbc_raw.shape
    xbc = jax.nn.silu(_fb_depthwise_conv(xbc_raw, conv_w, conv_b))
    xs, bs, cs = jnp.split(xbc, (D_SSD, D_SSD + SSD_GROUPS * SSD_STATE), axis=-1)
    xs = xs.reshape(bsz, seqlen, SSD_GROUPS, SSD_HPG, SSD_HEAD_DIM)
    bs = bs.reshape(bsz, seqlen, SSD_GROUPS, SSD_STATE)
    cs = cs.reshape(bsz, seqlen, SSD_GROUPS, SSD_STATE)
    dt = jax.nn.softplus(dt_raw.astype(F32) + dt_bias.reshape(-1).astype(F32))
    dt = dt.reshape(bsz, seqlen, 2, SSD_GROUPS, SSD_HPG)
    return xs, bs, cs, dt[:, :, 0], dt[:, :, 1]


def _fb_ssd_chunked(x, dt, a, b_in, c_in, h0, return_y=True):
    bsz, seqlen, g, r, p = x.shape
    n = b_in.shape[-1]
    nc = seqlen // SSD_CHUNK
    dtf = dt.astype(F32)
    xdt = (x.astype(F32) * dtf[..., None]).reshape(bsz, nc, SSD_CHUNK, g, r, p)
    log_a = (dtf * a.astype(F32)).reshape(bsz, nc, SSD_CHUNK, g, r)
    bc = b_in.astype(F32).reshape(bsz, nc, SSD_CHUNK, g, n)
    cc = c_in.astype(F32).reshape(bsz, nc, SSD_CHUNK, g, n)
    a_cs = jnp.cumsum(log_a, axis=2)
    a_last = a_cs[:, :, -1]
    states = jnp.einsum('bcsgn,bcsgr,bcsgrp->bcgrpn', bc, jnp.exp(a_last[:, :, None] - a_cs), xdt)

    def step(h, inp):
        s_c, d_c = inp
        return h * d_c[..., None, None] + s_c, h

    h_fin, h_in = lax.scan(step, h0.astype(F32),
                           (jnp.moveaxis(states, 1, 0), jnp.moveaxis(jnp.exp(a_last), 1, 0)))
    if not return_y:
        return None, h_fin
    h_in = jnp.moveaxis(h_in, 0, 1)
    tri = jnp.tril(jnp.ones((SSD_CHUNK, SSD_CHUNK), bool))
    seg = a_cs[:, :, :, None] - a_cs[:, :, None, :]
    decay = jnp.exp(jnp.where(tri[:, :, None, None], seg, -jnp.inf))
    cb = jnp.einsum('bcqgn,bcsgn->bcqsg', cc, bc)
    y_diag = jnp.einsum('bcqsgr,bcsgrp->bcqgrp', cb[..., None] * decay, xdt)
    y_off = jnp.einsum('bcqgn,bcgrpn->bcqgrp', cc, h_in) * jnp.exp(a_cs)[..., None]
    y = (y_diag + y_off).reshape(bsz, seqlen, g, r, p)
    return y.astype(x.dtype), h_fin


def _fb_gated_group_rmsnorm(y, z, w):
    shp = y.shape
    u = y.astype(F32) * jax.nn.silu(z.astype(F32))
    u = u.reshape(*shp[:-1], SSD_GROUPS, shp[-1] // SSD_GROUPS)
    u = u * lax.rsqrt(jnp.mean(u * u, axis=-1, keepdims=True) + EPS)
    return (u.reshape(shp) * w.astype(F32)).astype(y.dtype)


def _fb_ssd(pm, pdt, conv_w, conv_b, dt_bias, a_log, d_skip, ssd_norm_w, clen, need_ctx):
    pm = pm.astype(F32)
    bsz = pm.shape[0]
    z = pm[:, :, COL_Z:COL_Z + D_SSD]
    xbc = pm[:, :, COL_XBC:COL_XBC + XBC_DIM]
    dt = pdt[:, :, :2 * SSD_HEADS]
    z_c, z_l = z[:, :clen], z[:, clen:]
    a_dir = -jnp.exp(a_log.astype(F32)).reshape(2, SSD_GROUPS, SSD_HPG)
    d_sum = (d_skip[0] + d_skip[1]).reshape(SSD_GROUPS, SSD_HPG)[..., None]
    xs_l, b_l, c_l, dtf_l, dtb_l = _fb_ssd_inputs(xbc[:, clen:], dt[:, clen:], conv_w, conv_b, dt_bias)
    xs_c, b_c, c_c, dtf_c, dtb_c = _fb_ssd_inputs(xbc[:, :clen], dt[:, :clen], conv_w, conv_b, dt_bias)
    h0 = jnp.zeros((bsz, SSD_GROUPS, SSD_HPG, SSD_HEAD_DIM, SSD_STATE), F32)
    yf_c, sf_c = _fb_ssd_chunked(xs_c, dtf_c, a_dir[0], b_c, c_c, h0, need_ctx)
    yb_c, sb_c = _fb_ssd_chunked(_fb_flip(xs_c), _fb_flip(dtb_c), a_dir[1], _fb_flip(b_c), _fb_flip(c_c), h0,
                                 need_ctx)
    yf_l, _ = _fb_ssd_chunked(xs_l, dtf_l, a_dir[0], b_l, c_l, sf_c)
    yb_l, _ = _fb_ssd_chunked(_fb_flip(xs_l), _fb_flip(dtb_l), a_dir[1], _fb_flip(b_l), _fb_flip(c_l), sb_c)
    seqlen = xs_l.shape[1]
    y_l = (yf_l + _fb_flip(yb_l) + xs_l * d_sum.astype(xs_l.dtype)).reshape(bsz, seqlen, D_SSD)
    y_l = _fb_gated_group_rmsnorm(y_l, z_l, ssd_norm_w)
    if need_ctx:
        y_c = (yf_c + _fb_flip(yb_c) + xs_c * d_sum.astype(xs_c.dtype)).reshape(bsz, clen, D_SSD)
        y_c = _fb_gated_group_rmsnorm(y_c, z_c, ssd_norm_w)
    else:
        y_c = jnp.zeros((bsz, clen, D_SSD), F32)
    return jnp.concatenate([y_c, y_l], axis=1)


def _fb_axial_rope(x, pos_r, pos_c):
    hd = x.shape[-1]
    quarter = hd // 4
    inv = ROPE_BASE ** (-jnp.arange(quarter, dtype=F32) / quarter)

    def rot(xh, pos):
        ang = pos[:, None] * inv[None, :]
        cos = jnp.cos(ang)[None, :, None, :].astype(x.dtype)
        sin = jnp.sin(ang)[None, :, None, :].astype(x.dtype)
        x1, x2 = jnp.split(xh, 2, axis=-1)
        return jnp.concatenate([x1 * cos - x2 * sin, x1 * sin + x2 * cos], axis=-1)

    return jnp.concatenate([rot(x[..., :hd // 2], pos_r), rot(x[..., hd // 2:], pos_c)], axis=-1)


def _fb_na_latent(q, k, v, kc, vc, rpb):
    bsz, seqlen, h, hd = q.shape
    rows = seqlen // GRID_W
    kr = min(NA_ROWS_MAX, rows)
    scale = hd ** -0.5
    qg = q.reshape(bsz, rows, GRID_W, h, hd)
    kg = k.reshape(bsz, rows, GRID_W, h, hd)
    vg = v.reshape(bsz, rows, GRID_W, h, hd)
    col = jnp.arange(GRID_W)
    col_start = jnp.clip(col - NA_COLS // 2, 0, GRID_W - NA_COLS)
    col_ok = (col[None, :] >= col_start[:, None]) & (col[None, :] < col_start[:, None] + NA_COLS)
    col_idx = jnp.clip(col[None, :] - col[:, None], -(NA_COLS - 1), NA_COLS - 1) + NA_COLS - 1
    rpb_c = rpb.astype(F32)[:, :, col_idx]

    def one_row(args):
        r, q_r = args
        r0 = jnp.clip(r - kr // 2, 0, rows - kr)
        k_band = lax.dynamic_slice_in_dim(kg, r0, kr, axis=1)
        v_band = lax.dynamic_slice_in_dim(vg, r0, kr, axis=1)
        dr_idx = r0 + jnp.arange(kr) - r + NA_ROWS_MAX - 1
        bias = jnp.take(rpb_c, dr_idx, axis=1)
        bias = jnp.where(col_ok[None, None], bias, NEG_INF).transpose(0, 2, 1, 3)
        s_win = jnp.einsum('bqhd,biwhd->bhqiw', q_r, k_band).astype(F32) * scale + bias[None]
        s_ctx = jnp.einsum('bqhd,bchd->bhqc', q_r, kc).astype(F32) * scale
        s = jnp.concatenate([s_win.reshape(bsz, h, GRID_W, kr * GRID_W), s_ctx], axis=-1)
        pr = jax.nn.softmax(s, axis=-1).astype(v.dtype)
        p_win = pr[..., :kr * GRID_W].reshape(bsz, h, GRID_W, kr, GRID_W)
        p_ctx = pr[..., kr * GRID_W:]
        return (jnp.einsum('bhqiw,biwhd->bqhd', p_win, v_band)
                + jnp.einsum('bhqc,bchd->bqhd', p_ctx, vc))

    out = lax.map(one_row, (jnp.arange(rows), jnp.moveaxis(qg, 1, 0)))
    return jnp.moveaxis(out, 0, 1).reshape(bsz, seqlen, h * hd)


def _fb_ctx_attention(q, k, v):
    bsz, clen, h, hd = q.shape
    s = jnp.einsum('bqhd,bkhd->bhqk', q, k).astype(F32) * hd ** -0.5
    pr = jax.nn.softmax(s, axis=-1).astype(v.dtype)
    return jnp.einsum('bhqk,bkhd->bqhd', pr, v).reshape(bsz, clen, h * hd)


def _fb_na(pm, q_norm_w, k_norm_w, rpb, clen, need_ctx):
    pm = pm.astype(F32)
    bsz, t, _ = pm.shape
    seqlen = t - clen
    q = pm[:, :, COL_Q:COL_Q + D_NA]
    k = pm[:, :, COL_K:COL_K + D_NA]
    v = pm[:, :, COL_V:COL_V + D_NA]
    tt = jnp.arange(seqlen)
    pos_r = (tt // GRID_W).astype(F32)
    pos_c = (tt % GRID_W).astype(F32)
    hs = (NA_HEADS, NA_HEAD_DIM)
    k_c = _fb_rms_norm(k[:, :clen].reshape(bsz, clen, *hs), k_norm_w)
    v_c = v[:, :clen].reshape(bsz, clen, *hs)
    q_l = _fb_axial_rope(_fb_rms_norm(q[:, clen:].reshape(bsz, seqlen, *hs), q_norm_w), pos_r, pos_c)
    k_l = _fb_axial_rope(_fb_rms_norm(k[:, clen:].reshape(bsz, seqlen, *hs), k_norm_w), pos_r, pos_c)
    v_l = v[:, clen:].reshape(bsz, seqlen, *hs)
    y_l = _fb_na_latent(q_l, k_l, v_l, k_c, v_c, rpb)
    if need_ctx:
        q_c = _fb_rms_norm(q[:, :clen].reshape(bsz, clen, *hs), q_norm_w)
        y_c = _fb_ctx_attention(q_c, k_c, v_c)
    else:
        y_c = jnp.zeros((bsz, clen, D_NA), F32)
    return jnp.concatenate([y_c, y_l], axis=1)


def _fb_swiglu(h, w1, w3, w2):
    return (jax.nn.silu(h @ w1) * (h @ w3)) @ w2


def _fb_moe(h, w_router, w1, w3, w2):
    blk = 256
    n_tok, d = h.shape
    n_assign = n_tok * TOP_K
    logits = jnp.dot(h, w_router, precision=HIGHEST).astype(F32)
    top_v, top_i = lax.top_k(logits, TOP_K)
    gates = jax.nn.softmax(top_v, axis=-1)
    flat_e = top_i.reshape(-1)
    order = jnp.argsort(flat_e)
    sorted_e = flat_e[order]
    tok = order // TOP_K
    counts = jnp.bincount(flat_e, length=N_EXPERTS)
    padded = (counts + blk - 1) // blk * blk
    pad_end = jnp.cumsum(padded)
    pad_start = pad_end - padded
    start = jnp.cumsum(counts) - counts
    dest = pad_start[sorted_e] + jnp.arange(n_assign) - start[sorted_e]
    n_blocks = -(-n_assign // blk) + N_EXPERTS
    rows = jnp.zeros((n_blocks * blk, d), h.dtype).at[dest].set(h[tok])
    block_e = jnp.minimum(jnp.searchsorted(pad_end, jnp.arange(n_blocks) * blk, side='right'),
                          N_EXPERTS - 1)

    def run_block(args):
        xb, e = args
        return _fb_swiglu(xb, w1[e], w3[e], w2[e])

    y_rows = lax.map(run_block, (rows.reshape(n_blocks, blk, d), block_e)).reshape(-1, d)
    contrib = y_rows[dest] * gates.reshape(-1)[order][:, None].astype(h.dtype)
    return jnp.zeros_like(h).at[tok].add(contrib)


def _tile_mod(mod_l, bsz, tiles_per_batch):
    t = jnp.arange(bsz * tiles_per_batch)
    idx = jnp.where(t % tiles_per_batch == 0, bsz, t // tiles_per_batch)
    return jnp.transpose(mod_l[:, idx, :], (1, 0, 2))


def kernel(x, c, ctx, c_ctx, w_mod, b_mod, norm1_w, norm2_w, w_in, conv_w, conv_b, dt_bias, a_log, d_skip,
           ssd_norm_w, q_norm_w, k_norm_w, rpb, w_br_ssd, w_br_na, w_out, w_ff1, w_ff3, w_ff2, w_router,
           w_e1, w_e3, w_e2):
    bsz, seqlen, d = x.shape
    clen = ctx.shape[1]
    n_layers = w_mod.shape[0]
    assert clen == MOD_TILE and seqlen % MOD_TILE == 0 and bsz < 16
    t = clen + seqlen
    m = bsz * t
    tiles_per_batch = t // MOD_TILE

    cond = jnp.zeros((16, d), F32).at[:bsz].set(c).at[bsz].set(c_ctx)
    mod = _modulation(cond, w_mod, b_mod)
    modt = [_tile_mod(mod[i], bsz, tiles_per_batch) for i in range(n_layers)]

    h = jnp.concatenate([ctx, x], axis=1).reshape(m, d)
    a = _adaln(h, norm1_w[0], modt[0])
    for i in range(n_layers):
        last = i == n_layers - 1
        wi = w_in[i]
        w_main = jnp.concatenate([wi[:, :IN_SPLITS[3]], wi[:, IN_SPLITS[4]:]], axis=1).astype(BF16)
        w_dt = jnp.pad(wi[:, IN_SPLITS[3]:IN_SPLITS[4]], ((0, 0), (0, DT_PAD - 2 * SSD_HEADS))).astype(BF16)
        p_main = _matmul(a, w_main, BF16, 512, 2560, "inproj")
        p_dt = _matmul(a, w_dt, F32, 1024, DT_PAD, "inproj_dt")
        pm3 = p_main.reshape(bsz, t, MAIN_DIM)
        y_ssd = _fb_ssd(pm3, p_dt.reshape(bsz, t, DT_PAD), conv_w[i], conv_b[i], dt_bias[i], a_log[i],
                        d_skip[i], ssd_norm_w[i], clen, not last).astype(BF16).reshape(m, d)
        y_na = _fb_na(pm3, q_norm_w[i], k_norm_w[i], rpb[i], clen, not last).astype(BF16).reshape(m, d)
        h, f = _merge(p_main, y_ssd, y_na, h, w_br_ssd[i].astype(BF16), w_br_na[i].astype(BF16),
                      w_out[i].astype(BF16), norm2_w[i], modt[i])
        fi = i // 2
        if i % 2 == 0:
            h, a = _ffn(f, h, w_ff1[fi].astype(BF16), w_ff3[fi].astype(BF16), w_ff2[fi].astype(BF16),
                        norm1_w[min(i + 1, n_layers - 1)], modt[i], modt[min(i + 1, n_layers - 1)])
        else:
            f_lat = f.reshape(bsz, t, d)[:, clen:].reshape(bsz * seqlen, d).astype(F32)
            f_out = _fb_moe(f_lat, w_router[fi], w_e1[fi], w_e3[fi], w_e2[fi]).reshape(bsz, seqlen, d)
            g2 = mod[i, 5, :bsz][:, None, :]
            h3 = h.reshape(bsz, t, d)
            h = h3.at[:, clen:].add(g2 * f_out).reshape(m, d)
    return h.reshape(bsz, t, d)[:, clen:]
```

```python
import functools
import math

import jax
import jax.numpy as jnp
from jax import lax
from jax.experimental import pallas as pl
from jax.experimental.pallas import tpu as pltpu

D_MODEL = 1024
GRID_W = 64
SSD_HEAD_DIM = 64
SSD_HEADS = D_MODEL // SSD_HEAD_DIM
D_SSD = SSD_HEADS * SSD_HEAD_DIM
SSD_GROUPS = 2
SSD_HPG = SSD_HEADS // SSD_GROUPS
SSD_STATE = 128
SSD_CONV = 5
SSD_CHUNK = 128
XBC_DIM = D_SSD + 2 * SSD_GROUPS * SSD_STATE
NA_HEAD_DIM = 64
NA_HEADS = D_MODEL // NA_HEAD_DIM
D_NA = NA_HEADS * NA_HEAD_DIM
NA_ROWS_MAX = 8
NA_COLS = 16
ROPE_BASE = 10000.0
N_EXPERTS = 8
TOP_K = 2
EPS = 1e-6
NEG_INF = -1e30
IN_SPLITS = (D_MODEL, 2 * D_MODEL, 2 * D_MODEL + D_SSD, 2 * D_MODEL + D_SSD + XBC_DIM,
             2 * D_MODEL + D_SSD + XBC_DIM + 2 * SSD_HEADS,
             2 * D_MODEL + D_SSD + XBC_DIM + 2 * SSD_HEADS + D_NA,
             2 * D_MODEL + D_SSD + XBC_DIM + 2 * SSD_HEADS + 2 * D_NA)

F32 = jnp.float32
BF16 = jnp.bfloat16
HIGHEST = lax.Precision.HIGHEST

MOD_TILE = 256
LANE = 128
VMEM_LIMIT = 48 * 1024 * 1024

COL_GS, COL_GN, COL_Z, COL_XBC, COL_Q, COL_K, COL_V = 0, 1024, 2048, 3072, 4608, 5632, 6656
MAIN_DIM = 7680
DT_PAD = LANE


def _params(sem):
    return pltpu.CompilerParams(dimension_semantics=sem, vmem_limit_bytes=VMEM_LIMIT)


def _rms_mod(x, nw, shift, scale):
    ms = jnp.mean(x * x, axis=-1, keepdims=True)
    return (x * lax.rsqrt(ms + EPS) * nw) * (1.0 + scale) + shift


def _mod_kernel(cond_ref, w_ref, b_ref, o_ref):
    c = cond_ref[...]
    s = c * jax.nn.sigmoid(c)
    o_ref[...] = jnp.dot(s, w_ref[...], precision=HIGHEST, preferred_element_type=F32) + b_ref[...]


def _modulation(cond, w_mod, b_mod):
    n_layers, d, _ = w_mod.shape
    rows = cond.shape[0]
    b4 = b_mod.reshape(n_layers, 6, 1, d)
    return pl.pallas_call(
        _mod_kernel,
        grid=(n_layers, 6),
        in_specs=[pl.BlockSpec((rows, d), lambda l, k: (0, 0)),
                  pl.BlockSpec((None, d, d), lambda l, k: (l, 0, k)),
                  pl.BlockSpec((None, None, 1, d), lambda l, k: (l, k, 0, 0))],
        out_specs=pl.BlockSpec((None, None, rows, d), lambda l, k: (l, k, 0, 0)),
        out_shape=jax.ShapeDtypeStruct((n_layers, 6, rows, d), F32),
        compiler_params=_params(("arbitrary", "arbitrary")),
        name="modulation",
    )(cond, w_mod, b4)


def _adaln_kernel(h_ref, nw_ref, modt_ref, o_ref, *, nsub):
    nw = nw_ref[...]
    for s in range(nsub):
        rows = pl.ds(s * MOD_TILE, MOD_TILE)
        x = h_ref[rows, :]
        o_ref[rows, :] = _rms_mod(x, nw, modt_ref[s, 0:1, :], modt_ref[s, 1:2, :]).astype(o_ref.dtype)


def _adaln(h, nw, modt, tm=1024):
    m, d = h.shape
    nsub = tm // MOD_TILE
    return pl.pallas_call(
        functools.partial(_adaln_kernel, nsub=nsub),
        grid=(m // tm,),
        in_specs=[pl.BlockSpec((tm, d), lambda i: (i, 0)),
                  pl.BlockSpec((1, d), lambda i: (0, 0)),
                  pl.BlockSpec((nsub, 6, d), lambda i: (i, 0, 0))],
        out_specs=pl.BlockSpec((tm, d), lambda i: (i, 0)),
        out_shape=jax.ShapeDtypeStruct((m, d), BF16),
        compiler_params=_params(("arbitrary",)),
        name="adaln",
    )(h, nw.reshape(1, d), modt)


def _matmul_kernel(a_ref, w_ref, o_ref):
    o_ref[...] = jnp.dot(a_ref[...], w_ref[...], preferred_element_type=F32).astype(o_ref.dtype)


def _matmul(a, w, out_dtype, tm, tn, name):
    m, k = a.shape
    n = w.shape[1]
    return pl.pallas_call(
        _matmul_kernel,
        grid=(n // tn, m // tm),
        in_specs=[pl.BlockSpec((tm, k), lambda j, i: (i, 0)),
                  pl.BlockSpec((k, tn), lambda j, i: (0, j))],
        out_specs=pl.BlockSpec((tm, tn), lambda j, i: (i, j)),
        out_shape=jax.ShapeDtypeStruct((m, n), out_dtype),
        compiler_params=_params(("arbitrary", "arbitrary")),
        name=name,
    )(a, w)


def _merge_kernel(gs_ref, gn_ref, ys_ref, yn_ref, h_ref, wbs_ref, wbn_ref, wo_ref, nw_ref, modt_ref,
                  ho_ref, f_ref, *, nsub, row_tile_out):
    a = jnp.dot(ys_ref[...], wbs_ref[...], preferred_element_type=F32)
    b = jnp.dot(yn_ref[...], wbn_ref[...], preferred_element_type=F32)
    mixed = (jax.nn.sigmoid(gs_ref[...].astype(F32)) * a + jax.nn.sigmoid(gn_ref[...].astype(F32)) * b)
    m = jnp.dot(mixed.astype(BF16), wo_ref[...], preferred_element_type=F32)
    nw = nw_ref[...]
    for s in range(nsub):
        r0, r1 = s * MOD_TILE, (s + 1) * MOD_TILE
        h = h_ref[r0:r1, :] + modt_ref[s, 2:3, :] * m[r0:r1, :]
        ho_ref[r0:r1, :] = h
        f = _rms_mod(h, nw, modt_ref[s, 3:4, :], modt_ref[s, 4:5, :])
        if row_tile_out:
            _store_tokens(f_ref, r0, f)
        else:
            f_ref[r0:r1, :] = f.astype(f_ref.dtype)


def _merge(p_main, y_ssd, y_na, h, wbs, wbn, wo, nw2, modt, row_tile_out, tm=512):
    m, d = h.shape
    nsub = tm // MOD_TILE
    row = lambda i: (i, 0)
    const = lambda i: (0, 0)
    if row_tile_out:
        f_spec, f_shape = pl.BlockSpec((tm * ROW_SEG, LANE), row), jax.ShapeDtypeStruct((m * ROW_SEG, LANE), F32)
    else:
        f_spec, f_shape = pl.BlockSpec((tm, d), row), jax.ShapeDtypeStruct((m, d), BF16)
    return pl.pallas_call(
        functools.partial(_merge_kernel, nsub=nsub, row_tile_out=row_tile_out),
        grid=(m // tm,),
        in_specs=[pl.BlockSpec((tm, d), lambda i: (i, COL_GS // D_MODEL)),
                  pl.BlockSpec((tm, d), lambda i: (i, COL_GN // D_MODEL)),
                  pl.BlockSpec((tm, d), row), pl.BlockSpec((tm, d), row), pl.BlockSpec((tm, d), row),
                  pl.BlockSpec((d, d), const), pl.BlockSpec((d, d), const), pl.BlockSpec((d, d), const),
                  pl.BlockSpec((1, d), const),
                  pl.BlockSpec((nsub, 6, d), lambda i: (i, 0, 0))],
        out_specs=[pl.BlockSpec((tm, d), row), f_spec],
        out_shape=[jax.ShapeDtypeStruct((m, d), F32), f_shape],
        compiler_params=_params(("arbitrary",)),
        name="merge",
    )(p_main, p_main, y_ssd, y_na, h, wbs, wbn, wo, nw2.reshape(1, d), modt)


def _ffn_kernel(x_ref, h_ref, w1_ref, w3_ref, w2_ref, nw_ref, modt_ref, modn_ref, ho_ref, an_ref, acc_ref,
                *, nsub):
    j = pl.program_id(1)

    @pl.when(j == 0)
    def _():
        acc_ref[...] = jnp.zeros_like(acc_ref)

    x = x_ref[...]
    h1 = jnp.dot(x, w1_ref[...], preferred_element_type=F32)
    h3 = jnp.dot(x, w3_ref[...], preferred_element_type=F32)
    act = (h1 * jax.nn.sigmoid(h1) * h3).astype(BF16)
    acc_ref[...] += jnp.dot(act, w2_ref[...], preferred_element_type=F32)

    @pl.when(j == pl.num_programs(1) - 1)
    def _():
        nw = nw_ref[...]
        for s in range(nsub):
            r0, r1 = s * MOD_TILE, (s + 1) * MOD_TILE
            h = h_ref[r0:r1, :] + modt_ref[s, 5:6, :] * acc_ref[r0:r1, :]
            ho_ref[r0:r1, :] = h
            an_ref[r0:r1, :] = _rms_mod(h, nw, modn_ref[s, 0:1, :], modn_ref[s, 1:2, :]).astype(an_ref.dtype)


def _ffn(x, h, w1, w3, w2, nw_next, modt, modt_next, tm=512, tf=1408):
    m, d = h.shape
    ff = w1.shape[1]
    nsub = tm // MOD_TILE
    row = lambda i, j: (i, 0)
    return pl.pallas_call(
        functools.partial(_ffn_kernel, nsub=nsub),
        grid=(m // tm, ff // tf),
        in_specs=[pl.BlockSpec((tm, d), row), pl.BlockSpec((tm, d), row),
                  pl.BlockSpec((d, tf), lambda i, j: (0, j)), pl.BlockSpec((d, tf), lambda i, j: (0, j)),
                  pl.BlockSpec((tf, d), lambda i, j: (j, 0)),
                  pl.BlockSpec((1, d), lambda i, j: (0, 0)),
                  pl.BlockSpec((nsub, 6, d), lambda i, j: (i, 0, 0)),
                  pl.BlockSpec((nsub, 6, d), lambda i, j: (i, 0, 0))],
        out_specs=[pl.BlockSpec((tm, d), row), pl.BlockSpec((tm, d), row)],
        out_shape=[jax.ShapeDtypeStruct((m, d), F32), jax.ShapeDtypeStruct((m, d), BF16)],
        scratch_shapes=[pltpu.VMEM((tm, d), F32)],
        compiler_params=_params(("arbitrary", "arbitrary")),
        name="ffn",
    )(x, h, w1, w3, w2, nw_next.reshape(1, d), modt, modt_next)


NA_BAND = NA_ROWS_MAX * GRID_W
HEAD_PAIR = LANE // NA_HEAD_DIM


def _head_rms(x, w, lane_lo):
    sq = x * x
    tot = jnp.sum(sq, axis=-1, keepdims=True)
    lo = jnp.sum(jnp.where(lane_lo, sq, 0.0), axis=-1, keepdims=True)
    ms = jnp.where(lane_lo, lo, tot - lo) * (1.0 / NA_HEAD_DIM)
    return x * lax.rsqrt(ms + EPS) * w


def _rope(x, cos, sin, first_half):
    partner = jnp.where(first_half, pltpu.roll(x, LANE - 16, axis=1), pltpu.roll(x, 16, axis=1))
    return x * cos + partner * sin


def _stack_heads(x, lane_lo):
    zero = jnp.zeros_like(x)
    return jnp.concatenate([jnp.where(lane_lo, x, zero), jnp.where(lane_lo, zero, x)], axis=0)


def _attend(qs, k_list, v_list, bias_list, lane_lo):
    nt = (((1,), (1,)), ((), ()))
    s_list = []
    for k, bias in zip(k_list, bias_list):
        s = lax.dot_general(qs, k, nt, preferred_element_type=F32)
        s_list.append(s if bias is None else s + bias)
    mx = s_list[0].max(axis=-1, keepdims=True)
    for s in s_list[1:]:
        mx = jnp.maximum(mx, s.max(axis=-1, keepdims=True))
    den = None
    acc = None
    for s, v in zip(s_list, v_list):
        p = jnp.exp(s - mx)
        d = p.sum(axis=-1, keepdims=True)
        o = jnp.dot(p.astype(BF16), v, preferred_element_type=F32)
        den = d if den is None else den + d
        acc = o if acc is None else acc + o
    acc = acc / den
    n = acc.shape[0] // 2
    return jnp.where(lane_lo, acc[:n], acc[n:])


def _na_kernel(q_ref, k_ref, v_ref, bias_ref, rowc_ref, rows_ref, colc_ref, cols_ref, qw_ref, kw_ref,
               o_ref, qn_ref, kn_ref, *, clen, n_rows):
    lane = lax.broadcasted_iota(jnp.int32, (1, LANE), 1)
    lane_lo = lane < NA_HEAD_DIM
    first_half = (lane % 32) < 16
    qw = qw_ref[...] * (NA_HEAD_DIM ** -0.5)
    kw = kw_ref[...]
    kr = NA_ROWS_MAX

    qn_ref[0:clen, :] = _head_rms(q_ref[0:clen, :].astype(F32), qw, lane_lo).astype(BF16)
    kn_ref[0:clen, :] = _head_rms(k_ref[0:clen, :].astype(F32), kw, lane_lo).astype(BF16)
    colc = colc_ref[...]
    cols = cols_ref[...]

    def prep(r, carry):
        rows = pl.ds(pl.multiple_of(clen + r * GRID_W, GRID_W), GRID_W)
        cos = rowc_ref[pl.ds(r, 1), :] + colc
        sin = rows_ref[pl.ds(r, 1), :] + cols
        q = _head_rms(q_ref[rows, :].astype(F32), qw, lane_lo)
        k = _head_rms(k_ref[rows, :].astype(F32), kw, lane_lo)
        qn_ref[rows, :] = _rope(q, cos, sin, first_half).astype(BF16)
        kn_ref[rows, :] = _rope(k, cos, sin, first_half).astype(BF16)
        return carry

    lax.fori_loop(0, n_rows, prep, 0)

    kc = kn_ref[0:clen, :]
    vc = v_ref[0:clen, :]
    for blk in range(clen // GRID_W):
        rows = slice(blk * GRID_W, (blk + 1) * GRID_W)
        qs = _stack_heads(qn_ref[rows, :], lane_lo)
        o_ref[rows, :] = _attend(qs, [kc], [vc], [None], lane_lo).astype(o_ref.dtype)

    def body(r, carry):
        r0 = jnp.clip(r - kr // 2, 0, n_rows - kr)
        off = r0 - r + (NA_ROWS_MAX - 1)
        rows = pl.ds(pl.multiple_of(clen + r * GRID_W, GRID_W), GRID_W)
        band = pl.ds(pl.multiple_of(clen + r0 * GRID_W, GRID_W), NA_BAND)
        qs = _stack_heads(qn_ref[rows, :], lane_lo)
        out = _attend(qs, [kn_ref[band, :], kc], [v_ref[band, :], vc], [bias_ref[off], None], lane_lo)
        o_ref[rows, :] = out.astype(o_ref.dtype)
        return carry

    lax.fori_loop(0, n_rows, body, 0)


def _na_tables(rpb, n_rows):
    col = jnp.arange(GRID_W)
    col_start = jnp.clip(col - NA_COLS // 2, 0, GRID_W - NA_COLS)
    col_ok = (col[None, :] >= col_start[:, None]) & (col[None, :] < col_start[:, None] + NA_COLS)
    col_idx = jnp.clip(col[None, :] - col[:, None], -(NA_COLS - 1), NA_COLS - 1) + NA_COLS - 1
    rpb_c = jnp.where(col_ok[None, None], rpb.astype(F32)[:, :, col_idx], NEG_INF)
    dr = jnp.arange(NA_ROWS_MAX)[:, None] + jnp.arange(NA_ROWS_MAX)[None, :]
    tab = rpb_c[:, dr]
    tab = jnp.transpose(tab, (0, 1, 3, 2, 4)).reshape(NA_HEADS, NA_ROWS_MAX, GRID_W, NA_BAND)
    tab = tab.reshape(NA_HEADS // HEAD_PAIR, HEAD_PAIR, NA_ROWS_MAX, GRID_W, NA_BAND)
    tab = jnp.transpose(tab, (0, 2, 1, 3, 4)).reshape(NA_HEADS // HEAD_PAIR, NA_ROWS_MAX,
                                                       HEAD_PAIR * GRID_W, NA_BAND)
    quarter = NA_HEAD_DIM // 4
    inv = ROPE_BASE ** (-jnp.arange(quarter, dtype=F32) / quarter)

    def tables(pos, lo):
        ang = pos[:, None] * inv[None, :]
        z = jnp.zeros_like(ang)
        c, s = jnp.cos(ang), jnp.sin(ang)
        ch = jnp.concatenate([c, c, z, z] if lo else [z, z, c, c], axis=-1)
        sh = jnp.concatenate([-s, s, z, z] if lo else [z, z, -s, s], axis=-1)
        return jnp.tile(ch, (1, HEAD_PAIR)), jnp.tile(sh, (1, HEAD_PAIR))

    rowc, rows = tables(jnp.arange(n_rows).astype(F32), True)
    colc, cols = tables(jnp.arange(GRID_W).astype(F32), False)
    return tab, rowc, rows, colc, cols


def _na(pm3, q_norm_w, k_norm_w, rpb, clen):
    bsz, t, _ = pm3.shape
    n_rows = (t - clen) // GRID_W
    tab, rowc, rows, colc, cols = _na_tables(rpb, n_rows)
    qw = jnp.tile(q_norm_w.astype(F32), HEAD_PAIR).reshape(1, LANE)
    kw = jnp.tile(k_norm_w.astype(F32), HEAD_PAIR).reshape(1, LANE)
    n_pairs = NA_HEADS // HEAD_PAIR
    const = lambda p, b: (0, 0)
    tok = lambda c0: pl.BlockSpec((None, t, LANE), lambda p, b: (b, 0, c0 // LANE + p))
    return pl.pallas_call(
        functools.partial(_na_kernel, clen=clen, n_rows=n_rows),
        grid=(n_pairs, bsz),
        in_specs=[tok(COL_Q), tok(COL_K), tok(COL_V),
                  pl.BlockSpec((None, NA_ROWS_MAX, HEAD_PAIR * GRID_W, NA_BAND), lambda p, b: (p, 0, 0, 0)),
                  pl.BlockSpec((n_rows, LANE), const), pl.BlockSpec((n_rows, LANE), const),
                  pl.BlockSpec((GRID_W, LANE), const), pl.BlockSpec((GRID_W, LANE), const),
                  pl.BlockSpec((1, LANE), const), pl.BlockSpec((1, LANE), const)],
        out_specs=pl.BlockSpec((None, t, LANE), lambda p, b: (b, 0, p)),
        out_shape=jax.ShapeDtypeStruct((bsz, t, D_NA), BF16),
        scratch_shapes=[pltpu.VMEM((t, LANE), BF16), pltpu.VMEM((t, LANE), BF16)],
        compiler_params=_params(("arbitrary", "arbitrary")),
        name="natten",
    )(pm3, pm3, pm3, tab, rowc, rows, colc, cols, qw, kw)


HALO = 16
CONV_PAD = SSD_CONV // 2
BC_DIM = SSD_GROUPS * SSD_STATE
PAIRS_PER_GROUP = SSD_HPG // HEAD_PAIR


def _ssd_chunk_index(s, rev, cc, nc):
    if not rev:
        return s
    return jnp.where(s < cc, cc - 1 - s, nc - 1 - (s - cc))


def _ssd_core(main_ref, prev_ref, next_ref, dt_ref, cw_ref, cb_ref, dtb_ref, a_ref, exp_ref,
              h_ref, ext_ref, yoff_ref, y_ref, *, rev, cc, nc):
    q = SSD_CHUNK
    s = pl.program_id(1)
    cidx = _ssd_chunk_index(s, rev, cc, nc)

    @pl.when(s == 0)
    def _():
        h_ref[...] = jnp.zeros_like(h_ref)

    keep_prev = jnp.where((cidx == 0) | (cidx == cc), 0.0, 1.0)
    keep_next = jnp.where((cidx == cc - 1) | (cidx == nc - 1), 0.0, 1.0)
    ext_ref[0:8, :] = prev_ref[...].astype(F32)[HALO - 8:HALO, :] * keep_prev
    ext_ref[8:8 + q, :] = main_ref[...].astype(F32)
    ext_ref[8 + q:16 + q, :] = next_ref[...].astype(F32)[0:8, :] * keep_next
    conv = cb_ref[...] + cw_ref[0:1, :] * ext_ref[pl.ds(8 - CONV_PAD, q), :]
    for k in range(1, SSD_CONV):
        conv = conv + cw_ref[k:k + 1, :] * ext_ref[pl.ds(8 - CONV_PAD + k, q), :]
    xbc = conv * jax.nn.sigmoid(conv)
    xs = xbc[:, :D_SSD]

    hb = SSD_HEADS if rev else 0
    x = dt_ref[...] + dtb_ref[...]
    dt = jnp.maximum(x, 0.0) + jnp.log(1.0 + jnp.exp(-jnp.abs(x)))
    la = dt * a_ref[...]
    ri = lax.broadcasted_iota(jnp.int32, (q, q), 0)
    ci = lax.broadcasted_iota(jnp.int32, (q, q), 1)
    allowed = (ci >= ri) if rev else (ci <= ri)
    tri = jnp.where(allowed, 1.0, 0.0).astype(F32)
    a_cs = jnp.dot(tri, la, precision=HIGHEST, preferred_element_type=F32)
    last = 0 if rev else q - 1
    a_cs_t = a_cs.T
    dt_t = dt.T
    e_all = jnp.exp(a_cs)
    wdt_t = jnp.exp(a_cs_t[:, last:last + 1] - a_cs_t) * dt_t
    blk = (last // 8) * 8
    e_last = jnp.exp(jnp.dot(a_cs[blk:blk + 8, :], exp_ref[...], precision=HIGHEST,
                             preferred_element_type=F32))[last - blk:last - blk + 1, hb * SSD_HEAD_DIM:
                                                          (hb + SSD_HEADS) * SSD_HEAD_DIM]

    lane_lo = lax.broadcasted_iota(jnp.int32, (1, LANE), 1) < SSD_HEAD_DIM
    nt = (((1,), (1,)), ((), ()))
    for g in range(SSD_GROUPS):
        b_g = xbc[:, D_SSD + g * SSD_STATE:D_SSD + (g + 1) * SSD_STATE]
        c_g = xbc[:, D_SSD + BC_DIM + g * SSD_STATE:D_SSD + BC_DIM + (g + 1) * SSD_STATE].astype(BF16)
        cols = slice(g * SSD_HPG * SSD_HEAD_DIM, (g + 1) * SSD_HPG * SSD_HEAD_DIM)
        yoff_ref[:, cols] = jnp.dot(c_g, h_ref[:, cols].astype(BF16), preferred_element_type=F32)
        cb = lax.dot_general(c_g, b_g.astype(BF16), nt, preferred_element_type=F32)
        b_t = b_g.T
        for pj in range(PAIRS_PER_GROUP):
            pair = g * PAIRS_PER_GROUP + pj
            lanes = slice(pair * LANE, (pair + 1) * LANE)
            lhs, ecol = [], []
            for hh in range(HEAD_PAIR):
                head = hb + pair * HEAD_PAIR + hh
                seg = a_cs[:, head:head + 1] - a_cs_t[head:head + 1, :]
                decay = jnp.exp(jnp.where(allowed, seg, NEG_INF))
                lhs.append((cb * decay * dt_t[head:head + 1, :]).astype(BF16))
                ecol.append(jnp.broadcast_to(e_all[:, head:head + 1], (q, LANE)))
            for hh in range(HEAD_PAIR):
                head = hb + pair * HEAD_PAIR + hh
                lhs.append((b_t * wdt_t[head:head + 1, :]).astype(BF16))
            res = jnp.dot(jnp.concatenate(lhs, axis=0), xs[:, lanes].astype(BF16), preferred_element_type=F32)
            y_ref[:, lanes] = (jnp.where(lane_lo, res[0:q], res[q:2 * q])
                               + yoff_ref[:, lanes] * jnp.where(lane_lo, ecol[0], ecol[1]))
            h_ref[:, lanes] = (h_ref[:, lanes] * e_last[:, lanes]
                               + jnp.where(lane_lo, res[2 * q:3 * q], res[3 * q:4 * q]))
    return xs


def _ssd_fwd_kernel(main_ref, prev_ref, next_ref, dt_ref, cw_ref, cb_ref, dtb_ref, a_ref, exp_ref, dsum_ref,
                    o_ref, h_ref, ext_ref, yoff_ref, y_ref, *, cc, nc):
    xs = _ssd_core(main_ref, prev_ref, next_ref, dt_ref, cw_ref, cb_ref, dtb_ref, a_ref, exp_ref,
                   h_ref, ext_ref, yoff_ref, y_ref, rev=False, cc=cc, nc=nc)
    o_ref[...] = y_ref[...] + xs * dsum_ref[...]


def _ssd_bwd_kernel(main_ref, prev_ref, next_ref, dt_ref, cw_ref, cb_ref, dtb_ref, a_ref, exp_ref, y1_ref, z_ref,
                    nw_ref, o_ref, h_ref, ext_ref, yoff_ref, y_ref, *, cc, nc):
    _ssd_core(main_ref, prev_ref, next_ref, dt_ref, cw_ref, cb_ref, dtb_ref, a_ref, exp_ref,
              h_ref, ext_ref, yoff_ref, y_ref, rev=True, cc=cc, nc=nc)
    z = z_ref[...].astype(F32)
    u = (y_ref[...] + y1_ref[...]) * (z * jax.nn.sigmoid(z))
    gw = D_SSD // SSD_GROUPS
    for g in range(SSD_GROUPS):
        ug = u[:, g * gw:(g + 1) * gw]
        ms = jnp.mean(ug * ug, axis=-1, keepdims=True)
        o_ref[:, g * gw:(g + 1) * gw] = (ug * lax.rsqrt(ms + EPS) * nw_ref[:, g * gw:(g + 1) * gw]).astype(o_ref.dtype)


def _ssd(pm3, pdt3, conv_w, conv_b, dt_bias, a_log, d_skip, ssd_norm_w, clen):
    bsz, t, _ = pm3.shape
    q = SSD_CHUNK
    nc, cc = t // q, clen // q
    nh = 2 * SSD_HEADS
    cw = jnp.pad(conv_w.astype(F32), ((0, 8 - SSD_CONV), (0, 0)))
    cb = conv_b.astype(F32).reshape(1, XBC_DIM)
    dtb = jnp.pad(dt_bias.astype(F32).reshape(1, nh), ((0, 0), (0, DT_PAD - nh)))
    a_neg = jnp.pad(-jnp.exp(a_log.astype(F32)).reshape(1, nh), ((0, 0), (0, DT_PAD - nh)))
    dsum = jnp.repeat((d_skip[0] + d_skip[1]).astype(F32), SSD_HEAD_DIM).reshape(1, D_SSD)
    expand = jnp.repeat(jnp.eye(DT_PAD, dtype=F32)[:, :nh], SSD_HEAD_DIM, axis=1)
    nw = ssd_norm_w.astype(F32).reshape(1, D_SSD)
    hpb = q // HALO
    n_halo = t // HALO

    def specs(rev):
        ch = lambda b, s: _ssd_chunk_index(s, rev, cc, nc)
        const = lambda b, s: (0, 0)
        return [
            pl.BlockSpec((None, q, XBC_DIM), lambda b, s: (b, ch(b, s), COL_XBC // XBC_DIM)),
            pl.BlockSpec((None, HALO, XBC_DIM),
                         lambda b, s: (b, jnp.maximum(ch(b, s) * hpb - 1, 0), COL_XBC // XBC_DIM)),
            pl.BlockSpec((None, HALO, XBC_DIM),
                         lambda b, s: (b, jnp.minimum((ch(b, s) + 1) * hpb, n_halo - 1), COL_XBC // XBC_DIM)),
            pl.BlockSpec((None, q, DT_PAD), lambda b, s: (b, ch(b, s), 0)),
            pl.BlockSpec((8, XBC_DIM), const), pl.BlockSpec((1, XBC_DIM), const),
            pl.BlockSpec((1, DT_PAD), const), pl.BlockSpec((1, DT_PAD), const),
            pl.BlockSpec((DT_PAD, nh * SSD_HEAD_DIM), const),
        ], (lambda b, s: (b, ch(b, s), 0)), const

    scratch = [pltpu.VMEM((SSD_STATE, D_SSD), F32), pltpu.VMEM((q + 16, XBC_DIM), F32),
               pltpu.VMEM((q, D_SSD), F32), pltpu.VMEM((q, D_SSD), F32)]
    in_f, tok_f, const_f = specs(False)
    y1 = pl.pallas_call(
        functools.partial(_ssd_fwd_kernel, cc=cc, nc=nc),
        grid=(bsz, nc),
        in_specs=in_f + [pl.BlockSpec((1, D_SSD), const_f)],
        out_specs=pl.BlockSpec((None, q, D_SSD), tok_f),
        out_shape=jax.ShapeDtypeStruct((bsz, t, D_SSD), F32),
        scratch_shapes=scratch,
        compiler_params=_params(("arbitrary", "arbitrary")),
        name="ssd_fwd",
    )(pm3, pm3, pm3, pdt3, cw, cb, dtb, a_neg, expand, dsum)
    in_b, tok_b, const_b = specs(True)
    return pl.pallas_call(
        functools.partial(_ssd_bwd_kernel, cc=cc, nc=nc),
        grid=(bsz, nc),
        in_specs=in_b + [pl.BlockSpec((None, q, D_SSD), tok_b),
                         pl.BlockSpec((None, q, D_SSD), lambda b, s: (b, _ssd_chunk_index(s, True, cc, nc),
                                                                        COL_Z // D_SSD)),
                         pl.BlockSpec((1, D_SSD), const_b)],
        out_specs=pl.BlockSpec((None, q, D_SSD), tok_b),
        out_shape=jax.ShapeDtypeStruct((bsz, t, D_SSD), BF16),
        scratch_shapes=scratch,
        compiler_params=_params(("arbitrary", "arbitrary")),
        name="ssd_bwd",
    )(pm3, pm3, pm3, pdt3, cw, cb, dtb, a_neg, expand, y1, pm3, nw)


MOE_TB = 512
ROUTE_TM = 512
INFO_W = 8
BIG_NEG = -3.0e38
ROW_SEG = D_MODEL // LANE


def _load_tokens(ref, t0, n):
    return jnp.concatenate([ref[pl.ds(t0 * ROW_SEG + c, n, stride=ROW_SEG), :] for c in range(ROW_SEG)], axis=1)


def _store_tokens(ref, t0, val):
    n = val.shape[0]
    for c in range(ROW_SEG):
        ref[pl.ds(t0 * ROW_SEG + c, n, stride=ROW_SEG), :] = val[:, c * LANE:(c + 1) * LANE]


def _route_kernel(f_ref, wr_ref, info_ref, cnt_ref, run_ref):
    tm = f_ref.shape[0] // ROW_SEG

    @pl.when(pl.program_id(0) == 0)
    def _():
        run_ref[...] = jnp.zeros_like(run_ref)

    logits = jnp.dot(_load_tokens(f_ref, 0, tm).astype(BF16), wr_ref[...], preferred_element_type=F32)
    lane = lax.broadcasted_iota(jnp.int32, (tm, LANE), 1).astype(F32)
    lg = jnp.where(lane < N_EXPERTS, logits, BIG_NEG)
    m1 = lg.max(axis=-1, keepdims=True)
    i1 = jnp.where(lg == m1, lane, float(LANE)).min(axis=-1, keepdims=True)
    lg2 = jnp.where(lane == i1, BIG_NEG, lg)
    m2 = lg2.max(axis=-1, keepdims=True)
    i2 = jnp.where(lg2 == m2, lane, float(LANE)).min(axis=-1, keepdims=True)
    ex = jnp.exp(m2 - m1)
    g0 = 1.0 / (1.0 + ex)
    g1 = ex / (1.0 + ex)
    sel0 = lane == i1
    sel1 = lane == i2
    onehot = jnp.where(sel0 | sel1, 1.0, 0.0)
    ri = lax.broadcasted_iota(jnp.int32, (tm, tm), 0)
    ci = lax.broadcasted_iota(jnp.int32, (tm, tm), 1)
    strict_lower = jnp.where(ci < ri, 1.0, 0.0).astype(BF16)
    before = jnp.dot(strict_lower, onehot.astype(BF16), preferred_element_type=F32) + run_ref[...]
    pos0 = jnp.where(sel0, before, 0.0).sum(axis=-1, keepdims=True)
    pos1 = jnp.where(sel1, before, 0.0).sum(axis=-1, keepdims=True)
    run_ref[...] = run_ref[...] + onehot.sum(axis=0, keepdims=True)
    cnt_ref[...] = run_ref[...]
    info = jnp.where(lane == 0, i1, jnp.where(lane == 1, i2, jnp.where(lane == 2, g0, jnp.where(
        lane == 3, g1, jnp.where(lane == 4, pos0, jnp.where(lane == 5, pos1, 0.0))))))
    info_ref[...] = info[:, :INFO_W]


def _route(f8, w_router):
    m, d = f8.shape[0] // ROW_SEG, D_MODEL
    wr = jnp.pad(w_router.astype(BF16), ((0, 0), (0, LANE - N_EXPERTS)))
    return pl.pallas_call(
        _route_kernel,
        grid=(m // ROUTE_TM,),
        in_specs=[pl.BlockSpec((ROUTE_TM * ROW_SEG, LANE), lambda i: (i, 0)),
                  pl.BlockSpec((d, LANE), lambda i: (0, 0))],
        out_specs=[pl.BlockSpec((ROUTE_TM, INFO_W), lambda i: (i, 0)), pl.BlockSpec((1, LANE), lambda i: (0, 0))],
        out_shape=[jax.ShapeDtypeStruct((m, INFO_W), F32), jax.ShapeDtypeStruct((1, LANE), F32)],
        scratch_shapes=[pltpu.VMEM((1, LANE), F32)],
        compiler_params=_params(("arbitrary",)),
        name="moe_route",
    )(f8, wr)


def _row_copy(src_ref, src_row, dst_ref, dst_row, sem):
    src = src_ref.at[pl.ds(pl.multiple_of(src_row * ROW_SEG, ROW_SEG), ROW_SEG)]
    dst = dst_ref.at[pl.ds(pl.multiple_of(dst_row * ROW_SEG, ROW_SEG), ROW_SEG)]
    return pltpu.make_async_copy(src, dst, sem)


def _dispatch_kernel(dest_hbm, f_ref, init_hbm, rows_hbm, dest_smem, idx_sem, row_sem):
    del init_hbm
    tm = f_ref.shape[0] // ROW_SEG
    base = pl.multiple_of(pl.program_id(0) * (TOP_K * tm), TOP_K * tm)
    idx_copy = pltpu.make_async_copy(dest_hbm.at[pl.ds(base, TOP_K * tm)], dest_smem, idx_sem)
    idx_copy.start()
    idx_copy.wait()

    def start(t, c):
        for k in range(TOP_K):
            _row_copy(f_ref, t, rows_hbm, dest_smem[TOP_K * t + k], row_sem).start()
        return c

    lax.fori_loop(0, tm, start, 0, unroll=8)

    def wait(t, c):
        for k in range(TOP_K):
            _row_copy(f_ref, 0, rows_hbm, 0, row_sem).wait()
        return c

    lax.fori_loop(0, tm, wait, 0, unroll=8)


def _dispatch(f8, dest_flat, n_rows):
    m = f8.shape[0] // ROW_SEG
    return pl.pallas_call(
        _dispatch_kernel,
        grid=(m // ROUTE_TM,),
        in_specs=[pl.BlockSpec(memory_space=pl.ANY), pl.BlockSpec((ROUTE_TM * ROW_SEG, LANE), lambda i: (i, 0)),
                  pl.BlockSpec(memory_space=pl.ANY)],
        out_specs=pl.BlockSpec(memory_space=pl.ANY),
        out_shape=jax.ShapeDtypeStruct((n_rows * ROW_SEG, LANE), f8.dtype),
        scratch_shapes=[pltpu.SMEM((TOP_K * ROUTE_TM,), jnp.int32), pltpu.SemaphoreType.DMA(()),
                        pltpu.SemaphoreType.DMA(())],
        input_output_aliases={2: 0},
        compiler_params=_params(("arbitrary",)),
        name="moe_dispatch",
    )(dest_flat, f8, jnp.zeros((n_rows * ROW_SEG, LANE), f8.dtype))


def _gffn_kernel(be_ref, nu_ref, x_ref, w1_ref, w3_ref, w2_ref, y_ref, acc_ref):
    i, j = pl.program_id(0), pl.program_id(1)

    @pl.when(i < nu_ref[0])
    def _():
        @pl.when(j == 0)
        def _():
            acc_ref[...] = jnp.zeros_like(acc_ref)

        x = _load_tokens(x_ref, 0, MOE_TB).astype(BF16)
        h1 = jnp.dot(x, w1_ref[...], preferred_element_type=F32)
        h3 = jnp.dot(x, w3_ref[...], preferred_element_type=F32)
        act = (h1 * jax.nn.sigmoid(h1) * h3).astype(BF16)
        acc_ref[...] += jnp.dot(act, w2_ref[...], preferred_element_type=F32)

        @pl.when(j == pl.num_programs(1) - 1)
        def _():
            _store_tokens(y_ref, 0, acc_ref[...])

    @pl.when((i >= nu_ref[0]) & (j == pl.num_programs(1) - 1))
    def _():
        y_ref[...] = jnp.zeros_like(y_ref)


def _gffn(rows, block_e, n_used, w1, w3, w2, tf=1792):
    r, d = rows.shape[0] // ROW_SEG, D_MODEL
    ff = w1.shape[2]
    nb = r // MOE_TB
    grid_spec = pltpu.PrefetchScalarGridSpec(
        num_scalar_prefetch=2,
        grid=(nb, ff // tf),
        in_specs=[pl.BlockSpec((MOE_TB * ROW_SEG, LANE), lambda i, j, be, nu: (i, 0)),
                  pl.BlockSpec((None, d, tf), lambda i, j, be, nu: (be[i], 0, j)),
                  pl.BlockSpec((None, d, tf), lambda i, j, be, nu: (be[i], 0, j)),
                  pl.BlockSpec((None, tf, d), lambda i, j, be, nu: (be[i], j, 0))],
        out_specs=pl.BlockSpec((MOE_TB * ROW_SEG, LANE), lambda i, j, be, nu: (i, 0)),
        scratch_shapes=[pltpu.VMEM((MOE_TB, d), F32)],
    )
    return pl.pallas_call(
        _gffn_kernel,
        grid_spec=grid_spec,
        out_shape=jax.ShapeDtypeStruct((r * ROW_SEG, LANE), F32),
        compiler_params=_params(("arbitrary", "arbitrary")),
        name="moe_ffn",
    )(block_e, n_used, rows, w1, w3, w2)


def _combine_kernel(dest_hbm, y_hbm, h_ref, info_ref, modt_ref, o_ref, dest_smem, buf_ref, idx_sem, row_sem,
                    *, nsub):
    tm = h_ref.shape[0]
    base = pl.multiple_of(pl.program_id(0) * (TOP_K * tm), TOP_K * tm)
    idx_copy = pltpu.make_async_copy(dest_hbm.at[pl.ds(base, TOP_K * tm)], dest_smem, idx_sem)
    idx_copy.start()
    idx_copy.wait()

    def start(t, c):
        for k in range(TOP_K):
            _row_copy(y_hbm, dest_smem[TOP_K * t + k], buf_ref.at[k], t, row_sem).start()
        return c

    lax.fori_loop(0, tm, start, 0, unroll=8)

    def wait(t, c):
        for k in range(TOP_K):
            _row_copy(y_hbm, 0, buf_ref.at[k], 0, row_sem).wait()
        return c

    lax.fori_loop(0, tm, wait, 0, unroll=8)
    for s in range(nsub):
        r0, r1 = s * MOD_TILE, (s + 1) * MOD_TILE
        info = info_ref[r0:r1, :]
        mix = (info[:, 2:3] * _load_tokens(buf_ref.at[0], r0, MOD_TILE)
               + info[:, 3:4] * _load_tokens(buf_ref.at[1], r0, MOD_TILE))
        o_ref[r0:r1, :] = h_ref[r0:r1, :] + modt_ref[s, 5:6, :] * mix


def _combine(y_rows, dest_flat, h, info, modt):
    m, d = h.shape
    nsub = ROUTE_TM // MOD_TILE
    return pl.pallas_call(
        functools.partial(_combine_kernel, nsub=nsub),
        grid=(m // ROUTE_TM,),
        in_specs=[pl.BlockSpec(memory_space=pl.ANY), pl.BlockSpec(memory_space=pl.ANY),
                  pl.BlockSpec((ROUTE_TM, d), lambda i: (i, 0)),
                  pl.BlockSpec((ROUTE_TM, INFO_W), lambda i: (i, 0)),
                  pl.BlockSpec((nsub, 6, d), lambda i: (i, 0, 0))],
        out_specs=pl.BlockSpec((ROUTE_TM, d), lambda i: (i, 0)),
        out_shape=jax.ShapeDtypeStruct((m, d), F32),
        scratch_shapes=[pltpu.SMEM((TOP_K * ROUTE_TM,), jnp.int32),
                        pltpu.VMEM((TOP_K, ROUTE_TM * ROW_SEG, LANE), F32),
                        pltpu.SemaphoreType.DMA(()), pltpu.SemaphoreType.DMA(())],
        compiler_params=_params(("arbitrary",)),
        name="moe_combine",
    )(dest_flat, y_rows, h, info, modt)


def _moe(f8, h, w_router, w1, w3, w2, modt):
    m, d = h.shape
    info, cnt = _route(f8, w_router)
    e = info[:, 0:TOP_K].astype(jnp.int32)
    pos = info[:, 4:4 + TOP_K].astype(jnp.int32)
    counts = cnt[0, :N_EXPERTS].astype(jnp.int32)
    padded = (counts + MOE_TB - 1) // MOE_TB * MOE_TB
    pad_end = jnp.cumsum(padded)
    pad_start = pad_end - padded
    dest_flat = (pad_start[e] + pos).reshape(-1)
    n_rows = m * TOP_K + N_EXPERTS * MOE_TB
    nb = n_rows // MOE_TB
    block_e = jnp.minimum(jnp.searchsorted(pad_end, jnp.arange(nb, dtype=jnp.int32) * MOE_TB, side='right'),
                          N_EXPERTS - 1).astype(jnp.int32)
    n_used = (pad_end[-1:] // MOE_TB).astype(jnp.int32)
    rows = _dispatch(f8, dest_flat, n_rows)
    y_rows = _gffn(rows, block_e, n_used, w1.astype(BF16), w3.astype(BF16), w2.astype(BF16))
    return _combine(y_rows, dest_flat, h, info, modt)


def _fb_rms_norm(x, w):
    xf = x.astype(F32)
    y = xf * lax.rsqrt(jnp.mean(xf * xf, axis=-1, keepdims=True) + EPS)
    return (y * w.astype(F32)).astype(x.dtype)


def _fb_flip(a):
    return jnp.flip(a, axis=1)


def _fb_depthwise_conv(x, w, b):
    y = lax.conv_general_dilated(x, w[:, None, :].astype(x.dtype), window_strides=(1,),
                                 padding=[(SSD_CONV // 2, SSD_CONV // 2)],
                                 dimension_numbers=('NWC', 'WIO', 'NWC'),
                                 feature_group_count=x.shape[-1])
    return y + b.astype(x.dtype)


def _fb_ssd_inputs(xbc_raw, dt_raw, conv_w, conv_b, dt_bias):
    bsz, seqlen, _ = xbc_raw.shape
    xbc = jax.nn.silu(_fb_depthwise_conv(xbc_raw, conv_w, conv_b))
    xs, bs, cs = jnp.split(xbc, (D_SSD, D_SSD + SSD_GROUPS * SSD_STATE), axis=-1)
    xs = xs.reshape(bsz, seqlen, SSD_GROUPS, SSD_HPG, SSD_HEAD_DIM)
    bs = bs.reshape(bsz, seqlen, SSD_GROUPS, SSD_STATE)
    cs = cs.reshape(bsz, seqlen, SSD_GROUPS, SSD_STATE)
    dt = jax.nn.softplus(dt_raw.astype(F32) + dt_bias.reshape(-1).astype(F32))
    dt = dt.reshape(bsz, seqlen, 2, SSD_GROUPS, SSD_HPG)
    return xs, bs, cs, dt[:, :, 0], dt[:, :, 1]


def _fb_ssd_chunked(x, dt, a, b_in, c_in, h0, return_y=True):
    bsz, seqlen, g, r, p = x.shape
    n = b_in.shape[-1]
    nc = seqlen // SSD_CHUNK
    dtf = dt.astype(F32)
    xdt = (x.astype(F32) * dtf[..., None]).reshape(bsz, nc, SSD_CHUNK, g, r, p)
    log_a = (dtf * a.astype(F32)).reshape(bsz, nc, SSD_CHUNK, g, r)
    bc = b_in.astype(F32).reshape(bsz, nc, SSD_CHUNK, g, n)
    cc = c_in.astype(F32).reshape(bsz, nc, SSD_CHUNK, g, n)
    a_cs = jnp.cumsum(log_a, axis=2)
    a_last = a_cs[:, :, -1]
    states = jnp.einsum('bcsgn,bcsgr,bcsgrp->bcgrpn', bc, jnp.exp(a_last[:, :, None] - a_cs), xdt)

    def step(h, inp):
        s_c, d_c = inp
        return h * d_c[..., None, None] + s_c, h

    h_fin, h_in = lax.scan(step, h0.astype(F32),
                           (jnp.moveaxis(states, 1, 0), jnp.moveaxis(jnp.exp(a_last), 1, 0)))
    if not return_y:
        return None, h_fin
    h_in = jnp.moveaxis(h_in, 0, 1)
    tri = jnp.tril(jnp.ones((SSD_CHUNK, SSD_CHUNK), bool))
    seg = a_cs[:, :, :, None] - a_cs[:, :, None, :]
    decay = jnp.exp(jnp.where(tri[:, :, None, None], seg, -jnp.inf))
    cb = jnp.einsum('bcqgn,bcsgn->bcqsg', cc, bc)
    y_diag = jnp.einsum('bcqsgr,bcsgrp->bcqgrp', cb[..., None] * decay, xdt)
    y_off = jnp.einsum('bcqgn,bcgrpn->bcqgrp', cc, h_in) * jnp.exp(a_cs)[..., None]
    y = (y_diag + y_off).reshape(bsz, seqlen, g, r, p)
    return y.astype(x.dtype), h_fin


def _fb_gated_group_rmsnorm(y, z, w):
    shp = y.shape
    u = y.astype(F32) * jax.nn.silu(z.astype(F32))
    u = u.reshape(*shp[:-1], SSD_GROUPS, shp[-1] // SSD_GROUPS)
    u = u * lax.rsqrt(jnp.mean(u * u, axis=-1, keepdims=True) + EPS)
    return (u.reshape(shp) * w.astype(F32)).astype(y.dtype)


def _fb_ssd(pm, pdt, conv_w, conv_b, dt_bias, a_log, d_skip, ssd_norm_w, clen, need_ctx):
    pm = pm.astype(F32)
    bsz = pm.shape[0]
    z = pm[:, :, COL_Z:COL_Z + D_SSD]
    xbc = pm[:, :, COL_XBC:COL_XBC + XBC_DIM]
    dt = pdt[:, :, :2 * SSD_HEADS]
    z_c, z_l = z[:, :clen], z[:, clen:]
    a_dir = -jnp.exp(a_log.astype(F32)).reshape(2, SSD_GROUPS, SSD_HPG)
    d_sum = (d_skip[0] + d_skip[1]).reshape(SSD_GROUPS, SSD_HPG)[..., None]
    xs_l, b_l, c_l, dtf_l, dtb_l = _fb_ssd_inputs(xbc[:, clen:], dt[:, clen:], conv_w, conv_b, dt_bias)
    xs_c, b_c, c_c, dtf_c, dtb_c = _fb_ssd_inputs(xbc[:, :clen], dt[:, :clen], conv_w, conv_b, dt_bias)
    h0 = jnp.zeros((bsz, SSD_GROUPS, SSD_HPG, SSD_HEAD_DIM, SSD_STATE), F32)
    yf_c, sf_c = _fb_ssd_chunked(xs_c, dtf_c, a_dir[0], b_c, c_c, h0, need_ctx)
    yb_c, sb_c = _fb_ssd_chunked(_fb_flip(xs_c), _fb_flip(dtb_c), a_dir[1], _fb_flip(b_c), _fb_flip(c_c), h0,
                                 need_ctx)
    yf_l, _ = _fb_ssd_chunked(xs_l, dtf_l, a_dir[0], b_l, c_l, sf_c)
    yb_l, _ = _fb_ssd_chunked(_fb_flip(xs_l), _fb_flip(dtb_l), a_dir[1], _fb_flip(b_l), _fb_flip(c_l), sb_c)
    seqlen = xs_l.shape[1]
    y_l = (yf_l + _fb_flip(yb_l) + xs_l * d_sum.astype(xs_l.dtype)).reshape(bsz, seqlen, D_SSD)
    y_l = _fb_gated_group_rmsnorm(y_l, z_l, ssd_norm_w)
    if need_ctx:
        y_c = (yf_c + _fb_flip(yb_c) + xs_c * d_sum.astype(xs_c.dtype)).reshape(bsz, clen, D_SSD)
        y_c = _fb_gated_group_rmsnorm(y_c, z_c, ssd_norm_w)
    else:
        y_c = jnp.zeros((bsz, clen, D_SSD), F32)
    return jnp.concatenate([y_c, y_l], axis=1)


def _fb_axial_rope(x, pos_r, pos_c):
    hd = x.shape[-1]
    quarter = hd // 4
    inv = ROPE_BASE ** (-jnp.arange(quarter, dtype=F32) / quarter)

    def rot(xh, pos):
        ang = pos[:, None] * inv[None, :]
        cos = jnp.cos(ang)[None, :, None, :].astype(x.dtype)
        sin = jnp.sin(ang)[None, :, None, :].astype(x.dtype)
        x1, x2 = jnp.split(xh, 2, axis=-1)
        return jnp.concatenate([x1 * cos - x2 * sin, x1 * sin + x2 * cos], axis=-1)

    return jnp.concatenate([rot(x[..., :hd // 2], pos_r), rot(x[..., hd // 2:], pos_c)], axis=-1)


def _fb_na_latent(q, k, v, kc, vc, rpb):
    bsz, seqlen, h, hd = q.shape
    rows = seqlen // GRID_W
    kr = min(NA_ROWS_MAX, rows)
    scale = hd ** -0.5
    qg = q.reshape(bsz, rows, GRID_W, h, hd)
    kg = k.reshape(bsz, rows, GRID_W, h, hd)
    vg = v.reshape(bsz, rows, GRID_W, h, hd)
    col = jnp.arange(GRID_W)
    col_start = jnp.clip(col - NA_COLS // 2, 0, GRID_W - NA_COLS)
    col_ok = (col[None, :] >= col_start[:, None]) & (col[None, :] < col_start[:, None] + NA_COLS)
    col_idx = jnp.clip(col[None, :] - col[:, None], -(NA_COLS - 1), NA_COLS - 1) + NA_COLS - 1
    rpb_c = rpb.astype(F32)[:, :, col_idx]

    def one_row(args):
        r, q_r = args
        r0 = jnp.clip(r - kr // 2, 0, rows - kr)
        k_band = lax.dynamic_slice_in_dim(kg, r0, kr, axis=1)
        v_band = lax.dynamic_slice_in_dim(vg, r0, kr, axis=1)
        dr_idx = r0 + jnp.arange(kr) - r + NA_ROWS_MAX - 1
        bias = jnp.take(rpb_c, dr_idx, axis=1)
        bias = jnp.where(col_ok[None, None], bias, NEG_INF).transpose(0, 2, 1, 3)
        s_win = jnp.einsum('bqhd,biwhd->bhqiw', q_r, k_band).astype(F32) * scale + bias[None]
        s_ctx = jnp.einsum('bqhd,bchd->bhqc', q_r, kc).astype(F32) * scale
        s = jnp.concatenate([s_win.reshape(bsz, h, GRID_W, kr * GRID_W), s_ctx], axis=-1)
        pr = jax.nn.softmax(s, axis=-1).astype(v.dtype)
        p_win = pr[..., :kr * GRID_W].reshape(bsz, h, GRID_W, kr, GRID_W)
        p_ctx = pr[..., kr * GRID_W:]
        return (jnp.einsum('bhqiw,biwhd->bqhd', p_win, v_band)
                + jnp.einsum('bhqc,bchd->bqhd', p_ctx, vc))

    out = lax.map(one_row, (jnp.arange(rows), jnp.moveaxis(qg, 1, 0)))
    return jnp.moveaxis(out, 0, 1).reshape(bsz, seqlen, h * hd)


def _fb_ctx_attention(q, k, v):
    bsz, clen, h, hd = q.shape
    s = jnp.einsum('bqhd,bkhd->bhqk', q, k).astype(F32) * hd ** -0.5
    pr = jax.nn.softmax(s, axis=-1).astype(v.dtype)
    return jnp.einsum('bhqk,bkhd->bqhd', pr, v).reshape(bsz, clen, h * hd)


def _fb_na(pm, q_norm_w, k_norm_w, rpb, clen, need_ctx):
    pm = pm.astype(F32)
    bsz, t, _ = pm.shape
    seqlen = t - clen
    q = pm[:, :, COL_Q:COL_Q + D_NA]
    k = pm[:, :, COL_K:COL_K + D_NA]
    v = pm[:, :, COL_V:COL_V + D_NA]
    tt = jnp.arange(seqlen)
    pos_r = (tt // GRID_W).astype(F32)
    pos_c = (tt % GRID_W).astype(F32)
    hs = (NA_HEADS, NA_HEAD_DIM)
    k_c = _fb_rms_norm(k[:, :clen].reshape(bsz, clen, *hs), k_norm_w)
    v_c = v[:, :clen].reshape(bsz, clen, *hs)
    q_l = _fb_axial_rope(_fb_rms_norm(q[:, clen:].reshape(bsz, seqlen, *hs), q_norm_w), pos_r, pos_c)
    k_l = _fb_axial_rope(_fb_rms_norm(k[:, clen:].reshape(bsz, seqlen, *hs), k_norm_w), pos_r, pos_c)
    v_l = v[:, clen:].reshape(bsz, seqlen, *hs)
    y_l = _fb_na_latent(q_l, k_l, v_l, k_c, v_c, rpb)
    if need_ctx:
        q_c = _fb_rms_norm(q[:, :clen].reshape(bsz, clen, *hs), q_norm_w)
        y_c = _fb_ctx_attention(q_c, k_c, v_c)
    else:
        y_c = jnp.zeros((bsz, clen, D_NA), F32)
    return jnp.concatenate([y_c, y_l], axis=1)


def _fb_swiglu(h, w1, w3, w2):
    return (jax.nn.silu(h @ w1) * (h @ w3)) @ w2


def _fb_moe(h, w_router, w1, w3, w2):
    blk = 256
    n_tok, d = h.shape
    n_assign = n_tok * TOP_K
    logits = jnp.dot(h, w_router, precision=HIGHEST).astype(F32)
    top_v, top_i = lax.top_k(logits, TOP_K)
    gates = jax.nn.softmax(top_v, axis=-1)
    flat_e = top_i.reshape(-1)
    order = jnp.argsort(flat_e)
    sorted_e = flat_e[order]
    tok = order // TOP_K
    counts = jnp.bincount(flat_e, length=N_EXPERTS)
    padded = (counts + blk - 1) // blk * blk
    pad_end = jnp.cumsum(padded)
    pad_start = pad_end - padded
    start = jnp.cumsum(counts) - counts
    dest = pad_start[sorted_e] + jnp.arange(n_assign) - start[sorted_e]
    n_blocks = -(-n_assign // blk) + N_EXPERTS
    rows = jnp.zeros((n_blocks * blk, d), h.dtype).at[dest].set(h[tok])
    block_e = jnp.minimum(jnp.searchsorted(pad_end, jnp.arange(n_blocks) * blk, side='right'),
                          N_EXPERTS - 1)

    def run_block(args):
        xb, e = args
        return _fb_swiglu(xb, w1[e], w3[e], w2[e])

    y_rows = lax.map(run_block, (rows.reshape(n_blocks, blk, d), block_e)).reshape(-1, d)
    contrib = y_rows[dest] * gates.reshape(-1)[order][:, None].astype(h.dtype)
    return jnp.zeros_like(h).at[tok].add(contrib)


def _tile_mod(mod_l, bsz, tiles_per_batch):
    t = jnp.arange(bsz * tiles_per_batch)
    idx = jnp.where(t % tiles_per_batch == 0, bsz, t // tiles_per_batch)
    return jnp.transpose(mod_l[:, idx, :], (1, 0, 2))


def kernel(x, c, ctx, c_ctx, w_mod, b_mod, norm1_w, norm2_w, w_in, conv_w, conv_b, dt_bias, a_log, d_skip,
           ssd_norm_w, q_norm_w, k_norm_w, rpb, w_br_ssd, w_br_na, w_out, w_ff1, w_ff3, w_ff2, w_router,
           w_e1, w_e3, w_e2):
    bsz, seqlen, d = x.shape
    clen = ctx.shape[1]
    n_layers = w_mod.shape[0]
    assert clen == MOD_TILE and seqlen % MOD_TILE == 0 and bsz < 16
    t = clen + seqlen
    m = bsz * t
    tiles_per_batch = t // MOD_TILE

    cond = jnp.zeros((16, d), F32).at[:bsz].set(c).at[bsz].set(c_ctx)
    mod = _modulation(cond, w_mod, b_mod)
    modt = [_tile_mod(mod[i], bsz, tiles_per_batch) for i in range(n_layers)]

    h = jnp.concatenate([ctx, x], axis=1).reshape(m, d)
    a = _adaln(h, norm1_w[0], modt[0])
    for i in range(n_layers):
        last = i == n_layers - 1
        wi = w_in[i]
        w_main = jnp.concatenate([wi[:, :IN_SPLITS[3]], wi[:, IN_SPLITS[4]:]], axis=1).astype(BF16)
        w_dt = jnp.pad(wi[:, IN_SPLITS[3]:IN_SPLITS[4]], ((0, 0), (0, DT_PAD - 2 * SSD_HEADS))).astype(BF16)
        p_main = _matmul(a, w_main, BF16, 512, 2560, "inproj")
        p_dt = _matmul(a, w_dt, F32, 1024, DT_PAD, "inproj_dt")
        pm3 = p_main.reshape(bsz, t, MAIN_DIM)
        y_ssd = _ssd(pm3, p_dt.reshape(bsz, t, DT_PAD), conv_w[i], conv_b[i], dt_bias[i], a_log[i],
                     d_skip[i], ssd_norm_w[i], clen).reshape(m, d)
        y_na = _na(pm3, q_norm_w[i], k_norm_w[i], rpb[i], clen).reshape(m, d)
        fi = i // 2
        moe_layer = i % 2 == 1
        h, f = _merge(p_main, y_ssd, y_na, h, w_br_ssd[i].astype(BF16), w_br_na[i].astype(BF16),
                      w_out[i].astype(BF16), norm2_w[i], modt[i], row_tile_out=moe_layer)
        nxt = min(i + 1, n_layers - 1)
        if not moe_layer:
            h, a = _ffn(f, h, w_ff1[fi].astype(BF16), w_ff3[fi].astype(BF16), w_ff2[fi].astype(BF16),
                        norm1_w[nxt], modt[i], modt[nxt])
        else:
            h = _moe(f, h, w_router[fi], w_e1[fi], w_e3[fi], w_e2[fi], modt[i])
            if not last:
                a = _adaln(h, norm1_w[nxt], modt[nxt])
    return h.reshape(bsz, t, d)[:, clen:]
```

```python
import functools
import math

import jax
import jax.numpy as jnp
from jax import lax
from jax.experimental import pallas as pl
from jax.experimental.pallas import tpu as pltpu

D_MODEL = 1024
GRID_W = 64
SSD_HEAD_DIM = 64
SSD_HEADS = D_MODEL // SSD_HEAD_DIM
D_SSD = SSD_HEADS * SSD_HEAD_DIM
SSD_GROUPS = 2
SSD_HPG = SSD_HEADS // SSD_GROUPS
SSD_STATE = 128
SSD_CONV = 5
SSD_CHUNK = 128
XBC_DIM = D_SSD + 2 * SSD_GROUPS * SSD_STATE
NA_HEAD_DIM = 64
NA_HEADS = D_MODEL // NA_HEAD_DIM
D_NA = NA_HEADS * NA_HEAD_DIM
NA_ROWS_MAX = 8
NA_COLS = 16
ROPE_BASE = 10000.0
N_EXPERTS = 8
TOP_K = 2
EPS = 1e-6
NEG_INF = -1e30
IN_SPLITS = (D_MODEL, 2 * D_MODEL, 2 * D_MODEL + D_SSD, 2 * D_MODEL + D_SSD + XBC_DIM,
             2 * D_MODEL + D_SSD + XBC_DIM + 2 * SSD_HEADS,
             2 * D_MODEL + D_SSD + XBC_DIM + 2 * SSD_HEADS + D_NA,
             2 * D_MODEL + D_SSD + XBC_DIM + 2 * SSD_HEADS + 2 * D_NA)

F32 = jnp.float32
BF16 = jnp.bfloat16
HIGHEST = lax.Precision.HIGHEST

MOD_TILE = 256
LANE = 128
VMEM_LIMIT = 48 * 1024 * 1024

COL_GS, COL_GN, COL_Z, COL_XBC, COL_Q, COL_K, COL_V = 0, 1024, 2048, 3072, 4608, 5632, 6656
MAIN_DIM = 7680
DT_PAD = LANE


def _params(sem):
    return pltpu.CompilerParams(dimension_semantics=sem, vmem_limit_bytes=VMEM_LIMIT)


def _rms_mod(x, nw, shift, scale):
    ms = jnp.mean(x * x, axis=-1, keepdims=True)
    return (x * lax.rsqrt(ms + EPS) * nw) * (1.0 + scale) + shift


def _mod_kernel(cond_ref, w_ref, b_ref, o_ref):
    c = cond_ref[...]
    s = c * jax.nn.sigmoid(c)
    o_ref[...] = jnp.dot(s, w_ref[...], precision=HIGHEST, preferred_element_type=F32) + b_ref[...]


def _modulation(cond, w_mod, b_mod):
    n_layers, d, _ = w_mod.shape
    rows = cond.shape[0]
    b4 = b_mod.reshape(n_layers, 6, 1, d)
    return pl.pallas_call(
        _mod_kernel,
        grid=(n_layers, 6),
        in_specs=[pl.BlockSpec((rows, d), lambda l, k: (0, 0)),
                  pl.BlockSpec((None, d, d), lambda l, k: (l, 0, k)),
                  pl.BlockSpec((None, None, 1, d), lambda l, k: (l, k, 0, 0))],
        out_specs=pl.BlockSpec((None, None, rows, d), lambda l, k: (l, k, 0, 0)),
        out_shape=jax.ShapeDtypeStruct((n_layers, 6, rows, d), F32),
        compiler_params=_params(("arbitrary", "arbitrary")),
        name="modulation",
    )(cond, w_mod, b4)


def _adaln_kernel(h_ref, nw_ref, modt_ref, o_ref, *, nsub):
    nw = nw_ref[...]
    for s in range(nsub):
        rows = pl.ds(s * MOD_TILE, MOD_TILE)
        x = h_ref[rows, :]
        o_ref[rows, :] = _rms_mod(x, nw, modt_ref[s, 0:1, :], modt_ref[s, 1:2, :]).astype(o_ref.dtype)


def _adaln(h, nw, modt, tm=1024):
    m, d = h.shape
    nsub = tm // MOD_TILE
    return pl.pallas_call(
        functools.partial(_adaln_kernel, nsub=nsub),
        grid=(m // tm,),
        in_specs=[pl.BlockSpec((tm, d), lambda i: (i, 0)),
                  pl.BlockSpec((1, d), lambda i: (0, 0)),
                  pl.BlockSpec((nsub, 6, d), lambda i: (i, 0, 0))],
        out_specs=pl.BlockSpec((tm, d), lambda i: (i, 0)),
        out_shape=jax.ShapeDtypeStruct((m, d), BF16),
        compiler_params=_params(("arbitrary",)),
        name="adaln",
    )(h, nw.reshape(1, d), modt)


def _matmul_kernel(a_ref, w_ref, o_ref):
    o_ref[...] = jnp.dot(a_ref[...], w_ref[...], preferred_element_type=F32).astype(o_ref.dtype)


def _matmul(a, w, out_dtype, tm, tn, name):
    m, k = a.shape
    n = w.shape[1]
    return pl.pallas_call(
        _matmul_kernel,
        grid=(n // tn, m // tm),
        in_specs=[pl.BlockSpec((tm, k), lambda j, i: (i, 0)),
                  pl.BlockSpec((k, tn), lambda j, i: (0, j))],
        out_specs=pl.BlockSpec((tm, tn), lambda j, i: (i, j)),
        out_shape=jax.ShapeDtypeStruct((m, n), out_dtype),
        compiler_params=_params(("arbitrary", "arbitrary")),
        name=name,
    )(a, w)


def _merge_kernel(gs_ref, gn_ref, ys_ref, yn_ref, h_ref, wbs_ref, wbn_ref, wo_ref, nw_ref, modt_ref,
                  ho_ref, f_ref, *, nsub, row_tile_out):
    a = jnp.dot(ys_ref[...], wbs_ref[...], preferred_element_type=F32)
    b = jnp.dot(yn_ref[...], wbn_ref[...], preferred_element_type=F32)
    mixed = (jax.nn.sigmoid(gs_ref[...].astype(F32)) * a + jax.nn.sigmoid(gn_ref[...].astype(F32)) * b)
    m = jnp.dot(mixed.astype(BF16), wo_ref[...], preferred_element_type=F32)
    nw = nw_ref[...]
    for s in range(nsub):
        r0, r1 = s * MOD_TILE, (s + 1) * MOD_TILE
        h = h_ref[r0:r1, :] + modt_ref[s, 2:3, :] * m[r0:r1, :]
        ho_ref[r0:r1, :] = h
        f = _rms_mod(h, nw, modt_ref[s, 3:4, :], modt_ref[s, 4:5, :])
        if row_tile_out:
            _store_tokens(f_ref, r0, f)
        else:
            f_ref[r0:r1, :] = f.astype(f_ref.dtype)


def _merge(p_main, y_ssd, y_na, h, wbs, wbn, wo, nw2, modt, row_tile_out, tm=512):
    m, d = h.shape
    nsub = tm // MOD_TILE
    row = lambda i: (i, 0)
    const = lambda i: (0, 0)
    if row_tile_out:
        f_spec, f_shape = pl.BlockSpec((tm * ROW_SEG, LANE), row), jax.ShapeDtypeStruct((m * ROW_SEG, LANE), F32)
    else:
        f_spec, f_shape = pl.BlockSpec((tm, d), row), jax.ShapeDtypeStruct((m, d), BF16)
    return pl.pallas_call(
        functools.partial(_merge_kernel, nsub=nsub, row_tile_out=row_tile_out),
        grid=(m // tm,),
        in_specs=[pl.BlockSpec((tm, d), lambda i: (i, COL_GS // D_MODEL)),
                  pl.BlockSpec((tm, d), lambda i: (i, COL_GN // D_MODEL)),
                  pl.BlockSpec((tm, d), row), pl.BlockSpec((tm, d), row), pl.BlockSpec((tm, d), row),
                  pl.BlockSpec((d, d), const), pl.BlockSpec((d, d), const), pl.BlockSpec((d, d), const),
                  pl.BlockSpec((1, d), const),
                  pl.BlockSpec((nsub, 6, d), lambda i: (i, 0, 0))],
        out_specs=[pl.BlockSpec((tm, d), row), f_spec],
        out_shape=[jax.ShapeDtypeStruct((m, d), F32), f_shape],
        compiler_params=_params(("arbitrary",)),
        name="merge",
    )(p_main, p_main, y_ssd, y_na, h, wbs, wbn, wo, nw2.reshape(1, d), modt)


def _ffn_kernel(x_ref, h_ref, w1_ref, w3_ref, w2_ref, nw_ref, modt_ref, modn_ref, ho_ref, an_ref, acc_ref,
                *, nsub):
    j = pl.program_id(1)

    @pl.when(j == 0)
    def _():
        acc_ref[...] = jnp.zeros_like(acc_ref)

    x = x_ref[...]
    h1 = jnp.dot(x, w1_ref[...], preferred_element_type=F32)
    h3 = jnp.dot(x, w3_ref[...], preferred_element_type=F32)
    act = (h1 * jax.nn.sigmoid(h1) * h3).astype(BF16)
    acc_ref[...] += jnp.dot(act, w2_ref[...], preferred_element_type=F32)

    @pl.when(j == pl.num_programs(1) - 1)
    def _():
        nw = nw_ref[...]
        for s in range(nsub):
            r0, r1 = s * MOD_TILE, (s + 1) * MOD_TILE
            h = h_ref[r0:r1, :] + modt_ref[s, 5:6, :] * acc_ref[r0:r1, :]
            ho_ref[r0:r1, :] = h
            an_ref[r0:r1, :] = _rms_mod(h, nw, modn_ref[s, 0:1, :], modn_ref[s, 1:2, :]).astype(an_ref.dtype)


def _ffn(x, h, w1, w3, w2, nw_next, modt, modt_next, tm=512, tf=1408):
    m, d = h.shape
    ff = w1.shape[1]
    nsub = tm // MOD_TILE
    row = lambda i, j: (i, 0)
    return pl.pallas_call(
        functools.partial(_ffn_kernel, nsub=nsub),
        grid=(m // tm, ff // tf),
        in_specs=[pl.BlockSpec((tm, d), row), pl.BlockSpec((tm, d), row),
                  pl.BlockSpec((d, tf), lambda i, j: (0, j)), pl.BlockSpec((d, tf), lambda i, j: (0, j)),
                  pl.BlockSpec((tf, d), lambda i, j: (j, 0)),
                  pl.BlockSpec((1, d), lambda i, j: (0, 0)),
                  pl.BlockSpec((nsub, 6, d), lambda i, j: (i, 0, 0)),
                  pl.BlockSpec((nsub, 6, d), lambda i, j: (i, 0, 0))],
        out_specs=[pl.BlockSpec((tm, d), row), pl.BlockSpec((tm, d), row)],
        out_shape=[jax.ShapeDtypeStruct((m, d), F32), jax.ShapeDtypeStruct((m, d), BF16)],
        scratch_shapes=[pltpu.VMEM((tm, d), F32)],
        compiler_params=_params(("arbitrary", "arbitrary")),
        name="ffn",
    )(x, h, w1, w3, w2, nw_next.reshape(1, d), modt, modt_next)


NA_BAND = NA_ROWS_MAX * GRID_W
HEAD_PAIR = LANE // NA_HEAD_DIM
NA_UNROLL = 4
NA_PREP_UNROLL = 8


def _head_mean_matrix():
    ri = lax.broadcasted_iota(jnp.int32, (2 * LANE, LANE), 0) % LANE
    ci = lax.broadcasted_iota(jnp.int32, (2 * LANE, LANE), 1)
    return jnp.where(ri // NA_HEAD_DIM == ci // NA_HEAD_DIM, 1.0 / NA_HEAD_DIM, 0.0).astype(BF16)


def _head_rms(x, w, mean_mat):
    sq = x * x
    hi = sq.astype(BF16)
    lo = (sq - hi.astype(F32)).astype(BF16)
    ms = jnp.dot(jnp.concatenate([hi, lo], axis=1), mean_mat, preferred_element_type=F32)
    return x * lax.rsqrt(ms + EPS) * w


def _rope(x, cos, sin, first_half):
    partner = jnp.where(first_half, pltpu.roll(x, LANE - 16, axis=1), pltpu.roll(x, 16, axis=1))
    return x * cos + partner * sin


def _stack_heads(x, lane_lo):
    zero = jnp.zeros_like(x)
    return jnp.concatenate([jnp.where(lane_lo, x, zero), jnp.where(lane_lo, zero, x)], axis=0)


def _attend(qs, k_list, v_list, bias_list, lane_lo):
    nt = (((1,), (1,)), ((), ()))
    s_list = []
    for k, bias in zip(k_list, bias_list):
        s = lax.dot_general(qs, k, nt, preferred_element_type=F32)
        s_list.append(s if bias is None else s + bias)
    mx = s_list[0].max(axis=-1, keepdims=True)
    for s in s_list[1:]:
        mx = jnp.maximum(mx, s.max(axis=-1, keepdims=True))
    den = None
    acc = None
    for s, v in zip(s_list, v_list):
        p = jnp.exp(s - mx)
        d = p.sum(axis=-1, keepdims=True)
        o = jnp.dot(p.astype(BF16), v, preferred_element_type=F32)
        den = d if den is None else den + d
        acc = o if acc is None else acc + o
    acc = acc / den
    n = acc.shape[0] // 2
    return jnp.where(lane_lo, acc[:n], acc[n:])


def _na_kernel(q_ref, k_ref, v_ref, bias_ref, rowc_ref, rows_ref, colc_ref, cols_ref, qw_ref, kw_ref,
               o_ref, qn_ref, kn_ref, s_ref, m_ref, *, clen, n_rows):
    lane = lax.broadcasted_iota(jnp.int32, (1, LANE), 1)
    lane_lo = lane < NA_HEAD_DIM
    first_half = (lane % 32) < 16
    qw = qw_ref[...] * (NA_HEAD_DIM ** -0.5)
    kw = kw_ref[...]
    kr = NA_ROWS_MAX

    mean_mat = _head_mean_matrix()
    qn_ref[0:clen, :] = _head_rms(q_ref[0:clen, :].astype(F32), qw, mean_mat).astype(BF16)
    kn_ref[0:clen, :] = _head_rms(k_ref[0:clen, :].astype(F32), kw, mean_mat).astype(BF16)
    colc = colc_ref[...]
    cols = cols_ref[...]

    def prep(r, carry):
        rows = pl.ds(pl.multiple_of(clen + r * GRID_W, GRID_W), GRID_W)
        cos = rowc_ref[pl.ds(r, 1), :] + colc
        sin = rows_ref[pl.ds(r, 1), :] + cols
        q = _head_rms(q_ref[rows, :].astype(F32), qw, mean_mat)
        k = _head_rms(k_ref[rows, :].astype(F32), kw, mean_mat)
        qn_ref[rows, :] = _rope(q, cos, sin, first_half).astype(BF16)
        kn_ref[rows, :] = _rope(k, cos, sin, first_half).astype(BF16)
        return carry

    lax.fori_loop(0, n_rows, prep, 0, unroll=NA_PREP_UNROLL)

    kc = kn_ref[0:clen, :]
    vc = v_ref[0:clen, :]
    for blk in range(clen // GRID_W):
        rows = slice(blk * GRID_W, (blk + 1) * GRID_W)
        qs = _stack_heads(qn_ref[rows, :], lane_lo)
        o_ref[rows, :] = _attend(qs, [kc], [vc], [None], lane_lo).astype(o_ref.dtype)

    nt = (((1,), (1,)), ((), ()))
    n_blk = (NA_BAND + clen) // LANE

    def band_of(r):
        r0 = jnp.clip(r - kr // 2, 0, n_rows - kr)
        return r0, pl.ds(pl.multiple_of(clen + r0 * GRID_W, GRID_W), NA_BAND)

    def scores(r, slot):
        r0, band = band_of(r)
        rows = pl.ds(pl.multiple_of(clen + r * GRID_W, GRID_W), GRID_W)
        qs = _stack_heads(qn_ref[rows, :], lane_lo)
        keys = jnp.concatenate([kn_ref[band, :], kn_ref[0:clen, :]], axis=0)
        s = lax.dot_general(qs, keys, nt, preferred_element_type=F32)
        bias = bias_ref[r0 - r + (NA_ROWS_MAX - 1)]
        blocks = [s[:, j * LANE:(j + 1) * LANE] for j in range(n_blk)]
        for j in range(NA_BAND // LANE):
            blocks[j] = blocks[j] + bias[:, j * LANE:(j + 1) * LANE]
        mx = blocks[0]
        for j in range(n_blk):
            s_ref[slot, :, j * LANE:(j + 1) * LANE] = blocks[j]
            mx = jnp.maximum(mx, blocks[j])
        m_ref[slot] = jnp.broadcast_to(mx.max(axis=-1, keepdims=True), mx.shape)

    def output(r, slot):
        _, band = band_of(r)
        rows = pl.ds(pl.multiple_of(clen + r * GRID_W, GRID_W), GRID_W)
        mx = m_ref[slot]
        p = [jnp.exp(s_ref[slot, :, j * LANE:(j + 1) * LANE] - mx) for j in range(n_blk)]
        den = p[0]
        for pj in p[1:]:
            den = den + pj
        den = den.sum(axis=-1, keepdims=True)
        pb = jnp.concatenate([pj.astype(BF16) for pj in p], axis=1)
        vals = jnp.concatenate([v_ref[band, :], v_ref[0:clen, :]], axis=0)
        acc = jnp.dot(pb, vals, preferred_element_type=F32) / den
        o_ref[rows, :] = jnp.where(lane_lo, acc[:GRID_W], acc[GRID_W:]).astype(o_ref.dtype)

    for u in range(NA_UNROLL):
        scores(u, u)

    def body(j, carry):
        r = 2 * NA_UNROLL * j
        for half in range(2):
            for u in range(NA_UNROLL):
                output(r + half * NA_UNROLL + u, half * NA_UNROLL + u)
            for u in range(NA_UNROLL):
                nxt = jnp.minimum(r + (half + 1) * NA_UNROLL + u, n_rows - 1)
                scores(nxt, (1 - half) * NA_UNROLL + u)
        return carry

    lax.fori_loop(0, n_rows // (2 * NA_UNROLL), body, 0)


def _na_tables(rpb, n_rows):
    col = jnp.arange(GRID_W)
    col_start = jnp.clip(col - NA_COLS // 2, 0, GRID_W - NA_COLS)
    col_ok = (col[None, :] >= col_start[:, None]) & (col[None, :] < col_start[:, None] + NA_COLS)
    col_idx = jnp.clip(col[None, :] - col[:, None], -(NA_COLS - 1), NA_COLS - 1) + NA_COLS - 1
    rpb_c = jnp.where(col_ok[None, None], rpb.astype(F32)[:, :, col_idx], NEG_INF)
    dr = jnp.arange(NA_ROWS_MAX)[:, None] + jnp.arange(NA_ROWS_MAX)[None, :]
    tab = rpb_c[:, dr]
    tab = jnp.transpose(tab, (0, 1, 3, 2, 4)).reshape(NA_HEADS, NA_ROWS_MAX, GRID_W, NA_BAND)
    tab = tab.reshape(NA_HEADS // HEAD_PAIR, HEAD_PAIR, NA_ROWS_MAX, GRID_W, NA_BAND)
    tab = jnp.transpose(tab, (0, 2, 1, 3, 4)).reshape(NA_HEADS // HEAD_PAIR, NA_ROWS_MAX,
                                                       HEAD_PAIR * GRID_W, NA_BAND)
    quarter = NA_HEAD_DIM // 4
    inv = ROPE_BASE ** (-jnp.arange(quarter, dtype=F32) / quarter)

    def tables(pos, lo):
        ang = pos[:, None] * inv[None, :]
        z = jnp.zeros_like(ang)
        c, s = jnp.cos(ang), jnp.sin(ang)
        ch = jnp.concatenate([c, c, z, z] if lo else [z, z, c, c], axis=-1)
        sh = jnp.concatenate([-s, s, z, z] if lo else [z, z, -s, s], axis=-1)
        return jnp.tile(ch, (1, HEAD_PAIR)), jnp.tile(sh, (1, HEAD_PAIR))

    rowc, rows = tables(jnp.arange(n_rows).astype(F32), True)
    colc, cols = tables(jnp.arange(GRID_W).astype(F32), False)
    return tab, rowc, rows, colc, cols


def _na(pm3, q_norm_w, k_norm_w, rpb, clen):
    bsz, t, _ = pm3.shape
    n_rows = (t - clen) // GRID_W
    tab, rowc, rows, colc, cols = _na_tables(rpb, n_rows)
    qw = jnp.tile(q_norm_w.astype(F32), HEAD_PAIR).reshape(1, LANE)
    kw = jnp.tile(k_norm_w.astype(F32), HEAD_PAIR).reshape(1, LANE)
    n_pairs = NA_HEADS // HEAD_PAIR
    const = lambda p, b: (0, 0)
    tok = lambda c0: pl.BlockSpec((None, t, LANE), lambda p, b: (b, 0, c0 // LANE + p))
    return pl.pallas_call(
        functools.partial(_na_kernel, clen=clen, n_rows=n_rows),
        grid=(n_pairs, bsz),
        in_specs=[tok(COL_Q), tok(COL_K), tok(COL_V),
                  pl.BlockSpec((None, NA_ROWS_MAX, HEAD_PAIR * GRID_W, NA_BAND), lambda p, b: (p, 0, 0, 0)),
                  pl.BlockSpec((n_rows, LANE), const), pl.BlockSpec((n_rows, LANE), const),
                  pl.BlockSpec((GRID_W, LANE), const), pl.BlockSpec((GRID_W, LANE), const),
                  pl.BlockSpec((1, LANE), const), pl.BlockSpec((1, LANE), const)],
        out_specs=pl.BlockSpec((None, t, LANE), lambda p, b: (b, 0, p)),
        out_shape=jax.ShapeDtypeStruct((bsz, t, D_NA), BF16),
        scratch_shapes=[pltpu.VMEM((t, LANE), BF16), pltpu.VMEM((t, LANE), BF16),
                        pltpu.VMEM((2 * NA_UNROLL, HEAD_PAIR * GRID_W, NA_BAND + clen), F32),
                        pltpu.VMEM((2 * NA_UNROLL, HEAD_PAIR * GRID_W, LANE), F32)],
        compiler_params=_params(("arbitrary", "arbitrary")),
        name="natten",
    )(pm3, pm3, pm3, tab, rowc, rows, colc, cols, qw, kw)


HALO = 16
CONV_PAD = SSD_CONV // 2
BC_DIM = SSD_GROUPS * SSD_STATE
PAIRS_PER_GROUP = SSD_HPG // HEAD_PAIR


def _ssd_chunk_index(s, rev, cc, nc):
    if not rev:
        return s
    return jnp.where(s < cc, cc - 1 - s, nc - 1 - (s - cc))


def _ssd_core(main_ref, prev_ref, next_ref, dt_ref, cw_ref, cb_ref, dtb_ref, a_ref, exp_ref,
              h_ref, yoff_ref, y_ref, *, rev, cc, nc):
    q = SSD_CHUNK
    s = pl.program_id(1)
    cidx = _ssd_chunk_index(s, rev, cc, nc)

    @pl.when(s == 0)
    def _():
        h_ref[...] = jnp.zeros_like(h_ref)

    first_col = jnp.where((cidx == 0) | (cidx == cc), HALO, 0)
    end_col = jnp.where((cidx == cc - 1) | (cidx == nc - 1), HALO + q, q + 2 * HALO)
    ext = jnp.concatenate([prev_ref[...], main_ref[...], next_ref[...]], axis=0)
    ti = lax.broadcasted_iota(jnp.int32, (q, q + 2 * HALO), 0)
    ji = lax.broadcasted_iota(jnp.int32, (q, q + 2 * HALO), 1)
    in_segment = (ji >= first_col) & (ji < end_col)
    conv = cb_ref[...] + cw_ref[CONV_PAD:CONV_PAD + 1, :] * main_ref[...].astype(F32)
    for k in range(SSD_CONV):
        if k == CONV_PAD:
            continue
        shift = jnp.where((ji == ti + (HALO + k - CONV_PAD)) & in_segment, 1.0, 0.0).astype(BF16)
        conv = conv + cw_ref[k:k + 1, :] * jnp.dot(shift, ext, preferred_element_type=F32)
    xbc = conv * jax.nn.sigmoid(conv)
    xs = xbc[:, :D_SSD]

    hb = SSD_HEADS if rev else 0
    x = dt_ref[...] + dtb_ref[...]
    dt = jnp.maximum(x, 0.0) + jnp.log(1.0 + jnp.exp(-jnp.abs(x)))
    la = dt * a_ref[...]
    ri = lax.broadcasted_iota(jnp.int32, (q, q), 0)
    ci = lax.broadcasted_iota(jnp.int32, (q, q), 1)
    allowed = (ci >= ri) if rev else (ci <= ri)
    tri = jnp.where(allowed, 1.0, 0.0).astype(F32)
    a_cs = jnp.dot(tri, la, precision=HIGHEST, preferred_element_type=F32)
    last = 0 if rev else q - 1
    a_cs_t = a_cs.T
    e_all = jnp.exp(a_cs)
    src_t = a_cs_t - jnp.log(dt.T)
    wdt_t = jnp.exp(a_cs_t[:, last:last + 1] - src_t)
    neg_mask = jnp.where(allowed, 0.0, NEG_INF)
    blk = (last // 8) * 8
    e_last = jnp.exp(jnp.dot(a_cs[blk:blk + 8, :], exp_ref[...], precision=HIGHEST,
                             preferred_element_type=F32))[last - blk:last - blk + 1, hb * SSD_HEAD_DIM:
                                                          (hb + SSD_HEADS) * SSD_HEAD_DIM]

    lane_lo = lax.broadcasted_iota(jnp.int32, (1, LANE), 1) < SSD_HEAD_DIM
    nt = (((1,), (1,)), ((), ()))
    for g in range(SSD_GROUPS):
        b_g = xbc[:, D_SSD + g * SSD_STATE:D_SSD + (g + 1) * SSD_STATE]
        c_g = xbc[:, D_SSD + BC_DIM + g * SSD_STATE:D_SSD + BC_DIM + (g + 1) * SSD_STATE].astype(BF16)
        cols = slice(g * SSD_HPG * SSD_HEAD_DIM, (g + 1) * SSD_HPG * SSD_HEAD_DIM)
        yoff_ref[:, cols] = jnp.dot(c_g, h_ref[:, cols].astype(BF16), preferred_element_type=F32)
        cb = lax.dot_general(c_g, b_g.astype(BF16), nt, preferred_element_type=F32)
        b_t = b_g.T
        for pj in range(PAIRS_PER_GROUP):
            pair = g * PAIRS_PER_GROUP + pj
            lanes = slice(pair * LANE, (pair + 1) * LANE)
            lhs, ecol = [], []
            for hh in range(HEAD_PAIR):
                head = hb + pair * HEAD_PAIR + hh
                seg = (a_cs[:, head:head + 1] - src_t[head:head + 1, :]) + neg_mask
                lhs.append((cb * jnp.exp(seg)).astype(BF16))
                ecol.append(jnp.broadcast_to(e_all[:, head:head + 1], (q, LANE)))
            for hh in range(HEAD_PAIR):
                head = hb + pair * HEAD_PAIR + hh
                lhs.append((b_t * wdt_t[head:head + 1, :]).astype(BF16))
            res = jnp.dot(jnp.concatenate(lhs, axis=0), xs[:, lanes].astype(BF16), preferred_element_type=F32)
            y_ref[:, lanes] = (jnp.where(lane_lo, res[0:q], res[q:2 * q])
                               + yoff_ref[:, lanes] * jnp.where(lane_lo, ecol[0], ecol[1]))
            h_ref[:, lanes] = (h_ref[:, lanes] * e_last[:, lanes]
                               + jnp.where(lane_lo, res[2 * q:3 * q], res[3 * q:4 * q]))
    return xs


def _ssd_fwd_kernel(main_ref, prev_ref, next_ref, dt_ref, cw_ref, cb_ref, dtb_ref, a_ref, exp_ref, dsum_ref,
                    o_ref, h_ref, yoff_ref, y_ref, *, cc, nc):
    xs = _ssd_core(main_ref, prev_ref, next_ref, dt_ref, cw_ref, cb_ref, dtb_ref, a_ref, exp_ref,
                   h_ref, yoff_ref, y_ref, rev=False, cc=cc, nc=nc)
    o_ref[...] = y_ref[...] + xs * dsum_ref[...]


def _ssd_bwd_kernel(main_ref, prev_ref, next_ref, dt_ref, cw_ref, cb_ref, dtb_ref, a_ref, exp_ref, y1_ref, z_ref,
                    nw_ref, o_ref, h_ref, yoff_ref, y_ref, *, cc, nc):
    _ssd_core(main_ref, prev_ref, next_ref, dt_ref, cw_ref, cb_ref, dtb_ref, a_ref, exp_ref,
              h_ref, yoff_ref, y_ref, rev=True, cc=cc, nc=nc)
    z = z_ref[...].astype(F32)
    u = (y_ref[...] + y1_ref[...]) * (z * jax.nn.sigmoid(z))
    gw = D_SSD // SSD_GROUPS
    for g in range(SSD_GROUPS):
        ug = u[:, g * gw:(g + 1) * gw]
        ms = jnp.mean(ug * ug, axis=-1, keepdims=True)
        o_ref[:, g * gw:(g + 1) * gw] = (ug * lax.rsqrt(ms + EPS) * nw_ref[:, g * gw:(g + 1) * gw]).astype(o_ref.dtype)


def _ssd(pm3, pdt3, conv_w, conv_b, dt_bias, a_log, d_skip, ssd_norm_w, clen):
    bsz, t, _ = pm3.shape
    q = SSD_CHUNK
    nc, cc = t // q, clen // q
    nh = 2 * SSD_HEADS
    cw = jnp.pad(conv_w.astype(F32), ((0, 8 - SSD_CONV), (0, 0)))
    cb = conv_b.astype(F32).reshape(1, XBC_DIM)
    dtb = jnp.pad(dt_bias.astype(F32).reshape(1, nh), ((0, 0), (0, DT_PAD - nh)))
    a_neg = jnp.pad(-jnp.exp(a_log.astype(F32)).reshape(1, nh), ((0, 0), (0, DT_PAD - nh)))
    dsum = jnp.repeat((d_skip[0] + d_skip[1]).astype(F32), SSD_HEAD_DIM).reshape(1, D_SSD)
    expand = jnp.repeat(jnp.eye(DT_PAD, dtype=F32)[:, :nh], SSD_HEAD_DIM, axis=1)
    nw = ssd_norm_w.astype(F32).reshape(1, D_SSD)
    hpb = q // HALO
    n_halo = t // HALO

    def specs(rev):
        ch = lambda b, s: _ssd_chunk_index(s, rev, cc, nc)
        const = lambda b, s: (0, 0)
        return [
            pl.BlockSpec((None, q, XBC_DIM), lambda b, s: (b, ch(b, s), COL_XBC // XBC_DIM)),
            pl.BlockSpec((None, HALO, XBC_DIM),
                         lambda b, s: (b, jnp.maximum(ch(b, s) * hpb - 1, 0), COL_XBC // XBC_DIM)),
            pl.BlockSpec((None, HALO, XBC_DIM),
                         lambda b, s: (b, jnp.minimum((ch(b, s) + 1) * hpb, n_halo - 1), COL_XBC // XBC_DIM)),
            pl.BlockSpec((None, q, DT_PAD), lambda b, s: (b, ch(b, s), 0)),
            pl.BlockSpec((8, XBC_DIM), const), pl.BlockSpec((1, XBC_DIM), const),
            pl.BlockSpec((1, DT_PAD), const), pl.BlockSpec((1, DT_PAD), const),
            pl.BlockSpec((DT_PAD, nh * SSD_HEAD_DIM), const),
        ], (lambda b, s: (b, ch(b, s), 0)), const

    scratch = [pltpu.VMEM((SSD_STATE, D_SSD), F32), pltpu.VMEM((q, D_SSD), F32), pltpu.VMEM((q, D_SSD), F32)]
    in_f, tok_f, const_f = specs(False)
    y1 = pl.pallas_call(
        functools.partial(_ssd_fwd_kernel, cc=cc, nc=nc),
        grid=(bsz, nc),
        in_specs=in_f + [pl.BlockSpec((1, D_SSD), const_f)],
        out_specs=pl.BlockSpec((None, q, D_SSD), tok_f),
        out_shape=jax.ShapeDtypeStruct((bsz, t, D_SSD), F32),
        scratch_shapes=scratch,
        compiler_params=_params(("arbitrary", "arbitrary")),
        name="ssd_fwd",
    )(pm3, pm3, pm3, pdt3, cw, cb, dtb, a_neg, expand, dsum)
    in_b, tok_b, const_b = specs(True)
    return pl.pallas_call(
        functools.partial(_ssd_bwd_kernel, cc=cc, nc=nc),
        grid=(bsz, nc),
        in_specs=in_b + [pl.BlockSpec((None, q, D_SSD), tok_b),
                         pl.BlockSpec((None, q, D_SSD), lambda b, s: (b, _ssd_chunk_index(s, True, cc, nc),
                                                                        COL_Z // D_SSD)),
                         pl.BlockSpec((1, D_SSD), const_b)],
        out_specs=pl.BlockSpec((None, q, D_SSD), tok_b),
        out_shape=jax.ShapeDtypeStruct((bsz, t, D_SSD), BF16),
        scratch_shapes=scratch,
        compiler_params=_params(("arbitrary", "arbitrary")),
        name="ssd_bwd",
    )(pm3, pm3, pm3, pdt3, cw, cb, dtb, a_neg, expand, y1, pm3, nw)


MOE_TB = 512
ROUTE_TM = 512
INFO_W = 8
BIG_NEG = -3.0e38
ROW_SEG = D_MODEL // LANE


def _load_tokens(ref, t0, n):
    return jnp.concatenate([ref[pl.ds(t0 * ROW_SEG + c, n, stride=ROW_SEG), :] for c in range(ROW_SEG)], axis=1)


def _store_tokens(ref, t0, val):
    n = val.shape[0]
    for c in range(ROW_SEG):
        ref[pl.ds(t0 * ROW_SEG + c, n, stride=ROW_SEG), :] = val[:, c * LANE:(c + 1) * LANE]


def _route_kernel(f_ref, wr_ref, info_ref, cnt_ref, run_ref):
    tm = f_ref.shape[0] // ROW_SEG

    @pl.when(pl.program_id(0) == 0)
    def _():
        run_ref[...] = jnp.zeros_like(run_ref)

    logits = jnp.dot(_load_tokens(f_ref, 0, tm).astype(BF16), wr_ref[...], preferred_element_type=F32)
    lane = lax.broadcasted_iota(jnp.int32, (tm, LANE), 1).astype(F32)
    lg = jnp.where(lane < N_EXPERTS, logits, BIG_NEG)
    m1 = lg.max(axis=-1, keepdims=True)
    i1 = jnp.where(lg == m1, lane, float(LANE)).min(axis=-1, keepdims=True)
    lg2 = jnp.where(lane == i1, BIG_NEG, lg)
    m2 = lg2.max(axis=-1, keepdims=True)
    i2 = jnp.where(lg2 == m2, lane, float(LANE)).min(axis=-1, keepdims=True)
    ex = jnp.exp(m2 - m1)
    g0 = 1.0 / (1.0 + ex)
    g1 = ex / (1.0 + ex)
    sel0 = lane == i1
    sel1 = lane == i2
    onehot = jnp.where(sel0 | sel1, 1.0, 0.0)
    ri = lax.broadcasted_iota(jnp.int32, (tm, tm), 0)
    ci = lax.broadcasted_iota(jnp.int32, (tm, tm), 1)
    strict_lower = jnp.where(ci < ri, 1.0, 0.0).astype(BF16)
    before = jnp.dot(strict_lower, onehot.astype(BF16), preferred_element_type=F32) + run_ref[...]
    pos0 = jnp.where(sel0, before, 0.0).sum(axis=-1, keepdims=True)
    pos1 = jnp.where(sel1, before, 0.0).sum(axis=-1, keepdims=True)
    run_ref[...] = run_ref[...] + onehot.sum(axis=0, keepdims=True)
    cnt_ref[...] = run_ref[...]
    info = jnp.where(lane == 0, i1, jnp.where(lane == 1, i2, jnp.where(lane == 2, g0, jnp.where(
        lane == 3, g1, jnp.where(lane == 4, pos0, jnp.where(lane == 5, pos1, 0.0))))))
    info_ref[...] = info[:, :INFO_W]


def _route(f8, w_router):
    m, d = f8.shape[0] // ROW_SEG, D_MODEL
    wr = jnp.pad(w_router.astype(BF16), ((0, 0), (0, LANE - N_EXPERTS)))
    return pl.pallas_call(
        _route_kernel,
        grid=(m // ROUTE_TM,),
        in_specs=[pl.BlockSpec((ROUTE_TM * ROW_SEG, LANE), lambda i: (i, 0)),
                  pl.BlockSpec((d, LANE), lambda i: (0, 0))],
        out_specs=[pl.BlockSpec((ROUTE_TM, INFO_W), lambda i: (i, 0)), pl.BlockSpec((1, LANE), lambda i: (0, 0))],
        out_shape=[jax.ShapeDtypeStruct((m, INFO_W), F32), jax.ShapeDtypeStruct((1, LANE), F32)],
        scratch_shapes=[pltpu.VMEM((1, LANE), F32)],
        compiler_params=_params(("arbitrary",)),
        name="moe_route",
    )(f8, wr)


def _row_copy(src_ref, src_row, dst_ref, dst_row, sem):
    src = src_ref.at[pl.ds(pl.multiple_of(src_row * ROW_SEG, ROW_SEG), ROW_SEG)]
    dst = dst_ref.at[pl.ds(pl.multiple_of(dst_row * ROW_SEG, ROW_SEG), ROW_SEG)]
    return pltpu.make_async_copy(src, dst, sem)


def _dispatch_kernel(dest_hbm, f_ref, init_hbm, rows_hbm, dest_smem, idx_sem, row_sem):
    del init_hbm
    tm = f_ref.shape[0] // ROW_SEG
    base = pl.multiple_of(pl.program_id(0) * (TOP_K * tm), TOP_K * tm)
    idx_copy = pltpu.make_async_copy(dest_hbm.at[pl.ds(base, TOP_K * tm)], dest_smem, idx_sem)
    idx_copy.start()
    idx_copy.wait()

    def start(t, c):
        for k in range(TOP_K):
            _row_copy(f_ref, t, rows_hbm, dest_smem[TOP_K * t + k], row_sem).start()
        return c

    lax.fori_loop(0, tm, start, 0, unroll=8)

    def wait(t, c):
        for k in range(TOP_K):
            _row_copy(f_ref, 0, rows_hbm, 0, row_sem).wait()
        return c

    lax.fori_loop(0, tm, wait, 0, unroll=8)


def _dispatch(f8, dest_flat, n_rows):
    m = f8.shape[0] // ROW_SEG
    return pl.pallas_call(
        _dispatch_kernel,
        grid=(m // ROUTE_TM,),
        in_specs=[pl.BlockSpec(memory_space=pl.ANY), pl.BlockSpec((ROUTE_TM * ROW_SEG, LANE), lambda i: (i, 0)),
                  pl.BlockSpec(memory_space=pl.ANY)],
        out_specs=pl.BlockSpec(memory_space=pl.ANY),
        out_shape=jax.ShapeDtypeStruct((n_rows * ROW_SEG, LANE), f8.dtype),
        scratch_shapes=[pltpu.SMEM((TOP_K * ROUTE_TM,), jnp.int32), pltpu.SemaphoreType.DMA(()),
                        pltpu.SemaphoreType.DMA(())],
        input_output_aliases={2: 0},
        compiler_params=_params(("arbitrary",)),
        name="moe_dispatch",
    )(dest_flat, f8, jnp.zeros((n_rows * ROW_SEG, LANE), f8.dtype))


def _gffn_kernel(be_ref, nu_ref, x_ref, w1_ref, w3_ref, w2_ref, y_ref, acc_ref):
    i, j = pl.program_id(0), pl.program_id(1)

    @pl.when(i < nu_ref[0])
    def _():
        @pl.when(j == 0)
        def _():
            acc_ref[...] = jnp.zeros_like(acc_ref)

        x = _load_tokens(x_ref, 0, MOE_TB).astype(BF16)
        h1 = jnp.dot(x, w1_ref[...], preferred_element_type=F32)
        h3 = jnp.dot(x, w3_ref[...], preferred_element_type=F32)
        act = (h1 * jax.nn.sigmoid(h1) * h3).astype(BF16)
        acc_ref[...] += jnp.dot(act, w2_ref[...], preferred_element_type=F32)

        @pl.when(j == pl.num_programs(1) - 1)
        def _():
            _store_tokens(y_ref, 0, acc_ref[...])

    @pl.when((i >= nu_ref[0]) & (j == pl.num_programs(1) - 1))
    def _():
        y_ref[...] = jnp.zeros_like(y_ref)


def _gffn(rows, block_e, n_used, w1, w3, w2, tf=1792):
    r, d = rows.shape[0] // ROW_SEG, D_MODEL
    ff = w1.shape[2]
    nb = r // MOE_TB
    grid_spec = pltpu.PrefetchScalarGridSpec(
        num_scalar_prefetch=2,
        grid=(nb, ff // tf),
        in_specs=[pl.BlockSpec((MOE_TB * ROW_SEG, LANE), lambda i, j, be, nu: (i, 0)),
                  pl.BlockSpec((None, d, tf), lambda i, j, be, nu: (be[i], 0, j)),
                  pl.BlockSpec((None, d, tf), lambda i, j, be, nu: (be[i], 0, j)),
                  pl.BlockSpec((None, tf, d), lambda i, j, be, nu: (be[i], j, 0))],
        out_specs=pl.BlockSpec((MOE_TB * ROW_SEG, LANE), lambda i, j, be, nu: (i, 0)),
        scratch_shapes=[pltpu.VMEM((MOE_TB, d), F32)],
    )
    return pl.pallas_call(
        _gffn_kernel,
        grid_spec=grid_spec,
        out_shape=jax.ShapeDtypeStruct((r * ROW_SEG, LANE), F32),
        compiler_params=_params(("arbitrary", "arbitrary")),
        name="moe_ffn",
    )(block_e, n_used, rows, w1, w3, w2)


def _combine_kernel(dest_hbm, y_hbm, h_ref, info_ref, modt_ref, o_ref, dest_smem, buf_ref, idx_sem, row_sem,
                    *, nsub):
    tm = h_ref.shape[0]
    base = pl.multiple_of(pl.program_id(0) * (TOP_K * tm), TOP_K * tm)
    idx_copy = pltpu.make_async_copy(dest_hbm.at[pl.ds(base, TOP_K * tm)], dest_smem, idx_sem)
    idx_copy.start()
    idx_copy.wait()

    def start(t, c):
        for k in range(TOP_K):
            _row_copy(y_hbm, dest_smem[TOP_K * t + k], buf_ref.at[k], t, row_sem).start()
        return c

    lax.fori_loop(0, tm, start, 0, unroll=8)

    def wait(t, c):
        for k in range(TOP_K):
            _row_copy(y_hbm, 0, buf_ref.at[k], 0, row_sem).wait()
        return c

    lax.fori_loop(0, tm, wait, 0, unroll=8)
    for s in range(nsub):
        r0, r1 = s * MOD_TILE, (s + 1) * MOD_TILE
        info = info_ref[r0:r1, :]
        mix = (info[:, 2:3] * _load_tokens(buf_ref.at[0], r0, MOD_TILE)
               + info[:, 3:4] * _load_tokens(buf_ref.at[1], r0, MOD_TILE))
        o_ref[r0:r1, :] = h_ref[r0:r1, :] + modt_ref[s, 5:6, :] * mix


def _combine(y_rows, dest_flat, h, info, modt):
    m, d = h.shape
    nsub = ROUTE_TM // MOD_TILE
    return pl.pallas_call(
        functools.partial(_combine_kernel, nsub=nsub),
        grid=(m // ROUTE_TM,),
        in_specs=[pl.BlockSpec(memory_space=pl.ANY), pl.BlockSpec(memory_space=pl.ANY),
                  pl.BlockSpec((ROUTE_TM, d), lambda i: (i, 0)),
                  pl.BlockSpec((ROUTE_TM, INFO_W), lambda i: (i, 0)),
                  pl.BlockSpec((nsub, 6, d), lambda i: (i, 0, 0))],
        out_specs=pl.BlockSpec((ROUTE_TM, d), lambda i: (i, 0)),
        out_shape=jax.ShapeDtypeStruct((m, d), F32),
        scratch_shapes=[pltpu.SMEM((TOP_K * ROUTE_TM,), jnp.int32),
                        pltpu.VMEM((TOP_K, ROUTE_TM * ROW_SEG, LANE), F32),
                        pltpu.SemaphoreType.DMA(()), pltpu.SemaphoreType.DMA(())],
        compiler_params=_params(("arbitrary",)),
        name="moe_combine",
    )(dest_flat, y_rows, h, info, modt)


def _moe(f8, h, w_router, w1, w3, w2, modt):
    m, d = h.shape
    info, cnt = _route(f8, w_router)
    e = info[:, 0:TOP_K].astype(jnp.int32)
    pos = info[:, 4:4 + TOP_K].astype(jnp.int32)
    counts = cnt[0, :N_EXPERTS].astype(jnp.int32)
    padded = (counts + MOE_TB - 1) // MOE_TB * MOE_TB
    pad_end = jnp.cumsum(padded)
    pad_start = pad_end - padded
    dest_flat = (pad_start[e] + pos).reshape(-1)
    n_rows = m * TOP_K + N_EXPERTS * MOE_TB
    nb = n_rows // MOE_TB
    block_e = jnp.minimum(jnp.searchsorted(pad_end, jnp.arange(nb, dtype=jnp.int32) * MOE_TB, side='right'),
                          N_EXPERTS - 1).astype(jnp.int32)
    n_used = (pad_end[-1:] // MOE_TB).astype(jnp.int32)
    rows = _dispatch(f8, dest_flat, n_rows)
    y_rows = _gffn(rows, block_e, n_used, w1.astype(BF16), w3.astype(BF16), w2.astype(BF16))
    return _combine(y_rows, dest_flat, h, info, modt)


def _fb_rms_norm(x, w):
    xf = x.astype(F32)
    y = xf * lax.rsqrt(jnp.mean(xf * xf, axis=-1, keepdims=True) + EPS)
    return (y * w.astype(F32)).astype(x.dtype)


def _fb_flip(a):
    return jnp.flip(a, axis=1)


def _fb_depthwise_conv(x, w, b):
    y = lax.conv_general_dilated(x, w[:, None, :].astype(x.dtype), window_strides=(1,),
                                 padding=[(SSD_CONV // 2, SSD_CONV // 2)],
                                 dimension_numbers=('NWC', 'WIO', 'NWC'),
                                 feature_group_count=x.shape[-1])
    return y + b.astype(x.dtype)


def _fb_ssd_inputs(xbc_raw, dt_raw, conv_w, conv_b, dt_bias):
    bsz, seqlen, _ = xbc_raw.shape
    xbc = jax.nn.silu(_fb_depthwise_conv(xbc_raw, conv_w, conv_b))
    xs, bs, cs = jnp.split(xbc, (D_SSD, D_SSD + SSD_GROUPS * SSD_STATE), axis=-1)
    xs = xs.reshape(bsz, seqlen, SSD_GROUPS, SSD_HPG, SSD_HEAD_DIM)
    bs = bs.reshape(bsz, seqlen, SSD_GROUPS, SSD_STATE)
    cs = cs.reshape(bsz, seqlen, SSD_GROUPS, SSD_STATE)
    dt = jax.nn.softplus(dt_raw.astype(F32) + dt_bias.reshape(-1).astype(F32))
    dt = dt.reshape(bsz, seqlen, 2, SSD_GROUPS, SSD_HPG)
    return xs, bs, cs, dt[:, :, 0], dt[:, :, 1]


def _fb_ssd_chunked(x, dt, a, b_in, c_in, h0, return_y=True):
    bsz, seqlen, g, r, p = x.shape
    n = b_in.shape[-1]
    nc = seqlen // SSD_CHUNK
    dtf = dt.astype(F32)
    xdt = (x.astype(F32) * dtf[..., None]).reshape(bsz, nc, SSD_CHUNK, g, r, p)
    log_a = (dtf * a.astype(F32)).reshape(bsz, nc, SSD_CHUNK, g, r)
    bc = b_in.astype(F32).reshape(bsz, nc, SSD_CHUNK, g, n)
    cc = c_in.astype(F32).reshape(bsz, nc, SSD_CHUNK, g, n)
    a_cs = jnp.cumsum(log_a, axis=2)
    a_last = a_cs[:, :, -1]
    states = jnp.einsum('bcsgn,bcsgr,bcsgrp->bcgrpn', bc, jnp.exp(a_last[:, :, None] - a_cs), xdt)

    def step(h, inp):
        s_c, d_c = inp
        return h * d_c[..., None, None] + s_c, h

    h_fin, h_in = lax.scan(step, h0.astype(F32),
                           (jnp.moveaxis(states, 1, 0), jnp.moveaxis(jnp.exp(a_last), 1, 0)))
    if not return_y:
        return None, h_fin
    h_in = jnp.moveaxis(h_in, 0, 1)
    tri = jnp.tril(jnp.ones((SSD_CHUNK, SSD_CHUNK), bool))
    seg = a_cs[:, :, :, None] - a_cs[:, :, None, :]
    decay = jnp.exp(jnp.where(tri[:, :, None, None], seg, -jnp.inf))
    cb = jnp.einsum('bcqgn,bcsgn->bcqsg', cc, bc)
    y_diag = jnp.einsum('bcqsgr,bcsgrp->bcqgrp', cb[..., None] * decay, xdt)
    y_off = jnp.einsum('bcqgn,bcgrpn->bcqgrp', cc, h_in) * jnp.exp(a_cs)[..., None]
    y = (y_diag + y_off).reshape(bsz, seqlen, g, r, p)
    return y.astype(x.dtype), h_fin


def _fb_gated_group_rmsnorm(y, z, w):
    shp = y.shape
    u = y.astype(F32) * jax.nn.silu(z.astype(F32))
    u = u.reshape(*shp[:-1], SSD_GROUPS, shp[-1] // SSD_GROUPS)
    u = u * lax.rsqrt(jnp.mean(u * u, axis=-1, keepdims=True) + EPS)
    return (u.reshape(shp) * w.astype(F32)).astype(y.dtype)


def _fb_ssd(pm, pdt, conv_w, conv_b, dt_bias, a_log, d_skip, ssd_norm_w, clen, need_ctx):
    pm = pm.astype(F32)
    bsz = pm.shape[0]
    z = pm[:, :, COL_Z:COL_Z + D_SSD]
    xbc = pm[:, :, COL_XBC:COL_XBC + XBC_DIM]
    dt = pdt[:, :, :2 * SSD_HEADS]
    z_c, z_l = z[:, :clen], z[:, clen:]
    a_dir = -jnp.exp(a_log.astype(F32)).reshape(2, SSD_GROUPS, SSD_HPG)
    d_sum = (d_skip[0] + d_skip[1]).reshape(SSD_GROUPS, SSD_HPG)[..., None]
    xs_l, b_l, c_l, dtf_l, dtb_l = _fb_ssd_inputs(xbc[:, clen:], dt[:, clen:], conv_w, conv_b, dt_bias)
    xs_c, b_c, c_c, dtf_c, dtb_c = _fb_ssd_inputs(xbc[:, :clen], dt[:, :clen], conv_w, conv_b, dt_bias)
    h0 = jnp.zeros((bsz, SSD_GROUPS, SSD_HPG, SSD_HEAD_DIM, SSD_STATE), F32)
    yf_c, sf_c = _fb_ssd_chunked(xs_c, dtf_c, a_dir[0], b_c, c_c, h0, need_ctx)
    yb_c, sb_c = _fb_ssd_chunked(_fb_flip(xs_c), _fb_flip(dtb_c), a_dir[1], _fb_flip(b_c), _fb_flip(c_c), h0,
                                 need_ctx)
    yf_l, _ = _fb_ssd_chunked(xs_l, dtf_l, a_dir[0], b_l, c_l, sf_c)
    yb_l, _ = _fb_ssd_chunked(_fb_flip(xs_l), _fb_flip(dtb_l), a_dir[1], _fb_flip(b_l), _fb_flip(c_l), sb_c)
    seqlen = xs_l.shape[1]
    y_l = (yf_l + _fb_flip(yb_l) + xs_l * d_sum.astype(xs_l.dtype)).reshape(bsz, seqlen, D_SSD)
    y_l = _fb_gated_group_rmsnorm(y_l, z_l, ssd_norm_w)
    if need_ctx:
        y_c = (yf_c + _fb_flip(yb_c) + xs_c * d_sum.astype(xs_c.dtype)).reshape(bsz, clen, D_SSD)
        y_c = _fb_gated_group_rmsnorm(y_c, z_c, ssd_norm_w)
    else:
        y_c = jnp.zeros((bsz, clen, D_SSD), F32)
    return jnp.concatenate([y_c, y_l], axis=1)


def _fb_axial_rope(x, pos_r, pos_c):
    hd = x.shape[-1]
    quarter = hd // 4
    inv = ROPE_BASE ** (-jnp.arange(quarter, dtype=F32) / quarter)

    def rot(xh, pos):
        ang = pos[:, None] * inv[None, :]
        cos = jnp.cos(ang)[None, :, None, :].astype(x.dtype)
        sin = jnp.sin(ang)[None, :, None, :].astype(x.dtype)
        x1, x2 = jnp.split(xh, 2, axis=-1)
        return jnp.concatenate([x1 * cos - x2 * sin, x1 * sin + x2 * cos], axis=-1)

    return jnp.concatenate([rot(x[..., :hd // 2], pos_r), rot(x[..., hd // 2:], pos_c)], axis=-1)


def _fb_na_latent(q, k, v, kc, vc, rpb):
    bsz, seqlen, h, hd = q.shape
    rows = seqlen // GRID_W
    kr = min(NA_ROWS_MAX, rows)
    scale = hd ** -0.5
    qg = q.reshape(bsz, rows, GRID_W, h, hd)
    kg = k.reshape(bsz, rows, GRID_W, h, hd)
    vg = v.reshape(bsz, rows, GRID_W, h, hd)
    col = jnp.arange(GRID_W)
    col_start = jnp.clip(col - NA_COLS // 2, 0, GRID_W - NA_COLS)
    col_ok = (col[None, :] >= col_start[:, None]) & (col[None, :] < col_start[:, None] + NA_COLS)
    col_idx = jnp.clip(col[None, :] - col[:, None], -(NA_COLS - 1), NA_COLS - 1) + NA_COLS - 1
    rpb_c = rpb.astype(F32)[:, :, col_idx]

    def one_row(args):
        r, q_r = args
        r0 = jnp.clip(r - kr // 2, 0, rows - kr)
        k_band = lax.dynamic_slice_in_dim(kg, r0, kr, axis=1)
        v_band = lax.dynamic_slice_in_dim(vg, r0, kr, axis=1)
        dr_idx = r0 + jnp.arange(kr) - r + NA_ROWS_MAX - 1
        bias = jnp.take(rpb_c, dr_idx, axis=1)
        bias = jnp.where(col_ok[None, None], bias, NEG_INF).transpose(0, 2, 1, 3)
        s_win = jnp.einsum('bqhd,biwhd->bhqiw', q_r, k_band).astype(F32) * scale + bias[None]
        s_ctx = jnp.einsum('bqhd,bchd->bhqc', q_r, kc).astype(F32) * scale
        s = jnp.concatenate([s_win.reshape(bsz, h, GRID_W, kr * GRID_W), s_ctx], axis=-1)
        pr = jax.nn.softmax(s, axis=-1).astype(v.dtype)
        p_win = pr[..., :kr * GRID_W].reshape(bsz, h, GRID_W, kr, GRID_W)
        p_ctx = pr[..., kr * GRID_W:]
        return (jnp.einsum('bhqiw,biwhd->bqhd', p_win, v_band)
                + jnp.einsum('bhqc,bchd->bqhd', p_ctx, vc))

    out = lax.map(one_row, (jnp.arange(rows), jnp.moveaxis(qg, 1, 0)))
    return jnp.moveaxis(out, 0, 1).reshape(bsz, seqlen, h * hd)


def _fb_ctx_attention(q, k, v):
    bsz, clen, h, hd = q.shape
    s = jnp.einsum('bqhd,bkhd->bhqk', q, k).astype(F32) * hd ** -0.5
    pr = jax.nn.softmax(s, axis=-1).astype(v.dtype)
    return jnp.einsum('bhqk,bkhd->bqhd', pr, v).reshape(bsz, clen, h * hd)


def _fb_na(pm, q_norm_w, k_norm_w, rpb, clen, need_ctx):
    pm = pm.astype(F32)
    bsz, t, _ = pm.shape
    seqlen = t - clen
    q = pm[:, :, COL_Q:COL_Q + D_NA]
    k = pm[:, :, COL_K:COL_K + D_NA]
    v = pm[:, :, COL_V:COL_V + D_NA]
    tt = jnp.arange(seqlen)
    pos_r = (tt // GRID_W).astype(F32)
    pos_c = (tt % GRID_W).astype(F32)
    hs = (NA_HEADS, NA_HEAD_DIM)
    k_c = _fb_rms_norm(k[:, :clen].reshape(bsz, clen, *hs), k_norm_w)
    v_c = v[:, :clen].reshape(bsz, clen, *hs)
    q_l = _fb_axial_rope(_fb_rms_norm(q[:, clen:].reshape(bsz, seqlen, *hs), q_norm_w), pos_r, pos_c)
    k_l = _fb_axial_rope(_fb_rms_norm(k[:, clen:].reshape(bsz, seqlen, *hs), k_norm_w), pos_r, pos_c)
    v_l = v[:, clen:].reshape(bsz, seqlen, *hs)
    y_l = _fb_na_latent(q_l, k_l, v_l, k_c, v_c, rpb)
    if need_ctx:
        q_c = _fb_rms_norm(q[:, :clen].reshape(bsz, clen, *hs), q_norm_w)
        y_c = _fb_ctx_attention(q_c, k_c, v_c)
    else:
        y_c = jnp.zeros((bsz, clen, D_NA), F32)
    return jnp.concatenate([y_c, y_l], axis=1)


def _fb_swiglu(h, w1, w3, w2):
    return (jax.nn.silu(h @ w1) * (h @ w3)) @ w2


def _fb_moe(h, w_router, w1, w3, w2):
    blk = 256
    n_tok, d = h.shape
    n_assign = n_tok * TOP_K
    logits = jnp.dot(h, w_router, precision=HIGHEST).astype(F32)
    top_v, top_i = lax.top_k(logits, TOP_K)
    gates = jax.nn.softmax(top_v, axis=-1)
    flat_e = top_i.reshape(-1)
    order = jnp.argsort(flat_e)
    sorted_e = flat_e[order]
    tok = order // TOP_K
    counts = jnp.bincount(flat_e, length=N_EXPERTS)
    padded = (counts + blk - 1) // blk * blk
    pad_end = jnp.cumsum(padded)
    pad_start = pad_end - padded
    start = jnp.cumsum(counts) - counts
    dest = pad_start[sorted_e] + jnp.arange(n_assign) - start[sorted_e]
    n_blocks = -(-n_assign // blk) + N_EXPERTS
    rows = jnp.zeros((n_blocks * blk, d), h.dtype).at[dest].set(h[tok])
    block_e = jnp.minimum(jnp.searchsorted(pad_end, jnp.arange(n_blocks) * blk, side='right'),
                          N_EXPERTS - 1)

    def run_block(args):
        xb, e = args
        return _fb_swiglu(xb, w1[e], w3[e], w2[e])

    y_rows = lax.map(run_block, (rows.reshape(n_blocks, blk, d), block_e)).reshape(-1, d)
    contrib = y_rows[dest] * gates.reshape(-1)[order][:, None].astype(h.dtype)
    return jnp.zeros_like(h).at[tok].add(contrib)


def _tile_mod(mod_l, bsz, tiles_per_batch):
    t = jnp.arange(bsz * tiles_per_batch)
    idx = jnp.where(t % tiles_per_batch == 0, bsz, t // tiles_per_batch)
    return jnp.transpose(mod_l[:, idx, :], (1, 0, 2))


def kernel(x, c, ctx, c_ctx, w_mod, b_mod, norm1_w, norm2_w, w_in, conv_w, conv_b, dt_bias, a_log, d_skip,
           ssd_norm_w, q_norm_w, k_norm_w, rpb, w_br_ssd, w_br_na, w_out, w_ff1, w_ff3, w_ff2, w_router,
           w_e1, w_e3, w_e2):
    bsz, seqlen, d = x.shape
    clen = ctx.shape[1]
    n_layers = w_mod.shape[0]
    assert clen == MOD_TILE and seqlen % MOD_TILE == 0 and bsz < 16
    t = clen + seqlen
    m = bsz * t
    tiles_per_batch = t // MOD_TILE

    cond = jnp.zeros((16, d), F32).at[:bsz].set(c).at[bsz].set(c_ctx)
    mod = _modulation(cond, w_mod, b_mod)
    modt = [_tile_mod(mod[i], bsz, tiles_per_batch) for i in range(n_layers)]

    h = jnp.concatenate([ctx, x], axis=1).reshape(m, d)
    a = _adaln(h, norm1_w[0], modt[0])
    for i in range(n_layers):
        last = i == n_layers - 1
        wi = w_in[i]
        w_main = jnp.concatenate([wi[:, :IN_SPLITS[3]], wi[:, IN_SPLITS[4]:]], axis=1).astype(BF16)
        w_dt = jnp.pad(wi[:, IN_SPLITS[3]:IN_SPLITS[4]], ((0, 0), (0, DT_PAD - 2 * SSD_HEADS))).astype(BF16)
        p_main = _matmul(a, w_main, BF16, 512, 2560, "inproj")
        p_dt = _matmul(a, w_dt, F32, 1024, DT_PAD, "inproj_dt")
        pm3 = p_main.reshape(bsz, t, MAIN_DIM)
        y_ssd = _ssd(pm3, p_dt.reshape(bsz, t, DT_PAD), conv_w[i], conv_b[i], dt_bias[i], a_log[i],
                     d_skip[i], ssd_norm_w[i], clen).reshape(m, d)
        y_na = _na(pm3, q_norm_w[i], k_norm_w[i], rpb[i], clen).reshape(m, d)
        fi = i // 2
        moe_layer = i % 2 == 1
        h, f = _merge(p_main, y_ssd, y_na, h, w_br_ssd[i].astype(BF16), w_br_na[i].astype(BF16),
                      w_out[i].astype(BF16), norm2_w[i], modt[i], row_tile_out=moe_layer)
        nxt = min(i + 1, n_layers - 1)
        if not moe_layer:
            h, a = _ffn(f, h, w_ff1[fi].astype(BF16), w_ff3[fi].astype(BF16), w_ff2[fi].astype(BF16),
                        norm1_w[nxt], modt[i], modt[nxt])
        else:
            h = _moe(f, h, w_router[fi], w_e1[fi], w_e3[fi], w_e2[fi], modt[i])
            if not last:
                a = _adaln(h, norm1_w[nxt], modt[nxt])
    return h.reshape(bsz, t, d)[:, clen:]
```

```python
import functools
import math

import jax
import jax.numpy as jnp
from jax import lax
from jax.experimental import pallas as pl
from jax.experimental.pallas import tpu as pltpu

D_MODEL = 1024
GRID_W = 64
SSD_HEAD_DIM = 64
SSD_HEADS = D_MODEL // SSD_HEAD_DIM
D_SSD = SSD_HEADS * SSD_HEAD_DIM
SSD_GROUPS = 2
SSD_HPG = SSD_HEADS // SSD_GROUPS
SSD_STATE = 128
SSD_CONV = 5
SSD_CHUNK = 128
XBC_DIM = D_SSD + 2 * SSD_GROUPS * SSD_STATE
NA_HEAD_DIM = 64
NA_HEADS = D_MODEL // NA_HEAD_DIM
D_NA = NA_HEADS * NA_HEAD_DIM
NA_ROWS_MAX = 8
NA_COLS = 16
ROPE_BASE = 10000.0
N_EXPERTS = 8
TOP_K = 2
EPS = 1e-6
NEG_INF = -1e30
IN_SPLITS = (D_MODEL, 2 * D_MODEL, 2 * D_MODEL + D_SSD, 2 * D_MODEL + D_SSD + XBC_DIM,
             2 * D_MODEL + D_SSD + XBC_DIM + 2 * SSD_HEADS,
             2 * D_MODEL + D_SSD + XBC_DIM + 2 * SSD_HEADS + D_NA,
             2 * D_MODEL + D_SSD + XBC_DIM + 2 * SSD_HEADS + 2 * D_NA)

F32 = jnp.float32
BF16 = jnp.bfloat16
HIGHEST = lax.Precision.HIGHEST

MOD_TILE = 256
LANE = 128
VMEM_LIMIT = 48 * 1024 * 1024

COL_GS, COL_GN, COL_Z, COL_XBC, COL_Q, COL_K, COL_V = 0, 1024, 2048, 3072, 4608, 5632, 6656
MAIN_DIM = 7680
DT_PAD = LANE


def _params(sem):
    return pltpu.CompilerParams(dimension_semantics=sem, vmem_limit_bytes=VMEM_LIMIT)


def _rms_mod(x, nw, shift, scale):
    ms = jnp.mean(x * x, axis=-1, keepdims=True)
    return (x * lax.rsqrt(ms + EPS) * nw) * (1.0 + scale) + shift


def _mod_kernel(cond_ref, w_ref, b_ref, o_ref):
    c = cond_ref[...]
    s = c * jax.nn.sigmoid(c)
    o_ref[...] = jnp.dot(s, w_ref[...], precision=HIGHEST, preferred_element_type=F32) + b_ref[...]


def _modulation(cond, w_mod, b_mod):
    n_layers, d, _ = w_mod.shape
    rows = cond.shape[0]
    b4 = b_mod.reshape(n_layers, 6, 1, d)
    return pl.pallas_call(
        _mod_kernel,
        grid=(n_layers, 6),
        in_specs=[pl.BlockSpec((rows, d), lambda l, k: (0, 0)),
                  pl.BlockSpec((None, d, d), lambda l, k: (l, 0, k)),
                  pl.BlockSpec((None, None, 1, d), lambda l, k: (l, k, 0, 0))],
        out_specs=pl.BlockSpec((None, None, rows, d), lambda l, k: (l, k, 0, 0)),
        out_shape=jax.ShapeDtypeStruct((n_layers, 6, rows, d), F32),
        compiler_params=_params(("arbitrary", "arbitrary")),
        name="modulation",
    )(cond, w_mod, b4)


def _adaln_kernel(h_ref, nw_ref, modt_ref, o_ref, *, nsub):
    nw = nw_ref[...]
    for s in range(nsub):
        rows = pl.ds(s * MOD_TILE, MOD_TILE)
        x = h_ref[rows, :]
        o_ref[rows, :] = _rms_mod(x, nw, modt_ref[s, 0:1, :], modt_ref[s, 1:2, :]).astype(o_ref.dtype)


def _adaln(h, nw, modt, tm=1024):
    m, d = h.shape
    nsub = tm // MOD_TILE
    return pl.pallas_call(
        functools.partial(_adaln_kernel, nsub=nsub),
        grid=(m // tm,),
        in_specs=[pl.BlockSpec((tm, d), lambda i: (i, 0)),
                  pl.BlockSpec((1, d), lambda i: (0, 0)),
                  pl.BlockSpec((nsub, 6, d), lambda i: (i, 0, 0))],
        out_specs=pl.BlockSpec((tm, d), lambda i: (i, 0)),
        out_shape=jax.ShapeDtypeStruct((m, d), BF16),
        compiler_params=_params(("arbitrary",)),
        name="adaln",
    )(h, nw.reshape(1, d), modt)


def _matmul_kernel(a_ref, w_ref, o_ref):
    o_ref[...] = jnp.dot(a_ref[...], w_ref[...], preferred_element_type=F32).astype(o_ref.dtype)


def _matmul(a, w, out_dtype, tm, tn, name):
    m, k = a.shape
    n = w.shape[1]
    return pl.pallas_call(
        _matmul_kernel,
        grid=(n // tn, m // tm),
        in_specs=[pl.BlockSpec((tm, k), lambda j, i: (i, 0)),
                  pl.BlockSpec((k, tn), lambda j, i: (0, j))],
        out_specs=pl.BlockSpec((tm, tn), lambda j, i: (i, j)),
        out_shape=jax.ShapeDtypeStruct((m, n), out_dtype),
        compiler_params=_params(("arbitrary", "arbitrary")),
        name=name,
    )(a, w)


def _merge_kernel(gs_ref, gn_ref, ys_ref, yn_ref, h_ref, wbs_ref, wbn_ref, wo_ref, nw_ref, modt_ref,
                  ho_ref, f_ref, *, nsub, row_tile_out):
    a = jnp.dot(ys_ref[...], wbs_ref[...], preferred_element_type=F32)
    b = jnp.dot(yn_ref[...], wbn_ref[...], preferred_element_type=F32)
    mixed = (jax.nn.sigmoid(gs_ref[...].astype(F32)) * a + jax.nn.sigmoid(gn_ref[...].astype(F32)) * b)
    m = jnp.dot(mixed.astype(BF16), wo_ref[...], preferred_element_type=F32)
    nw = nw_ref[...]
    for s in range(nsub):
        r0, r1 = s * MOD_TILE, (s + 1) * MOD_TILE
        h = h_ref[r0:r1, :] + modt_ref[s, 2:3, :] * m[r0:r1, :]
        ho_ref[r0:r1, :] = h
        f = _rms_mod(h, nw, modt_ref[s, 3:4, :], modt_ref[s, 4:5, :])
        if row_tile_out:
            _store_tokens(f_ref, r0, f)
        else:
            f_ref[r0:r1, :] = f.astype(f_ref.dtype)


def _merge(p_main, y_ssd, y_na, h, wbs, wbn, wo, nw2, modt, row_tile_out, tm=512):
    m, d = h.shape
    nsub = tm // MOD_TILE
    row = lambda i: (i, 0)
    const = lambda i: (0, 0)
    if row_tile_out:
        f_spec, f_shape = pl.BlockSpec((tm * ROW_SEG, LANE), row), jax.ShapeDtypeStruct((m * ROW_SEG, LANE), F32)
    else:
        f_spec, f_shape = pl.BlockSpec((tm, d), row), jax.ShapeDtypeStruct((m, d), BF16)
    return pl.pallas_call(
        functools.partial(_merge_kernel, nsub=nsub, row_tile_out=row_tile_out),
        grid=(m // tm,),
        in_specs=[pl.BlockSpec((tm, d), lambda i: (i, COL_GS // D_MODEL)),
                  pl.BlockSpec((tm, d), lambda i: (i, COL_GN // D_MODEL)),
                  pl.BlockSpec((tm, d), row), pl.BlockSpec((tm, d), row), pl.BlockSpec((tm, d), row),
                  pl.BlockSpec((d, d), const), pl.BlockSpec((d, d), const), pl.BlockSpec((d, d), const),
                  pl.BlockSpec((1, d), const),
                  pl.BlockSpec((nsub, 6, d), lambda i: (i, 0, 0))],
        out_specs=[pl.BlockSpec((tm, d), row), f_spec],
        out_shape=[jax.ShapeDtypeStruct((m, d), F32), f_shape],
        compiler_params=_params(("arbitrary",)),
        name="merge",
    )(p_main, p_main, y_ssd, y_na, h, wbs, wbn, wo, nw2.reshape(1, d), modt)


def _ffn_kernel(x_ref, h_ref, w1_ref, w3_ref, w2_ref, nw_ref, modt_ref, modn_ref, ho_ref, an_ref, acc_ref,
                *, nsub):
    j = pl.program_id(1)

    @pl.when(j == 0)
    def _():
        acc_ref[...] = jnp.zeros_like(acc_ref)

    x = x_ref[...]
    h1 = jnp.dot(x, w1_ref[...], preferred_element_type=F32)
    h3 = jnp.dot(x, w3_ref[...], preferred_element_type=F32)
    act = (h1 * jax.nn.sigmoid(h1) * h3).astype(BF16)
    acc_ref[...] += jnp.dot(act, w2_ref[...], preferred_element_type=F32)

    @pl.when(j == pl.num_programs(1) - 1)
    def _():
        nw = nw_ref[...]
        for s in range(nsub):
            r0, r1 = s * MOD_TILE, (s + 1) * MOD_TILE
            h = h_ref[r0:r1, :] + modt_ref[s, 5:6, :] * acc_ref[r0:r1, :]
            ho_ref[r0:r1, :] = h
            an_ref[r0:r1, :] = _rms_mod(h, nw, modn_ref[s, 0:1, :], modn_ref[s, 1:2, :]).astype(an_ref.dtype)


def _ffn(x, h, w1, w3, w2, nw_next, modt, modt_next, tm=512, tf=1408):
    m, d = h.shape
    ff = w1.shape[1]
    nsub = tm // MOD_TILE
    row = lambda i, j: (i, 0)
    return pl.pallas_call(
        functools.partial(_ffn_kernel, nsub=nsub),
        grid=(m // tm, ff // tf),
        in_specs=[pl.BlockSpec((tm, d), row), pl.BlockSpec((tm, d), row),
                  pl.BlockSpec((d, tf), lambda i, j: (0, j)), pl.BlockSpec((d, tf), lambda i, j: (0, j)),
                  pl.BlockSpec((tf, d), lambda i, j: (j, 0)),
                  pl.BlockSpec((1, d), lambda i, j: (0, 0)),
                  pl.BlockSpec((nsub, 6, d), lambda i, j: (i, 0, 0)),
                  pl.BlockSpec((nsub, 6, d), lambda i, j: (i, 0, 0))],
        out_specs=[pl.BlockSpec((tm, d), row), pl.BlockSpec((tm, d), row)],
        out_shape=[jax.ShapeDtypeStruct((m, d), F32), jax.ShapeDtypeStruct((m, d), BF16)],
        scratch_shapes=[pltpu.VMEM((tm, d), F32)],
        compiler_params=_params(("arbitrary", "arbitrary")),
        name="ffn",
    )(x, h, w1, w3, w2, nw_next.reshape(1, d), modt, modt_next)


NA_BAND = NA_ROWS_MAX * GRID_W
HEAD_PAIR = LANE // NA_HEAD_DIM
NA_UNROLL = 4
NA_PREP_UNROLL = 8


def _head_mean_matrix():
    ri = lax.broadcasted_iota(jnp.int32, (2 * LANE, LANE), 0) % LANE
    ci = lax.broadcasted_iota(jnp.int32, (2 * LANE, LANE), 1)
    return jnp.where(ri // NA_HEAD_DIM == ci // NA_HEAD_DIM, 1.0 / NA_HEAD_DIM, 0.0).astype(BF16)


def _head_rms(x, w, mean_mat):
    sq = x * x
    hi = sq.astype(BF16)
    lo = (sq - hi.astype(F32)).astype(BF16)
    ms = jnp.dot(jnp.concatenate([hi, lo], axis=1), mean_mat, preferred_element_type=F32)
    return x * lax.rsqrt(ms + EPS) * w


def _rope(x, cos, sin, first_half):
    partner = jnp.where(first_half, pltpu.roll(x, LANE - 16, axis=1), pltpu.roll(x, 16, axis=1))
    return x * cos + partner * sin


def _stack_heads(x, lane_lo):
    zero = jnp.zeros_like(x)
    return jnp.concatenate([jnp.where(lane_lo, x, zero), jnp.where(lane_lo, zero, x)], axis=0)


def _attend(qs, k_list, v_list, bias_list, lane_lo):
    nt = (((1,), (1,)), ((), ()))
    s_list = []
    for k, bias in zip(k_list, bias_list):
        s = lax.dot_general(qs, k, nt, preferred_element_type=F32)
        s_list.append(s if bias is None else s + bias)
    mx = s_list[0].max(axis=-1, keepdims=True)
    for s in s_list[1:]:
        mx = jnp.maximum(mx, s.max(axis=-1, keepdims=True))
    den = None
    acc = None
    for s, v in zip(s_list, v_list):
        p = jnp.exp(s - mx)
        d = p.sum(axis=-1, keepdims=True)
        o = jnp.dot(p.astype(BF16), v, preferred_element_type=F32)
        den = d if den is None else den + d
        acc = o if acc is None else acc + o
    acc = acc / den
    n = acc.shape[0] // 2
    return jnp.where(lane_lo, acc[:n], acc[n:])


def _na_kernel(q_ref, k_ref, v_ref, bias_ref, rowc_ref, rows_ref, colc_ref, cols_ref, qw_ref, kw_ref,
               o_ref, qn_ref, kn_ref, s_ref, m_ref, *, clen, n_rows):
    lane = lax.broadcasted_iota(jnp.int32, (1, LANE), 1)
    lane_lo = lane < NA_HEAD_DIM
    first_half = (lane % 32) < 16
    qw = qw_ref[...] * (NA_HEAD_DIM ** -0.5)
    kw = kw_ref[...]
    kr = NA_ROWS_MAX

    mean_mat = _head_mean_matrix()
    qn_ref[0:clen, :] = _head_rms(q_ref[0:clen, :].astype(F32), qw, mean_mat).astype(BF16)
    kn_ref[0:clen, :] = _head_rms(k_ref[0:clen, :].astype(F32), kw, mean_mat).astype(BF16)
    colc = colc_ref[...]
    cols = cols_ref[...]

    def prep(r, carry):
        rows = pl.ds(pl.multiple_of(clen + r * GRID_W, GRID_W), GRID_W)
        cos = rowc_ref[pl.ds(r, 1), :] + colc
        sin = rows_ref[pl.ds(r, 1), :] + cols
        q = _head_rms(q_ref[rows, :].astype(F32), qw, mean_mat)
        k = _head_rms(k_ref[rows, :].astype(F32), kw, mean_mat)
        qn_ref[rows, :] = _rope(q, cos, sin, first_half).astype(BF16)
        kn_ref[rows, :] = _rope(k, cos, sin, first_half).astype(BF16)
        return carry

    lax.fori_loop(0, n_rows, prep, 0, unroll=NA_PREP_UNROLL)

    kc = kn_ref[0:clen, :]
    vc = v_ref[0:clen, :]
    for blk in range(clen // GRID_W):
        rows = slice(blk * GRID_W, (blk + 1) * GRID_W)
        qs = _stack_heads(qn_ref[rows, :], lane_lo)
        o_ref[rows, :] = _attend(qs, [kc], [vc], [None], lane_lo).astype(o_ref.dtype)

    nt = (((1,), (1,)), ((), ()))
    n_blk = (NA_BAND + clen) // LANE

    def band_of(r):
        r0 = jnp.clip(r - kr // 2, 0, n_rows - kr)
        return r0, pl.ds(pl.multiple_of(clen + r0 * GRID_W, GRID_W), NA_BAND)

    n_win = NA_BAND // LANE
    sq = HEAD_PAIR * GRID_W

    def scores(rows_r, slots):
        qs = [_stack_heads(qn_ref[pl.ds(pl.multiple_of(clen + r * GRID_W, GRID_W), GRID_W), :], lane_lo)
              for r in rows_r]
        s_ctx = lax.dot_general(jnp.concatenate(qs, axis=0), kn_ref[0:clen, :], nt, preferred_element_type=F32)
        for u, (r, slot) in enumerate(zip(rows_r, slots)):
            r0, band = band_of(r)
            s_win = lax.dot_general(qs[u], kn_ref[band, :], nt, preferred_element_type=F32)
            bias = bias_ref[r0 - r + (NA_ROWS_MAX - 1)]
            blocks = [s_win[:, j * LANE:(j + 1) * LANE] + bias[:, j * LANE:(j + 1) * LANE] for j in range(n_win)]
            blocks += [s_ctx[u * sq:(u + 1) * sq, j * LANE:(j + 1) * LANE] for j in range(n_blk - n_win)]
            mx = blocks[0]
            for j in range(n_blk):
                s_ref[slot, :, j * LANE:(j + 1) * LANE] = blocks[j]
                mx = jnp.maximum(mx, blocks[j])
            m_ref[slot] = jnp.broadcast_to(mx.max(axis=-1, keepdims=True), mx.shape)

    def output(rows_r, slots):
        accs, dens, p_ctx = [], [], []
        for r, slot in zip(rows_r, slots):
            _, band = band_of(r)
            mx = m_ref[slot]
            p = [jnp.exp(s_ref[slot, :, j * LANE:(j + 1) * LANE] - mx) for j in range(n_blk)]
            den = p[0]
            for pj in p[1:]:
                den = den + pj
            dens.append(den.sum(axis=-1, keepdims=True))
            pb = [pj.astype(BF16) for pj in p]
            accs.append(jnp.dot(jnp.concatenate(pb[:n_win], axis=1), v_ref[band, :], preferred_element_type=F32))
            p_ctx.append(jnp.concatenate(pb[n_win:], axis=1))
        acc_ctx = jnp.dot(jnp.concatenate(p_ctx, axis=0), v_ref[0:clen, :], preferred_element_type=F32)
        for u, r in enumerate(rows_r):
            acc = (accs[u] + acc_ctx[u * sq:(u + 1) * sq]) / dens[u]
            rows = pl.ds(pl.multiple_of(clen + r * GRID_W, GRID_W), GRID_W)
            o_ref[rows, :] = jnp.where(lane_lo, acc[:GRID_W], acc[GRID_W:]).astype(o_ref.dtype)

    scores(list(range(NA_UNROLL)), list(range(NA_UNROLL)))

    def body(j, carry):
        r = 2 * NA_UNROLL * j
        for half in range(2):
            output([r + half * NA_UNROLL + u for u in range(NA_UNROLL)],
                   [half * NA_UNROLL + u for u in range(NA_UNROLL)])
            scores([jnp.minimum(r + (half + 1) * NA_UNROLL + u, n_rows - 1) for u in range(NA_UNROLL)],
                   [(1 - half) * NA_UNROLL + u for u in range(NA_UNROLL)])
        return carry

    lax.fori_loop(0, n_rows // (2 * NA_UNROLL), body, 0)


def _na_tables(rpb, n_rows):
    col = jnp.arange(GRID_W)
    col_start = jnp.clip(col - NA_COLS // 2, 0, GRID_W - NA_COLS)
    col_ok = (col[None, :] >= col_start[:, None]) & (col[None, :] < col_start[:, None] + NA_COLS)
    col_idx = jnp.clip(col[None, :] - col[:, None], -(NA_COLS - 1), NA_COLS - 1) + NA_COLS - 1
    rpb_c = jnp.where(col_ok[None, None], rpb.astype(F32)[:, :, col_idx], NEG_INF)
    dr = jnp.arange(NA_ROWS_MAX)[:, None] + jnp.arange(NA_ROWS_MAX)[None, :]
    tab = rpb_c[:, dr]
    tab = jnp.transpose(tab, (0, 1, 3, 2, 4)).reshape(NA_HEADS, NA_ROWS_MAX, GRID_W, NA_BAND)
    tab = tab.reshape(NA_HEADS // HEAD_PAIR, HEAD_PAIR, NA_ROWS_MAX, GRID_W, NA_BAND)
    tab = jnp.transpose(tab, (0, 2, 1, 3, 4)).reshape(NA_HEADS // HEAD_PAIR, NA_ROWS_MAX,
                                                       HEAD_PAIR * GRID_W, NA_BAND)
    quarter = NA_HEAD_DIM // 4
    inv = ROPE_BASE ** (-jnp.arange(quarter, dtype=F32) / quarter)

    def tables(pos, lo):
        ang = pos[:, None] * inv[None, :]
        z = jnp.zeros_like(ang)
        c, s = jnp.cos(ang), jnp.sin(ang)
        ch = jnp.concatenate([c, c, z, z] if lo else [z, z, c, c], axis=-1)
        sh = jnp.concatenate([-s, s, z, z] if lo else [z, z, -s, s], axis=-1)
        return jnp.tile(ch, (1, HEAD_PAIR)), jnp.tile(sh, (1, HEAD_PAIR))

    rowc, rows = tables(jnp.arange(n_rows).astype(F32), True)
    colc, cols = tables(jnp.arange(GRID_W).astype(F32), False)
    return tab, rowc, rows, colc, cols


def _na(pm3, q_norm_w, k_norm_w, rpb, clen):
    bsz, t, _ = pm3.shape
    n_rows = (t - clen) // GRID_W
    tab, rowc, rows, colc, cols = _na_tables(rpb, n_rows)
    qw = jnp.tile(q_norm_w.astype(F32), HEAD_PAIR).reshape(1, LANE)
    kw = jnp.tile(k_norm_w.astype(F32), HEAD_PAIR).reshape(1, LANE)
    n_pairs = NA_HEADS // HEAD_PAIR
    const = lambda p, b: (0, 0)
    tok = lambda c0: pl.BlockSpec((None, t, LANE), lambda p, b: (b, 0, c0 // LANE + p))
    return pl.pallas_call(
        functools.partial(_na_kernel, clen=clen, n_rows=n_rows),
        grid=(n_pairs, bsz),
        in_specs=[tok(COL_Q), tok(COL_K), tok(COL_V),
                  pl.BlockSpec((None, NA_ROWS_MAX, HEAD_PAIR * GRID_W, NA_BAND), lambda p, b: (p, 0, 0, 0)),
                  pl.BlockSpec((n_rows, LANE), const), pl.BlockSpec((n_rows, LANE), const),
                  pl.BlockSpec((GRID_W, LANE), const), pl.BlockSpec((GRID_W, LANE), const),
                  pl.BlockSpec((1, LANE), const), pl.BlockSpec((1, LANE), const)],
        out_specs=pl.BlockSpec((None, t, LANE), lambda p, b: (b, 0, p)),
        out_shape=jax.ShapeDtypeStruct((bsz, t, D_NA), BF16),
        scratch_shapes=[pltpu.VMEM((t, LANE), BF16), pltpu.VMEM((t, LANE), BF16),
                        pltpu.VMEM((2 * NA_UNROLL, HEAD_PAIR * GRID_W, NA_BAND + clen), F32),
                        pltpu.VMEM((2 * NA_UNROLL, HEAD_PAIR * GRID_W, LANE), F32)],
        compiler_params=_params(("arbitrary", "arbitrary")),
        name="natten",
    )(pm3, pm3, pm3, tab, rowc, rows, colc, cols, qw, kw)


HALO = 16
CONV_PAD = SSD_CONV // 2
BC_DIM = SSD_GROUPS * SSD_STATE
PAIRS_PER_GROUP = SSD_HPG // HEAD_PAIR


def _ssd_chunk_index(s, rev, cc, nc):
    if not rev:
        return s
    return jnp.where(s < cc, cc - 1 - s, nc - 1 - (s - cc))


def _ssd_conv_kernel(main_ref, prev_ref, next_ref, cw_ref, cb_ref, o_ref, *, cc, nc):
    q = SSD_CHUNK
    cidx = pl.program_id(1)
    first_col = jnp.where((cidx == 0) | (cidx == cc), HALO, 0)
    end_col = jnp.where((cidx == cc - 1) | (cidx == nc - 1), HALO + q, q + 2 * HALO)
    ext = jnp.concatenate([prev_ref[...], main_ref[...], next_ref[...]], axis=0)
    ti = lax.broadcasted_iota(jnp.int32, (q, q + 2 * HALO), 0)
    ji = lax.broadcasted_iota(jnp.int32, (q, q + 2 * HALO), 1)
    in_segment = (ji >= first_col) & (ji < end_col)
    conv = cb_ref[...] + cw_ref[CONV_PAD:CONV_PAD + 1, :] * main_ref[...].astype(F32)
    for k in range(SSD_CONV):
        if k == CONV_PAD:
            continue
        shift = jnp.where((ji == ti + (HALO + k - CONV_PAD)) & in_segment, 1.0, 0.0).astype(BF16)
        conv = conv + cw_ref[k:k + 1, :] * jnp.dot(shift, ext, preferred_element_type=F32)
    o_ref[...] = (conv * jax.nn.sigmoid(conv)).astype(o_ref.dtype)


def _ssd_core(xbc_ref, dt_ref, dtb_ref, a_ref, exp_ref, h_ref, yoff_ref, y_ref, *, rev):
    q = SSD_CHUNK
    xbc = xbc_ref[...]

    hb = SSD_HEADS if rev else 0
    x = dt_ref[...] + dtb_ref[...]
    dt = jnp.maximum(x, 0.0) + jnp.log(1.0 + jnp.exp(-jnp.abs(x)))
    la = dt * a_ref[...]
    ri = lax.broadcasted_iota(jnp.int32, (q, q), 0)
    ci = lax.broadcasted_iota(jnp.int32, (q, q), 1)
    allowed = (ci >= ri) if rev else (ci <= ri)
    tri = jnp.where(allowed, 1.0, 0.0).astype(F32)
    a_cs = jnp.dot(tri, la, precision=HIGHEST, preferred_element_type=F32)
    last = 0 if rev else q - 1
    a_cs_t = a_cs.T
    e_all = jnp.exp(a_cs)
    src_t = a_cs_t - jnp.log(dt.T)
    wdt_t = jnp.exp(a_cs_t[:, last:last + 1] - src_t)
    neg_mask = jnp.where(allowed, 0.0, NEG_INF)
    blk = (last // 8) * 8
    e_last = jnp.exp(jnp.dot(a_cs[blk:blk + 8, :], exp_ref[...], precision=HIGHEST,
                             preferred_element_type=F32))[last - blk:last - blk + 1, hb * SSD_HEAD_DIM:
                                                          (hb + SSD_HEADS) * SSD_HEAD_DIM]

    lane_lo = lax.broadcasted_iota(jnp.int32, (1, LANE), 1) < SSD_HEAD_DIM
    nt = (((1,), (1,)), ((), ()))
    for g in range(SSD_GROUPS):
        b_g = xbc[:, D_SSD + g * SSD_STATE:D_SSD + (g + 1) * SSD_STATE]
        c_g = xbc[:, D_SSD + BC_DIM + g * SSD_STATE:D_SSD + BC_DIM + (g + 1) * SSD_STATE]
        cols = slice(g * SSD_HPG * SSD_HEAD_DIM, (g + 1) * SSD_HPG * SSD_HEAD_DIM)
        yoff_ref[:, cols] = jnp.dot(c_g, h_ref[:, cols].astype(BF16), preferred_element_type=F32)
        cb = lax.dot_general(c_g, b_g, nt, preferred_element_type=F32)
        b_t = b_g.astype(F32).T
        for pj in range(PAIRS_PER_GROUP):
            pair = g * PAIRS_PER_GROUP + pj
            lanes = slice(pair * LANE, (pair + 1) * LANE)
            lhs, ecol = [], []
            for hh in range(HEAD_PAIR):
                head = hb + pair * HEAD_PAIR + hh
                seg = (a_cs[:, head:head + 1] - src_t[head:head + 1, :]) + neg_mask
                lhs.append((cb * jnp.exp(seg)).astype(BF16))
                ecol.append(jnp.broadcast_to(e_all[:, head:head + 1], (q, LANE)))
            for hh in range(HEAD_PAIR):
                head = hb + pair * HEAD_PAIR + hh
                lhs.append((b_t * wdt_t[head:head + 1, :]).astype(BF16))
            res = jnp.dot(jnp.concatenate(lhs, axis=0), xbc[:, lanes], preferred_element_type=F32)
            y_ref[:, lanes] = (jnp.where(lane_lo, res[0:q], res[q:2 * q])
                               + yoff_ref[:, lanes] * jnp.where(lane_lo, ecol[0], ecol[1]))
            h_ref[:, lanes] = (h_ref[:, lanes] * e_last[:, lanes]
                               + jnp.where(lane_lo, res[2 * q:3 * q], res[3 * q:4 * q]))


def _ssd_scan_kernel(xf_ref, dtf_ref, xb_ref, dtb_ref, bias_ref, a_ref, exp_ref, dsum_ref, of_ref, ob_ref,
                     hf_ref, yofff_ref, hb_ref, yoffb_ref, yf_ref, yb_ref):
    @pl.when(pl.program_id(1) == 0)
    def _():
        hf_ref[...] = jnp.zeros_like(hf_ref)
        hb_ref[...] = jnp.zeros_like(hb_ref)

    _ssd_core(xf_ref, dtf_ref, bias_ref, a_ref, exp_ref, hf_ref, yofff_ref, yf_ref, rev=False)
    _ssd_core(xb_ref, dtb_ref, bias_ref, a_ref, exp_ref, hb_ref, yoffb_ref, yb_ref, rev=True)
    of_ref[...] = (yf_ref[...] + xf_ref[:, 0:D_SSD].astype(F32) * dsum_ref[...]).astype(of_ref.dtype)
    ob_ref[...] = yb_ref[...].astype(ob_ref.dtype)


def _ssd_norm_kernel(yf_ref, yb_ref, z_ref, nw_ref, o_ref):
    z = z_ref[...].astype(F32)
    u = (yf_ref[...].astype(F32) + yb_ref[...].astype(F32)) * (z * jax.nn.sigmoid(z))
    gw = D_SSD // SSD_GROUPS
    for g in range(SSD_GROUPS):
        ug = u[:, g * gw:(g + 1) * gw]
        ms = jnp.mean(ug * ug, axis=-1, keepdims=True)
        o_ref[:, g * gw:(g + 1) * gw] = (ug * lax.rsqrt(ms + EPS) * nw_ref[:, g * gw:(g + 1) * gw]).astype(o_ref.dtype)


def _ssd(pm3, pdt3, conv_w, conv_b, dt_bias, a_log, d_skip, ssd_norm_w, clen):
    bsz, t, _ = pm3.shape
    q = SSD_CHUNK
    nc, cc = t // q, clen // q
    nh = 2 * SSD_HEADS
    cw = jnp.pad(conv_w.astype(F32), ((0, 8 - SSD_CONV), (0, 0)))
    cb = conv_b.astype(F32).reshape(1, XBC_DIM)
    dtb = jnp.pad(dt_bias.astype(F32).reshape(1, nh), ((0, 0), (0, DT_PAD - nh)))
    a_neg = jnp.pad(-jnp.exp(a_log.astype(F32)).reshape(1, nh), ((0, 0), (0, DT_PAD - nh)))
    dsum = jnp.repeat((d_skip[0] + d_skip[1]).astype(F32), SSD_HEAD_DIM).reshape(1, D_SSD)
    expand = jnp.repeat(jnp.eye(DT_PAD, dtype=F32)[:, :nh], SSD_HEAD_DIM, axis=1)
    nw = ssd_norm_w.astype(F32).reshape(1, D_SSD)
    hpb = q // HALO
    n_halo = t // HALO

    const = lambda b, s: (0, 0)
    xcol = COL_XBC // XBC_DIM
    xbc = pl.pallas_call(
        functools.partial(_ssd_conv_kernel, cc=cc, nc=nc),
        grid=(bsz, nc),
        in_specs=[pl.BlockSpec((None, q, XBC_DIM), lambda b, s: (b, s, xcol)),
                  pl.BlockSpec((None, HALO, XBC_DIM), lambda b, s: (b, jnp.maximum(s * hpb - 1, 0), xcol)),
                  pl.BlockSpec((None, HALO, XBC_DIM),
                               lambda b, s: (b, jnp.minimum((s + 1) * hpb, n_halo - 1), xcol)),
                  pl.BlockSpec((8, XBC_DIM), const), pl.BlockSpec((1, XBC_DIM), const)],
        out_specs=pl.BlockSpec((None, q, XBC_DIM), lambda b, s: (b, s, 0)),
        out_shape=jax.ShapeDtypeStruct((bsz, t, XBC_DIM), BF16),
        compiler_params=_params(("arbitrary", "arbitrary")),
        name="ssd_conv",
    )(pm3, pm3, pm3, cw, cb)

    tok_f = lambda b, s: (b, s, 0)
    tok_b = lambda b, s: (b, _ssd_chunk_index(s, True, cc, nc), 0)
    state = [pltpu.VMEM((SSD_STATE, D_SSD), F32), pltpu.VMEM((q, D_SSD), F32)]
    yf, yb = pl.pallas_call(
        _ssd_scan_kernel,
        grid=(bsz, nc),
        in_specs=[pl.BlockSpec((None, q, XBC_DIM), tok_f), pl.BlockSpec((None, q, DT_PAD), tok_f),
                  pl.BlockSpec((None, q, XBC_DIM), tok_b), pl.BlockSpec((None, q, DT_PAD), tok_b),
                  pl.BlockSpec((1, DT_PAD), const), pl.BlockSpec((1, DT_PAD), const),
                  pl.BlockSpec((DT_PAD, nh * SSD_HEAD_DIM), const), pl.BlockSpec((1, D_SSD), const)],
        out_specs=[pl.BlockSpec((None, q, D_SSD), tok_f), pl.BlockSpec((None, q, D_SSD), tok_b)],
        out_shape=[jax.ShapeDtypeStruct((bsz, t, D_SSD), BF16), jax.ShapeDtypeStruct((bsz, t, D_SSD), BF16)],
        scratch_shapes=state + state + [pltpu.VMEM((q, D_SSD), F32), pltpu.VMEM((q, D_SSD), F32)],
        compiler_params=_params(("arbitrary", "arbitrary")),
        name="ssd_scan",
    )(xbc, pdt3, xbc, pdt3, dtb, a_neg, expand, dsum)
    tn = MOD_TILE
    row = lambda b, i: (b, i, 0)
    return pl.pallas_call(
        _ssd_norm_kernel,
        grid=(bsz, t // tn),
        in_specs=[pl.BlockSpec((None, tn, D_SSD), row), pl.BlockSpec((None, tn, D_SSD), row),
                  pl.BlockSpec((None, tn, D_SSD), lambda b, i: (b, i, COL_Z // D_SSD)),
                  pl.BlockSpec((1, D_SSD), lambda b, i: (0, 0))],
        out_specs=pl.BlockSpec((None, tn, D_SSD), row),
        out_shape=jax.ShapeDtypeStruct((bsz, t, D_SSD), BF16),
        compiler_params=_params(("arbitrary", "arbitrary")),
        name="ssd_norm",
    )(yf, yb, pm3, nw)


MOE_TB = 512
ROUTE_TM = 512
INFO_W = 8
BIG_NEG = -3.0e38
ROW_SEG = D_MODEL // LANE


def _load_tokens(ref, t0, n):
    return jnp.concatenate([ref[pl.ds(t0 * ROW_SEG + c, n, stride=ROW_SEG), :] for c in range(ROW_SEG)], axis=1)


def _store_tokens(ref, t0, val):
    n = val.shape[0]
    for c in range(ROW_SEG):
        ref[pl.ds(t0 * ROW_SEG + c, n, stride=ROW_SEG), :] = val[:, c * LANE:(c + 1) * LANE]


def _route_kernel(f_ref, wr_ref, info_ref, cnt_ref, run_ref):
    tm = f_ref.shape[0] // ROW_SEG

    @pl.when(pl.program_id(0) == 0)
    def _():
        run_ref[...] = jnp.zeros_like(run_ref)

    logits = jnp.dot(_load_tokens(f_ref, 0, tm).astype(BF16), wr_ref[...], preferred_element_type=F32)
    lane = lax.broadcasted_iota(jnp.int32, (tm, LANE), 1).astype(F32)
    lg = jnp.where(lane < N_EXPERTS, logits, BIG_NEG)
    m1 = lg.max(axis=-1, keepdims=True)
    i1 = jnp.where(lg == m1, lane, float(LANE)).min(axis=-1, keepdims=True)
    lg2 = jnp.where(lane == i1, BIG_NEG, lg)
    m2 = lg2.max(axis=-1, keepdims=True)
    i2 = jnp.where(lg2 == m2, lane, float(LANE)).min(axis=-1, keepdims=True)
    ex = jnp.exp(m2 - m1)
    g0 = 1.0 / (1.0 + ex)
    g1 = ex / (1.0 + ex)
    sel0 = lane == i1
    sel1 = lane == i2
    onehot = jnp.where(sel0 | sel1, 1.0, 0.0)
    ri = lax.broadcasted_iota(jnp.int32, (tm, tm), 0)
    ci = lax.broadcasted_iota(jnp.int32, (tm, tm), 1)
    strict_lower = jnp.where(ci < ri, 1.0, 0.0).astype(BF16)
    before = jnp.dot(strict_lower, onehot.astype(BF16), preferred_element_type=F32) + run_ref[...]
    pos0 = jnp.where(sel0, before, 0.0).sum(axis=-1, keepdims=True)
    pos1 = jnp.where(sel1, before, 0.0).sum(axis=-1, keepdims=True)
    run_ref[...] = run_ref[...] + onehot.sum(axis=0, keepdims=True)
    cnt_ref[...] = run_ref[...]
    info = jnp.where(lane == 0, i1, jnp.where(lane == 1, i2, jnp.where(lane == 2, g0, jnp.where(
        lane == 3, g1, jnp.where(lane == 4, pos0, jnp.where(lane == 5, pos1, 0.0))))))
    info_ref[...] = info[:, :INFO_W]


def _route(f8, w_router):
    m, d = f8.shape[0] // ROW_SEG, D_MODEL
    wr = jnp.pad(w_router.astype(BF16), ((0, 0), (0, LANE - N_EXPERTS)))
    return pl.pallas_call(
        _route_kernel,
        grid=(m // ROUTE_TM,),
        in_specs=[pl.BlockSpec((ROUTE_TM * ROW_SEG, LANE), lambda i: (i, 0)),
                  pl.BlockSpec((d, LANE), lambda i: (0, 0))],
        out_specs=[pl.BlockSpec((ROUTE_TM, INFO_W), lambda i: (i, 0)), pl.BlockSpec((1, LANE), lambda i: (0, 0))],
        out_shape=[jax.ShapeDtypeStruct((m, INFO_W), F32), jax.ShapeDtypeStruct((1, LANE), F32)],
        scratch_shapes=[pltpu.VMEM((1, LANE), F32)],
        compiler_params=_params(("arbitrary",)),
        name="moe_route",
    )(f8, wr)


def _row_copy(src_ref, src_row, dst_ref, dst_row, sem):
    src = src_ref.at[pl.ds(pl.multiple_of(src_row * ROW_SEG, ROW_SEG), ROW_SEG)]
    dst = dst_ref.at[pl.ds(pl.multiple_of(dst_row * ROW_SEG, ROW_SEG), ROW_SEG)]
    return pltpu.make_async_copy(src, dst, sem)


def _wait_rows(src_ref, dst_ref, sem, n):
    def wait(t, c):
        _row_copy(src_ref, 0, dst_ref, 0, sem).wait()
        return c

    lax.fori_loop(0, n, wait, 0, unroll=8)


def _dispatch_kernel(dest_hbm, f_ref, init_hbm, rows_hbm, dest_smem, stage_ref, idx_sem, row_sem):
    del init_hbm
    tm = f_ref.shape[0] // ROW_SEG
    i, n = pl.program_id(0), pl.num_programs(0)
    slot = i % 2
    stage = stage_ref.at[slot]
    sem = row_sem.at[slot]

    @pl.when(i >= 2)
    def _():
        _wait_rows(stage, rows_hbm, sem, TOP_K * tm)

    stage[...] = f_ref[...]
    base = pl.multiple_of(i * (TOP_K * tm), TOP_K * tm)
    idx_copy = pltpu.make_async_copy(dest_hbm.at[pl.ds(base, TOP_K * tm)], dest_smem, idx_sem)
    idx_copy.start()
    idx_copy.wait()

    def start(t, c):
        for k in range(TOP_K):
            _row_copy(stage, t, rows_hbm, dest_smem[TOP_K * t + k], sem).start()
        return c

    lax.fori_loop(0, tm, start, 0, unroll=8)

    @pl.when(i == n - 1)
    def _():
        _wait_rows(stage, rows_hbm, sem, TOP_K * tm)

        @pl.when(n >= 2)
        def _():
            _wait_rows(stage_ref.at[1 - slot], rows_hbm, row_sem.at[1 - slot], TOP_K * tm)


def _dispatch(f8, dest_flat, n_rows):
    m = f8.shape[0] // ROW_SEG
    return pl.pallas_call(
        _dispatch_kernel,
        grid=(m // ROUTE_TM,),
        in_specs=[pl.BlockSpec(memory_space=pl.ANY), pl.BlockSpec((ROUTE_TM * ROW_SEG, LANE), lambda i: (i, 0)),
                  pl.BlockSpec(memory_space=pl.ANY)],
        out_specs=pl.BlockSpec(memory_space=pl.ANY),
        out_shape=jax.ShapeDtypeStruct((n_rows * ROW_SEG, LANE), f8.dtype),
        scratch_shapes=[pltpu.SMEM((TOP_K * ROUTE_TM,), jnp.int32),
                        pltpu.VMEM((2, ROUTE_TM * ROW_SEG, LANE), f8.dtype),
                        pltpu.SemaphoreType.DMA(()), pltpu.SemaphoreType.DMA((2,))],
        input_output_aliases={2: 0},
        compiler_params=_params(("arbitrary",)),
        name="moe_dispatch",
    )(dest_flat, f8, jnp.zeros((n_rows * ROW_SEG, LANE), f8.dtype))


def _gffn_kernel(be_ref, nu_ref, x_ref, w1_ref, w3_ref, w2_ref, y_ref, acc_ref):
    i, j = pl.program_id(0), pl.program_id(1)

    @pl.when(i < nu_ref[0])
    def _():
        @pl.when(j == 0)
        def _():
            acc_ref[...] = jnp.zeros_like(acc_ref)

        x = _load_tokens(x_ref, 0, MOE_TB).astype(BF16)
        h1 = jnp.dot(x, w1_ref[...], preferred_element_type=F32)
        h3 = jnp.dot(x, w3_ref[...], preferred_element_type=F32)
        act = (h1 * jax.nn.sigmoid(h1) * h3).astype(BF16)
        acc_ref[...] += jnp.dot(act, w2_ref[...], preferred_element_type=F32)

        @pl.when(j == pl.num_programs(1) - 1)
        def _():
            _store_tokens(y_ref, 0, acc_ref[...])

    @pl.when((i >= nu_ref[0]) & (j == pl.num_programs(1) - 1))
    def _():
        y_ref[...] = jnp.zeros_like(y_ref)


def _gffn(rows, block_e, n_used, w1, w3, w2, tf=1792):
    r, d = rows.shape[0] // ROW_SEG, D_MODEL
    ff = w1.shape[2]
    nb = r // MOE_TB
    grid_spec = pltpu.PrefetchScalarGridSpec(
        num_scalar_prefetch=2,
        grid=(nb, ff // tf),
        in_specs=[pl.BlockSpec((MOE_TB * ROW_SEG, LANE), lambda i, j, be, nu: (i, 0)),
                  pl.BlockSpec((None, d, tf), lambda i, j, be, nu: (be[i], 0, j)),
                  pl.BlockSpec((None, d, tf), lambda i, j, be, nu: (be[i], 0, j)),
                  pl.BlockSpec((None, tf, d), lambda i, j, be, nu: (be[i], j, 0))],
        out_specs=pl.BlockSpec((MOE_TB * ROW_SEG, LANE), lambda i, j, be, nu: (i, 0)),
        scratch_shapes=[pltpu.VMEM((MOE_TB, d), F32)],
    )
    return pl.pallas_call(
        _gffn_kernel,
        grid_spec=grid_spec,
        out_shape=jax.ShapeDtypeStruct((r * ROW_SEG, LANE), F32),
        compiler_params=_params(("arbitrary", "arbitrary")),
        name="moe_ffn",
    )(block_e, n_used, rows, w1, w3, w2)


def _combine_kernel(dest_hbm, y_hbm, h_ref, info_ref, modt_ref, o_ref, dest_smem, buf_ref, idx_sem, row_sem,
                    *, nsub):
    tm = h_ref.shape[0]
    i, n = pl.program_id(0), pl.num_programs(0)

    def gather(step, slot):
        base = pl.multiple_of(step * (TOP_K * tm), TOP_K * tm)
        idx_copy = pltpu.make_async_copy(dest_hbm.at[pl.ds(base, TOP_K * tm)], dest_smem, idx_sem)
        idx_copy.start()
        idx_copy.wait()

        def start(t, c):
            for k in range(TOP_K):
                _row_copy(y_hbm, dest_smem[TOP_K * t + k], buf_ref.at[slot, k], t, row_sem.at[slot]).start()
            return c

        lax.fori_loop(0, tm, start, 0, unroll=8)

    @pl.when(i == 0)
    def _():
        gather(0, 0)

    @pl.when(i + 1 < n)
    def _():
        gather(i + 1, (i + 1) % 2)

    slot = i % 2
    _wait_rows(y_hbm, buf_ref.at[slot, 0], row_sem.at[slot], TOP_K * tm)
    for s in range(nsub):
        r0, r1 = s * MOD_TILE, (s + 1) * MOD_TILE
        info = info_ref[r0:r1, :]
        mix = (info[:, 2:3] * _load_tokens(buf_ref.at[slot, 0], r0, MOD_TILE)
               + info[:, 3:4] * _load_tokens(buf_ref.at[slot, 1], r0, MOD_TILE))
        o_ref[r0:r1, :] = h_ref[r0:r1, :] + modt_ref[s, 5:6, :] * mix


def _combine(y_rows, dest_flat, h, info, modt):
    m, d = h.shape
    nsub = ROUTE_TM // MOD_TILE
    return pl.pallas_call(
        functools.partial(_combine_kernel, nsub=nsub),
        grid=(m // ROUTE_TM,),
        in_specs=[pl.BlockSpec(memory_space=pl.ANY), pl.BlockSpec(memory_space=pl.ANY),
                  pl.BlockSpec((ROUTE_TM, d), lambda i: (i, 0)),
                  pl.BlockSpec((ROUTE_TM, INFO_W), lambda i: (i, 0)),
                  pl.BlockSpec((nsub, 6, d), lambda i: (i, 0, 0))],
        out_specs=pl.BlockSpec((ROUTE_TM, d), lambda i: (i, 0)),
        out_shape=jax.ShapeDtypeStruct((m, d), F32),
        scratch_shapes=[pltpu.SMEM((TOP_K * ROUTE_TM,), jnp.int32),
                        pltpu.VMEM((2, TOP_K, ROUTE_TM * ROW_SEG, LANE), F32),
                        pltpu.SemaphoreType.DMA(()), pltpu.SemaphoreType.DMA((2,))],
        compiler_params=_params(("arbitrary",)),
        name="moe_combine",
    )(dest_flat, y_rows, h, info, modt)


def _moe(f8, h, w_router, w1, w3, w2, modt):
    m, d = h.shape
    info, cnt = _route(f8, w_router)
    e = info[:, 0:TOP_K].astype(jnp.int32)
    pos = info[:, 4:4 + TOP_K].astype(jnp.int32)
    counts = cnt[0, :N_EXPERTS].astype(jnp.int32)
    padded = (counts + MOE_TB - 1) // MOE_TB * MOE_TB
    pad_end = jnp.cumsum(padded)
    pad_start = pad_end - padded
    dest_flat = (pad_start[e] + pos).reshape(-1)
    n_rows = m * TOP_K + N_EXPERTS * MOE_TB
    nb = n_rows // MOE_TB
    block_e = jnp.minimum(jnp.searchsorted(pad_end, jnp.arange(nb, dtype=jnp.int32) * MOE_TB, side='right'),
                          N_EXPERTS - 1).astype(jnp.int32)
    n_used = (pad_end[-1:] // MOE_TB).astype(jnp.int32)
    rows = _dispatch(f8, dest_flat, n_rows)
    y_rows = _gffn(rows, block_e, n_used, w1.astype(BF16), w3.astype(BF16), w2.astype(BF16))
    return _combine(y_rows, dest_flat, h, info, modt)


def _fb_rms_norm(x, w):
    xf = x.astype(F32)
    y = xf * lax.rsqrt(jnp.mean(xf * xf, axis=-1, keepdims=True) + EPS)
    return (y * w.astype(F32)).astype(x.dtype)


def _fb_flip(a):
    return jnp.flip(a, axis=1)


def _fb_depthwise_conv(x, w, b):
    y = lax.conv_general_dilated(x, w[:, None, :].astype(x.dtype), window_strides=(1,),
                                 padding=[(SSD_CONV // 2, SSD_CONV // 2)],
                                 dimension_numbers=('NWC', 'WIO', 'NWC'),
                                 feature_group_count=x.shape[-1])
    return y + b.astype(x.dtype)


def _fb_ssd_inputs(xbc_raw, dt_raw, conv_w, conv_b, dt_bias):
    bsz, seqlen, _ = xbc_raw.shape
    xbc = jax.nn.silu(_fb_depthwise_conv(xbc_raw, conv_w, conv_b))
    xs, bs, cs = jnp.split(xbc, (D_SSD, D_SSD + SSD_GROUPS * SSD_STATE), axis=-1)
    xs = xs.reshape(bsz, seqlen, SSD_GROUPS, SSD_HPG, SSD_HEAD_DIM)
    bs = bs.reshape(bsz, seqlen, SSD_GROUPS, SSD_STATE)
    cs = cs.reshape(bsz, seqlen, SSD_GROUPS, SSD_STATE)
    dt = jax.nn.softplus(dt_raw.astype(F32) + dt_bias.reshape(-1).astype(F32))
    dt = dt.reshape(bsz, seqlen, 2, SSD_GROUPS, SSD_HPG)
    return xs, bs, cs, dt[:, :, 0], dt[:, :, 1]


def _fb_ssd_chunked(x, dt, a, b_in, c_in, h0, return_y=True):
    bsz, seqlen, g, r, p = x.shape
    n = b_in.shape[-1]
    nc = seqlen // SSD_CHUNK
    dtf = dt.astype(F32)
    xdt = (x.astype(F32) * dtf[..., None]).reshape(bsz, nc, SSD_CHUNK, g, r, p)
    log_a = (dtf * a.astype(F32)).reshape(bsz, nc, SSD_CHUNK, g, r)
    bc = b_in.astype(F32).reshape(bsz, nc, SSD_CHUNK, g, n)
    cc = c_in.astype(F32).reshape(bsz, nc, SSD_CHUNK, g, n)
    a_cs = jnp.cumsum(log_a, axis=2)
    a_last = a_cs[:, :, -1]
    states = jnp.einsum('bcsgn,bcsgr,bcsgrp->bcgrpn', bc, jnp.exp(a_last[:, :, None] - a_cs), xdt)

    def step(h, inp):
        s_c, d_c = inp
        return h * d_c[..., None, None] + s_c, h

    h_fin, h_in = lax.scan(step, h0.astype(F32),
                           (jnp.moveaxis(states, 1, 0), jnp.moveaxis(jnp.exp(a_last), 1, 0)))
    if not return_y:
        return None, h_fin
    h_in = jnp.moveaxis(h_in, 0, 1)
    tri = jnp.tril(jnp.ones((SSD_CHUNK, SSD_CHUNK), bool))
    seg = a_cs[:, :, :, None] - a_cs[:, :, None, :]
    decay = jnp.exp(jnp.where(tri[:, :, None, None], seg, -jnp.inf))
    cb = jnp.einsum('bcqgn,bcsgn->bcqsg', cc, bc)
    y_diag = jnp.einsum('bcqsgr,bcsgrp->bcqgrp', cb[..., None] * decay, xdt)
    y_off = jnp.einsum('bcqgn,bcgrpn->bcqgrp', cc, h_in) * jnp.exp(a_cs)[..., None]
    y = (y_diag + y_off).reshape(bsz, seqlen, g, r, p)
    return y.astype(x.dtype), h_fin


def _fb_gated_group_rmsnorm(y, z, w):
    shp = y.shape
    u = y.astype(F32) * jax.nn.silu(z.astype(F32))
    u = u.reshape(*shp[:-1], SSD_GROUPS, shp[-1] // SSD_GROUPS)
    u = u * lax.rsqrt(jnp.mean(u * u, axis=-1, keepdims=True) + EPS)
    return (u.reshape(shp) * w.astype(F32)).astype(y.dtype)


def _fb_ssd(pm, pdt, conv_w, conv_b, dt_bias, a_log, d_skip, ssd_norm_w, clen, need_ctx):
    pm = pm.astype(F32)
    bsz = pm.shape[0]
    z = pm[:, :, COL_Z:COL_Z + D_SSD]
    xbc = pm[:, :, COL_XBC:COL_XBC + XBC_DIM]
    dt = pdt[:, :, :2 * SSD_HEADS]
    z_c, z_l = z[:, :clen], z[:, clen:]
    a_dir = -jnp.exp(a_log.astype(F32)).reshape(2, SSD_GROUPS, SSD_HPG)
    d_sum = (d_skip[0] + d_skip[1]).reshape(SSD_GROUPS, SSD_HPG)[..., None]
    xs_l, b_l, c_l, dtf_l, dtb_l = _fb_ssd_inputs(xbc[:, clen:], dt[:, clen:], conv_w, conv_b, dt_bias)
    xs_c, b_c, c_c, dtf_c, dtb_c = _fb_ssd_inputs(xbc[:, :clen], dt[:, :clen], conv_w, conv_b, dt_bias)
    h0 = jnp.zeros((bsz, SSD_GROUPS, SSD_HPG, SSD_HEAD_DIM, SSD_STATE), F32)
    yf_c, sf_c = _fb_ssd_chunked(xs_c, dtf_c, a_dir[0], b_c, c_c, h0, need_ctx)
    yb_c, sb_c = _fb_ssd_chunked(_fb_flip(xs_c), _fb_flip(dtb_c), a_dir[1], _fb_flip(b_c), _fb_flip(c_c), h0,
                                 need_ctx)
    yf_l, _ = _fb_ssd_chunked(xs_l, dtf_l, a_dir[0], b_l, c_l, sf_c)
    yb_l, _ = _fb_ssd_chunked(_fb_flip(xs_l), _fb_flip(dtb_l), a_dir[1], _fb_flip(b_l), _fb_flip(c_l), sb_c)
    seqlen = xs_l.shape[1]
    y_l = (yf_l + _fb_flip(yb_l) + xs_l * d_sum.astype(xs_l.dtype)).reshape(bsz, seqlen, D_SSD)
    y_l = _fb_gated_group_rmsnorm(y_l, z_l, ssd_norm_w)
    if need_ctx:
        y_c = (yf_c + _fb_flip(yb_c) + xs_c * d_sum.astype(xs_c.dtype)).reshape(bsz, clen, D_SSD)
        y_c = _fb_gated_group_rmsnorm(y_c, z_c, ssd_norm_w)
    else:
        y_c = jnp.zeros((bsz, clen, D_SSD), F32)
    return jnp.concatenate([y_c, y_l], axis=1)


def _fb_axial_rope(x, pos_r, pos_c):
    hd = x.shape[-1]
    quarter = hd // 4
    inv = ROPE_BASE ** (-jnp.arange(quarter, dtype=F32) / quarter)

    def rot(xh, pos):
        ang = pos[:, None] * inv[None, :]
        cos = jnp.cos(ang)[None, :, None, :].astype(x.dtype)
        sin = jnp.sin(ang)[None, :, None, :].astype(x.dtype)
        x1, x2 = jnp.split(xh, 2, axis=-1)
        return jnp.concatenate([x1 * cos - x2 * sin, x1 * sin + x2 * cos], axis=-1)

    return jnp.concatenate([rot(x[..., :hd // 2], pos_r), rot(x[..., hd // 2:], pos_c)], axis=-1)


def _fb_na_latent(q, k, v, kc, vc, rpb):
    bsz, seqlen, h, hd = q.shape
    rows = seqlen // GRID_W
    kr = min(NA_ROWS_MAX, rows)
    scale = hd ** -0.5
    qg = q.reshape(bsz, rows, GRID_W, h, hd)
    kg = k.reshape(bsz, rows, GRID_W, h, hd)
    vg = v.reshape(bsz, rows, GRID_W, h, hd)
    col = jnp.arange(GRID_W)
    col_start = jnp.clip(col - NA_COLS // 2, 0, GRID_W - NA_COLS)
    col_ok = (col[None, :] >= col_start[:, None]) & (col[None, :] < col_start[:, None] + NA_COLS)
    col_idx = jnp.clip(col[None, :] - col[:, None], -(NA_COLS - 1), NA_COLS - 1) + NA_COLS - 1
    rpb_c = rpb.astype(F32)[:, :, col_idx]

    def one_row(args):
        r, q_r = args
        r0 = jnp.clip(r - kr // 2, 0, rows - kr)
        k_band = lax.dynamic_slice_in_dim(kg, r0, kr, axis=1)
        v_band = lax.dynamic_slice_in_dim(vg, r0, kr, axis=1)
        dr_idx = r0 + jnp.arange(kr) - r + NA_ROWS_MAX - 1
        bias = jnp.take(rpb_c, dr_idx, axis=1)
        bias = jnp.where(col_ok[None, None], bias, NEG_INF).transpose(0, 2, 1, 3)
        s_win = jnp.einsum('bqhd,biwhd->bhqiw', q_r, k_band).astype(F32) * scale + bias[None]
        s_ctx = jnp.einsum('bqhd,bchd->bhqc', q_r, kc).astype(F32) * scale
        s = jnp.concatenate([s_win.reshape(bsz, h, GRID_W, kr * GRID_W), s_ctx], axis=-1)
        pr = jax.nn.softmax(s, axis=-1).astype(v.dtype)
        p_win = pr[..., :kr * GRID_W].reshape(bsz, h, GRID_W, kr, GRID_W)
        p_ctx = pr[..., kr * GRID_W:]
        return (jnp.einsum('bhqiw,biwhd->bqhd', p_win, v_band)
                + jnp.einsum('bhqc,bchd->bqhd', p_ctx, vc))

    out = lax.map(one_row, (jnp.arange(rows), jnp.moveaxis(qg, 1, 0)))
    return jnp.moveaxis(out, 0, 1).reshape(bsz, seqlen, h * hd)


def _fb_ctx_attention(q, k, v):
    bsz, clen, h, hd = q.shape
    s = jnp.einsum('bqhd,bkhd->bhqk', q, k).astype(F32) * hd ** -0.5
    pr = jax.nn.softmax(s, axis=-1).astype(v.dtype)
    return jnp.einsum('bhqk,bkhd->bqhd', pr, v).reshape(bsz, clen, h * hd)


def _fb_na(pm, q_norm_w, k_norm_w, rpb, clen, need_ctx):
    pm = pm.astype(F32)
    bsz, t, _ = pm.shape
    seqlen = t - clen
    q = pm[:, :, COL_Q:COL_Q + D_NA]
    k = pm[:, :, COL_K:COL_K + D_NA]
    v = pm[:, :, COL_V:COL_V + D_NA]
    tt = jnp.arange(seqlen)
    pos_r = (tt // GRID_W).astype(F32)
    pos_c = (tt % GRID_W).astype(F32)
    hs = (NA_HEADS, NA_HEAD_DIM)
    k_c = _fb_rms_norm(k[:, :clen].reshape(bsz, clen, *hs), k_norm_w)
    v_c = v[:, :clen].reshape(bsz, clen, *hs)
    q_l = _fb_axial_rope(_fb_rms_norm(q[:, clen:].reshape(bsz, seqlen, *hs), q_norm_w), pos_r, pos_c)
    k_l = _fb_axial_rope(_fb_rms_norm(k[:, clen:].reshape(bsz, seqlen, *hs), k_norm_w), pos_r, pos_c)
    v_l = v[:, clen:].reshape(bsz, seqlen, *hs)
    y_l = _fb_na_latent(q_l, k_l, v_l, k_c, v_c, rpb)
    if need_ctx:
        q_c = _fb_rms_norm(q[:, :clen].reshape(bsz, clen, *hs), q_norm_w)
        y_c = _fb_ctx_attention(q_c, k_c, v_c)
    else:
        y_c = jnp.zeros((bsz, clen, D_NA), F32)
    return jnp.concatenate([y_c, y_l], axis=1)


def _fb_swiglu(h, w1, w3, w2):
    return (jax.nn.silu(h @ w1) * (h @ w3)) @ w2


def _fb_moe(h, w_router, w1, w3, w2):
    blk = 256
    n_tok, d = h.shape
    n_assign = n_tok * TOP_K
    logits = jnp.dot(h, w_router, precision=HIGHEST).astype(F32)
    top_v, top_i = lax.top_k(logits, TOP_K)
    gates = jax.nn.softmax(top_v, axis=-1)
    flat_e = top_i.reshape(-1)
    order = jnp.argsort(flat_e)
    sorted_e = flat_e[order]
    tok = order // TOP_K
    counts = jnp.bincount(flat_e, length=N_EXPERTS)
    padded = (counts + blk - 1) // blk * blk
    pad_end = jnp.cumsum(padded)
    pad_start = pad_end - padded
    start = jnp.cumsum(counts) - counts
    dest = pad_start[sorted_e] + jnp.arange(n_assign) - start[sorted_e]
    n_blocks = -(-n_assign // blk) + N_EXPERTS
    rows = jnp.zeros((n_blocks * blk, d), h.dtype).at[dest].set(h[tok])
    block_e = jnp.minimum(jnp.searchsorted(pad_end, jnp.arange(n_blocks) * blk, side='right'),
                          N_EXPERTS - 1)

    def run_block(args):
        xb, e = args
        return _fb_swiglu(xb, w1[e], w3[e], w2[e])

    y_rows = lax.map(run_block, (rows.reshape(n_blocks, blk, d), block_e)).reshape(-1, d)
    contrib = y_rows[dest] * gates.reshape(-1)[order][:, None].astype(h.dtype)
    return jnp.zeros_like(h).at[tok].add(contrib)


def _tile_mod(mod_l, bsz, tiles_per_batch):
    t = jnp.arange(bsz * tiles_per_batch)
    idx = jnp.where(t % tiles_per_batch == 0, bsz, t // tiles_per_batch)
    return jnp.transpose(mod_l[:, idx, :], (1, 0, 2))


def kernel(x, c, ctx, c_ctx, w_mod, b_mod, norm1_w, norm2_w, w_in, conv_w, conv_b, dt_bias, a_log, d_skip,
           ssd_norm_w, q_norm_w, k_norm_w, rpb, w_br_ssd, w_br_na, w_out, w_ff1, w_ff3, w_ff2, w_router,
           w_e1, w_e3, w_e2):
    bsz, seqlen, d = x.shape
    clen = ctx.shape[1]
    n_layers = w_mod.shape[0]
    assert clen == MOD_TILE and seqlen % MOD_TILE == 0 and bsz < 16
    t = clen + seqlen
    m = bsz * t
    tiles_per_batch = t // MOD_TILE

    cond = jnp.zeros((16, d), F32).at[:bsz].set(c).at[bsz].set(c_ctx)
    mod = _modulation(cond, w_mod, b_mod)
    modt = [_tile_mod(mod[i], bsz, tiles_per_batch) for i in range(n_layers)]

    h = jnp.concatenate([ctx, x], axis=1).reshape(m, d)
    a = _adaln(h, norm1_w[0], modt[0])
    for i in range(n_layers):
        last = i == n_layers - 1
        wi = w_in[i]
        w_main = jnp.concatenate([wi[:, :IN_SPLITS[3]], wi[:, IN_SPLITS[4]:]], axis=1).astype(BF16)
        w_dt = jnp.pad(wi[:, IN_SPLITS[3]:IN_SPLITS[4]], ((0, 0), (0, DT_PAD - 2 * SSD_HEADS))).astype(BF16)
        p_main = _matmul(a, w_main, BF16, 512, 2560, "inproj")
        p_dt = _matmul(a, w_dt, F32, 1024, DT_PAD, "inproj_dt")
        pm3 = p_main.reshape(bsz, t, MAIN_DIM)
        y_ssd = _ssd(pm3, p_dt.reshape(bsz, t, DT_PAD), conv_w[i], conv_b[i], dt_bias[i], a_log[i],
                     d_skip[i], ssd_norm_w[i], clen).reshape(m, d)
        y_na = _na(pm3, q_norm_w[i], k_norm_w[i], rpb[i], clen).reshape(m, d)
        fi = i // 2
        moe_layer = i % 2 == 1
        h, f = _merge(p_main, y_ssd, y_na, h, w_br_ssd[i].astype(BF16), w_br_na[i].astype(BF16),
                      w_out[i].astype(BF16), norm2_w[i], modt[i], row_tile_out=moe_layer)
        nxt = min(i + 1, n_layers - 1)
        if not moe_layer:
            h, a = _ffn(f, h, w_ff1[fi].astype(BF16), w_ff3[fi].astype(BF16), w_ff2[fi].astype(BF16),
                        norm1_w[nxt], modt[i], modt[nxt])
        else:
            h = _moe(f, h, w_router[fi], w_e1[fi], w_e3[fi], w_e2[fi], modt[i])
            if not last:
                a = _adaln(h, norm1_w[nxt], modt[nxt])
    return h.reshape(bsz, t, d)[:, clen:]
```

```python
import functools
import math

import jax
import jax.numpy as jnp
from jax import lax
from jax.experimental import pallas as pl
from jax.experimental.pallas import tpu as pltpu

D_MODEL = 1024
GRID_W = 64
SSD_HEAD_DIM = 64
SSD_HEADS = D_MODEL // SSD_HEAD_DIM
D_SSD = SSD_HEADS * SSD_HEAD_DIM
SSD_GROUPS = 2
SSD_HPG = SSD_HEADS // SSD_GROUPS
SSD_STATE = 128
SSD_CONV = 5
SSD_CHUNK = 128
XBC_DIM = D_SSD + 2 * SSD_GROUPS * SSD_STATE
NA_HEAD_DIM = 64
NA_HEADS = D_MODEL // NA_HEAD_DIM
D_NA = NA_HEADS * NA_HEAD_DIM
NA_ROWS_MAX = 8
NA_COLS = 16
ROPE_BASE = 10000.0
N_EXPERTS = 8
TOP_K = 2
EPS = 1e-6
NEG_INF = -1e30
IN_SPLITS = (D_MODEL, 2 * D_MODEL, 2 * D_MODEL + D_SSD, 2 * D_MODEL + D_SSD + XBC_DIM,
             2 * D_MODEL + D_SSD + XBC_DIM + 2 * SSD_HEADS,
             2 * D_MODEL + D_SSD + XBC_DIM + 2 * SSD_HEADS + D_NA,
             2 * D_MODEL + D_SSD + XBC_DIM + 2 * SSD_HEADS + 2 * D_NA)

F32 = jnp.float32
BF16 = jnp.bfloat16
HIGHEST = lax.Precision.HIGHEST

MOD_TILE = 256
LANE = 128
VMEM_LIMIT = 48 * 1024 * 1024

COL_GS, COL_GN, COL_Z, COL_XBC, COL_Q, COL_K, COL_V = 0, 1024, 2048, 3072, 4608, 5632, 6656
MAIN_DIM = 7680
DT_PAD = LANE


def _params(sem):
    return pltpu.CompilerParams(dimension_semantics=sem, vmem_limit_bytes=VMEM_LIMIT)


def _rms_mod(x, nw, shift, scale):
    ms = jnp.mean(x * x, axis=-1, keepdims=True)
    return (x * lax.rsqrt(ms + EPS) * nw) * (1.0 + scale) + shift


def _mod_kernel(cond_ref, w_ref, b_ref, o_ref):
    c = cond_ref[...]
    s = c * jax.nn.sigmoid(c)
    o_ref[...] = jnp.dot(s, w_ref[...], precision=HIGHEST, preferred_element_type=F32) + b_ref[...]


def _modulation(cond, w_mod, b_mod):
    n_layers, d, _ = w_mod.shape
    rows = cond.shape[0]
    b4 = b_mod.reshape(n_layers, 6, 1, d)
    return pl.pallas_call(
        _mod_kernel,
        grid=(n_layers, 6),
        in_specs=[pl.BlockSpec((rows, d), lambda l, k: (0, 0)),
                  pl.BlockSpec((None, d, d), lambda l, k: (l, 0, k)),
                  pl.BlockSpec((None, None, 1, d), lambda l, k: (l, k, 0, 0))],
        out_specs=pl.BlockSpec((None, None, rows, d), lambda l, k: (l, k, 0, 0)),
        out_shape=jax.ShapeDtypeStruct((n_layers, 6, rows, d), F32),
        compiler_params=_params(("arbitrary", "arbitrary")),
        name="modulation",
    )(cond, w_mod, b4)


def _adaln_kernel(h_ref, nw_ref, modt_ref, o_ref, *, nsub):
    nw = nw_ref[...]
    for s in range(nsub):
        rows = pl.ds(s * MOD_TILE, MOD_TILE)
        x = h_ref[rows, :]
        o_ref[rows, :] = _rms_mod(x, nw, modt_ref[s, 0:1, :], modt_ref[s, 1:2, :]).astype(o_ref.dtype)


def _adaln(h, nw, modt, tm=1024):
    m, d = h.shape
    nsub = tm // MOD_TILE
    return pl.pallas_call(
        functools.partial(_adaln_kernel, nsub=nsub),
        grid=(m // tm,),
        in_specs=[pl.BlockSpec((tm, d), lambda i: (i, 0)),
                  pl.BlockSpec((1, d), lambda i: (0, 0)),
                  pl.BlockSpec((nsub, 6, d), lambda i: (i, 0, 0))],
        out_specs=pl.BlockSpec((tm, d), lambda i: (i, 0)),
        out_shape=jax.ShapeDtypeStruct((m, d), BF16),
        compiler_params=_params(("arbitrary",)),
        name="adaln",
    )(h, nw.reshape(1, d), modt)


def _matmul_kernel(a_ref, w_ref, o_ref):
    o_ref[...] = jnp.dot(a_ref[...], w_ref[...], preferred_element_type=F32).astype(o_ref.dtype)


def _matmul(a, w, out_dtype, tm, tn, name):
    m, k = a.shape
    n = w.shape[1]
    return pl.pallas_call(
        _matmul_kernel,
        grid=(n // tn, m // tm),
        in_specs=[pl.BlockSpec((tm, k), lambda j, i: (i, 0)),
                  pl.BlockSpec((k, tn), lambda j, i: (0, j))],
        out_specs=pl.BlockSpec((tm, tn), lambda j, i: (i, j)),
        out_shape=jax.ShapeDtypeStruct((m, n), out_dtype),
        compiler_params=_params(("arbitrary", "arbitrary")),
        name=name,
    )(a, w)


def _merge_kernel(gs_ref, gn_ref, z_ref, yf_ref, yb_ref, yn_ref, h_ref, wbs_ref, wbn_ref, wo_ref, snw_ref,
                  nw_ref, modt_ref, ho_ref, f_ref, *, nsub, row_tile_out):
    z = z_ref[...].astype(F32)
    u = (yf_ref[...].astype(F32) + yb_ref[...].astype(F32)) * (z * jax.nn.sigmoid(z))
    gw = D_SSD // SSD_GROUPS
    ys = []
    for g in range(SSD_GROUPS):
        ug = u[:, g * gw:(g + 1) * gw]
        ms = jnp.mean(ug * ug, axis=-1, keepdims=True)
        ys.append((ug * lax.rsqrt(ms + EPS) * snw_ref[:, g * gw:(g + 1) * gw]).astype(BF16))
    a = jnp.dot(jnp.concatenate(ys, axis=1), wbs_ref[...], preferred_element_type=F32)
    b = jnp.dot(yn_ref[...], wbn_ref[...], preferred_element_type=F32)
    mixed = (jax.nn.sigmoid(gs_ref[...].astype(F32)) * a + jax.nn.sigmoid(gn_ref[...].astype(F32)) * b)
    m = jnp.dot(mixed.astype(BF16), wo_ref[...], preferred_element_type=F32)
    nw = nw_ref[...]
    for s in range(nsub):
        r0, r1 = s * MOD_TILE, (s + 1) * MOD_TILE
        h = h_ref[r0:r1, :] + modt_ref[s, 2:3, :] * m[r0:r1, :]
        ho_ref[r0:r1, :] = h
        f = _rms_mod(h, nw, modt_ref[s, 3:4, :], modt_ref[s, 4:5, :])
        if row_tile_out:
            _store_tokens(f_ref, r0, f)
        else:
            f_ref[r0:r1, :] = f.astype(f_ref.dtype)


def _merge(p_main, yf, yb, y_na, h, wbs, wbn, wo, ssd_nw, nw2, modt, row_tile_out, tm=512):
    m, d = h.shape
    nsub = tm // MOD_TILE
    row = lambda i: (i, 0)
    const = lambda i: (0, 0)
    if row_tile_out:
        f_spec, f_shape = pl.BlockSpec((tm * ROW_SEG, LANE), row), jax.ShapeDtypeStruct((m * ROW_SEG, LANE), F32)
    else:
        f_spec, f_shape = pl.BlockSpec((tm, d), row), jax.ShapeDtypeStruct((m, d), BF16)
    return pl.pallas_call(
        functools.partial(_merge_kernel, nsub=nsub, row_tile_out=row_tile_out),
        grid=(m // tm,),
        in_specs=[pl.BlockSpec((tm, d), lambda i: (i, COL_GS // D_MODEL)),
                  pl.BlockSpec((tm, d), lambda i: (i, COL_GN // D_MODEL)),
                  pl.BlockSpec((tm, d), lambda i: (i, COL_Z // D_MODEL)),
                  pl.BlockSpec((tm, d), row), pl.BlockSpec((tm, d), row), pl.BlockSpec((tm, d), row),
                  pl.BlockSpec((tm, d), row),
                  pl.BlockSpec((d, d), const), pl.BlockSpec((d, d), const), pl.BlockSpec((d, d), const),
                  pl.BlockSpec((1, d), const), pl.BlockSpec((1, d), const),
                  pl.BlockSpec((nsub, 6, d), lambda i: (i, 0, 0))],
        out_specs=[pl.BlockSpec((tm, d), row), f_spec],
        out_shape=[jax.ShapeDtypeStruct((m, d), F32), f_shape],
        compiler_params=_params(("arbitrary",)),
        name="merge",
    )(p_main, p_main, p_main, yf, yb, y_na, h, wbs, wbn, wo, ssd_nw.astype(F32).reshape(1, d), nw2.reshape(1, d),
      modt)


def _ffn_kernel(x_ref, h_ref, w1_ref, w3_ref, w2_ref, nw_ref, modt_ref, modn_ref, ho_ref, an_ref, acc_ref,
                *, nsub):
    j = pl.program_id(1)

    @pl.when(j == 0)
    def _():
        acc_ref[...] = jnp.zeros_like(acc_ref)

    x = x_ref[...]
    h1 = jnp.dot(x, w1_ref[...], preferred_element_type=F32)
    h3 = jnp.dot(x, w3_ref[...], preferred_element_type=F32)
    act = (h1 * jax.nn.sigmoid(h1) * h3).astype(BF16)
    acc_ref[...] += jnp.dot(act, w2_ref[...], preferred_element_type=F32)

    @pl.when(j == pl.num_programs(1) - 1)
    def _():
        nw = nw_ref[...]
        for s in range(nsub):
            r0, r1 = s * MOD_TILE, (s + 1) * MOD_TILE
            h = h_ref[r0:r1, :] + modt_ref[s, 5:6, :] * acc_ref[r0:r1, :]
            ho_ref[r0:r1, :] = h
            an_ref[r0:r1, :] = _rms_mod(h, nw, modn_ref[s, 0:1, :], modn_ref[s, 1:2, :]).astype(an_ref.dtype)


def _ffn(x, h, w1, w3, w2, nw_next, modt, modt_next, tm=512, tf=1408):
    m, d = h.shape
    ff = w1.shape[1]
    nsub = tm // MOD_TILE
    row = lambda i, j: (i, 0)
    return pl.pallas_call(
        functools.partial(_ffn_kernel, nsub=nsub),
        grid=(m // tm, ff // tf),
        in_specs=[pl.BlockSpec((tm, d), row), pl.BlockSpec((tm, d), row),
                  pl.BlockSpec((d, tf), lambda i, j: (0, j)), pl.BlockSpec((d, tf), lambda i, j: (0, j)),
                  pl.BlockSpec((tf, d), lambda i, j: (j, 0)),
                  pl.BlockSpec((1, d), lambda i, j: (0, 0)),
                  pl.BlockSpec((nsub, 6, d), lambda i, j: (i, 0, 0)),
                  pl.BlockSpec((nsub, 6, d), lambda i, j: (i, 0, 0))],
        out_specs=[pl.BlockSpec((tm, d), row), pl.BlockSpec((tm, d), row)],
        out_shape=[jax.ShapeDtypeStruct((m, d), F32), jax.ShapeDtypeStruct((m, d), BF16)],
        scratch_shapes=[pltpu.VMEM((tm, d), F32)],
        compiler_params=_params(("arbitrary", "arbitrary")),
        name="ffn",
    )(x, h, w1, w3, w2, nw_next.reshape(1, d), modt, modt_next)


NA_BAND = NA_ROWS_MAX * GRID_W
HEAD_PAIR = LANE // NA_HEAD_DIM
NA_UNROLL = 8
NA_PREP_UNROLL = 8


def _head_mean_matrix():
    ri = lax.broadcasted_iota(jnp.int32, (2 * LANE, LANE), 0) % LANE
    ci = lax.broadcasted_iota(jnp.int32, (2 * LANE, LANE), 1)
    return jnp.where(ri // NA_HEAD_DIM == ci // NA_HEAD_DIM, 1.0 / NA_HEAD_DIM, 0.0).astype(BF16)


def _head_rms(x, w, mean_mat):
    sq = x * x
    hi = sq.astype(BF16)
    lo = (sq - hi.astype(F32)).astype(BF16)
    ms = jnp.dot(jnp.concatenate([hi, lo], axis=1), mean_mat, preferred_element_type=F32)
    return x * lax.rsqrt(ms + EPS) * w


def _rope(x, cos, sin, first_half):
    partner = jnp.where(first_half, pltpu.roll(x, LANE - 16, axis=1), pltpu.roll(x, 16, axis=1))
    return x * cos + partner * sin


def _stack_heads(x, lane_lo):
    zero = jnp.zeros_like(x)
    return jnp.concatenate([jnp.where(lane_lo, x, zero), jnp.where(lane_lo, zero, x)], axis=0)


def _attend(qs, k_list, v_list, bias_list, lane_lo):
    nt = (((1,), (1,)), ((), ()))
    s_list = []
    for k, bias in zip(k_list, bias_list):
        s = lax.dot_general(qs, k, nt, preferred_element_type=F32)
        s_list.append(s if bias is None else s + bias)
    mx = s_list[0].max(axis=-1, keepdims=True)
    for s in s_list[1:]:
        mx = jnp.maximum(mx, s.max(axis=-1, keepdims=True))
    den = None
    acc = None
    for s, v in zip(s_list, v_list):
        p = jnp.exp(s - mx)
        d = p.sum(axis=-1, keepdims=True)
        o = jnp.dot(p.astype(BF16), v, preferred_element_type=F32)
        den = d if den is None else den + d
        acc = o if acc is None else acc + o
    acc = acc / den
    n = acc.shape[0] // 2
    return jnp.where(lane_lo, acc[:n], acc[n:])


def _na_kernel(q_ref, k_ref, v_ref, bias_ref, rowc_ref, rows_ref, colc_ref, cols_ref, qw_ref, kw_ref,
               o_ref, qn_ref, kn_ref, s_ref, m_ref, *, clen, n_rows):
    lane = lax.broadcasted_iota(jnp.int32, (1, LANE), 1)
    lane_lo = lane < NA_HEAD_DIM
    first_half = (lane % 32) < 16
    qw = qw_ref[...] * (NA_HEAD_DIM ** -0.5)
    kw = kw_ref[...]
    kr = NA_ROWS_MAX

    mean_mat = _head_mean_matrix()
    qn_ref[0:clen, :] = _head_rms(q_ref[0:clen, :].astype(F32), qw, mean_mat).astype(BF16)
    kn_ref[0:clen, :] = _head_rms(k_ref[0:clen, :].astype(F32), kw, mean_mat).astype(BF16)
    colc = colc_ref[...]
    cols = cols_ref[...]

    def prep(r, carry):
        rows = pl.ds(pl.multiple_of(clen + r * GRID_W, GRID_W), GRID_W)
        cos = rowc_ref[pl.ds(r, 1), :] + colc
        sin = rows_ref[pl.ds(r, 1), :] + cols
        q = _head_rms(q_ref[rows, :].astype(F32), qw, mean_mat)
        k = _head_rms(k_ref[rows, :].astype(F32), kw, mean_mat)
        qn_ref[rows, :] = _rope(q, cos, sin, first_half).astype(BF16)
        kn_ref[rows, :] = _rope(k, cos, sin, first_half).astype(BF16)
        return carry

    lax.fori_loop(0, n_rows, prep, 0, unroll=NA_PREP_UNROLL)

    kc = kn_ref[0:clen, :]
    vc = v_ref[0:clen, :]
    for blk in range(clen // GRID_W):
        rows = slice(blk * GRID_W, (blk + 1) * GRID_W)
        qs = _stack_heads(qn_ref[rows, :], lane_lo)
        o_ref[rows, :] = _attend(qs, [kc], [vc], [None], lane_lo).astype(o_ref.dtype)

    nt = (((1,), (1,)), ((), ()))
    n_blk = (NA_BAND + clen) // LANE

    def band_of(r):
        r0 = jnp.clip(r - kr // 2, 0, n_rows - kr)
        return r0, pl.ds(pl.multiple_of(clen + r0 * GRID_W, GRID_W), NA_BAND)

    n_win = NA_BAND // LANE
    sq = HEAD_PAIR * GRID_W

    def scores(rows_r, slots):
        qs = [_stack_heads(qn_ref[pl.ds(pl.multiple_of(clen + r * GRID_W, GRID_W), GRID_W), :], lane_lo)
              for r in rows_r]
        s_ctx = lax.dot_general(jnp.concatenate(qs, axis=0), kn_ref[0:clen, :], nt, preferred_element_type=F32)
        for u, (r, slot) in enumerate(zip(rows_r, slots)):
            r0, band = band_of(r)
            s_win = lax.dot_general(qs[u], kn_ref[band, :], nt, preferred_element_type=F32)
            bias = bias_ref[r0 - r + (NA_ROWS_MAX - 1)]
            blocks = [s_win[:, j * LANE:(j + 1) * LANE] + bias[:, j * LANE:(j + 1) * LANE] for j in range(n_win)]
            blocks += [s_ctx[u * sq:(u + 1) * sq, j * LANE:(j + 1) * LANE] for j in range(n_blk - n_win)]
            mx = blocks[0]
            for j in range(n_blk):
                s_ref[slot, :, j * LANE:(j + 1) * LANE] = blocks[j]
                mx = jnp.maximum(mx, blocks[j])
            m_ref[slot] = jnp.broadcast_to(mx.max(axis=-1, keepdims=True), mx.shape)

    def output(rows_r, slots):
        accs, dens, p_ctx = [], [], []
        for r, slot in zip(rows_r, slots):
            _, band = band_of(r)
            mx = m_ref[slot]
            p = [jnp.exp(s_ref[slot, :, j * LANE:(j + 1) * LANE] - mx) for j in range(n_blk)]
            den = p[0]
            for pj in p[1:]:
                den = den + pj
            dens.append(den.sum(axis=-1, keepdims=True))
            pb = [pj.astype(BF16) for pj in p]
            accs.append(jnp.dot(jnp.concatenate(pb[:n_win], axis=1), v_ref[band, :], preferred_element_type=F32))
            p_ctx.append(jnp.concatenate(pb[n_win:], axis=1))
        acc_ctx = jnp.dot(jnp.concatenate(p_ctx, axis=0), v_ref[0:clen, :], preferred_element_type=F32)
        for u, r in enumerate(rows_r):
            acc = (accs[u] + acc_ctx[u * sq:(u + 1) * sq]) / dens[u]
            rows = pl.ds(pl.multiple_of(clen + r * GRID_W, GRID_W), GRID_W)
            o_ref[rows, :] = jnp.where(lane_lo, acc[:GRID_W], acc[GRID_W:]).astype(o_ref.dtype)

    scores(list(range(NA_UNROLL)), list(range(NA_UNROLL)))

    def body(j, carry):
        r = 2 * NA_UNROLL * j
        for half in range(2):
            output([r + half * NA_UNROLL + u for u in range(NA_UNROLL)],
                   [half * NA_UNROLL + u for u in range(NA_UNROLL)])
            scores([jnp.minimum(r + (half + 1) * NA_UNROLL + u, n_rows - 1) for u in range(NA_UNROLL)],
                   [(1 - half) * NA_UNROLL + u for u in range(NA_UNROLL)])
        return carry

    lax.fori_loop(0, n_rows // (2 * NA_UNROLL), body, 0)


def _na_tables(rpb, n_rows):
    col = jnp.arange(GRID_W)
    col_start = jnp.clip(col - NA_COLS // 2, 0, GRID_W - NA_COLS)
    col_ok = (col[None, :] >= col_start[:, None]) & (col[None, :] < col_start[:, None] + NA_COLS)
    col_idx = jnp.clip(col[None, :] - col[:, None], -(NA_COLS - 1), NA_COLS - 1) + NA_COLS - 1
    rpb_c = jnp.where(col_ok[None, None], rpb.astype(F32)[:, :, col_idx], NEG_INF)
    dr = jnp.arange(NA_ROWS_MAX)[:, None] + jnp.arange(NA_ROWS_MAX)[None, :]
    tab = rpb_c[:, dr]
    tab = jnp.transpose(tab, (0, 1, 3, 2, 4)).reshape(NA_HEADS, NA_ROWS_MAX, GRID_W, NA_BAND)
    tab = tab.reshape(NA_HEADS // HEAD_PAIR, HEAD_PAIR, NA_ROWS_MAX, GRID_W, NA_BAND)
    tab = jnp.transpose(tab, (0, 2, 1, 3, 4)).reshape(NA_HEADS // HEAD_PAIR, NA_ROWS_MAX,
                                                       HEAD_PAIR * GRID_W, NA_BAND)
    quarter = NA_HEAD_DIM // 4
    inv = ROPE_BASE ** (-jnp.arange(quarter, dtype=F32) / quarter)

    def tables(pos, lo):
        ang = pos[:, None] * inv[None, :]
        z = jnp.zeros_like(ang)
        c, s = jnp.cos(ang), jnp.sin(ang)
        ch = jnp.concatenate([c, c, z, z] if lo else [z, z, c, c], axis=-1)
        sh = jnp.concatenate([-s, s, z, z] if lo else [z, z, -s, s], axis=-1)
        return jnp.tile(ch, (1, HEAD_PAIR)), jnp.tile(sh, (1, HEAD_PAIR))

    rowc, rows = tables(jnp.arange(n_rows).astype(F32), True)
    colc, cols = tables(jnp.arange(GRID_W).astype(F32), False)
    return tab, rowc, rows, colc, cols


def _na(pm3, q_norm_w, k_norm_w, rpb, clen):
    bsz, t, _ = pm3.shape
    n_rows = (t - clen) // GRID_W
    tab, rowc, rows, colc, cols = _na_tables(rpb, n_rows)
    qw = jnp.tile(q_norm_w.astype(F32), HEAD_PAIR).reshape(1, LANE)
    kw = jnp.tile(k_norm_w.astype(F32), HEAD_PAIR).reshape(1, LANE)
    n_pairs = NA_HEADS // HEAD_PAIR
    const = lambda p, b: (0, 0)
    tok = lambda c0: pl.BlockSpec((None, t, LANE), lambda p, b: (b, 0, c0 // LANE + p))
    return pl.pallas_call(
        functools.partial(_na_kernel, clen=clen, n_rows=n_rows),
        grid=(n_pairs, bsz),
        in_specs=[tok(COL_Q), tok(COL_K), tok(COL_V),
                  pl.BlockSpec((None, NA_ROWS_MAX, HEAD_PAIR * GRID_W, NA_BAND), lambda p, b: (p, 0, 0, 0)),
                  pl.BlockSpec((n_rows, LANE), const), pl.BlockSpec((n_rows, LANE), const),
                  pl.BlockSpec((GRID_W, LANE), const), pl.BlockSpec((GRID_W, LANE), const),
                  pl.BlockSpec((1, LANE), const), pl.BlockSpec((1, LANE), const)],
        out_specs=pl.BlockSpec((None, t, LANE), lambda p, b: (b, 0, p)),
        out_shape=jax.ShapeDtypeStruct((bsz, t, D_NA), BF16),
        scratch_shapes=[pltpu.VMEM((t, LANE), BF16), pltpu.VMEM((t, LANE), BF16),
                        pltpu.VMEM((2 * NA_UNROLL, HEAD_PAIR * GRID_W, NA_BAND + clen), F32),
                        pltpu.VMEM((2 * NA_UNROLL, HEAD_PAIR * GRID_W, LANE), F32)],
        compiler_params=_params(("arbitrary", "arbitrary")),
        name="natten",
    )(pm3, pm3, pm3, tab, rowc, rows, colc, cols, qw, kw)


HALO = 16
CONV_PAD = SSD_CONV // 2
BC_DIM = SSD_GROUPS * SSD_STATE
PAIRS_PER_GROUP = SSD_HPG // HEAD_PAIR


def _ssd_chunk_index(s, rev, cc, nc):
    if not rev:
        return s
    return jnp.where(s < cc, cc - 1 - s, nc - 1 - (s - cc))


def _ssd_conv_kernel(main_ref, prev_ref, next_ref, cw_ref, cb_ref, o_ref, *, cc, nc):
    q = SSD_CHUNK
    cidx = pl.program_id(1)
    first_col = jnp.where((cidx == 0) | (cidx == cc), HALO, 0)
    end_col = jnp.where((cidx == cc - 1) | (cidx == nc - 1), HALO + q, q + 2 * HALO)
    ext = jnp.concatenate([prev_ref[...], main_ref[...], next_ref[...]], axis=0)
    ti = lax.broadcasted_iota(jnp.int32, (q, q + 2 * HALO), 0)
    ji = lax.broadcasted_iota(jnp.int32, (q, q + 2 * HALO), 1)
    in_segment = (ji >= first_col) & (ji < end_col)
    conv = cb_ref[...] + cw_ref[CONV_PAD:CONV_PAD + 1, :] * main_ref[...].astype(F32)
    for k in range(SSD_CONV):
        if k == CONV_PAD:
            continue
        shift = jnp.where((ji == ti + (HALO + k - CONV_PAD)) & in_segment, 1.0, 0.0).astype(BF16)
        conv = conv + cw_ref[k:k + 1, :] * jnp.dot(shift, ext, preferred_element_type=F32)
    o_ref[...] = (conv * jax.nn.sigmoid(conv)).astype(o_ref.dtype)


def _ssd_core(xbc_ref, dt_ref, dtb_ref, a_ref, exp_ref, h_ref, yoff_ref, y_ref, *, rev):
    q = SSD_CHUNK
    xbc = xbc_ref[...]

    hb = SSD_HEADS if rev else 0
    x = dt_ref[...] + dtb_ref[...]
    dt = jnp.maximum(x, 0.0) + jnp.log(1.0 + jnp.exp(-jnp.abs(x)))
    la = dt * a_ref[...]
    ri = lax.broadcasted_iota(jnp.int32, (q, q), 0)
    ci = lax.broadcasted_iota(jnp.int32, (q, q), 1)
    allowed = (ci >= ri) if rev else (ci <= ri)
    tri = jnp.where(allowed, 1.0, 0.0).astype(F32)
    a_cs = jnp.dot(tri, la, precision=HIGHEST, preferred_element_type=F32)
    last = 0 if rev else q - 1
    a_cs_t = a_cs.T
    e_all = jnp.exp(a_cs)
    src_t = a_cs_t - jnp.log(dt.T)
    wdt_t = jnp.exp(a_cs_t[:, last:last + 1] - src_t)
    neg_mask = jnp.where(allowed, 0.0, NEG_INF)
    blk = (last // 8) * 8
    e_last = jnp.exp(jnp.dot(a_cs[blk:blk + 8, :], exp_ref[...], precision=HIGHEST,
                             preferred_element_type=F32))[last - blk:last - blk + 1, hb * SSD_HEAD_DIM:
                                                          (hb + SSD_HEADS) * SSD_HEAD_DIM]

    lane_lo = lax.broadcasted_iota(jnp.int32, (1, LANE), 1) < SSD_HEAD_DIM
    nt = (((1,), (1,)), ((), ()))
    for g in range(SSD_GROUPS):
        b_g = xbc[:, D_SSD + g * SSD_STATE:D_SSD + (g + 1) * SSD_STATE]
        c_g = xbc[:, D_SSD + BC_DIM + g * SSD_STATE:D_SSD + BC_DIM + (g + 1) * SSD_STATE]
        cols = slice(g * SSD_HPG * SSD_HEAD_DIM, (g + 1) * SSD_HPG * SSD_HEAD_DIM)
        yoff_ref[:, cols] = jnp.dot(c_g, h_ref[:, cols].astype(BF16), preferred_element_type=F32)
        cb = lax.dot_general(c_g, b_g, nt, preferred_element_type=F32)
        b_t = b_g.astype(F32).T
        for pj in range(PAIRS_PER_GROUP):
            pair = g * PAIRS_PER_GROUP + pj
            lanes = slice(pair * LANE, (pair + 1) * LANE)
            lhs, ecol = [], []
            for hh in range(HEAD_PAIR):
                head = hb + pair * HEAD_PAIR + hh
                seg = (a_cs[:, head:head + 1] - src_t[head:head + 1, :]) + neg_mask
                lhs.append((cb * jnp.exp(seg)).astype(BF16))
                ecol.append(jnp.broadcast_to(e_all[:, head:head + 1], (q, LANE)))
            for hh in range(HEAD_PAIR):
                head = hb + pair * HEAD_PAIR + hh
                lhs.append((b_t * wdt_t[head:head + 1, :]).astype(BF16))
            res = jnp.dot(jnp.concatenate(lhs, axis=0), xbc[:, lanes], preferred_element_type=F32)
            y_ref[:, lanes] = (jnp.where(lane_lo, res[0:q], res[q:2 * q])
                               + yoff_ref[:, lanes] * jnp.where(lane_lo, ecol[0], ecol[1]))
            h_ref[:, lanes] = (h_ref[:, lanes] * e_last[:, lanes]
                               + jnp.where(lane_lo, res[2 * q:3 * q], res[3 * q:4 * q]))


def _ssd_scan_kernel(xf_ref, dtf_ref, xb_ref, dtb_ref, bias_ref, a_ref, exp_ref, dsum_ref, of_ref, ob_ref,
                     hf_ref, yofff_ref, hb_ref, yoffb_ref, yf_ref, yb_ref):
    @pl.when(pl.program_id(1) == 0)
    def _():
        hf_ref[...] = jnp.zeros_like(hf_ref)
        hb_ref[...] = jnp.zeros_like(hb_ref)

    _ssd_core(xf_ref, dtf_ref, bias_ref, a_ref, exp_ref, hf_ref, yofff_ref, yf_ref, rev=False)
    _ssd_core(xb_ref, dtb_ref, bias_ref, a_ref, exp_ref, hb_ref, yoffb_ref, yb_ref, rev=True)
    of_ref[...] = (yf_ref[...] + xf_ref[:, 0:D_SSD].astype(F32) * dsum_ref[...]).astype(of_ref.dtype)
    ob_ref[...] = yb_ref[...].astype(ob_ref.dtype)


def _ssd(pm3, pdt3, conv_w, conv_b, dt_bias, a_log, d_skip, clen):
    bsz, t, _ = pm3.shape
    q = SSD_CHUNK
    nc, cc = t // q, clen // q
    nh = 2 * SSD_HEADS
    cw = jnp.pad(conv_w.astype(F32), ((0, 8 - SSD_CONV), (0, 0)))
    cb = conv_b.astype(F32).reshape(1, XBC_DIM)
    dtb = jnp.pad(dt_bias.astype(F32).reshape(1, nh), ((0, 0), (0, DT_PAD - nh)))
    a_neg = jnp.pad(-jnp.exp(a_log.astype(F32)).reshape(1, nh), ((0, 0), (0, DT_PAD - nh)))
    dsum = jnp.repeat((d_skip[0] + d_skip[1]).astype(F32), SSD_HEAD_DIM).reshape(1, D_SSD)
    expand = jnp.repeat(jnp.eye(DT_PAD, dtype=F32)[:, :nh], SSD_HEAD_DIM, axis=1)
    hpb = q // HALO
    n_halo = t // HALO

    const = lambda b, s: (0, 0)
    xcol = COL_XBC // XBC_DIM
    xbc = pl.pallas_call(
        functools.partial(_ssd_conv_kernel, cc=cc, nc=nc),
        grid=(bsz, nc),
        in_specs=[pl.BlockSpec((None, q, XBC_DIM), lambda b, s: (b, s, xcol)),
                  pl.BlockSpec((None, HALO, XBC_DIM), lambda b, s: (b, jnp.maximum(s * hpb - 1, 0), xcol)),
                  pl.BlockSpec((None, HALO, XBC_DIM),
                               lambda b, s: (b, jnp.minimum((s + 1) * hpb, n_halo - 1), xcol)),
                  pl.BlockSpec((8, XBC_DIM), const), pl.BlockSpec((1, XBC_DIM), const)],
        out_specs=pl.BlockSpec((None, q, XBC_DIM), lambda b, s: (b, s, 0)),
        out_shape=jax.ShapeDtypeStruct((bsz, t, XBC_DIM), BF16),
        compiler_params=_params(("arbitrary", "arbitrary")),
        name="ssd_conv",
    )(pm3, pm3, pm3, cw, cb)

    tok_f = lambda b, s: (b, s, 0)
    tok_b = lambda b, s: (b, _ssd_chunk_index(s, True, cc, nc), 0)
    state = [pltpu.VMEM((SSD_STATE, D_SSD), F32), pltpu.VMEM((q, D_SSD), F32)]
    return pl.pallas_call(
        _ssd_scan_kernel,
        grid=(bsz, nc),
        in_specs=[pl.BlockSpec((None, q, XBC_DIM), tok_f), pl.BlockSpec((None, q, DT_PAD), tok_f),
                  pl.BlockSpec((None, q, XBC_DIM), tok_b), pl.BlockSpec((None, q, DT_PAD), tok_b),
                  pl.BlockSpec((1, DT_PAD), const), pl.BlockSpec((1, DT_PAD), const),
                  pl.BlockSpec((DT_PAD, nh * SSD_HEAD_DIM), const), pl.BlockSpec((1, D_SSD), const)],
        out_specs=[pl.BlockSpec((None, q, D_SSD), tok_f), pl.BlockSpec((None, q, D_SSD), tok_b)],
        out_shape=[jax.ShapeDtypeStruct((bsz, t, D_SSD), BF16), jax.ShapeDtypeStruct((bsz, t, D_SSD), BF16)],
        scratch_shapes=state + state + [pltpu.VMEM((q, D_SSD), F32), pltpu.VMEM((q, D_SSD), F32)],
        compiler_params=_params(("arbitrary", "arbitrary")),
        name="ssd_scan",
    )(xbc, pdt3, xbc, pdt3, dtb, a_neg, expand, dsum)


MOE_TB = 512
ROUTE_TM = 512
INFO_W = 8
BIG_NEG = -3.0e38
ROW_SEG = D_MODEL // LANE


def _load_tokens(ref, t0, n):
    return jnp.concatenate([ref[pl.ds(t0 * ROW_SEG + c, n, stride=ROW_SEG), :] for c in range(ROW_SEG)], axis=1)


def _store_tokens(ref, t0, val):
    n = val.shape[0]
    for c in range(ROW_SEG):
        ref[pl.ds(t0 * ROW_SEG + c, n, stride=ROW_SEG), :] = val[:, c * LANE:(c + 1) * LANE]


def _route_kernel(f_ref, wr_ref, info_ref, cnt_ref, run_ref):
    tm = f_ref.shape[0] // ROW_SEG

    @pl.when(pl.program_id(0) == 0)
    def _():
        run_ref[...] = jnp.zeros_like(run_ref)

    logits = jnp.dot(_load_tokens(f_ref, 0, tm).astype(BF16), wr_ref[...], preferred_element_type=F32)
    lane = lax.broadcasted_iota(jnp.int32, (tm, LANE), 1).astype(F32)
    lg = jnp.where(lane < N_EXPERTS, logits, BIG_NEG)
    m1 = lg.max(axis=-1, keepdims=True)
    i1 = jnp.where(lg == m1, lane, float(LANE)).min(axis=-1, keepdims=True)
    lg2 = jnp.where(lane == i1, BIG_NEG, lg)
    m2 = lg2.max(axis=-1, keepdims=True)
    i2 = jnp.where(lg2 == m2, lane, float(LANE)).min(axis=-1, keepdims=True)
    ex = jnp.exp(m2 - m1)
    g0 = 1.0 / (1.0 + ex)
    g1 = ex / (1.0 + ex)
    sel0 = lane == i1
    sel1 = lane == i2
    onehot = jnp.where(sel0 | sel1, 1.0, 0.0)
    ri = lax.broadcasted_iota(jnp.int32, (tm, tm), 0)
    ci = lax.broadcasted_iota(jnp.int32, (tm, tm), 1)
    strict_lower = jnp.where(ci < ri, 1.0, 0.0).astype(BF16)
    before = jnp.dot(strict_lower, onehot.astype(BF16), preferred_element_type=F32) + run_ref[...]
    pos0 = jnp.where(sel0, before, 0.0).sum(axis=-1, keepdims=True)
    pos1 = jnp.where(sel1, before, 0.0).sum(axis=-1, keepdims=True)
    run_ref[...] = run_ref[...] + onehot.sum(axis=0, keepdims=True)
    cnt_ref[...] = run_ref[...]
    info = jnp.where(lane == 0, i1, jnp.where(lane == 1, i2, jnp.where(lane == 2, g0, jnp.where(
        lane == 3, g1, jnp.where(lane == 4, pos0, jnp.where(lane == 5, pos1, 0.0))))))
    info_ref[...] = info[:, :INFO_W]


def _route(f8, w_router):
    m, d = f8.shape[0] // ROW_SEG, D_MODEL
    wr = jnp.pad(w_router.astype(BF16), ((0, 0), (0, LANE - N_EXPERTS)))
    return pl.pallas_call(
        _route_kernel,
        grid=(m // ROUTE_TM,),
        in_specs=[pl.BlockSpec((ROUTE_TM * ROW_SEG, LANE), lambda i: (i, 0)),
                  pl.BlockSpec((d, LANE), lambda i: (0, 0))],
        out_specs=[pl.BlockSpec((ROUTE_TM, INFO_W), lambda i: (i, 0)), pl.BlockSpec((1, LANE), lambda i: (0, 0))],
        out_shape=[jax.ShapeDtypeStruct((m, INFO_W), F32), jax.ShapeDtypeStruct((1, LANE), F32)],
        scratch_shapes=[pltpu.VMEM((1, LANE), F32)],
        compiler_params=_params(("arbitrary",)),
        name="moe_route",
    )(f8, wr)


def _row_copy(src_ref, src_row, dst_ref, dst_row, sem):
    src = src_ref.at[pl.ds(pl.multiple_of(src_row * ROW_SEG, ROW_SEG), ROW_SEG)]
    dst = dst_ref.at[pl.ds(pl.multiple_of(dst_row * ROW_SEG, ROW_SEG), ROW_SEG)]
    return pltpu.make_async_copy(src, dst, sem)


def _wait_rows(src_ref, dst_ref, sem, n):
    def wait(t, c):
        _row_copy(src_ref, 0, dst_ref, 0, sem).wait()
        return c

    lax.fori_loop(0, n, wait, 0, unroll=8)


def _dispatch_kernel(pad_ref, dest_hbm, f_ref, rows_hbm, dest_smem, stage_ref, zero_ref, idx_sem, row_sem,
                     pad_sem):
    tm = f_ref.shape[0] // ROW_SEG
    i, n = pl.program_id(0), pl.num_programs(0)
    slot = i % 2
    stage = stage_ref.at[slot]
    sem = row_sem.at[slot]

    @pl.when(i == 0)
    def _():
        zero_ref[...] = jnp.zeros_like(zero_ref)
        for e in range(N_EXPERTS):
            first, count = pad_ref[e], pad_ref[N_EXPERTS + e]

            def start_zero(j, c):
                _row_copy(zero_ref, 0, rows_hbm, first + j, pad_sem).start()
                return c

            lax.fori_loop(0, count, start_zero, 0)
        for e in range(N_EXPERTS):
            def wait_zero(j, c):
                _row_copy(zero_ref, 0, rows_hbm, 0, pad_sem).wait()
                return c

            lax.fori_loop(0, pad_ref[N_EXPERTS + e], wait_zero, 0)

    @pl.when(i >= 2)
    def _():
        _wait_rows(stage, rows_hbm, sem, TOP_K * tm)

    stage[...] = f_ref[...]
    base = pl.multiple_of(i * (TOP_K * tm), TOP_K * tm)
    idx_copy = pltpu.make_async_copy(dest_hbm.at[pl.ds(base, TOP_K * tm)], dest_smem, idx_sem)
    idx_copy.start()
    idx_copy.wait()

    def start(t, c):
        for k in range(TOP_K):
            _row_copy(stage, t, rows_hbm, dest_smem[TOP_K * t + k], sem).start()
        return c

    lax.fori_loop(0, tm, start, 0, unroll=8)

    @pl.when(i == n - 1)
    def _():
        _wait_rows(stage, rows_hbm, sem, TOP_K * tm)

        @pl.when(n >= 2)
        def _():
            _wait_rows(stage_ref.at[1 - slot], rows_hbm, row_sem.at[1 - slot], TOP_K * tm)


def _dispatch(f8, dest_flat, pad_info, n_rows):
    m = f8.shape[0] // ROW_SEG
    grid_spec = pltpu.PrefetchScalarGridSpec(
        num_scalar_prefetch=1,
        grid=(m // ROUTE_TM,),
        in_specs=[pl.BlockSpec(memory_space=pl.ANY),
                  pl.BlockSpec((ROUTE_TM * ROW_SEG, LANE), lambda i, pad: (i, 0))],
        out_specs=pl.BlockSpec(memory_space=pl.ANY),
        scratch_shapes=[pltpu.SMEM((TOP_K * ROUTE_TM,), jnp.int32),
                        pltpu.VMEM((2, ROUTE_TM * ROW_SEG, LANE), f8.dtype),
                        pltpu.VMEM((ROW_SEG, LANE), f8.dtype),
                        pltpu.SemaphoreType.DMA(()), pltpu.SemaphoreType.DMA((2,)), pltpu.SemaphoreType.DMA(())],
    )
    return pl.pallas_call(
        _dispatch_kernel,
        grid_spec=grid_spec,
        out_shape=jax.ShapeDtypeStruct((n_rows * ROW_SEG, LANE), f8.dtype),
        compiler_params=_params(("arbitrary",)),
        name="moe_dispatch",
    )(pad_info, dest_flat, f8)


def _gffn_kernel(be_ref, nu_ref, x_ref, w1_ref, w3_ref, w2_ref, y_ref, acc_ref):
    i, j = pl.program_id(0), pl.program_id(1)

    @pl.when(i < nu_ref[0])
    def _():
        @pl.when(j == 0)
        def _():
            acc_ref[...] = jnp.zeros_like(acc_ref)

        x = _load_tokens(x_ref, 0, MOE_TB).astype(BF16)
        h1 = jnp.dot(x, w1_ref[...], preferred_element_type=F32)
        h3 = jnp.dot(x, w3_ref[...], preferred_element_type=F32)
        act = (h1 * jax.nn.sigmoid(h1) * h3).astype(BF16)
        acc_ref[...] += jnp.dot(act, w2_ref[...], preferred_element_type=F32)

        @pl.when(j == pl.num_programs(1) - 1)
        def _():
            _store_tokens(y_ref, 0, acc_ref[...])

    @pl.when((i >= nu_ref[0]) & (j == pl.num_programs(1) - 1))
    def _():
        y_ref[...] = jnp.zeros_like(y_ref)


def _gffn(rows, block_e, n_used, w1, w3, w2, tf=1792):
    r, d = rows.shape[0] // ROW_SEG, D_MODEL
    ff = w1.shape[2]
    nb = r // MOE_TB
    grid_spec = pltpu.PrefetchScalarGridSpec(
        num_scalar_prefetch=2,
        grid=(nb, ff // tf),
        in_specs=[pl.BlockSpec((MOE_TB * ROW_SEG, LANE), lambda i, j, be, nu: (i, 0)),
                  pl.BlockSpec((None, d, tf), lambda i, j, be, nu: (be[i], 0, j)),
                  pl.BlockSpec((None, d, tf), lambda i, j, be, nu: (be[i], 0, j)),
                  pl.BlockSpec((None, tf, d), lambda i, j, be, nu: (be[i], j, 0))],
        out_specs=pl.BlockSpec((MOE_TB * ROW_SEG, LANE), lambda i, j, be, nu: (i, 0)),
        scratch_shapes=[pltpu.VMEM((MOE_TB, d), F32)],
    )
    return pl.pallas_call(
        _gffn_kernel,
        grid_spec=grid_spec,
        out_shape=jax.ShapeDtypeStruct((r * ROW_SEG, LANE), F32),
        compiler_params=_params(("arbitrary", "arbitrary")),
        name="moe_ffn",
    )(block_e, n_used, rows, w1, w3, w2)


def _combine_kernel(dest_hbm, y_hbm, h_ref, info_ref, modt_ref, o_ref, dest_smem, buf_ref, idx_sem, row_sem,
                    *, nsub):
    tm = h_ref.shape[0]
    i, n = pl.program_id(0), pl.num_programs(0)

    def gather(step, slot):
        base = pl.multiple_of(step * (TOP_K * tm), TOP_K * tm)
        idx_copy = pltpu.make_async_copy(dest_hbm.at[pl.ds(base, TOP_K * tm)], dest_smem, idx_sem)
        idx_copy.start()
        idx_copy.wait()

        def start(t, c):
            for k in range(TOP_K):
                _row_copy(y_hbm, dest_smem[TOP_K * t + k], buf_ref.at[slot, k], t, row_sem.at[slot]).start()
            return c

        lax.fori_loop(0, tm, start, 0, unroll=8)

    @pl.when(i == 0)
    def _():
        gather(0, 0)

    @pl.when(i + 1 < n)
    def _():
        gather(i + 1, (i + 1) % 2)

    slot = i % 2
    _wait_rows(y_hbm, buf_ref.at[slot, 0], row_sem.at[slot], TOP_K * tm)
    for s in range(nsub):
        r0, r1 = s * MOD_TILE, (s + 1) * MOD_TILE
        info = info_ref[r0:r1, :]
        mix = (info[:, 2:3] * _load_tokens(buf_ref.at[slot, 0], r0, MOD_TILE)
               + info[:, 3:4] * _load_tokens(buf_ref.at[slot, 1], r0, MOD_TILE))
        o_ref[r0:r1, :] = h_ref[r0:r1, :] + modt_ref[s, 5:6, :] * mix


def _combine(y_rows, dest_flat, h, info, modt):
    m, d = h.shape
    nsub = ROUTE_TM // MOD_TILE
    return pl.pallas_call(
        functools.partial(_combine_kernel, nsub=nsub),
        grid=(m // ROUTE_TM,),
        in_specs=[pl.BlockSpec(memory_space=pl.ANY), pl.BlockSpec(memory_space=pl.ANY),
                  pl.BlockSpec((ROUTE_TM, d), lambda i: (i, 0)),
                  pl.BlockSpec((ROUTE_TM, INFO_W), lambda i: (i, 0)),
                  pl.BlockSpec((nsub, 6, d), lambda i: (i, 0, 0))],
        out_specs=pl.BlockSpec((ROUTE_TM, d), lambda i: (i, 0)),
        out_shape=jax.ShapeDtypeStruct((m, d), F32),
        scratch_shapes=[pltpu.SMEM((TOP_K * ROUTE_TM,), jnp.int32),
                        pltpu.VMEM((2, TOP_K, ROUTE_TM * ROW_SEG, LANE), F32),
                        pltpu.SemaphoreType.DMA(()), pltpu.SemaphoreType.DMA((2,))],
        compiler_params=_params(("arbitrary",)),
        name="moe_combine",
    )(dest_flat, y_rows, h, info, modt)


def _moe(f8, h, w_router, w1, w3, w2, modt):
    m, d = h.shape
    info, cnt = _route(f8, w_router)
    e = info[:, 0:TOP_K].astype(jnp.int32)
    pos = info[:, 4:4 + TOP_K].astype(jnp.int32)
    counts = cnt[0, :N_EXPERTS].astype(jnp.int32)
    padded = (counts + MOE_TB - 1) // MOE_TB * MOE_TB
    pad_end = jnp.cumsum(padded)
    pad_start = pad_end - padded
    dest_flat = (pad_start[e] + pos).reshape(-1)
    n_rows = m * TOP_K + N_EXPERTS * MOE_TB
    nb = n_rows // MOE_TB
    block_e = jnp.minimum(jnp.searchsorted(pad_end, jnp.arange(nb, dtype=jnp.int32) * MOE_TB, side='right'),
                          N_EXPERTS - 1).astype(jnp.int32)
    n_used = (pad_end[-1:] // MOE_TB).astype(jnp.int32)
    pad_info = jnp.concatenate([pad_start + counts, padded - counts]).astype(jnp.int32)
    rows = _dispatch(f8, dest_flat, pad_info, n_rows)
    y_rows = _gffn(rows, block_e, n_used, w1.astype(BF16), w3.astype(BF16), w2.astype(BF16))
    return _combine(y_rows, dest_flat, h, info, modt)


def _tile_mod(mod_l, bsz, tiles_per_batch):
    t = jnp.arange(bsz * tiles_per_batch)
    idx = jnp.where(t % tiles_per_batch == 0, bsz, t // tiles_per_batch)
    return jnp.transpose(mod_l[:, idx, :], (1, 0, 2))


def kernel(x, c, ctx, c_ctx, w_mod, b_mod, norm1_w, norm2_w, w_in, conv_w, conv_b, dt_bias, a_log, d_skip,
           ssd_norm_w, q_norm_w, k_norm_w, rpb, w_br_ssd, w_br_na, w_out, w_ff1, w_ff3, w_ff2, w_router,
           w_e1, w_e3, w_e2):
    bsz, seqlen, d = x.shape
    clen = ctx.shape[1]
    n_layers = w_mod.shape[0]
    assert clen == MOD_TILE and seqlen % MOD_TILE == 0 and bsz < 16
    t = clen + seqlen
    m = bsz * t
    tiles_per_batch = t // MOD_TILE

    cond = jnp.zeros((16, d), F32).at[:bsz].set(c).at[bsz].set(c_ctx)
    mod = _modulation(cond, w_mod, b_mod)
    modt = [_tile_mod(mod[i], bsz, tiles_per_batch) for i in range(n_layers)]

    h = jnp.concatenate([ctx, x], axis=1).reshape(m, d)
    a = _adaln(h, norm1_w[0], modt[0])
    for i in range(n_layers):
        last = i == n_layers - 1
        wi = w_in[i]
        w_main = jnp.concatenate([wi[:, :IN_SPLITS[3]], wi[:, IN_SPLITS[4]:]], axis=1).astype(BF16)
        w_dt = jnp.pad(wi[:, IN_SPLITS[3]:IN_SPLITS[4]], ((0, 0), (0, DT_PAD - 2 * SSD_HEADS))).astype(BF16)
        p_main = _matmul(a, w_main, BF16, 512, 2560, "inproj")
        p_dt = _matmul(a, w_dt, F32, 1024, DT_PAD, "inproj_dt")
        pm3 = p_main.reshape(bsz, t, MAIN_DIM)
        yf, yb = _ssd(pm3, p_dt.reshape(bsz, t, DT_PAD), conv_w[i], conv_b[i], dt_bias[i], a_log[i],
                      d_skip[i], clen)
        y_na = _na(pm3, q_norm_w[i], k_norm_w[i], rpb[i], clen).reshape(m, d)
        fi = i // 2
        moe_layer = i % 2 == 1
        h, f = _merge(p_main, yf.reshape(m, d), yb.reshape(m, d), y_na, h, w_br_ssd[i].astype(BF16),
                      w_br_na[i].astype(BF16), w_out[i].astype(BF16), ssd_norm_w[i], norm2_w[i], modt[i],
                      row_tile_out=moe_layer)
        nxt = min(i + 1, n_layers - 1)
        if not moe_layer:
            h, a = _ffn(f, h, w_ff1[fi].astype(BF16), w_ff3[fi].astype(BF16), w_ff2[fi].astype(BF16),
                        norm1_w[nxt], modt[i], modt[nxt])
        else:
            h = _moe(f, h, w_router[fi], w_e1[fi], w_e3[fi], w_e2[fi], modt[i])
            if not last:
                a = _adaln(h, norm1_w[nxt], modt[nxt])
    return h.reshape(bsz, t, d)[:, clen:]
```

```python
import functools
import math

import jax
import jax.numpy as jnp
from jax import lax
from jax.experimental import pallas as pl
from jax.experimental.pallas import tpu as pltpu

D_MODEL = 1024
GRID_W = 64
SSD_HEAD_DIM = 64
SSD_HEADS = D_MODEL // SSD_HEAD_DIM
D_SSD = SSD_HEADS * SSD_HEAD_DIM
SSD_GROUPS = 2
SSD_HPG = SSD_HEADS // SSD_GROUPS
SSD_STATE = 128
SSD_CONV = 5
SSD_CHUNK = 128
XBC_DIM = D_SSD + 2 * SSD_GROUPS * SSD_STATE
NA_HEAD_DIM = 64
NA_HEADS = D_MODEL // NA_HEAD_DIM
D_NA = NA_HEADS * NA_HEAD_DIM
NA_ROWS_MAX = 8
NA_COLS = 16
ROPE_BASE = 10000.0
N_EXPERTS = 8
TOP_K = 2
EPS = 1e-6
NEG_INF = -1e30
IN_SPLITS = (D_MODEL, 2 * D_MODEL, 2 * D_MODEL + D_SSD, 2 * D_MODEL + D_SSD + XBC_DIM,
             2 * D_MODEL + D_SSD + XBC_DIM + 2 * SSD_HEADS,
             2 * D_MODEL + D_SSD + XBC_DIM + 2 * SSD_HEADS + D_NA,
             2 * D_MODEL + D_SSD + XBC_DIM + 2 * SSD_HEADS + 2 * D_NA)

F32 = jnp.float32
BF16 = jnp.bfloat16
HIGHEST = lax.Precision.HIGHEST

MOD_TILE = 256
LANE = 128
VMEM_LIMIT = 48 * 1024 * 1024
COL_GS, COL_GN, COL_Z, COL_XBC, COL_Q, COL_K, COL_V = 0, 1024, 2048, 3072, 4608, 5632, 6656
MAIN_DIM = 7680
DT_PAD = LANE


def _params(sem):
    return pltpu.CompilerParams(dimension_semantics=sem, vmem_limit_bytes=VMEM_LIMIT)


def _rms_mod(x, nw, shift, scale):
    ms = jnp.mean(x * x, axis=-1, keepdims=True)
    return (x * lax.rsqrt(ms + EPS) * nw) * (1.0 + scale) + shift


def _mod_kernel(cond_ref, w_ref, b_ref, o_ref):
    c = cond_ref[...]
    s = c * jax.nn.sigmoid(c)
    o_ref[...] = jnp.dot(s, w_ref[...], precision=HIGHEST, preferred_element_type=F32) + b_ref[...]


def _modulation(cond, w_mod, b_mod):
    n_layers, d, _ = w_mod.shape
    rows = cond.shape[0]
    b4 = b_mod.reshape(n_layers, 6, 1, d)
    return pl.pallas_call(
        _mod_kernel,
        grid=(n_layers, 6),
        in_specs=[pl.BlockSpec((rows, d), lambda l, k: (0, 0)),
                  pl.BlockSpec((None, d, d), lambda l, k: (l, 0, k)),
                  pl.BlockSpec((None, None, 1, d), lambda l, k: (l, k, 0, 0))],
        out_specs=pl.BlockSpec((None, None, rows, d), lambda l, k: (l, k, 0, 0)),
        out_shape=jax.ShapeDtypeStruct((n_layers, 6, rows, d), F32),
        compiler_params=_params(("arbitrary", "arbitrary")),
        name="modulation",
    )(cond, w_mod, b4)


def _adaln_kernel(h_ref, nw_ref, modt_ref, o_ref, *, nsub):
    nw = nw_ref[...]
    for s in range(nsub):
        rows = pl.ds(s * MOD_TILE, MOD_TILE)
        x = h_ref[rows, :]
        o_ref[rows, :] = _rms_mod(x, nw, modt_ref[s, 0:1, :], modt_ref[s, 1:2, :]).astype(o_ref.dtype)


def _adaln(h, nw, modt, tm=1024):
    m, d = h.shape
    nsub = tm // MOD_TILE
    return pl.pallas_call(
        functools.partial(_adaln_kernel, nsub=nsub),
        grid=(m // tm,),
        in_specs=[pl.BlockSpec((tm, d), lambda i: (i, 0)),
                  pl.BlockSpec((1, d), lambda i: (0, 0)),
                  pl.BlockSpec((nsub, 6, d), lambda i: (i, 0, 0))],
        out_specs=pl.BlockSpec((tm, d), lambda i: (i, 0)),
        out_shape=jax.ShapeDtypeStruct((m, d), BF16),
        compiler_params=_params(("arbitrary",)),
        name="adaln",
    )(h, nw.reshape(1, d), modt)


def _matmul_kernel(a_ref, w_ref, o_ref):
    o_ref[...] = jnp.dot(a_ref[...], w_ref[...], preferred_element_type=F32).astype(o_ref.dtype)


def _matmul(a, w, out_dtype, tm, tn, name):
    m, k = a.shape
    n = w.shape[1]
    return pl.pallas_call(
        _matmul_kernel,
        grid=(n // tn, m // tm),
        in_specs=[pl.BlockSpec((tm, k), lambda j, i: (i, 0)),
                  pl.BlockSpec((k, tn), lambda j, i: (0, j))],
        out_specs=pl.BlockSpec((tm, tn), lambda j, i: (i, j)),
        out_shape=jax.ShapeDtypeStruct((m, n), out_dtype),
        compiler_params=_params(("arbitrary", "arbitrary")),
        name=name,
    )(a, w)


def _merge_kernel(gs_ref, gn_ref, z_ref, yf_ref, yb_ref, yn_ref, h_ref, wbs_ref, wbn_ref, wo_ref, snw_ref,
                  nw_ref, modt_ref, ho_ref, f_ref, *, nsub, row_tile_out):
    z = z_ref[...].astype(F32)
    u = (yf_ref[...].astype(F32) + yb_ref[...].astype(F32)) * (z * jax.nn.sigmoid(z))
    gw = D_SSD // SSD_GROUPS
    ys = []
    for g in range(SSD_GROUPS):
        ug = u[:, g * gw:(g + 1) * gw]
        ms = jnp.mean(ug * ug, axis=-1, keepdims=True)
        ys.append((ug * lax.rsqrt(ms + EPS) * snw_ref[:, g * gw:(g + 1) * gw]).astype(BF16))
    a = jnp.dot(jnp.concatenate(ys, axis=1), wbs_ref[...], preferred_element_type=F32)
    b = jnp.dot(yn_ref[...], wbn_ref[...], preferred_element_type=F32)
    mixed = (jax.nn.sigmoid(gs_ref[...].astype(F32)) * a + jax.nn.sigmoid(gn_ref[...].astype(F32)) * b)
    m = jnp.dot(mixed.astype(BF16), wo_ref[...], preferred_element_type=F32)
    nw = nw_ref[...]
    for s in range(nsub):
        r0, r1 = s * MOD_TILE, (s + 1) * MOD_TILE
        h = h_ref[r0:r1, :] + modt_ref[s, 2:3, :] * m[r0:r1, :]
        ho_ref[r0:r1, :] = h
        f = _rms_mod(h, nw, modt_ref[s, 3:4, :], modt_ref[s, 4:5, :])
        if row_tile_out:
            _store_tokens(f_ref, r0, f)
        else:
            f_ref[r0:r1, :] = f.astype(f_ref.dtype)


def _merge(p_main, yf, yb, y_na, h, wbs, wbn, wo, ssd_nw, nw2, modt, row_tile_out, tm=512):
    m, d = h.shape
    nsub = tm // MOD_TILE
    row = lambda i: (i, 0)
    const = lambda i: (0, 0)
    if row_tile_out:
        f_spec, f_shape = pl.BlockSpec((tm * ROW_SEG, LANE), row), jax.ShapeDtypeStruct((m * ROW_SEG, LANE), F32)
    else:
        f_spec, f_shape = pl.BlockSpec((tm, d), row), jax.ShapeDtypeStruct((m, d), BF16)
    return pl.pallas_call(
        functools.partial(_merge_kernel, nsub=nsub, row_tile_out=row_tile_out),
        grid=(m // tm,),
        in_specs=[pl.BlockSpec((tm, d), lambda i: (i, COL_GS // D_MODEL)),
                  pl.BlockSpec((tm, d), lambda i: (i, COL_GN // D_MODEL)),
                  pl.BlockSpec((tm, d), lambda i: (i, COL_Z // D_MODEL)),
                  pl.BlockSpec((tm, d), row), pl.BlockSpec((tm, d), row), pl.BlockSpec((tm, d), row),
                  pl.BlockSpec((tm, d), row),
                  pl.BlockSpec((d, d), const), pl.BlockSpec((d, d), const), pl.BlockSpec((d, d), const),
                  pl.BlockSpec((1, d), const), pl.BlockSpec((1, d), const),
                  pl.BlockSpec((nsub, 6, d), lambda i: (i, 0, 0))],
        out_specs=[pl.BlockSpec((tm, d), row), f_spec],
        out_shape=[jax.ShapeDtypeStruct((m, d), F32), f_shape],
        compiler_params=_params(("arbitrary",)),
        name="merge",
    )(p_main, p_main, p_main, yf, yb, y_na, h, wbs, wbn, wo, ssd_nw.astype(F32).reshape(1, d), nw2.reshape(1, d),
      modt)


def _ffn_kernel(x_ref, h_ref, w1_ref, w3_ref, w2_ref, nw_ref, modt_ref, modn_ref, ho_ref, an_ref, *, nsub):
    x = x_ref[...]
    h1 = jnp.dot(x, w1_ref[...], preferred_element_type=F32)
    h3 = jnp.dot(x, w3_ref[...], preferred_element_type=F32)
    act = (h1 * jax.nn.sigmoid(h1) * h3).astype(BF16)
    out = jnp.dot(act, w2_ref[...], preferred_element_type=F32)
    nw = nw_ref[...]
    for s in range(nsub):
        r0, r1 = s * MOD_TILE, (s + 1) * MOD_TILE
        h = h_ref[r0:r1, :] + modt_ref[s, 5:6, :] * out[r0:r1, :]
        ho_ref[r0:r1, :] = h
        an_ref[r0:r1, :] = _rms_mod(h, nw, modn_ref[s, 0:1, :], modn_ref[s, 1:2, :]).astype(an_ref.dtype)


def _ffn(x, h, w1, w3, w2, nw_next, modt, modt_next, tm=512):
    m, d = h.shape
    ff = w1.shape[1]
    nsub = tm // MOD_TILE
    row = lambda i: (i, 0)
    const = lambda i: (0, 0)
    once = pl.Buffered(1)
    return pl.pallas_call(
        functools.partial(_ffn_kernel, nsub=nsub),
        grid=(m // tm,),
        in_specs=[pl.BlockSpec((tm, d), row), pl.BlockSpec((tm, d), row),
                  pl.BlockSpec((d, ff), const, pipeline_mode=once),
                  pl.BlockSpec((d, ff), const, pipeline_mode=once),
                  pl.BlockSpec((ff, d), const, pipeline_mode=once),
                  pl.BlockSpec((1, d), const),
                  pl.BlockSpec((nsub, 6, d), lambda i: (i, 0, 0)),
                  pl.BlockSpec((nsub, 6, d), lambda i: (i, 0, 0))],
        out_specs=[pl.BlockSpec((tm, d), row), pl.BlockSpec((tm, d), row)],
        out_shape=[jax.ShapeDtypeStruct((m, d), F32), jax.ShapeDtypeStruct((m, d), BF16)],
        compiler_params=_params(("arbitrary",)),
        name="ffn",
    )(x, h, w1, w3, w2, nw_next.reshape(1, d), modt, modt_next)


NA_BAND = NA_ROWS_MAX * GRID_W
HEAD_PAIR = LANE // NA_HEAD_DIM
NA_UNROLL = 8
NA_PREP_UNROLL = 8


def _head_mean_matrix():
    ri = lax.broadcasted_iota(jnp.int32, (2 * LANE, LANE), 0) % LANE
    ci = lax.broadcasted_iota(jnp.int32, (2 * LANE, LANE), 1)
    return jnp.where(ri // NA_HEAD_DIM == ci // NA_HEAD_DIM, 1.0 / NA_HEAD_DIM, 0.0).astype(BF16)


def _head_rms(x, w, mean_mat):
    sq = x * x
    hi = sq.astype(BF16)
    lo = (sq - hi.astype(F32)).astype(BF16)
    ms = jnp.dot(jnp.concatenate([hi, lo], axis=1), mean_mat, preferred_element_type=F32)
    return x * lax.rsqrt(ms + EPS) * w


def _rope(x, cos, sin, first_half):
    partner = jnp.where(first_half, pltpu.roll(x, LANE - 16, axis=1), pltpu.roll(x, 16, axis=1))
    return x * cos + partner * sin


def _stack_heads(x, lane_lo):
    zero = jnp.zeros_like(x)
    return jnp.concatenate([jnp.where(lane_lo, x, zero), jnp.where(lane_lo, zero, x)], axis=0)


def _attend(qs, k_list, v_list, bias_list, lane_lo):
    nt = (((1,), (1,)), ((), ()))
    s_list = []
    for k, bias in zip(k_list, bias_list):
        s = lax.dot_general(qs, k, nt, preferred_element_type=F32)
        s_list.append(s if bias is None else s + bias)
    mx = s_list[0].max(axis=-1, keepdims=True)
    for s in s_list[1:]:
        mx = jnp.maximum(mx, s.max(axis=-1, keepdims=True))
    den = None
    acc = None
    for s, v in zip(s_list, v_list):
        p = jnp.exp(s - mx)
        d = p.sum(axis=-1, keepdims=True)
        o = jnp.dot(p.astype(BF16), v, preferred_element_type=F32)
        den = d if den is None else den + d
        acc = o if acc is None else acc + o
    acc = acc / den
    n = acc.shape[0] // 2
    return jnp.where(lane_lo, acc[:n], acc[n:])


def _na_kernel(q_ref, k_ref, v_ref, bias_ref, rowc_ref, rows_ref, colc_ref, cols_ref, qw_ref, kw_ref,
               o_ref, qn_ref, kn_ref, s_ref, m_ref, *, clen, n_rows):
    lane = lax.broadcasted_iota(jnp.int32, (1, LANE), 1)
    lane_lo = lane < NA_HEAD_DIM
    first_half = (lane % 32) < 16
    qw = qw_ref[...] * (NA_HEAD_DIM ** -0.5)
    kw = kw_ref[...]
    kr = NA_ROWS_MAX

    mean_mat = _head_mean_matrix()
    qn_ref[0:clen, :] = _head_rms(q_ref[0:clen, :].astype(F32), qw, mean_mat).astype(BF16)
    kn_ref[0:clen, :] = _head_rms(k_ref[0:clen, :].astype(F32), kw, mean_mat).astype(BF16)
    colc = colc_ref[...]
    cols = cols_ref[...]

    def prep(r, carry):
        rows = pl.ds(pl.multiple_of(clen + r * GRID_W, GRID_W), GRID_W)
        cos = rowc_ref[pl.ds(r, 1), :] + colc
        sin = rows_ref[pl.ds(r, 1), :] + cols
        q = _head_rms(q_ref[rows, :].astype(F32), qw, mean_mat)
        k = _head_rms(k_ref[rows, :].astype(F32), kw, mean_mat)
        qn_ref[rows, :] = _rope(q, cos, sin, first_half).astype(BF16)
        kn_ref[rows, :] = _rope(k, cos, sin, first_half).astype(BF16)
        return carry

    lax.fori_loop(0, n_rows, prep, 0, unroll=NA_PREP_UNROLL)

    kc = kn_ref[0:clen, :]
    vc = v_ref[0:clen, :]
    for blk in range(clen // GRID_W):
        rows = slice(blk * GRID_W, (blk + 1) * GRID_W)
        qs = _stack_heads(qn_ref[rows, :], lane_lo)
        o_ref[rows, :] = _attend(qs, [kc], [vc], [None], lane_lo).astype(o_ref.dtype)

    nt = (((1,), (1,)), ((), ()))
    n_blk = (NA_BAND + clen) // LANE

    def band_of(r):
        r0 = jnp.clip(r - kr // 2, 0, n_rows - kr)
        return r0, pl.ds(pl.multiple_of(clen + r0 * GRID_W, GRID_W), NA_BAND)

    n_win = NA_BAND // LANE
    sq = HEAD_PAIR * GRID_W

    def scores(rows_r, slots):
        qs = [_stack_heads(qn_ref[pl.ds(pl.multiple_of(clen + r * GRID_W, GRID_W), GRID_W), :], lane_lo)
              for r in rows_r]
        s_ctx = lax.dot_general(jnp.concatenate(qs, axis=0), kn_ref[0:clen, :], nt, preferred_element_type=F32)
        for u, (r, slot) in enumerate(zip(rows_r, slots)):
            r0, band = band_of(r)
            s_win = lax.dot_general(qs[u], kn_ref[band, :], nt, preferred_element_type=F32)
            bias = bias_ref[r0 - r + (NA_ROWS_MAX - 1)]
            blocks = [s_win[:, j * LANE:(j + 1) * LANE] + bias[:, j * LANE:(j + 1) * LANE] for j in range(n_win)]
            blocks += [s_ctx[u * sq:(u + 1) * sq, j * LANE:(j + 1) * LANE] for j in range(n_blk - n_win)]
            mx = blocks[0]
            for j in range(n_blk):
                s_ref[slot, :, j * LANE:(j + 1) * LANE] = blocks[j]
                mx = jnp.maximum(mx, blocks[j])
            m_ref[slot] = jnp.broadcast_to(mx.max(axis=-1, keepdims=True), mx.shape)

    def output(rows_r, slots):
        accs, dens, p_ctx = [], [], []
        for r, slot in zip(rows_r, slots):
            _, band = band_of(r)
            mx = m_ref[slot]
            p = [jnp.exp(s_ref[slot, :, j * LANE:(j + 1) * LANE] - mx) for j in range(n_blk)]
            den = p[0]
            for pj in p[1:]:
                den = den + pj
            dens.append(den.sum(axis=-1, keepdims=True))
            pb = [pj.astype(BF16) for pj in p]
            accs.append(jnp.dot(jnp.concatenate(pb[:n_win], axis=1), v_ref[band, :], preferred_element_type=F32))
            p_ctx.append(jnp.concatenate(pb[n_win:], axis=1))
        acc_ctx = jnp.dot(jnp.concatenate(p_ctx, axis=0), v_ref[0:clen, :], preferred_element_type=F32)
        for u, r in enumerate(rows_r):
            acc = (accs[u] + acc_ctx[u * sq:(u + 1) * sq]) / dens[u]
            rows = pl.ds(pl.multiple_of(clen + r * GRID_W, GRID_W), GRID_W)
            o_ref[rows, :] = jnp.where(lane_lo, acc[:GRID_W], acc[GRID_W:]).astype(o_ref.dtype)

    scores(list(range(NA_UNROLL)), list(range(NA_UNROLL)))

    def body(j, carry):
        r = 2 * NA_UNROLL * j
        for half in range(2):
            output([r + half * NA_UNROLL + u for u in range(NA_UNROLL)],
                   [half * NA_UNROLL + u for u in range(NA_UNROLL)])
            scores([jnp.minimum(r + (half + 1) * NA_UNROLL + u, n_rows - 1) for u in range(NA_UNROLL)],
                   [(1 - half) * NA_UNROLL + u for u in range(NA_UNROLL)])
        return carry

    lax.fori_loop(0, n_rows // (2 * NA_UNROLL), body, 0)


def _na_tables(rpb, n_rows):
    col = jnp.arange(GRID_W)
    col_start = jnp.clip(col - NA_COLS // 2, 0, GRID_W - NA_COLS)
    col_ok = (col[None, :] >= col_start[:, None]) & (col[None, :] < col_start[:, None] + NA_COLS)
    col_idx = jnp.clip(col[None, :] - col[:, None], -(NA_COLS - 1), NA_COLS - 1) + NA_COLS - 1
    rpb_c = jnp.where(col_ok[None, None], rpb.astype(F32)[:, :, col_idx], NEG_INF)
    dr = jnp.arange(NA_ROWS_MAX)[:, None] + jnp.arange(NA_ROWS_MAX)[None, :]
    tab = rpb_c[:, dr]
    tab = jnp.transpose(tab, (0, 1, 3, 2, 4)).reshape(NA_HEADS, NA_ROWS_MAX, GRID_W, NA_BAND)
    tab = tab.reshape(NA_HEADS // HEAD_PAIR, HEAD_PAIR, NA_ROWS_MAX, GRID_W, NA_BAND)
    tab = jnp.transpose(tab, (0, 2, 1, 3, 4)).reshape(NA_HEADS // HEAD_PAIR, NA_ROWS_MAX,
                                                       HEAD_PAIR * GRID_W, NA_BAND)
    quarter = NA_HEAD_DIM // 4
    inv = ROPE_BASE ** (-jnp.arange(quarter, dtype=F32) / quarter)

    def tables(pos, lo):
        ang = pos[:, None] * inv[None, :]
        z = jnp.zeros_like(ang)
        c, s = jnp.cos(ang), jnp.sin(ang)
        ch = jnp.concatenate([c, c, z, z] if lo else [z, z, c, c], axis=-1)
        sh = jnp.concatenate([-s, s, z, z] if lo else [z, z, -s, s], axis=-1)
        return jnp.tile(ch, (1, HEAD_PAIR)), jnp.tile(sh, (1, HEAD_PAIR))

    rowc, rows = tables(jnp.arange(n_rows).astype(F32), True)
    colc, cols = tables(jnp.arange(GRID_W).astype(F32), False)
    return tab, rowc, rows, colc, cols


def _na(pm3, q_norm_w, k_norm_w, rpb, clen):
    bsz, t, _ = pm3.shape
    n_rows = (t - clen) // GRID_W
    tab, rowc, rows, colc, cols = _na_tables(rpb, n_rows)
    qw = jnp.tile(q_norm_w.astype(F32), HEAD_PAIR).reshape(1, LANE)
    kw = jnp.tile(k_norm_w.astype(F32), HEAD_PAIR).reshape(1, LANE)
    n_pairs = NA_HEADS // HEAD_PAIR
    const = lambda p, b: (0, 0)
    tok = lambda c0: pl.BlockSpec((None, t, LANE), lambda p, b: (b, 0, c0 // LANE + p))
    return pl.pallas_call(
        functools.partial(_na_kernel, clen=clen, n_rows=n_rows),
        grid=(n_pairs, bsz),
        in_specs=[tok(COL_Q), tok(COL_K), tok(COL_V),
                  pl.BlockSpec((None, NA_ROWS_MAX, HEAD_PAIR * GRID_W, NA_BAND), lambda p, b: (p, 0, 0, 0)),
                  pl.BlockSpec((n_rows, LANE), const), pl.BlockSpec((n_rows, LANE), const),
                  pl.BlockSpec((GRID_W, LANE), const), pl.BlockSpec((GRID_W, LANE), const),
                  pl.BlockSpec((1, LANE), const), pl.BlockSpec((1, LANE), const)],
        out_specs=pl.BlockSpec((None, t, LANE), lambda p, b: (b, 0, p)),
        out_shape=jax.ShapeDtypeStruct((bsz, t, D_NA), BF16),
        scratch_shapes=[pltpu.VMEM((t, LANE), BF16), pltpu.VMEM((t, LANE), BF16),
                        pltpu.VMEM((2 * NA_UNROLL, HEAD_PAIR * GRID_W, NA_BAND + clen), F32),
                        pltpu.VMEM((2 * NA_UNROLL, HEAD_PAIR * GRID_W, LANE), F32)],
        compiler_params=_params(("arbitrary", "arbitrary")),
        name="natten",
    )(pm3, pm3, pm3, tab, rowc, rows, colc, cols, qw, kw)


HALO = 16
CONV_ROWS = SSD_CHUNK
CONV_PAD = SSD_CONV // 2
BC_DIM = SSD_GROUPS * SSD_STATE
PAIRS_PER_GROUP = SSD_HPG // HEAD_PAIR


def _ssd_chunk_index(s, rev, cc, nc):
    if not rev:
        return s
    return jnp.where(s < cc, cc - 1 - s, nc - 1 - (s - cc))


def _ssd_conv_kernel(main_ref, prev_ref, next_ref, cw_ref, cb_ref, o_ref, *, cc, nc):
    q = main_ref.shape[0]
    cidx = pl.program_id(1)
    first_col = jnp.where((cidx == 0) | (cidx == cc), HALO, 0)
    end_col = jnp.where((cidx == cc - 1) | (cidx == nc - 1), HALO + q, q + 2 * HALO)
    ext = jnp.concatenate([prev_ref[...], main_ref[...], next_ref[...]], axis=0)
    ti = lax.broadcasted_iota(jnp.int32, (q, q + 2 * HALO), 0)
    ji = lax.broadcasted_iota(jnp.int32, (q, q + 2 * HALO), 1)
    in_segment = (ji >= first_col) & (ji < end_col)
    conv = cb_ref[...] + cw_ref[CONV_PAD:CONV_PAD + 1, :] * main_ref[...].astype(F32)
    for k in range(SSD_CONV):
        if k == CONV_PAD:
            continue
        shift = jnp.where((ji == ti + (HALO + k - CONV_PAD)) & in_segment, 1.0, 0.0).astype(BF16)
        conv = conv + cw_ref[k:k + 1, :] * jnp.dot(shift, ext, preferred_element_type=F32)
    o_ref[...] = (conv * jax.nn.sigmoid(conv)).astype(o_ref.dtype)


def _ssd_core(xbc_ref, dt_ref, dtb_ref, a_ref, exp_ref, h_ref, yoff_ref, y_ref, *, rev):
    q = SSD_CHUNK
    xbc = xbc_ref[...]

    hb = SSD_HEADS if rev else 0
    x = dt_ref[...] + dtb_ref[...]
    dt = jnp.maximum(x, 0.0) + jnp.log(1.0 + jnp.exp(-jnp.abs(x)))
    la = dt * a_ref[...]
    ri = lax.broadcasted_iota(jnp.int32, (q, q), 0)
    ci = lax.broadcasted_iota(jnp.int32, (q, q), 1)
    allowed = (ci >= ri) if rev else (ci <= ri)
    tri = jnp.where(allowed, 1.0, 0.0).astype(F32)
    a_cs = jnp.dot(tri, la, precision=HIGHEST, preferred_element_type=F32)
    last = 0 if rev else q - 1
    a_cs_t = a_cs.T
    e_all = jnp.exp(a_cs)
    src_t = a_cs_t - jnp.log(dt.T)
    wdt_t = jnp.exp(a_cs_t[:, last:last + 1] - src_t)
    neg_mask = jnp.where(allowed, 0.0, NEG_INF)
    blk = (last // 8) * 8
    e_last = jnp.exp(jnp.dot(a_cs[blk:blk + 8, :], exp_ref[...], precision=HIGHEST,
                             preferred_element_type=F32))[last - blk:last - blk + 1, hb * SSD_HEAD_DIM:
                                                          (hb + SSD_HEADS) * SSD_HEAD_DIM]

    lane_lo = lax.broadcasted_iota(jnp.int32, (1, LANE), 1) < SSD_HEAD_DIM
    nt = (((1,), (1,)), ((), ()))
    for g in range(SSD_GROUPS):
        b_g = xbc[:, D_SSD + g * SSD_STATE:D_SSD + (g + 1) * SSD_STATE]
        c_g = xbc[:, D_SSD + BC_DIM + g * SSD_STATE:D_SSD + BC_DIM + (g + 1) * SSD_STATE]
        cols = slice(g * SSD_HPG * SSD_HEAD_DIM, (g + 1) * SSD_HPG * SSD_HEAD_DIM)
        yoff_ref[:, cols] = jnp.dot(c_g, h_ref[:, cols].astype(BF16), preferred_element_type=F32)
        cb = lax.dot_general(c_g, b_g, nt, preferred_element_type=F32)
        b_t = b_g.astype(F32).T
        for pj in range(PAIRS_PER_GROUP):
            pair = g * PAIRS_PER_GROUP + pj
            lanes = slice(pair * LANE, (pair + 1) * LANE)
            lhs, ecol = [], []
            for hh in range(HEAD_PAIR):
                head = hb + pair * HEAD_PAIR + hh
                seg = (a_cs[:, head:head + 1] - src_t[head:head + 1, :]) + neg_mask
                lhs.append((cb * jnp.exp(seg)).astype(BF16))
                ecol.append(jnp.broadcast_to(e_all[:, head:head + 1], (q, LANE)))
            for hh in range(HEAD_PAIR):
                head = hb + pair * HEAD_PAIR + hh
                lhs.append((b_t * wdt_t[head:head + 1, :]).astype(BF16))
            res = jnp.dot(jnp.concatenate(lhs, axis=0), xbc[:, lanes], preferred_element_type=F32)
            y_ref[:, lanes] = (jnp.where(lane_lo, res[0:q], res[q:2 * q])
                               + yoff_ref[:, lanes] * jnp.where(lane_lo, ecol[0], ecol[1]))
            h_ref[:, lanes] = (h_ref[:, lanes] * e_last[:, lanes]
                               + jnp.where(lane_lo, res[2 * q:3 * q], res[3 * q:4 * q]))


def _ssd_scan_kernel(xf_ref, dtf_ref, xb_ref, dtb_ref, bias_ref, a_ref, exp_ref, dsum_ref, of_ref, ob_ref,
                     hf_ref, yofff_ref, hb_ref, yoffb_ref, yf_ref, yb_ref):
    @pl.when(pl.program_id(1) == 0)
    def _():
        hf_ref[...] = jnp.zeros_like(hf_ref)
        hb_ref[...] = jnp.zeros_like(hb_ref)

    _ssd_core(xf_ref, dtf_ref, bias_ref, a_ref, exp_ref, hf_ref, yofff_ref, yf_ref, rev=False)
    _ssd_core(xb_ref, dtb_ref, bias_ref, a_ref, exp_ref, hb_ref, yoffb_ref, yb_ref, rev=True)
    of_ref[...] = (yf_ref[...] + xf_ref[:, 0:D_SSD].astype(F32) * dsum_ref[...]).astype(of_ref.dtype)
    ob_ref[...] = yb_ref[...].astype(ob_ref.dtype)


def _ssd(pm3, pdt3, conv_w, conv_b, dt_bias, a_log, d_skip, clen):
    bsz, t, _ = pm3.shape
    q = SSD_CHUNK
    nc, cc = t // q, clen // q
    nh = 2 * SSD_HEADS
    cw = jnp.pad(conv_w.astype(F32), ((0, 8 - SSD_CONV), (0, 0)))
    cb = conv_b.astype(F32).reshape(1, XBC_DIM)
    dtb = jnp.pad(dt_bias.astype(F32).reshape(1, nh), ((0, 0), (0, DT_PAD - nh)))
    a_neg = jnp.pad(-jnp.exp(a_log.astype(F32)).reshape(1, nh), ((0, 0), (0, DT_PAD - nh)))
    dsum = jnp.repeat((d_skip[0] + d_skip[1]).astype(F32), SSD_HEAD_DIM).reshape(1, D_SSD)
    expand = jnp.repeat(jnp.eye(DT_PAD, dtype=F32)[:, :nh], SSD_HEAD_DIM, axis=1)
    n_halo = t // HALO

    const = lambda b, s: (0, 0)
    xcol = COL_XBC // XBC_DIM
    cq = CONV_ROWS
    hpb = cq // HALO
    xbc = pl.pallas_call(
        functools.partial(_ssd_conv_kernel, cc=clen // cq, nc=t // cq),
        grid=(bsz, t // cq),
        in_specs=[pl.BlockSpec((None, cq, XBC_DIM), lambda b, s: (b, s, xcol)),
                  pl.BlockSpec((None, HALO, XBC_DIM), lambda b, s: (b, jnp.maximum(s * hpb - 1, 0), xcol)),
                  pl.BlockSpec((None, HALO, XBC_DIM),
                               lambda b, s: (b, jnp.minimum((s + 1) * hpb, n_halo - 1), xcol)),
                  pl.BlockSpec((8, XBC_DIM), const), pl.BlockSpec((1, XBC_DIM), const)],
        out_specs=pl.BlockSpec((None, cq, XBC_DIM), lambda b, s: (b, s, 0)),
        out_shape=jax.ShapeDtypeStruct((bsz, t, XBC_DIM), BF16),
        compiler_params=_params(("arbitrary", "arbitrary")),
        name="ssd_conv",
    )(pm3, pm3, pm3, cw, cb)

    tok_f = lambda b, s: (b, s, 0)
    tok_b = lambda b, s: (b, _ssd_chunk_index(s, True, cc, nc), 0)
    state = [pltpu.VMEM((SSD_STATE, D_SSD), F32), pltpu.VMEM((q, D_SSD), F32)]
    return pl.pallas_call(
        _ssd_scan_kernel,
        grid=(bsz, nc),
        in_specs=[pl.BlockSpec((None, q, XBC_DIM), tok_f), pl.BlockSpec((None, q, DT_PAD), tok_f),
                  pl.BlockSpec((None, q, XBC_DIM), tok_b), pl.BlockSpec((None, q, DT_PAD), tok_b),
                  pl.BlockSpec((1, DT_PAD), const), pl.BlockSpec((1, DT_PAD), const),
                  pl.BlockSpec((DT_PAD, nh * SSD_HEAD_DIM), const), pl.BlockSpec((1, D_SSD), const)],
        out_specs=[pl.BlockSpec((None, q, D_SSD), tok_f), pl.BlockSpec((None, q, D_SSD), tok_b)],
        out_shape=[jax.ShapeDtypeStruct((bsz, t, D_SSD), BF16), jax.ShapeDtypeStruct((bsz, t, D_SSD), BF16)],
        scratch_shapes=state + state + [pltpu.VMEM((q, D_SSD), F32), pltpu.VMEM((q, D_SSD), F32)],
        compiler_params=_params(("arbitrary", "arbitrary")),
        name="ssd_scan",
    )(xbc, pdt3, xbc, pdt3, dtb, a_neg, expand, dsum)


MOE_TB = 512
ROUTE_TM = 512
INFO_W = 8
BIG_NEG = -3.0e38
ROW_SEG = D_MODEL // LANE


def _load_tokens(ref, t0, n):
    return jnp.concatenate([ref[pl.ds(t0 * ROW_SEG + c, n, stride=ROW_SEG), :] for c in range(ROW_SEG)], axis=1)


def _store_tokens(ref, t0, val):
    n = val.shape[0]
    for c in range(ROW_SEG):
        ref[pl.ds(t0 * ROW_SEG + c, n, stride=ROW_SEG), :] = val[:, c * LANE:(c + 1) * LANE]


def _route_kernel(f_ref, wr_ref, info_ref, cnt_ref, run_ref):
    tm = f_ref.shape[0] // ROW_SEG

    @pl.when(pl.program_id(0) == 0)
    def _():
        run_ref[...] = jnp.zeros_like(run_ref)

    logits = jnp.dot(_load_tokens(f_ref, 0, tm).astype(BF16), wr_ref[...], preferred_element_type=F32)
    lane = lax.broadcasted_iota(jnp.int32, (tm, LANE), 1).astype(F32)
    lg = jnp.where(lane < N_EXPERTS, logits, BIG_NEG)
    m1 = lg.max(axis=-1, keepdims=True)
    i1 = jnp.where(lg == m1, lane, float(LANE)).min(axis=-1, keepdims=True)
    lg2 = jnp.where(lane == i1, BIG_NEG, lg)
    m2 = lg2.max(axis=-1, keepdims=True)
    i2 = jnp.where(lg2 == m2, lane, float(LANE)).min(axis=-1, keepdims=True)
    ex = jnp.exp(m2 - m1)
    g0 = 1.0 / (1.0 + ex)
    g1 = ex / (1.0 + ex)
    sel0 = lane == i1
    sel1 = lane == i2
    onehot = jnp.where(sel0 | sel1, 1.0, 0.0)
    ri = lax.broadcasted_iota(jnp.int32, (tm, tm), 0)
    ci = lax.broadcasted_iota(jnp.int32, (tm, tm), 1)
    strict_lower = jnp.where(ci < ri, 1.0, 0.0).astype(BF16)
    before = jnp.dot(strict_lower, onehot.astype(BF16), preferred_element_type=F32) + run_ref[...]
    pos0 = jnp.where(sel0, before, 0.0).sum(axis=-1, keepdims=True)
    pos1 = jnp.where(sel1, before, 0.0).sum(axis=-1, keepdims=True)
    run_ref[...] = run_ref[...] + onehot.sum(axis=0, keepdims=True)
    cnt_ref[...] = run_ref[...]
    info = jnp.where(lane == 0, i1, jnp.where(lane == 1, i2, jnp.where(lane == 2, g0, jnp.where(
        lane == 3, g1, jnp.where(lane == 4, pos0, jnp.where(lane == 5, pos1, 0.0))))))
    info_ref[...] = info[:, :INFO_W]


def _route(f8, w_router):
    m, d = f8.shape[0] // ROW_SEG, D_MODEL
    wr = jnp.pad(w_router.astype(BF16), ((0, 0), (0, LANE - N_EXPERTS)))
    return pl.pallas_call(
        _route_kernel,
        grid=(m // ROUTE_TM,),
        in_specs=[pl.BlockSpec((ROUTE_TM * ROW_SEG, LANE), lambda i: (i, 0)),
                  pl.BlockSpec((d, LANE), lambda i: (0, 0))],
        out_specs=[pl.BlockSpec((ROUTE_TM, INFO_W), lambda i: (i, 0)), pl.BlockSpec((1, LANE), lambda i: (0, 0))],
        out_shape=[jax.ShapeDtypeStruct((m, INFO_W), F32), jax.ShapeDtypeStruct((1, LANE), F32)],
        scratch_shapes=[pltpu.VMEM((1, LANE), F32)],
        compiler_params=_params(("arbitrary",)),
        name="moe_route",
    )(f8, wr)


def _row_copy(src_ref, src_row, dst_ref, dst_row, sem):
    src = src_ref.at[pl.ds(pl.multiple_of(src_row * ROW_SEG, ROW_SEG), ROW_SEG)]
    dst = dst_ref.at[pl.ds(pl.multiple_of(dst_row * ROW_SEG, ROW_SEG), ROW_SEG)]
    return pltpu.make_async_copy(src, dst, sem)


def _wait_rows(src_ref, dst_ref, sem, n):
    def wait(t, c):
        _row_copy(src_ref, 0, dst_ref, 0, sem).wait()
        return c

    lax.fori_loop(0, n, wait, 0, unroll=8)


def _dispatch_kernel(pad_ref, dest_hbm, f_ref, rows_hbm, dest_smem, stage_ref, zero_ref, idx_sem, row_sem,
                     pad_sem):
    tm = f_ref.shape[0] // ROW_SEG
    i, n = pl.program_id(0), pl.num_programs(0)
    slot = i % 2
    stage = stage_ref.at[slot]
    sem = row_sem.at[slot]

    @pl.when(i == 0)
    def _():
        zero_ref[...] = jnp.zeros_like(zero_ref)
        for e in range(N_EXPERTS):
            first, count = pad_ref[e], pad_ref[N_EXPERTS + e]

            def start_zero(j, c):
                _row_copy(zero_ref, 0, rows_hbm, first + j, pad_sem).start()
                return c

            lax.fori_loop(0, count, start_zero, 0)
        for e in range(N_EXPERTS):
            def wait_zero(j, c):
                _row_copy(zero_ref, 0, rows_hbm, 0, pad_sem).wait()
                return c

            lax.fori_loop(0, pad_ref[N_EXPERTS + e], wait_zero, 0)

    @pl.when(i >= 2)
    def _():
        _wait_rows(stage, rows_hbm, sem, TOP_K * tm)

    stage[...] = f_ref[...]
    base = pl.multiple_of(i * (TOP_K * tm), TOP_K * tm)
    idx_copy = pltpu.make_async_copy(dest_hbm.at[pl.ds(base, TOP_K * tm)], dest_smem, idx_sem)
    idx_copy.start()
    idx_copy.wait()

    def start(t, c):
        for k in range(TOP_K):
            _row_copy(stage, t, rows_hbm, dest_smem[TOP_K * t + k], sem).start()
        return c

    lax.fori_loop(0, tm, start, 0, unroll=8)

    @pl.when(i == n - 1)
    def _():
        _wait_rows(stage, rows_hbm, sem, TOP_K * tm)

        @pl.when(n >= 2)
        def _():
            _wait_rows(stage_ref.at[1 - slot], rows_hbm, row_sem.at[1 - slot], TOP_K * tm)


def _dispatch(f8, dest_flat, pad_info, n_rows):
    m = f8.shape[0] // ROW_SEG
    grid_spec = pltpu.PrefetchScalarGridSpec(
        num_scalar_prefetch=1,
        grid=(m // ROUTE_TM,),
        in_specs=[pl.BlockSpec(memory_space=pl.ANY),
                  pl.BlockSpec((ROUTE_TM * ROW_SEG, LANE), lambda i, pad: (i, 0))],
        out_specs=pl.BlockSpec(memory_space=pl.ANY),
        scratch_shapes=[pltpu.SMEM((TOP_K * ROUTE_TM,), jnp.int32),
                        pltpu.VMEM((2, ROUTE_TM * ROW_SEG, LANE), f8.dtype),
                        pltpu.VMEM((ROW_SEG, LANE), f8.dtype),
                        pltpu.SemaphoreType.DMA(()), pltpu.SemaphoreType.DMA((2,)), pltpu.SemaphoreType.DMA(())],
    )
    return pl.pallas_call(
        _dispatch_kernel,
        grid_spec=grid_spec,
        out_shape=jax.ShapeDtypeStruct((n_rows * ROW_SEG, LANE), f8.dtype),
        compiler_params=_params(("arbitrary",)),
        name="moe_dispatch",
    )(pad_info, dest_flat, f8)


def _gffn_kernel(be_ref, nu_ref, x_ref, w1_ref, w3_ref, w2_ref, y_ref):
    i = pl.program_id(0)

    @pl.when(i < nu_ref[0])
    def _():
        x = _load_tokens(x_ref, 0, MOE_TB).astype(BF16)
        h1 = jnp.dot(x, w1_ref[...], preferred_element_type=F32)
        h3 = jnp.dot(x, w3_ref[...], preferred_element_type=F32)
        act = (h1 * jax.nn.sigmoid(h1) * h3).astype(BF16)
        _store_tokens(y_ref, 0, jnp.dot(act, w2_ref[...], preferred_element_type=F32))

    @pl.when(i >= nu_ref[0])
    def _():
        y_ref[...] = jnp.zeros_like(y_ref)


def _gffn(rows, block_e, n_used, w1, w3, w2):
    r, d = rows.shape[0] // ROW_SEG, D_MODEL
    ff = w1.shape[2]
    nb = r // MOE_TB
    once = pl.Buffered(1)
    grid_spec = pltpu.PrefetchScalarGridSpec(
        num_scalar_prefetch=2,
        grid=(nb,),
        in_specs=[pl.BlockSpec((MOE_TB * ROW_SEG, LANE), lambda i, be, nu: (i, 0)),
                  pl.BlockSpec((None, d, ff), lambda i, be, nu: (be[i], 0, 0), pipeline_mode=once),
                  pl.BlockSpec((None, d, ff), lambda i, be, nu: (be[i], 0, 0), pipeline_mode=once),
                  pl.BlockSpec((None, ff, d), lambda i, be, nu: (be[i], 0, 0), pipeline_mode=once)],
        out_specs=pl.BlockSpec((MOE_TB * ROW_SEG, LANE), lambda i, be, nu: (i, 0)),
    )
    return pl.pallas_call(
        _gffn_kernel,
        grid_spec=grid_spec,
        out_shape=jax.ShapeDtypeStruct((r * ROW_SEG, LANE), F32),
        compiler_params=_params(("arbitrary",)),
        name="moe_ffn",
    )(block_e, n_used, rows, w1, w3, w2)


def _combine_kernel(dest_hbm, y_hbm, h_ref, info_ref, modt_ref, o_ref, dest_smem, buf_ref, idx_sem, row_sem,
                    *, nsub):
    tm = h_ref.shape[0]
    i, n = pl.program_id(0), pl.num_programs(0)

    def gather(step, slot):
        base = pl.multiple_of(step * (TOP_K * tm), TOP_K * tm)
        idx_copy = pltpu.make_async_copy(dest_hbm.at[pl.ds(base, TOP_K * tm)], dest_smem, idx_sem)
        idx_copy.start()
        idx_copy.wait()

        def start(t, c):
            for k in range(TOP_K):
                _row_copy(y_hbm, dest_smem[TOP_K * t + k], buf_ref.at[slot, k], t, row_sem.at[slot]).start()
            return c

        lax.fori_loop(0, tm, start, 0, unroll=8)

    @pl.when(i == 0)
    def _():
        gather(0, 0)

    @pl.when(i + 1 < n)
    def _():
        gather(i + 1, (i + 1) % 2)

    slot = i % 2
    _wait_rows(y_hbm, buf_ref.at[slot, 0], row_sem.at[slot], TOP_K * tm)
    for s in range(nsub):
        r0, r1 = s * MOD_TILE, (s + 1) * MOD_TILE
        info = info_ref[r0:r1, :]
        mix = (info[:, 2:3] * _load_tokens(buf_ref.at[slot, 0], r0, MOD_TILE)
               + info[:, 3:4] * _load_tokens(buf_ref.at[slot, 1], r0, MOD_TILE))
        o_ref[r0:r1, :] = h_ref[r0:r1, :] + modt_ref[s, 5:6, :] * mix


def _combine(y_rows, dest_flat, h, info, modt):
    m, d = h.shape
    nsub = ROUTE_TM // MOD_TILE
    return pl.pallas_call(
        functools.partial(_combine_kernel, nsub=nsub),
        grid=(m // ROUTE_TM,),
        in_specs=[pl.BlockSpec(memory_space=pl.ANY), pl.BlockSpec(memory_space=pl.ANY),
                  pl.BlockSpec((ROUTE_TM, d), lambda i: (i, 0)),
                  pl.BlockSpec((ROUTE_TM, INFO_W), lambda i: (i, 0)),
                  pl.BlockSpec((nsub, 6, d), lambda i: (i, 0, 0))],
        out_specs=pl.BlockSpec((ROUTE_TM, d), lambda i: (i, 0)),
        out_shape=jax.ShapeDtypeStruct((m, d), F32),
        scratch_shapes=[pltpu.SMEM((TOP_K * ROUTE_TM,), jnp.int32),
                        pltpu.VMEM((2, TOP_K, ROUTE_TM * ROW_SEG, LANE), F32),
                        pltpu.SemaphoreType.DMA(()), pltpu.SemaphoreType.DMA((2,))],
        compiler_params=_params(("arbitrary",)),
        name="moe_combine",
    )(dest_flat, y_rows, h, info, modt)


def _moe(f8, h, w_router, w1, w3, w2, modt):
    m, d = h.shape
    info, cnt = _route(f8, w_router)
    e = info[:, 0:TOP_K].astype(jnp.int32)
    pos = info[:, 4:4 + TOP_K].astype(jnp.int32)
    counts = cnt[0, :N_EXPERTS].astype(jnp.int32)
    padded = (counts + MOE_TB - 1) // MOE_TB * MOE_TB
    pad_end = jnp.cumsum(padded)
    pad_start = pad_end - padded
    dest_flat = (pad_start[e] + pos).reshape(-1)
    n_rows = m * TOP_K + N_EXPERTS * MOE_TB
    nb = n_rows // MOE_TB
    block_e = jnp.minimum(jnp.searchsorted(pad_end, jnp.arange(nb, dtype=jnp.int32) * MOE_TB, side='right'),
                          N_EXPERTS - 1).astype(jnp.int32)
    n_used = (pad_end[-1:] // MOE_TB).astype(jnp.int32)
    pad_info = jnp.concatenate([pad_start + counts, padded - counts]).astype(jnp.int32)
    rows = _dispatch(f8, dest_flat, pad_info, n_rows)
    y_rows = _gffn(rows, block_e, n_used, w1.astype(BF16), w3.astype(BF16), w2.astype(BF16))
    return _combine(y_rows, dest_flat, h, info, modt)


def _tile_mod(mod_l, bsz, tiles_per_batch):
    t = jnp.arange(bsz * tiles_per_batch)
    idx = jnp.where(t % tiles_per_batch == 0, bsz, t // tiles_per_batch)
    return jnp.transpose(mod_l[:, idx, :], (1, 0, 2))


def kernel(x, c, ctx, c_ctx, w_mod, b_mod, norm1_w, norm2_w, w_in, conv_w, conv_b, dt_bias, a_log, d_skip,
           ssd_norm_w, q_norm_w, k_norm_w, rpb, w_br_ssd, w_br_na, w_out, w_ff1, w_ff3, w_ff2, w_router,
           w_e1, w_e3, w_e2):
    bsz, seqlen, d = x.shape
    clen = ctx.shape[1]
    n_layers = w_mod.shape[0]
    assert clen == MOD_TILE and seqlen % MOD_TILE == 0 and bsz < 16
    t = clen + seqlen
    m = bsz * t
    tiles_per_batch = t // MOD_TILE

    cond = jnp.zeros((16, d), F32).at[:bsz].set(c).at[bsz].set(c_ctx)
    mod = _modulation(cond, w_mod, b_mod)
    modt = [_tile_mod(mod[i], bsz, tiles_per_batch) for i in range(n_layers)]

    h = jnp.concatenate([ctx, x], axis=1).reshape(m, d)
    a = _adaln(h, norm1_w[0], modt[0])
    for i in range(n_layers):
        last = i == n_layers - 1
        wi = w_in[i]
        w_main = jnp.concatenate([wi[:, :IN_SPLITS[3]], wi[:, IN_SPLITS[4]:]], axis=1).astype(BF16)
        w_dt = jnp.pad(wi[:, IN_SPLITS[3]:IN_SPLITS[4]], ((0, 0), (0, DT_PAD - 2 * SSD_HEADS))).astype(BF16)
        p_main = _matmul(a, w_main, BF16, 512, 2560, "inproj")
        p_dt = _matmul(a, w_dt, F32, 1024, DT_PAD, "inproj_dt")
        pm3 = p_main.reshape(bsz, t, MAIN_DIM)
        yf, yb = _ssd(pm3, p_dt.reshape(bsz, t, DT_PAD), conv_w[i], conv_b[i], dt_bias[i], a_log[i],
                      d_skip[i], clen)
        y_na = _na(pm3, q_norm_w[i], k_norm_w[i], rpb[i], clen).reshape(m, d)
        fi = i // 2
        moe_layer = i % 2 == 1
        h, f = _merge(p_main, yf.reshape(m, d), yb.reshape(m, d), y_na, h, w_br_ssd[i].astype(BF16),
                      w_br_na[i].astype(BF16), w_out[i].astype(BF16), ssd_norm_w[i], norm2_w[i], modt[i],
                      row_tile_out=moe_layer)
        nxt = min(i + 1, n_layers - 1)
        if not moe_layer:
            h, a = _ffn(f, h, w_ff1[fi].astype(BF16), w_ff3[fi].astype(BF16), w_ff2[fi].astype(BF16),
                        norm1_w[nxt], modt[i], modt[nxt])
        else:
            h = _moe(f, h, w_router[fi], w_e1[fi], w_e3[fi], w_e2[fi], modt[i])
            if not last:
                a = _adaln(h, norm1_w[nxt], modt[nxt])
    return h.reshape(bsz, t, d)[:, clen:]
```

```python
import functools
import math

import jax
import jax.numpy as jnp
from jax import lax
from jax.experimental import pallas as pl
from jax.experimental.pallas import tpu as pltpu

D_MODEL = 1024
GRID_W = 64
SSD_HEAD_DIM = 64
SSD_HEADS = D_MODEL // SSD_HEAD_DIM
D_SSD = SSD_HEADS * SSD_HEAD_DIM
SSD_GROUPS = 2
SSD_HPG = SSD_HEADS // SSD_GROUPS
SSD_STATE = 128
SSD_CONV = 5
SSD_CHUNK = 128
XBC_DIM = D_SSD + 2 * SSD_GROUPS * SSD_STATE
NA_HEAD_DIM = 64
NA_HEADS = D_MODEL // NA_HEAD_DIM
D_NA = NA_HEADS * NA_HEAD_DIM
NA_ROWS_MAX = 8
NA_COLS = 16
ROPE_BASE = 10000.0
N_EXPERTS = 8
TOP_K = 2
EPS = 1e-6
NEG_INF = -1e30
IN_SPLITS = (D_MODEL, 2 * D_MODEL, 2 * D_MODEL + D_SSD, 2 * D_MODEL + D_SSD + XBC_DIM,
             2 * D_MODEL + D_SSD + XBC_DIM + 2 * SSD_HEADS,
             2 * D_MODEL + D_SSD + XBC_DIM + 2 * SSD_HEADS + D_NA,
             2 * D_MODEL + D_SSD + XBC_DIM + 2 * SSD_HEADS + 2 * D_NA)

F32 = jnp.float32
BF16 = jnp.bfloat16
HIGHEST = lax.Precision.HIGHEST

MOD_TILE = 256
LANE = 128
VMEM_LIMIT = 48 * 1024 * 1024
COL_GS, COL_GN, COL_Z, COL_XBC, COL_Q, COL_K, COL_V = 0, 1024, 2048, 3072, 4608, 5632, 6656
MAIN_DIM = 7680
DT_PAD = LANE


def _params(sem):
    return pltpu.CompilerParams(dimension_semantics=sem, vmem_limit_bytes=VMEM_LIMIT)


def _rms_mod(x, nw, shift, scale):
    ms = jnp.mean(x * x, axis=-1, keepdims=True)
    return (x * lax.rsqrt(ms + EPS) * nw) * (1.0 + scale) + shift


def _mod_kernel(cond_ref, w_ref, b_ref, o_ref):
    c = cond_ref[...]
    s = c * jax.nn.sigmoid(c)
    o_ref[...] = jnp.dot(s, w_ref[...], precision=HIGHEST, preferred_element_type=F32) + b_ref[...]


def _modulation(cond, w_mod, b_mod):
    n_layers, d, _ = w_mod.shape
    rows = cond.shape[0]
    b4 = b_mod.reshape(n_layers, 6, 1, d)
    return pl.pallas_call(
        _mod_kernel,
        grid=(n_layers, 6),
        in_specs=[pl.BlockSpec((rows, d), lambda l, k: (0, 0)),
                  pl.BlockSpec((None, d, d), lambda l, k: (l, 0, k)),
                  pl.BlockSpec((None, None, 1, d), lambda l, k: (l, k, 0, 0))],
        out_specs=pl.BlockSpec((None, None, rows, d), lambda l, k: (l, k, 0, 0)),
        out_shape=jax.ShapeDtypeStruct((n_layers, 6, rows, d), F32),
        compiler_params=_params(("arbitrary", "arbitrary")),
        name="modulation",
    )(cond, w_mod, b4)


def _adaln_kernel(h_ref, nw_ref, modt_ref, o_ref, *, nsub):
    nw = nw_ref[...]
    for s in range(nsub):
        rows = pl.ds(s * MOD_TILE, MOD_TILE)
        x = h_ref[rows, :]
        o_ref[rows, :] = _rms_mod(x, nw, modt_ref[s, 0:1, :], modt_ref[s, 1:2, :]).astype(o_ref.dtype)


def _adaln(h, nw, modt, tm=1024):
    m, d = h.shape
    nsub = tm // MOD_TILE
    return pl.pallas_call(
        functools.partial(_adaln_kernel, nsub=nsub),
        grid=(m // tm,),
        in_specs=[pl.BlockSpec((tm, d), lambda i: (i, 0)),
                  pl.BlockSpec((1, d), lambda i: (0, 0)),
                  pl.BlockSpec((nsub, 6, d), lambda i: (i, 0, 0))],
        out_specs=pl.BlockSpec((tm, d), lambda i: (i, 0)),
        out_shape=jax.ShapeDtypeStruct((m, d), BF16),
        compiler_params=_params(("arbitrary",)),
        name="adaln",
    )(h, nw.reshape(1, d), modt)


def _matmul_kernel(a_ref, w_ref, o_ref):
    o_ref[...] = jnp.dot(a_ref[...], w_ref[...], preferred_element_type=F32).astype(o_ref.dtype)


def _matmul(a, w, out_dtype, tm, tn, name):
    m, k = a.shape
    n = w.shape[1]
    return pl.pallas_call(
        _matmul_kernel,
        grid=(n // tn, m // tm),
        in_specs=[pl.BlockSpec((tm, k), lambda j, i: (i, 0)),
                  pl.BlockSpec((k, tn), lambda j, i: (0, j))],
        out_specs=pl.BlockSpec((tm, tn), lambda j, i: (i, j)),
        out_shape=jax.ShapeDtypeStruct((m, n), out_dtype),
        compiler_params=_params(("arbitrary", "arbitrary")),
        name=name,
    )(a, w)


def _merge_kernel(gs_ref, gn_ref, z_ref, yf_ref, yb_ref, yn_ref, h_ref, wbs_ref, wbn_ref, wo_ref, snw_ref,
                  nw_ref, modt_ref, ho_ref, f_ref, *, nsub, row_tile_out):
    gw = D_SSD // SSD_GROUPS
    nw = nw_ref[...]
    for s in range(nsub):
        r0, r1 = s * MOD_TILE, (s + 1) * MOD_TILE
        z = z_ref[r0:r1, :].astype(F32)
        u = (yf_ref[r0:r1, :].astype(F32) + yb_ref[r0:r1, :].astype(F32)) * (z * jax.nn.sigmoid(z))
        ys = []
        for g in range(SSD_GROUPS):
            ug = u[:, g * gw:(g + 1) * gw]
            ms = jnp.mean(ug * ug, axis=-1, keepdims=True)
            ys.append((ug * lax.rsqrt(ms + EPS) * snw_ref[:, g * gw:(g + 1) * gw]).astype(BF16))
        a = jnp.dot(jnp.concatenate(ys, axis=1), wbs_ref[...], preferred_element_type=F32)
        b = jnp.dot(yn_ref[r0:r1, :], wbn_ref[...], preferred_element_type=F32)
        mixed = (jax.nn.sigmoid(gs_ref[r0:r1, :].astype(F32)) * a
                 + jax.nn.sigmoid(gn_ref[r0:r1, :].astype(F32)) * b)
        m = jnp.dot(mixed.astype(BF16), wo_ref[...], preferred_element_type=F32)
        h = h_ref[r0:r1, :] + modt_ref[s, 2:3, :] * m
        ho_ref[r0:r1, :] = h
        f = _rms_mod(h, nw, modt_ref[s, 3:4, :], modt_ref[s, 4:5, :])
        if row_tile_out:
            _store_tokens(f_ref, r0, f)
        else:
            f_ref[r0:r1, :] = f.astype(f_ref.dtype)


def _merge(p_main, yf, yb, y_na, h, wbs, wbn, wo, ssd_nw, nw2, modt, row_tile_out, tm=512):
    m, d = h.shape
    nsub = tm // MOD_TILE
    row = lambda i: (i, 0)
    const = lambda i: (0, 0)
    if row_tile_out:
        f_spec, f_shape = pl.BlockSpec((tm * ROW_SEG, LANE), row), jax.ShapeDtypeStruct((m * ROW_SEG, LANE), F32)
    else:
        f_spec, f_shape = pl.BlockSpec((tm, d), row), jax.ShapeDtypeStruct((m, d), BF16)
    return pl.pallas_call(
        functools.partial(_merge_kernel, nsub=nsub, row_tile_out=row_tile_out),
        grid=(m // tm,),
        in_specs=[pl.BlockSpec((tm, d), lambda i: (i, COL_GS // D_MODEL)),
                  pl.BlockSpec((tm, d), lambda i: (i, COL_GN // D_MODEL)),
                  pl.BlockSpec((tm, d), lambda i: (i, COL_Z // D_MODEL)),
                  pl.BlockSpec((tm, d), row), pl.BlockSpec((tm, d), row), pl.BlockSpec((tm, d), row),
                  pl.BlockSpec((tm, d), row),
                  pl.BlockSpec((d, d), const), pl.BlockSpec((d, d), const), pl.BlockSpec((d, d), const),
                  pl.BlockSpec((1, d), const), pl.BlockSpec((1, d), const),
                  pl.BlockSpec((nsub, 6, d), lambda i: (i, 0, 0))],
        out_specs=[pl.BlockSpec((tm, d), row), f_spec],
        out_shape=[jax.ShapeDtypeStruct((m, d), F32), f_shape],
        compiler_params=_params(("arbitrary",)),
        name="merge",
    )(p_main, p_main, p_main, yf, yb, y_na, h, wbs, wbn, wo, ssd_nw.astype(F32).reshape(1, d), nw2.reshape(1, d),
      modt)


def _ffn_kernel(x_ref, h_ref, w1_ref, w3_ref, w2_ref, nw_ref, modt_ref, modn_ref, ho_ref, an_ref, *, nsub):
    x = x_ref[...]
    h1 = jnp.dot(x, w1_ref[...], preferred_element_type=F32)
    h3 = jnp.dot(x, w3_ref[...], preferred_element_type=F32)
    act = (h1 * jax.nn.sigmoid(h1) * h3).astype(BF16)
    out = jnp.dot(act, w2_ref[...], preferred_element_type=F32)
    nw = nw_ref[...]
    for s in range(nsub):
        r0, r1 = s * MOD_TILE, (s + 1) * MOD_TILE
        h = h_ref[r0:r1, :] + modt_ref[s, 5:6, :] * out[r0:r1, :]
        ho_ref[r0:r1, :] = h
        an_ref[r0:r1, :] = _rms_mod(h, nw, modn_ref[s, 0:1, :], modn_ref[s, 1:2, :]).astype(an_ref.dtype)


def _ffn(x, h, w1, w3, w2, nw_next, modt, modt_next, tm=512):
    m, d = h.shape
    ff = w1.shape[1]
    nsub = tm // MOD_TILE
    row = lambda i: (i, 0)
    const = lambda i: (0, 0)
    once = pl.Buffered(1)
    return pl.pallas_call(
        functools.partial(_ffn_kernel, nsub=nsub),
        grid=(m // tm,),
        in_specs=[pl.BlockSpec((tm, d), row), pl.BlockSpec((tm, d), row),
                  pl.BlockSpec((d, ff), const, pipeline_mode=once),
                  pl.BlockSpec((d, ff), const, pipeline_mode=once),
                  pl.BlockSpec((ff, d), const, pipeline_mode=once),
                  pl.BlockSpec((1, d), const),
                  pl.BlockSpec((nsub, 6, d), lambda i: (i, 0, 0)),
                  pl.BlockSpec((nsub, 6, d), lambda i: (i, 0, 0))],
        out_specs=[pl.BlockSpec((tm, d), row), pl.BlockSpec((tm, d), row)],
        out_shape=[jax.ShapeDtypeStruct((m, d), F32), jax.ShapeDtypeStruct((m, d), BF16)],
        compiler_params=_params(("arbitrary",)),
        name="ffn",
    )(x, h, w1, w3, w2, nw_next.reshape(1, d), modt, modt_next)


NA_BAND = NA_ROWS_MAX * GRID_W
HEAD_PAIR = LANE // NA_HEAD_DIM
NA_UNROLL = 8
NA_PREP_UNROLL = 8


def _head_mean_matrix():
    ri = lax.broadcasted_iota(jnp.int32, (2 * LANE, LANE), 0) % LANE
    ci = lax.broadcasted_iota(jnp.int32, (2 * LANE, LANE), 1)
    half = NA_HEAD_DIM // 2
    same_head = (ri % NA_HEAD_DIM) // half == (ci % NA_HEAD_DIM) // half
    return jnp.where(same_head, 1.0 / NA_HEAD_DIM, 0.0).astype(BF16)


def _head_rms(x, w, mean_mat):
    sq = x * x
    hi = sq.astype(BF16)
    lo = (sq - hi.astype(F32)).astype(BF16)
    ms = jnp.dot(jnp.concatenate([hi, lo], axis=1), mean_mat, preferred_element_type=F32)
    return x * lax.rsqrt(ms + EPS) * w


def _qk_lane_order():
    quarter = NA_HEAD_DIM // 4
    order = []
    for half in range(2):
        for head in range(HEAD_PAIR):
            for part in range(2):
                start = head * NA_HEAD_DIM + part * 2 * quarter + half * quarter
                order += list(range(start, start + quarter))
    return order


def _qk_reorder(w):
    k, n = w.shape
    return w.reshape(k, n // LANE, LANE)[:, :, jnp.asarray(_qk_lane_order())].reshape(k, n)


def _rope(x, cos, sin):
    return x * cos + pltpu.roll(x, LANE // 2, axis=1) * sin


def _stack_heads(x, lane_lo):
    zero = jnp.zeros_like(x)
    return jnp.concatenate([jnp.where(lane_lo, x, zero), jnp.where(lane_lo, zero, x)], axis=0)


def _attend(qs, k_list, v_list, bias_list, lane_lo):
    nt = (((1,), (1,)), ((), ()))
    s_list = []
    for k, bias in zip(k_list, bias_list):
        s = lax.dot_general(qs, k, nt, preferred_element_type=F32)
        s_list.append(s if bias is None else s + bias)
    mx = s_list[0].max(axis=-1, keepdims=True)
    for s in s_list[1:]:
        mx = jnp.maximum(mx, s.max(axis=-1, keepdims=True))
    den = None
    acc = None
    for s, v in zip(s_list, v_list):
        p = jnp.exp(s - mx)
        d = p.sum(axis=-1, keepdims=True)
        o = jnp.dot(p.astype(BF16), v, preferred_element_type=F32)
        den = d if den is None else den + d
        acc = o if acc is None else acc + o
    acc = acc / den
    n = acc.shape[0] // 2
    return jnp.where(lane_lo, acc[:n], acc[n:])


def _na_kernel(q_ref, k_ref, v_ref, bias_ref, rowc_ref, rows_ref, colc_ref, cols_ref, qw_ref, kw_ref,
               o_ref, qn_ref, kn_ref, s_ref, m_ref, *, clen, n_rows):
    lane = lax.broadcasted_iota(jnp.int32, (1, LANE), 1)
    lane_lo = lane < NA_HEAD_DIM
    qk_h0 = (lane % NA_HEAD_DIM) < NA_HEAD_DIM // 2
    qw = qw_ref[...] * (NA_HEAD_DIM ** -0.5)
    kw = kw_ref[...]
    kr = NA_ROWS_MAX

    mean_mat = _head_mean_matrix()
    qn_ref[0:clen, :] = _head_rms(q_ref[0:clen, :].astype(F32), qw, mean_mat).astype(BF16)
    kn_ref[0:clen, :] = _head_rms(k_ref[0:clen, :].astype(F32), kw, mean_mat).astype(BF16)
    colc = colc_ref[...]
    cols = cols_ref[...]

    def prep(r, carry):
        rows = pl.ds(pl.multiple_of(clen + r * GRID_W, GRID_W), GRID_W)
        cos = rowc_ref[pl.ds(r, 1), :] + colc
        sin = rows_ref[pl.ds(r, 1), :] + cols
        q = _head_rms(q_ref[rows, :].astype(F32), qw, mean_mat)
        k = _head_rms(k_ref[rows, :].astype(F32), kw, mean_mat)
        qn_ref[rows, :] = _rope(q, cos, sin).astype(BF16)
        kn_ref[rows, :] = _rope(k, cos, sin).astype(BF16)
        return carry

    lax.fori_loop(0, n_rows, prep, 0, unroll=NA_PREP_UNROLL)

    kc = kn_ref[0:clen, :]
    vc = v_ref[0:clen, :]
    for blk in range(clen // GRID_W):
        rows = slice(blk * GRID_W, (blk + 1) * GRID_W)
        qs = _stack_heads(qn_ref[rows, :], qk_h0)
        o_ref[rows, :] = _attend(qs, [kc], [vc], [None], lane_lo).astype(o_ref.dtype)

    nt = (((1,), (1,)), ((), ()))
    n_blk = (NA_BAND + clen) // LANE

    def band_of(r):
        r0 = jnp.clip(r - kr // 2, 0, n_rows - kr)
        return r0, pl.ds(pl.multiple_of(clen + r0 * GRID_W, GRID_W), NA_BAND)

    n_win = NA_BAND // LANE
    sq = HEAD_PAIR * GRID_W

    def scores(rows_r, slots):
        qs = [_stack_heads(qn_ref[pl.ds(pl.multiple_of(clen + r * GRID_W, GRID_W), GRID_W), :], qk_h0)
              for r in rows_r]
        s_ctx = lax.dot_general(jnp.concatenate(qs, axis=0), kn_ref[0:clen, :], nt, preferred_element_type=F32)
        for u, (r, slot) in enumerate(zip(rows_r, slots)):
            r0, band = band_of(r)
            s_win = lax.dot_general(qs[u], kn_ref[band, :], nt, preferred_element_type=F32)
            bias = bias_ref[r0 - r + (NA_ROWS_MAX - 1)]
            blocks = [s_win[:, j * LANE:(j + 1) * LANE] + bias[:, j * LANE:(j + 1) * LANE] for j in range(n_win)]
            blocks += [s_ctx[u * sq:(u + 1) * sq, j * LANE:(j + 1) * LANE] for j in range(n_blk - n_win)]
            mx = blocks[0]
            for j in range(n_blk):
                s_ref[slot, :, j * LANE:(j + 1) * LANE] = blocks[j]
                mx = jnp.maximum(mx, blocks[j])
            m_ref[slot] = jnp.broadcast_to(mx.max(axis=-1, keepdims=True), mx.shape)

    def output(rows_r, slots):
        accs, dens, p_ctx = [], [], []
        for r, slot in zip(rows_r, slots):
            _, band = band_of(r)
            mx = m_ref[slot]
            p = [jnp.exp(s_ref[slot, :, j * LANE:(j + 1) * LANE] - mx) for j in range(n_blk)]
            den = p[0]
            for pj in p[1:]:
                den = den + pj
            dens.append(den.sum(axis=-1, keepdims=True))
            pb = [pj.astype(BF16) for pj in p]
            accs.append(jnp.dot(jnp.concatenate(pb[:n_win], axis=1), v_ref[band, :], preferred_element_type=F32))
            p_ctx.append(jnp.concatenate(pb[n_win:], axis=1))
        acc_ctx = jnp.dot(jnp.concatenate(p_ctx, axis=0), v_ref[0:clen, :], preferred_element_type=F32)
        for u, r in enumerate(rows_r):
            acc = (accs[u] + acc_ctx[u * sq:(u + 1) * sq]) / dens[u]
            rows = pl.ds(pl.multiple_of(clen + r * GRID_W, GRID_W), GRID_W)
            o_ref[rows, :] = jnp.where(lane_lo, acc[:GRID_W], acc[GRID_W:]).astype(o_ref.dtype)

    scores(list(range(NA_UNROLL)), list(range(NA_UNROLL)))

    def body(j, carry):
        r = 2 * NA_UNROLL * j
        for half in range(2):
            output([r + half * NA_UNROLL + u for u in range(NA_UNROLL)],
                   [half * NA_UNROLL + u for u in range(NA_UNROLL)])
            scores([jnp.minimum(r + (half + 1) * NA_UNROLL + u, n_rows - 1) for u in range(NA_UNROLL)],
                   [(1 - half) * NA_UNROLL + u for u in range(NA_UNROLL)])
        return carry

    lax.fori_loop(0, n_rows // (2 * NA_UNROLL), body, 0)


def _na_tables(rpb, n_rows):
    col = jnp.arange(GRID_W)
    col_start = jnp.clip(col - NA_COLS // 2, 0, GRID_W - NA_COLS)
    col_ok = (col[None, :] >= col_start[:, None]) & (col[None, :] < col_start[:, None] + NA_COLS)
    col_idx = jnp.clip(col[None, :] - col[:, None], -(NA_COLS - 1), NA_COLS - 1) + NA_COLS - 1
    rpb_c = jnp.where(col_ok[None, None], rpb.astype(F32)[:, :, col_idx], NEG_INF)
    dr = jnp.arange(NA_ROWS_MAX)[:, None] + jnp.arange(NA_ROWS_MAX)[None, :]
    tab = rpb_c[:, dr]
    tab = jnp.transpose(tab, (0, 1, 3, 2, 4)).reshape(NA_HEADS, NA_ROWS_MAX, GRID_W, NA_BAND)
    tab = tab.reshape(NA_HEADS // HEAD_PAIR, HEAD_PAIR, NA_ROWS_MAX, GRID_W, NA_BAND)
    tab = jnp.transpose(tab, (0, 2, 1, 3, 4)).reshape(NA_HEADS // HEAD_PAIR, NA_ROWS_MAX,
                                                       HEAD_PAIR * GRID_W, NA_BAND)
    quarter = NA_HEAD_DIM // 4
    inv = ROPE_BASE ** (-jnp.arange(quarter, dtype=F32) / quarter)

    def tables(pos, lo):
        ang = pos[:, None] * inv[None, :]
        z = jnp.zeros_like(ang)
        c, s = jnp.cos(ang), jnp.sin(ang)
        ch = jnp.concatenate([c, c, z, z] if lo else [z, z, c, c], axis=-1)
        sh = jnp.concatenate([-s, s, z, z] if lo else [z, z, -s, s], axis=-1)
        order = jnp.asarray(_qk_lane_order())
        return jnp.tile(ch, (1, HEAD_PAIR))[:, order], jnp.tile(sh, (1, HEAD_PAIR))[:, order]

    rowc, rows = tables(jnp.arange(n_rows).astype(F32), True)
    colc, cols = tables(jnp.arange(GRID_W).astype(F32), False)
    return tab, rowc, rows, colc, cols


def _na(pm3, q_norm_w, k_norm_w, rpb, clen):
    bsz, t, _ = pm3.shape
    n_rows = (t - clen) // GRID_W
    tab, rowc, rows, colc, cols = _na_tables(rpb, n_rows)
    order = jnp.asarray(_qk_lane_order())
    qw = jnp.tile(q_norm_w.astype(F32), HEAD_PAIR)[order].reshape(1, LANE)
    kw = jnp.tile(k_norm_w.astype(F32), HEAD_PAIR)[order].reshape(1, LANE)
    n_pairs = NA_HEADS // HEAD_PAIR
    const = lambda p, b: (0, 0)
    tok = lambda c0: pl.BlockSpec((None, t, LANE), lambda p, b: (b, 0, c0 // LANE + p))
    return pl.pallas_call(
        functools.partial(_na_kernel, clen=clen, n_rows=n_rows),
        grid=(n_pairs, bsz),
        in_specs=[tok(COL_Q), tok(COL_K), tok(COL_V),
                  pl.BlockSpec((None, NA_ROWS_MAX, HEAD_PAIR * GRID_W, NA_BAND), lambda p, b: (p, 0, 0, 0)),
                  pl.BlockSpec((n_rows, LANE), const), pl.BlockSpec((n_rows, LANE), const),
                  pl.BlockSpec((GRID_W, LANE), const), pl.BlockSpec((GRID_W, LANE), const),
                  pl.BlockSpec((1, LANE), const), pl.BlockSpec((1, LANE), const)],
        out_specs=pl.BlockSpec((None, t, LANE), lambda p, b: (b, 0, p)),
        out_shape=jax.ShapeDtypeStruct((bsz, t, D_NA), BF16),
        scratch_shapes=[pltpu.VMEM((t, LANE), BF16), pltpu.VMEM((t, LANE), BF16),
                        pltpu.VMEM((2 * NA_UNROLL, HEAD_PAIR * GRID_W, NA_BAND + clen), F32),
                        pltpu.VMEM((2 * NA_UNROLL, HEAD_PAIR * GRID_W, LANE), F32)],
        compiler_params=_params(("arbitrary", "arbitrary")),
        name="natten",
    )(pm3, pm3, pm3, tab, rowc, rows, colc, cols, qw, kw)


HALO = 16
CONV_ROWS = SSD_CHUNK
CONV_PAD = SSD_CONV // 2
BC_DIM = SSD_GROUPS * SSD_STATE
PAIRS_PER_GROUP = SSD_HPG // HEAD_PAIR


def _ssd_chunk_index(s, rev, cc, nc):
    if not rev:
        return s
    return jnp.where(s < cc, cc - 1 - s, nc - 1 - (s - cc))


def _ssd_conv_kernel(main_ref, prev_ref, next_ref, cw_ref, cb_ref, o_ref, *, cc, nc):
    q = main_ref.shape[0]
    cidx = pl.program_id(1)
    first_col = jnp.where((cidx == 0) | (cidx == cc), HALO, 0)
    end_col = jnp.where((cidx == cc - 1) | (cidx == nc - 1), HALO + q, q + 2 * HALO)
    ext = jnp.concatenate([prev_ref[...], main_ref[...], next_ref[...]], axis=0)
    ti = lax.broadcasted_iota(jnp.int32, (q, q + 2 * HALO), 0)
    ji = lax.broadcasted_iota(jnp.int32, (q, q + 2 * HALO), 1)
    in_segment = (ji >= first_col) & (ji < end_col)
    conv = cb_ref[...] + cw_ref[CONV_PAD:CONV_PAD + 1, :] * main_ref[...].astype(F32)
    for k in range(SSD_CONV):
        if k == CONV_PAD:
            continue
        shift = jnp.where((ji == ti + (HALO + k - CONV_PAD)) & in_segment, 1.0, 0.0).astype(BF16)
        conv = conv + cw_ref[k:k + 1, :] * jnp.dot(shift, ext, preferred_element_type=F32)
    o_ref[...] = (conv * jax.nn.sigmoid(conv)).astype(o_ref.dtype)


def _ssd_core(xbc_ref, dt_ref, dtb_ref, a_ref, exp_ref, h_ref, yoff_ref, y_ref, *, rev):
    q = SSD_CHUNK
    xbc = xbc_ref[...]

    hb = SSD_HEADS if rev else 0
    x = dt_ref[...] + dtb_ref[...]
    dt = jnp.maximum(x, 0.0) + jnp.log(1.0 + jnp.exp(-jnp.abs(x)))
    la = dt * a_ref[...]
    ri = lax.broadcasted_iota(jnp.int32, (q, q), 0)
    ci = lax.broadcasted_iota(jnp.int32, (q, q), 1)
    allowed = (ci >= ri) if rev else (ci <= ri)
    tri = jnp.where(allowed, 1.0, 0.0).astype(F32)
    a_cs = jnp.dot(tri, la, precision=HIGHEST, preferred_element_type=F32)
    last = 0 if rev else q - 1
    a_cs_t = a_cs.T
    e_all = jnp.exp(a_cs)
    src_t = a_cs_t - jnp.log(dt.T)
    wdt_t = jnp.exp(a_cs_t[:, last:last + 1] - src_t)
    neg_mask = jnp.where(allowed, 0.0, NEG_INF)
    blk = (last // 8) * 8
    e_last = jnp.exp(jnp.dot(a_cs[blk:blk + 8, :], exp_ref[...], precision=HIGHEST,
                             preferred_element_type=F32))[last - blk:last - blk + 1, hb * SSD_HEAD_DIM:
                                                          (hb + SSD_HEADS) * SSD_HEAD_DIM]

    lane_lo = lax.broadcasted_iota(jnp.int32, (1, LANE), 1) < SSD_HEAD_DIM
    nt = (((1,), (1,)), ((), ()))
    for g in range(SSD_GROUPS):
        b_g = xbc[:, D_SSD + g * SSD_STATE:D_SSD + (g + 1) * SSD_STATE]
        c_g = xbc[:, D_SSD + BC_DIM + g * SSD_STATE:D_SSD + BC_DIM + (g + 1) * SSD_STATE]
        cols = slice(g * SSD_HPG * SSD_HEAD_DIM, (g + 1) * SSD_HPG * SSD_HEAD_DIM)
        yoff_ref[:, cols] = jnp.dot(c_g, h_ref[:, cols].astype(BF16), preferred_element_type=F32)
        cb = lax.dot_general(c_g, b_g, nt, preferred_element_type=F32)
        b_t = b_g.astype(F32).T
        for pj in range(PAIRS_PER_GROUP):
            pair = g * PAIRS_PER_GROUP + pj
            lanes = slice(pair * LANE, (pair + 1) * LANE)
            lhs, ecol = [], []
            for hh in range(HEAD_PAIR):
                head = hb + pair * HEAD_PAIR + hh
                seg = (a_cs[:, head:head + 1] - src_t[head:head + 1, :]) + neg_mask
                lhs.append((cb * jnp.exp(seg)).astype(BF16))
                ecol.append(jnp.broadcast_to(e_all[:, head:head + 1], (q, LANE)))
            for hh in range(HEAD_PAIR):
                head = hb + pair * HEAD_PAIR + hh
                lhs.append((b_t * wdt_t[head:head + 1, :]).astype(BF16))
            res = jnp.dot(jnp.concatenate(lhs, axis=0), xbc[:, lanes], preferred_element_type=F32)
            y_ref[:, lanes] = (jnp.where(lane_lo, res[0:q], res[q:2 * q])
                               + yoff_ref[:, lanes] * jnp.where(lane_lo, ecol[0], ecol[1]))
            h_ref[:, lanes] = (h_ref[:, lanes] * e_last[:, lanes]
                               + jnp.where(lane_lo, res[2 * q:3 * q], res[3 * q:4 * q]))


def _ssd_scan_kernel(xf_ref, dtf_ref, xb_ref, dtb_ref, bias_ref, a_ref, exp_ref, dsum_ref, of_ref, ob_ref,
                     hf_ref, yofff_ref, hb_ref, yoffb_ref, yf_ref, yb_ref):
    @pl.when(pl.program_id(1) == 0)
    def _():
        hf_ref[...] = jnp.zeros_like(hf_ref)
        hb_ref[...] = jnp.zeros_like(hb_ref)

    _ssd_core(xf_ref, dtf_ref, bias_ref, a_ref, exp_ref, hf_ref, yofff_ref, yf_ref, rev=False)
    _ssd_core(xb_ref, dtb_ref, bias_ref, a_ref, exp_ref, hb_ref, yoffb_ref, yb_ref, rev=True)
    of_ref[...] = (yf_ref[...] + xf_ref[:, 0:D_SSD].astype(F32) * dsum_ref[...]).astype(of_ref.dtype)
    ob_ref[...] = yb_ref[...].astype(ob_ref.dtype)


def _ssd(pm3, pdt3, conv_w, conv_b, dt_bias, a_log, d_skip, clen):
    bsz, t, _ = pm3.shape
    q = SSD_CHUNK
    nc, cc = t // q, clen // q
    nh = 2 * SSD_HEADS
    cw = jnp.pad(conv_w.astype(F32), ((0, 8 - SSD_CONV), (0, 0)))
    cb = conv_b.astype(F32).reshape(1, XBC_DIM)
    dtb = jnp.pad(dt_bias.astype(F32).reshape(1, nh), ((0, 0), (0, DT_PAD - nh)))
    a_neg = jnp.pad(-jnp.exp(a_log.astype(F32)).reshape(1, nh), ((0, 0), (0, DT_PAD - nh)))
    dsum = jnp.repeat((d_skip[0] + d_skip[1]).astype(F32), SSD_HEAD_DIM).reshape(1, D_SSD)
    expand = jnp.repeat(jnp.eye(DT_PAD, dtype=F32)[:, :nh], SSD_HEAD_DIM, axis=1)
    n_halo = t // HALO

    const = lambda b, s: (0, 0)
    xcol = COL_XBC // XBC_DIM
    cq = CONV_ROWS
    hpb = cq // HALO
    xbc = pl.pallas_call(
        functools.partial(_ssd_conv_kernel, cc=clen // cq, nc=t // cq),
        grid=(bsz, t // cq),
        in_specs=[pl.BlockSpec((None, cq, XBC_DIM), lambda b, s: (b, s, xcol)),
                  pl.BlockSpec((None, HALO, XBC_DIM), lambda b, s: (b, jnp.maximum(s * hpb - 1, 0), xcol)),
                  pl.BlockSpec((None, HALO, XBC_DIM),
                               lambda b, s: (b, jnp.minimum((s + 1) * hpb, n_halo - 1), xcol)),
                  pl.BlockSpec((8, XBC_DIM), const), pl.BlockSpec((1, XBC_DIM), const)],
        out_specs=pl.BlockSpec((None, cq, XBC_DIM), lambda b, s: (b, s, 0)),
        out_shape=jax.ShapeDtypeStruct((bsz, t, XBC_DIM), BF16),
        compiler_params=_params(("arbitrary", "arbitrary")),
        name="ssd_conv",
    )(pm3, pm3, pm3, cw, cb)

    tok_f = lambda b, s: (b, s, 0)
    tok_b = lambda b, s: (b, _ssd_chunk_index(s, True, cc, nc), 0)
    state = [pltpu.VMEM((SSD_STATE, D_SSD), F32), pltpu.VMEM((q, D_SSD), F32)]
    return pl.pallas_call(
        _ssd_scan_kernel,
        grid=(bsz, nc),
        in_specs=[pl.BlockSpec((None, q, XBC_DIM), tok_f), pl.BlockSpec((None, q, DT_PAD), tok_f),
                  pl.BlockSpec((None, q, XBC_DIM), tok_b), pl.BlockSpec((None, q, DT_PAD), tok_b),
                  pl.BlockSpec((1, DT_PAD), const), pl.BlockSpec((1, DT_PAD), const),
                  pl.BlockSpec((DT_PAD, nh * SSD_HEAD_DIM), const), pl.BlockSpec((1, D_SSD), const)],
        out_specs=[pl.BlockSpec((None, q, D_SSD), tok_f), pl.BlockSpec((None, q, D_SSD), tok_b)],
        out_shape=[jax.ShapeDtypeStruct((bsz, t, D_SSD), BF16), jax.ShapeDtypeStruct((bsz, t, D_SSD), BF16)],
        scratch_shapes=state + state + [pltpu.VMEM((q, D_SSD), F32), pltpu.VMEM((q, D_SSD), F32)],
        compiler_params=_params(("arbitrary", "arbitrary")),
        name="ssd_scan",
    )(xbc, pdt3, xbc, pdt3, dtb, a_neg, expand, dsum)


MOE_TB = 512
ROUTE_TM = 512
INFO_W = 8
BIG_NEG = -3.0e38
ROW_SEG = D_MODEL // LANE


def _load_tokens(ref, t0, n):
    return jnp.concatenate([ref[pl.ds(t0 * ROW_SEG + c, n, stride=ROW_SEG), :] for c in range(ROW_SEG)], axis=1)


def _store_tokens(ref, t0, val):
    n = val.shape[0]
    for c in range(ROW_SEG):
        ref[pl.ds(t0 * ROW_SEG + c, n, stride=ROW_SEG), :] = val[:, c * LANE:(c + 1) * LANE]


def _route_kernel(f_ref, wr_ref, info_ref, cnt_ref, run_ref):
    tm = f_ref.shape[0] // ROW_SEG

    @pl.when(pl.program_id(0) == 0)
    def _():
        run_ref[...] = jnp.zeros_like(run_ref)

    logits = jnp.dot(_load_tokens(f_ref, 0, tm).astype(BF16), wr_ref[...], preferred_element_type=F32)
    lane = lax.broadcasted_iota(jnp.int32, (tm, LANE), 1).astype(F32)
    lg = jnp.where(lane < N_EXPERTS, logits, BIG_NEG)
    m1 = lg.max(axis=-1, keepdims=True)
    i1 = jnp.where(lg == m1, lane, float(LANE)).min(axis=-1, keepdims=True)
    lg2 = jnp.where(lane == i1, BIG_NEG, lg)
    m2 = lg2.max(axis=-1, keepdims=True)
    i2 = jnp.where(lg2 == m2, lane, float(LANE)).min(axis=-1, keepdims=True)
    ex = jnp.exp(m2 - m1)
    g0 = 1.0 / (1.0 + ex)
    g1 = ex / (1.0 + ex)
    sel0 = lane == i1
    sel1 = lane == i2
    onehot = jnp.where(sel0 | sel1, 1.0, 0.0)
    ri = lax.broadcasted_iota(jnp.int32, (tm, tm), 0)
    ci = lax.broadcasted_iota(jnp.int32, (tm, tm), 1)
    strict_lower = jnp.where(ci < ri, 1.0, 0.0).astype(BF16)
    before = jnp.dot(strict_lower, onehot.astype(BF16), preferred_element_type=F32) + run_ref[...]
    pos0 = jnp.where(sel0, before, 0.0).sum(axis=-1, keepdims=True)
    pos1 = jnp.where(sel1, before, 0.0).sum(axis=-1, keepdims=True)
    run_ref[...] = run_ref[...] + onehot.sum(axis=0, keepdims=True)
    cnt_ref[...] = run_ref[...]
    info = jnp.where(lane == 0, i1, jnp.where(lane == 1, i2, jnp.where(lane == 2, g0, jnp.where(
        lane == 3, g1, jnp.where(lane == 4, pos0, jnp.where(lane == 5, pos1, 0.0))))))
    info_ref[...] = info[:, :INFO_W]


def _route(f8, w_router):
    m, d = f8.shape[0] // ROW_SEG, D_MODEL
    wr = jnp.pad(w_router.astype(BF16), ((0, 0), (0, LANE - N_EXPERTS)))
    return pl.pallas_call(
        _route_kernel,
        grid=(m // ROUTE_TM,),
        in_specs=[pl.BlockSpec((ROUTE_TM * ROW_SEG, LANE), lambda i: (i, 0)),
                  pl.BlockSpec((d, LANE), lambda i: (0, 0))],
        out_specs=[pl.BlockSpec((ROUTE_TM, INFO_W), lambda i: (i, 0)), pl.BlockSpec((1, LANE), lambda i: (0, 0))],
        out_shape=[jax.ShapeDtypeStruct((m, INFO_W), F32), jax.ShapeDtypeStruct((1, LANE), F32)],
        scratch_shapes=[pltpu.VMEM((1, LANE), F32)],
        compiler_params=_params(("arbitrary",)),
        name="moe_route",
    )(f8, wr)


def _row_copy(src_ref, src_row, dst_ref, dst_row, sem):
    src = src_ref.at[pl.ds(pl.multiple_of(src_row * ROW_SEG, ROW_SEG), ROW_SEG)]
    dst = dst_ref.at[pl.ds(pl.multiple_of(dst_row * ROW_SEG, ROW_SEG), ROW_SEG)]
    return pltpu.make_async_copy(src, dst, sem)


def _wait_rows(src_ref, dst_ref, sem, n):
    def wait(t, c):
        _row_copy(src_ref, 0, dst_ref, 0, sem).wait()
        return c

    lax.fori_loop(0, n, wait, 0, unroll=8)


def _dispatch_kernel(pad_ref, dest_hbm, f_ref, rows_hbm, dest_smem, stage_ref, zero_ref, idx_sem, row_sem,
                     pad_sem):
    tm = f_ref.shape[0] // ROW_SEG
    i, n = pl.program_id(0), pl.num_programs(0)
    slot = i % 2
    stage = stage_ref.at[slot]
    sem = row_sem.at[slot]

    @pl.when(i == 0)
    def _():
        zero_ref[...] = jnp.zeros_like(zero_ref)
        for e in range(N_EXPERTS):
            first, count = pad_ref[e], pad_ref[N_EXPERTS + e]

            def start_zero(j, c):
                _row_copy(zero_ref, 0, rows_hbm, first + j, pad_sem).start()
                return c

            lax.fori_loop(0, count, start_zero, 0)
        for e in range(N_EXPERTS):
            def wait_zero(j, c):
                _row_copy(zero_ref, 0, rows_hbm, 0, pad_sem).wait()
                return c

            lax.fori_loop(0, pad_ref[N_EXPERTS + e], wait_zero, 0)

    @pl.when(i >= 2)
    def _():
        _wait_rows(stage, rows_hbm, sem, TOP_K * tm)

    stage[...] = f_ref[...]
    base = pl.multiple_of(i * (TOP_K * tm), TOP_K * tm)
    idx_copy = pltpu.make_async_copy(dest_hbm.at[pl.ds(base, TOP_K * tm)], dest_smem, idx_sem)
    idx_copy.start()
    idx_copy.wait()

    def start(t, c):
        for k in range(TOP_K):
            _row_copy(stage, t, rows_hbm, dest_smem[TOP_K * t + k], sem).start()
        return c

    lax.fori_loop(0, tm, start, 0, unroll=8)

    @pl.when(i == n - 1)
    def _():
        _wait_rows(stage, rows_hbm, sem, TOP_K * tm)

        @pl.when(n >= 2)
        def _():
            _wait_rows(stage_ref.at[1 - slot], rows_hbm, row_sem.at[1 - slot], TOP_K * tm)


def _dispatch(f8, dest_flat, pad_info, n_rows):
    m = f8.shape[0] // ROW_SEG
    grid_spec = pltpu.PrefetchScalarGridSpec(
        num_scalar_prefetch=1,
        grid=(m // ROUTE_TM,),
        in_specs=[pl.BlockSpec(memory_space=pl.ANY),
                  pl.BlockSpec((ROUTE_TM * ROW_SEG, LANE), lambda i, pad: (i, 0))],
        out_specs=pl.BlockSpec(memory_space=pl.ANY),
        scratch_shapes=[pltpu.SMEM((TOP_K * ROUTE_TM,), jnp.int32),
                        pltpu.VMEM((2, ROUTE_TM * ROW_SEG, LANE), f8.dtype),
                        pltpu.VMEM((ROW_SEG, LANE), f8.dtype),
                        pltpu.SemaphoreType.DMA(()), pltpu.SemaphoreType.DMA((2,)), pltpu.SemaphoreType.DMA(())],
    )
    return pl.pallas_call(
        _dispatch_kernel,
        grid_spec=grid_spec,
        out_shape=jax.ShapeDtypeStruct((n_rows * ROW_SEG, LANE), f8.dtype),
        compiler_params=_params(("arbitrary",)),
        name="moe_dispatch",
    )(pad_info, dest_flat, f8)


def _gffn_kernel(be_ref, nu_ref, x_ref, w1_ref, w3_ref, w2_ref, y_ref):
    i = pl.program_id(0)

    @pl.when(i < nu_ref[0])
    def _():
        x = _load_tokens(x_ref, 0, MOE_TB).astype(BF16)
        h1 = jnp.dot(x, w1_ref[...], preferred_element_type=F32)
        h3 = jnp.dot(x, w3_ref[...], preferred_element_type=F32)
        act = (h1 * jax.nn.sigmoid(h1) * h3).astype(BF16)
        _store_tokens(y_ref, 0, jnp.dot(act, w2_ref[...], preferred_element_type=F32))

    @pl.when(i >= nu_ref[0])
    def _():
        y_ref[...] = jnp.zeros_like(y_ref)


def _gffn(rows, block_e, n_used, w1, w3, w2):
    r, d = rows.shape[0] // ROW_SEG, D_MODEL
    ff = w1.shape[2]
    nb = r // MOE_TB
    once = pl.Buffered(1)
    grid_spec = pltpu.PrefetchScalarGridSpec(
        num_scalar_prefetch=2,
        grid=(nb,),
        in_specs=[pl.BlockSpec((MOE_TB * ROW_SEG, LANE), lambda i, be, nu: (i, 0)),
                  pl.BlockSpec((None, d, ff), lambda i, be, nu: (be[i], 0, 0), pipeline_mode=once),
                  pl.BlockSpec((None, d, ff), lambda i, be, nu: (be[i], 0, 0), pipeline_mode=once),
                  pl.BlockSpec((None, ff, d), lambda i, be, nu: (be[i], 0, 0), pipeline_mode=once)],
        out_specs=pl.BlockSpec((MOE_TB * ROW_SEG, LANE), lambda i, be, nu: (i, 0)),
    )
    return pl.pallas_call(
        _gffn_kernel,
        grid_spec=grid_spec,
        out_shape=jax.ShapeDtypeStruct((r * ROW_SEG, LANE), F32),
        compiler_params=_params(("arbitrary",)),
        name="moe_ffn",
    )(block_e, n_used, rows, w1, w3, w2)


def _combine_kernel(dest_hbm, y_hbm, h_ref, info_ref, modt_ref, o_ref, dest_smem, buf_ref, idx_sem, row_sem,
                    *, nsub):
    tm = h_ref.shape[0]
    i, n = pl.program_id(0), pl.num_programs(0)

    def gather(step, slot):
        base = pl.multiple_of(step * (TOP_K * tm), TOP_K * tm)
        idx_copy = pltpu.make_async_copy(dest_hbm.at[pl.ds(base, TOP_K * tm)], dest_smem, idx_sem)
        idx_copy.start()
        idx_copy.wait()

        def start(t, c):
            for k in range(TOP_K):
                _row_copy(y_hbm, dest_smem[TOP_K * t + k], buf_ref.at[slot, k], t, row_sem.at[slot]).start()
            return c

        lax.fori_loop(0, tm, start, 0, unroll=8)

    @pl.when(i == 0)
    def _():
        gather(0, 0)

    @pl.when(i + 1 < n)
    def _():
        gather(i + 1, (i + 1) % 2)

    slot = i % 2
    _wait_rows(y_hbm, buf_ref.at[slot, 0], row_sem.at[slot], TOP_K * tm)
    for s in range(nsub):
        r0, r1 = s * MOD_TILE, (s + 1) * MOD_TILE
        info = info_ref[r0:r1, :]
        mix = (info[:, 2:3] * _load_tokens(buf_ref.at[slot, 0], r0, MOD_TILE)
               + info[:, 3:4] * _load_tokens(buf_ref.at[slot, 1], r0, MOD_TILE))
        o_ref[r0:r1, :] = h_ref[r0:r1, :] + modt_ref[s, 5:6, :] * mix


def _combine(y_rows, dest_flat, h, info, modt):
    m, d = h.shape
    nsub = ROUTE_TM // MOD_TILE
    return pl.pallas_call(
        functools.partial(_combine_kernel, nsub=nsub),
        grid=(m // ROUTE_TM,),
        in_specs=[pl.BlockSpec(memory_space=pl.ANY), pl.BlockSpec(memory_space=pl.ANY),
                  pl.BlockSpec((ROUTE_TM, d), lambda i: (i, 0)),
                  pl.BlockSpec((ROUTE_TM, INFO_W), lambda i: (i, 0)),
                  pl.BlockSpec((nsub, 6, d), lambda i: (i, 0, 0))],
        out_specs=pl.BlockSpec((ROUTE_TM, d), lambda i: (i, 0)),
        out_shape=jax.ShapeDtypeStruct((m, d), F32),
        scratch_shapes=[pltpu.SMEM((TOP_K * ROUTE_TM,), jnp.int32),
                        pltpu.VMEM((2, TOP_K, ROUTE_TM * ROW_SEG, LANE), F32),
                        pltpu.SemaphoreType.DMA(()), pltpu.SemaphoreType.DMA((2,))],
        compiler_params=_params(("arbitrary",)),
        name="moe_combine",
    )(dest_flat, y_rows, h, info, modt)


def _moe(f8, h, w_router, w1, w3, w2, modt):
    m, d = h.shape
    info, cnt = _route(f8, w_router)
    e = info[:, 0:TOP_K].astype(jnp.int32)
    pos = info[:, 4:4 + TOP_K].astype(jnp.int32)
    counts = cnt[0, :N_EXPERTS].astype(jnp.int32)
    padded = (counts + MOE_TB - 1) // MOE_TB * MOE_TB
    pad_end = jnp.cumsum(padded)
    pad_start = pad_end - padded
    dest_flat = (pad_start[e] + pos).reshape(-1)
    n_rows = m * TOP_K + N_EXPERTS * MOE_TB
    nb = n_rows // MOE_TB
    block_e = jnp.minimum(jnp.searchsorted(pad_end, jnp.arange(nb, dtype=jnp.int32) * MOE_TB, side='right'),
                          N_EXPERTS - 1).astype(jnp.int32)
    n_used = (pad_end[-1:] // MOE_TB).astype(jnp.int32)
    pad_info = jnp.concatenate([pad_start + counts, padded - counts]).astype(jnp.int32)
    rows = _dispatch(f8, dest_flat, pad_info, n_rows)
    y_rows = _gffn(rows, block_e, n_used, w1.astype(BF16), w3.astype(BF16), w2.astype(BF16))
    return _combine(y_rows, dest_flat, h, info, modt)


def _tile_mod(mod_l, bsz, tiles_per_batch):
    t = jnp.arange(bsz * tiles_per_batch)
    idx = jnp.where(t % tiles_per_batch == 0, bsz, t // tiles_per_batch)
    return jnp.transpose(mod_l[:, idx, :], (1, 0, 2))


def kernel(x, c, ctx, c_ctx, w_mod, b_mod, norm1_w, norm2_w, w_in, conv_w, conv_b, dt_bias, a_log, d_skip,
           ssd_norm_w, q_norm_w, k_norm_w, rpb, w_br_ssd, w_br_na, w_out, w_ff1, w_ff3, w_ff2, w_router,
           w_e1, w_e3, w_e2):
    bsz, seqlen, d = x.shape
    clen = ctx.shape[1]
    n_layers = w_mod.shape[0]
    assert clen == MOD_TILE and seqlen % MOD_TILE == 0 and bsz < 16
    t = clen + seqlen
    m = bsz * t
    tiles_per_batch = t // MOD_TILE

    cond = jnp.zeros((16, d), F32).at[:bsz].set(c).at[bsz].set(c_ctx)
    mod = _modulation(cond, w_mod, b_mod)
    modt = [_tile_mod(mod[i], bsz, tiles_per_batch) for i in range(n_layers)]

    h = jnp.concatenate([ctx, x], axis=1).reshape(m, d)
    a = _adaln(h, norm1_w[0], modt[0])
    for i in range(n_layers):
        last = i == n_layers - 1
        wi = w_in[i]
        w_main = jnp.concatenate([wi[:, :IN_SPLITS[3]], _qk_reorder(wi[:, IN_SPLITS[4]:IN_SPLITS[5]]),
                                  _qk_reorder(wi[:, IN_SPLITS[5]:IN_SPLITS[6]]), wi[:, IN_SPLITS[6]:]],
                                 axis=1).astype(BF16)
        w_dt = jnp.pad(wi[:, IN_SPLITS[3]:IN_SPLITS[4]], ((0, 0), (0, DT_PAD - 2 * SSD_HEADS))).astype(BF16)
        p_main = _matmul(a, w_main, BF16, 512, 2560, "inproj")
        p_dt = _matmul(a, w_dt, F32, 1024, DT_PAD, "inproj_dt")
        pm3 = p_main.reshape(bsz, t, MAIN_DIM)
        yf, yb = _ssd(pm3, p_dt.reshape(bsz, t, DT_PAD), conv_w[i], conv_b[i], dt_bias[i], a_log[i],
                      d_skip[i], clen)
        y_na = _na(pm3, q_norm_w[i], k_norm_w[i], rpb[i], clen).reshape(m, d)
        fi = i // 2
        moe_layer = i % 2 == 1
        h, f = _merge(p_main, yf.reshape(m, d), yb.reshape(m, d), y_na, h, w_br_ssd[i].astype(BF16),
                      w_br_na[i].astype(BF16), w_out[i].astype(BF16), ssd_norm_w[i], norm2_w[i], modt[i],
                      row_tile_out=moe_layer)
        nxt = min(i + 1, n_layers - 1)
        if not moe_layer:
            h, a = _ffn(f, h, w_ff1[fi].astype(BF16), w_ff3[fi].astype(BF16), w_ff2[fi].astype(BF16),
                        norm1_w[nxt], modt[i], modt[nxt])
        else:
            h = _moe(f, h, w_router[fi], w_e1[fi], w_e3[fi], w_e2[fi], modt[i])
            if not last:
                a = _adaln(h, norm1_w[nxt], modt[nxt])
    return h.reshape(bsz, t, d)[:, clen:]
```

```python
import functools
import math

import jax
import jax.numpy as jnp
from jax import lax
from jax.experimental import pallas as pl
from jax.experimental.pallas import tpu as pltpu

D_MODEL = 1024
GRID_W = 64
SSD_HEAD_DIM = 64
SSD_HEADS = D_MODEL // SSD_HEAD_DIM
D_SSD = SSD_HEADS * SSD_HEAD_DIM
SSD_GROUPS = 2
SSD_HPG = SSD_HEADS // SSD_GROUPS
SSD_STATE = 128
SSD_CONV = 5
SSD_CHUNK = 128
XBC_DIM = D_SSD + 2 * SSD_GROUPS * SSD_STATE
NA_HEAD_DIM = 64
NA_HEADS = D_MODEL // NA_HEAD_DIM
D_NA = NA_HEADS * NA_HEAD_DIM
NA_ROWS_MAX = 8
NA_COLS = 16
ROPE_BASE = 10000.0
N_EXPERTS = 8
TOP_K = 2
EPS = 1e-6
NEG_INF = -1e30
IN_SPLITS = (D_MODEL, 2 * D_MODEL, 2 * D_MODEL + D_SSD, 2 * D_MODEL + D_SSD + XBC_DIM,
             2 * D_MODEL + D_SSD + XBC_DIM + 2 * SSD_HEADS,
             2 * D_MODEL + D_SSD + XBC_DIM + 2 * SSD_HEADS + D_NA,
             2 * D_MODEL + D_SSD + XBC_DIM + 2 * SSD_HEADS + 2 * D_NA)

F32 = jnp.float32
BF16 = jnp.bfloat16
HIGHEST = lax.Precision.HIGHEST

MOD_TILE = 256
LANE = 128
VMEM_LIMIT = 48 * 1024 * 1024
COL_GS, COL_GN, COL_Z, COL_XBC, COL_Q, COL_K, COL_V = 0, 1024, 2048, 3072, 4608, 5632, 6656
MAIN_DIM = 7680
DT_PAD = LANE


def _params(sem):
    return pltpu.CompilerParams(dimension_semantics=sem, vmem_limit_bytes=VMEM_LIMIT)


def _rms_mod(x, nw, shift, scale):
    ms = jnp.mean(x * x, axis=-1, keepdims=True)
    return (x * lax.rsqrt(ms + EPS) * nw) * (1.0 + scale) + shift


def _mod_kernel(cond_ref, w_ref, b_ref, o_ref):
    c = cond_ref[...]
    s = c * jax.nn.sigmoid(c)
    o_ref[...] = jnp.dot(s, w_ref[...], precision=HIGHEST, preferred_element_type=F32) + b_ref[...]


def _modulation(cond, w_mod, b_mod):
    n_layers, d, _ = w_mod.shape
    rows = cond.shape[0]
    b4 = b_mod.reshape(n_layers, 6, 1, d)
    return pl.pallas_call(
        _mod_kernel,
        grid=(n_layers, 6),
        in_specs=[pl.BlockSpec((rows, d), lambda l, k: (0, 0)),
                  pl.BlockSpec((None, d, d), lambda l, k: (l, 0, k)),
                  pl.BlockSpec((None, None, 1, d), lambda l, k: (l, k, 0, 0))],
        out_specs=pl.BlockSpec((None, None, rows, d), lambda l, k: (l, k, 0, 0)),
        out_shape=jax.ShapeDtypeStruct((n_layers, 6, rows, d), F32),
        compiler_params=_params(("arbitrary", "arbitrary")),
        name="modulation",
    )(cond, w_mod, b4)


def _adaln_kernel(h_ref, nw_ref, modt_ref, o_ref, *, nsub):
    nw = nw_ref[...]
    for s in range(nsub):
        rows = pl.ds(s * MOD_TILE, MOD_TILE)
        x = h_ref[rows, :]
        o_ref[rows, :] = _rms_mod(x, nw, modt_ref[s, 0:1, :], modt_ref[s, 1:2, :]).astype(o_ref.dtype)


def _adaln(h, nw, modt, tm=1024):
    m, d = h.shape
    nsub = tm // MOD_TILE
    return pl.pallas_call(
        functools.partial(_adaln_kernel, nsub=nsub),
        grid=(m // tm,),
        in_specs=[pl.BlockSpec((tm, d), lambda i: (i, 0)),
                  pl.BlockSpec((1, d), lambda i: (0, 0)),
                  pl.BlockSpec((nsub, 6, d), lambda i: (i, 0, 0))],
        out_specs=pl.BlockSpec((tm, d), lambda i: (i, 0)),
        out_shape=jax.ShapeDtypeStruct((m, d), BF16),
        compiler_params=_params(("arbitrary",)),
        name="adaln",
    )(h, nw.reshape(1, d), modt)


def _inproj_kernel(a_ref, w_ref, wdt_ref, o_ref, odt_ref):
    a = a_ref[...]
    o_ref[...] = jnp.dot(a, w_ref[...], preferred_element_type=F32).astype(o_ref.dtype)
    odt_ref[...] = jnp.dot(a, wdt_ref[...], preferred_element_type=F32)


def _inproj(a, w, w_dt, tm=512, tn=2560):
    m, k = a.shape
    n = w.shape[1]
    main, dt = pl.pallas_call(
        _inproj_kernel,
        grid=(n // tn, m // tm),
        in_specs=[pl.BlockSpec((tm, k), lambda j, i: (i, 0)),
                  pl.BlockSpec((k, tn), lambda j, i: (0, j)),
                  pl.BlockSpec((k, DT_PAD), lambda j, i: (0, 0))],
        out_specs=[pl.BlockSpec((tm, tn), lambda j, i: (i, j)),
                   pl.BlockSpec((None, tm, DT_PAD), lambda j, i: (j, i, 0))],
        out_shape=[jax.ShapeDtypeStruct((m, n), BF16), jax.ShapeDtypeStruct((n // tn, m, DT_PAD), F32)],
        compiler_params=_params(("arbitrary", "arbitrary")),
        name="inproj",
    )(a, w, w_dt)
    return main, dt[0]


def _merge_kernel(gs_ref, gn_ref, z_ref, yf_ref, yb_ref, yn_ref, h_ref, wbs_ref, wbn_ref, wo_ref, snw_ref,
                  nw_ref, modt_ref, ho_ref, f_ref, *, nsub, row_tile_out):
    gw = D_SSD // SSD_GROUPS
    nw = nw_ref[...]
    for s in range(nsub):
        r0, r1 = s * MOD_TILE, (s + 1) * MOD_TILE
        z = z_ref[r0:r1, :].astype(F32)
        u = (yf_ref[r0:r1, :].astype(F32) + yb_ref[r0:r1, :].astype(F32)) * (z * jax.nn.sigmoid(z))
        ys = []
        for g in range(SSD_GROUPS):
            ug = u[:, g * gw:(g + 1) * gw]
            ms = jnp.mean(ug * ug, axis=-1, keepdims=True)
            ys.append((ug * lax.rsqrt(ms + EPS) * snw_ref[:, g * gw:(g + 1) * gw]).astype(BF16))
        a = jnp.dot(jnp.concatenate(ys, axis=1), wbs_ref[...], preferred_element_type=F32)
        b = jnp.dot(yn_ref[r0:r1, :], wbn_ref[...], preferred_element_type=F32)
        mixed = (jax.nn.sigmoid(gs_ref[r0:r1, :].astype(F32)) * a
                 + jax.nn.sigmoid(gn_ref[r0:r1, :].astype(F32)) * b)
        m = jnp.dot(mixed.astype(BF16), wo_ref[...], preferred_element_type=F32)
        h = h_ref[r0:r1, :] + modt_ref[s, 2:3, :] * m
        ho_ref[r0:r1, :] = h
        f = _rms_mod(h, nw, modt_ref[s, 3:4, :], modt_ref[s, 4:5, :])
        if row_tile_out:
            _store_tokens(f_ref, r0, f)
        else:
            f_ref[r0:r1, :] = f.astype(f_ref.dtype)


def _merge(p_main, yf, yb, y_na, h, wbs, wbn, wo, ssd_nw, nw2, modt, row_tile_out, tm=512):
    m, d = h.shape
    nsub = tm // MOD_TILE
    row = lambda i: (i, 0)
    const = lambda i: (0, 0)
    if row_tile_out:
        f_spec, f_shape = pl.BlockSpec((tm * ROW_SEG, LANE), row), jax.ShapeDtypeStruct((m * ROW_SEG, LANE), F32)
    else:
        f_spec, f_shape = pl.BlockSpec((tm, d), row), jax.ShapeDtypeStruct((m, d), BF16)
    return pl.pallas_call(
        functools.partial(_merge_kernel, nsub=nsub, row_tile_out=row_tile_out),
        grid=(m // tm,),
        in_specs=[pl.BlockSpec((tm, d), lambda i: (i, COL_GS // D_MODEL)),
                  pl.BlockSpec((tm, d), lambda i: (i, COL_GN // D_MODEL)),
                  pl.BlockSpec((tm, d), lambda i: (i, COL_Z // D_MODEL)),
                  pl.BlockSpec((tm, d), row), pl.BlockSpec((tm, d), row), pl.BlockSpec((tm, d), row),
                  pl.BlockSpec((tm, d), row),
                  pl.BlockSpec((d, d), const), pl.BlockSpec((d, d), const), pl.BlockSpec((d, d), const),
                  pl.BlockSpec((1, d), const), pl.BlockSpec((1, d), const),
                  pl.BlockSpec((nsub, 6, d), lambda i: (i, 0, 0))],
        out_specs=[pl.BlockSpec((tm, d), row), f_spec],
        out_shape=[jax.ShapeDtypeStruct((m, d), F32), f_shape],
        compiler_params=_params(("arbitrary",)),
        name="merge",
    )(p_main, p_main, p_main, yf, yb, y_na, h, wbs, wbn, wo, ssd_nw.astype(F32).reshape(1, d), nw2.reshape(1, d),
      modt)


def _ffn_kernel(x_ref, h_ref, w1_ref, w3_ref, w2_ref, nw_ref, modt_ref, modn_ref, ho_ref, an_ref, *, nsub):
    x = x_ref[...]
    h1 = jnp.dot(x, w1_ref[...], preferred_element_type=F32)
    h3 = jnp.dot(x, w3_ref[...], preferred_element_type=F32)
    act = (h1 * jax.nn.sigmoid(h1) * h3).astype(BF16)
    out = jnp.dot(act, w2_ref[...], preferred_element_type=F32)
    nw = nw_ref[...]
    for s in range(nsub):
        r0, r1 = s * MOD_TILE, (s + 1) * MOD_TILE
        h = h_ref[r0:r1, :] + modt_ref[s, 5:6, :] * out[r0:r1, :]
        ho_ref[r0:r1, :] = h
        an_ref[r0:r1, :] = _rms_mod(h, nw, modn_ref[s, 0:1, :], modn_ref[s, 1:2, :]).astype(an_ref.dtype)


def _ffn(x, h, w1, w3, w2, nw_next, modt, modt_next, tm=512):
    m, d = h.shape
    ff = w1.shape[1]
    nsub = tm // MOD_TILE
    row = lambda i: (i, 0)
    const = lambda i: (0, 0)
    once = pl.Buffered(1)
    return pl.pallas_call(
        functools.partial(_ffn_kernel, nsub=nsub),
        grid=(m // tm,),
        in_specs=[pl.BlockSpec((tm, d), row), pl.BlockSpec((tm, d), row),
                  pl.BlockSpec((d, ff), const, pipeline_mode=once),
                  pl.BlockSpec((d, ff), const, pipeline_mode=once),
                  pl.BlockSpec((ff, d), const, pipeline_mode=once),
                  pl.BlockSpec((1, d), const),
                  pl.BlockSpec((nsub, 6, d), lambda i: (i, 0, 0)),
                  pl.BlockSpec((nsub, 6, d), lambda i: (i, 0, 0))],
        out_specs=[pl.BlockSpec((tm, d), row), pl.BlockSpec((tm, d), row)],
        out_shape=[jax.ShapeDtypeStruct((m, d), F32), jax.ShapeDtypeStruct((m, d), BF16)],
        compiler_params=_params(("arbitrary",)),
        name="ffn",
    )(x, h, w1, w3, w2, nw_next.reshape(1, d), modt, modt_next)


NA_BAND = NA_ROWS_MAX * GRID_W
HEAD_PAIR = LANE // NA_HEAD_DIM
NA_UNROLL = 8
NA_PREP_UNROLL = 8


def _head_mean_matrix():
    ri = lax.broadcasted_iota(jnp.int32, (2 * LANE, LANE), 0) % LANE
    ci = lax.broadcasted_iota(jnp.int32, (2 * LANE, LANE), 1)
    half = NA_HEAD_DIM // 2
    same_head = (ri % NA_HEAD_DIM) // half == (ci % NA_HEAD_DIM) // half
    return jnp.where(same_head, 1.0 / NA_HEAD_DIM, 0.0).astype(BF16)


def _head_rms(x, w, mean_mat):
    sq = x * x
    hi = sq.astype(BF16)
    lo = (sq - hi.astype(F32)).astype(BF16)
    ms = jnp.dot(jnp.concatenate([hi, lo], axis=1), mean_mat, preferred_element_type=F32)
    return x * lax.rsqrt(ms + EPS) * w


def _qk_lane_order():
    quarter = NA_HEAD_DIM // 4
    order = []
    for half in range(2):
        for head in range(HEAD_PAIR):
            for part in range(2):
                start = head * NA_HEAD_DIM + part * 2 * quarter + half * quarter
                order += list(range(start, start + quarter))
    return order


def _qk_reorder(w):
    k, n = w.shape
    return w.reshape(k, n // LANE, LANE)[:, :, jnp.asarray(_qk_lane_order())].reshape(k, n)


def _rope(x, cos, sin):
    return x * cos + pltpu.roll(x, LANE // 2, axis=1) * sin


def _stack_heads(x, lane_lo):
    zero = jnp.zeros_like(x)
    return jnp.concatenate([jnp.where(lane_lo, x, zero), jnp.where(lane_lo, zero, x)], axis=0)


def _attend(qs, k_list, v_list, bias_list, lane_lo):
    nt = (((1,), (1,)), ((), ()))
    s_list = []
    for k, bias in zip(k_list, bias_list):
        s = lax.dot_general(qs, k, nt, preferred_element_type=F32)
        s_list.append(s if bias is None else s + bias)
    mx = s_list[0].max(axis=-1, keepdims=True)
    for s in s_list[1:]:
        mx = jnp.maximum(mx, s.max(axis=-1, keepdims=True))
    den = None
    acc = None
    for s, v in zip(s_list, v_list):
        p = jnp.exp(s - mx)
        d = p.sum(axis=-1, keepdims=True)
        o = jnp.dot(p.astype(BF16), v, preferred_element_type=F32)
        den = d if den is None else den + d
        acc = o if acc is None else acc + o
    acc = acc / den
    n = acc.shape[0] // 2
    return jnp.where(lane_lo, acc[:n], acc[n:])


def _na_kernel(q_ref, k_ref, v_ref, bias_ref, rowc_ref, rows_ref, colc_ref, cols_ref, qw_ref, kw_ref,
               o_ref, qn_ref, kn_ref, s_ref, m_ref, *, clen, n_rows):
    lane = lax.broadcasted_iota(jnp.int32, (1, LANE), 1)
    lane_lo = lane < NA_HEAD_DIM
    qk_h0 = (lane % NA_HEAD_DIM) < NA_HEAD_DIM // 2
    qw = qw_ref[...] * (NA_HEAD_DIM ** -0.5)
    kw = kw_ref[...]
    kr = NA_ROWS_MAX

    mean_mat = _head_mean_matrix()
    qn_ref[0:clen, :] = _head_rms(q_ref[0:clen, :].astype(F32), qw, mean_mat).astype(BF16)
    kn_ref[0:clen, :] = _head_rms(k_ref[0:clen, :].astype(F32), kw, mean_mat).astype(BF16)
    colc = colc_ref[...]
    cols = cols_ref[...]

    def prep(r, carry):
        rows = pl.ds(pl.multiple_of(clen + r * GRID_W, GRID_W), GRID_W)
        cos = rowc_ref[pl.ds(r, 1), :] + colc
        sin = rows_ref[pl.ds(r, 1), :] + cols
        q = _head_rms(q_ref[rows, :].astype(F32), qw, mean_mat)
        k = _head_rms(k_ref[rows, :].astype(F32), kw, mean_mat)
        qn_ref[rows, :] = _rope(q, cos, sin).astype(BF16)
        kn_ref[rows, :] = _rope(k, cos, sin).astype(BF16)
        return carry

    lax.fori_loop(0, n_rows, prep, 0, unroll=NA_PREP_UNROLL)

    kc = kn_ref[0:clen, :]
    vc = v_ref[0:clen, :]
    for blk in range(clen // GRID_W):
        rows = slice(blk * GRID_W, (blk + 1) * GRID_W)
        qs = _stack_heads(qn_ref[rows, :], qk_h0)
        o_ref[rows, :] = _attend(qs, [kc], [vc], [None], lane_lo).astype(o_ref.dtype)

    nt = (((1,), (1,)), ((), ()))
    n_blk = (NA_BAND + clen) // LANE

    def band_of(r):
        r0 = jnp.clip(r - kr // 2, 0, n_rows - kr)
        return r0, pl.ds(pl.multiple_of(clen + r0 * GRID_W, GRID_W), NA_BAND)

    n_win = NA_BAND // LANE
    sq = HEAD_PAIR * GRID_W

    def scores(rows_r, slots):
        qs = [_stack_heads(qn_ref[pl.ds(pl.multiple_of(clen + r * GRID_W, GRID_W), GRID_W), :], qk_h0)
              for r in rows_r]
        s_ctx = lax.dot_general(jnp.concatenate(qs, axis=0), kn_ref[0:clen, :], nt, preferred_element_type=F32)
        for u, (r, slot) in enumerate(zip(rows_r, slots)):
            r0, band = band_of(r)
            s_win = lax.dot_general(qs[u], kn_ref[band, :], nt, preferred_element_type=F32)
            bias = bias_ref[r0 - r + (NA_ROWS_MAX - 1)]
            blocks = [s_win[:, j * LANE:(j + 1) * LANE] + bias[:, j * LANE:(j + 1) * LANE] for j in range(n_win)]
            blocks += [s_ctx[u * sq:(u + 1) * sq, j * LANE:(j + 1) * LANE] for j in range(n_blk - n_win)]
            mx = blocks[0]
            for j in range(n_blk):
                s_ref[slot, :, j * LANE:(j + 1) * LANE] = blocks[j]
                mx = jnp.maximum(mx, blocks[j])
            m_ref[slot] = jnp.broadcast_to(mx.max(axis=-1, keepdims=True), mx.shape)

    def output(rows_r, slots):
        accs, dens, p_ctx = [], [], []
        for r, slot in zip(rows_r, slots):
            _, band = band_of(r)
            mx = m_ref[slot]
            p = [jnp.exp(s_ref[slot, :, j * LANE:(j + 1) * LANE] - mx) for j in range(n_blk)]
            den = p[0]
            for pj in p[1:]:
                den = den + pj
            dens.append(den.sum(axis=-1, keepdims=True))
            pb = [pj.astype(BF16) for pj in p]
            accs.append(jnp.dot(jnp.concatenate(pb[:n_win], axis=1), v_ref[band, :], preferred_element_type=F32))
            p_ctx.append(jnp.concatenate(pb[n_win:], axis=1))
        acc_ctx = jnp.dot(jnp.concatenate(p_ctx, axis=0), v_ref[0:clen, :], preferred_element_type=F32)
        for u, r in enumerate(rows_r):
            acc = (accs[u] + acc_ctx[u * sq:(u + 1) * sq]) / dens[u]
            rows = pl.ds(pl.multiple_of(clen + r * GRID_W, GRID_W), GRID_W)
            o_ref[rows, :] = jnp.where(lane_lo, acc[:GRID_W], acc[GRID_W:]).astype(o_ref.dtype)

    scores(list(range(NA_UNROLL)), list(range(NA_UNROLL)))

    def body(j, carry):
        r = 2 * NA_UNROLL * j
        for half in range(2):
            output([r + half * NA_UNROLL + u for u in range(NA_UNROLL)],
                   [half * NA_UNROLL + u for u in range(NA_UNROLL)])
            scores([jnp.minimum(r + (half + 1) * NA_UNROLL + u, n_rows - 1) for u in range(NA_UNROLL)],
                   [(1 - half) * NA_UNROLL + u for u in range(NA_UNROLL)])
        return carry

    lax.fori_loop(0, n_rows // (2 * NA_UNROLL), body, 0)


def _na_tables(rpb, n_rows):
    col = jnp.arange(GRID_W)
    col_start = jnp.clip(col - NA_COLS // 2, 0, GRID_W - NA_COLS)
    col_ok = (col[None, :] >= col_start[:, None]) & (col[None, :] < col_start[:, None] + NA_COLS)
    col_idx = jnp.clip(col[None, :] - col[:, None], -(NA_COLS - 1), NA_COLS - 1) + NA_COLS - 1
    rpb_c = jnp.where(col_ok[None, None], rpb.astype(F32)[:, :, col_idx], NEG_INF)
    dr = jnp.arange(NA_ROWS_MAX)[:, None] + jnp.arange(NA_ROWS_MAX)[None, :]
    tab = rpb_c[:, dr]
    tab = jnp.transpose(tab, (0, 1, 3, 2, 4)).reshape(NA_HEADS, NA_ROWS_MAX, GRID_W, NA_BAND)
    tab = tab.reshape(NA_HEADS // HEAD_PAIR, HEAD_PAIR, NA_ROWS_MAX, GRID_W, NA_BAND)
    tab = jnp.transpose(tab, (0, 2, 1, 3, 4)).reshape(NA_HEADS // HEAD_PAIR, NA_ROWS_MAX,
                                                       HEAD_PAIR * GRID_W, NA_BAND)
    quarter = NA_HEAD_DIM // 4
    inv = ROPE_BASE ** (-jnp.arange(quarter, dtype=F32) / quarter)

    def tables(pos, lo):
        ang = pos[:, None] * inv[None, :]
        z = jnp.zeros_like(ang)
        c, s = jnp.cos(ang), jnp.sin(ang)
        ch = jnp.concatenate([c, c, z, z] if lo else [z, z, c, c], axis=-1)
        sh = jnp.concatenate([-s, s, z, z] if lo else [z, z, -s, s], axis=-1)
        order = jnp.asarray(_qk_lane_order())
        return jnp.tile(ch, (1, HEAD_PAIR))[:, order], jnp.tile(sh, (1, HEAD_PAIR))[:, order]

    rowc, rows = tables(jnp.arange(n_rows).astype(F32), True)
    colc, cols = tables(jnp.arange(GRID_W).astype(F32), False)
    return tab, rowc, rows, colc, cols


def _na(pm3, q_norm_w, k_norm_w, rpb, clen):
    bsz, t, _ = pm3.shape
    n_rows = (t - clen) // GRID_W
    tab, rowc, rows, colc, cols = _na_tables(rpb, n_rows)
    order = jnp.asarray(_qk_lane_order())
    qw = jnp.tile(q_norm_w.astype(F32), HEAD_PAIR)[order].reshape(1, LANE)
    kw = jnp.tile(k_norm_w.astype(F32), HEAD_PAIR)[order].reshape(1, LANE)
    n_pairs = NA_HEADS // HEAD_PAIR
    const = lambda p, b: (0, 0)
    tok = lambda c0: pl.BlockSpec((None, t, LANE), lambda p, b: (b, 0, c0 // LANE + p))
    return pl.pallas_call(
        functools.partial(_na_kernel, clen=clen, n_rows=n_rows),
        grid=(n_pairs, bsz),
        in_specs=[tok(COL_Q), tok(COL_K), tok(COL_V),
                  pl.BlockSpec((None, NA_ROWS_MAX, HEAD_PAIR * GRID_W, NA_BAND), lambda p, b: (p, 0, 0, 0)),
                  pl.BlockSpec((n_rows, LANE), const), pl.BlockSpec((n_rows, LANE), const),
                  pl.BlockSpec((GRID_W, LANE), const), pl.BlockSpec((GRID_W, LANE), const),
                  pl.BlockSpec((1, LANE), const), pl.BlockSpec((1, LANE), const)],
        out_specs=pl.BlockSpec((None, t, LANE), lambda p, b: (b, 0, p)),
        out_shape=jax.ShapeDtypeStruct((bsz, t, D_NA), BF16),
        scratch_shapes=[pltpu.VMEM((t, LANE), BF16), pltpu.VMEM((t, LANE), BF16),
                        pltpu.VMEM((2 * NA_UNROLL, HEAD_PAIR * GRID_W, NA_BAND + clen), F32),
                        pltpu.VMEM((2 * NA_UNROLL, HEAD_PAIR * GRID_W, LANE), F32)],
        compiler_params=_params(("arbitrary", "arbitrary")),
        name="natten",
    )(pm3, pm3, pm3, tab, rowc, rows, colc, cols, qw, kw)


HALO = 16
CONV_ROWS = SSD_CHUNK
CONV_PAD = SSD_CONV // 2
BC_DIM = SSD_GROUPS * SSD_STATE
PAIRS_PER_GROUP = SSD_HPG // HEAD_PAIR


def _ssd_chunk_index(s, rev, cc, nc):
    if not rev:
        return s
    return jnp.where(s < cc, cc - 1 - s, nc - 1 - (s - cc))


def _ssd_conv_kernel(main_ref, prev_ref, next_ref, cw_ref, cb_ref, o_ref, *, cc, nc):
    q = main_ref.shape[0]
    cidx = pl.program_id(1)
    first_col = jnp.where((cidx == 0) | (cidx == cc), HALO, 0)
    end_col = jnp.where((cidx == cc - 1) | (cidx == nc - 1), HALO + q, q + 2 * HALO)
    ext = jnp.concatenate([prev_ref[...], main_ref[...], next_ref[...]], axis=0)
    ti = lax.broadcasted_iota(jnp.int32, (q, q + 2 * HALO), 0)
    ji = lax.broadcasted_iota(jnp.int32, (q, q + 2 * HALO), 1)
    in_segment = (ji >= first_col) & (ji < end_col)
    conv = cb_ref[...] + cw_ref[CONV_PAD:CONV_PAD + 1, :] * main_ref[...].astype(F32)
    for k in range(SSD_CONV):
        if k == CONV_PAD:
            continue
        shift = jnp.where((ji == ti + (HALO + k - CONV_PAD)) & in_segment, 1.0, 0.0).astype(BF16)
        conv = conv + cw_ref[k:k + 1, :] * jnp.dot(shift, ext, preferred_element_type=F32)
    o_ref[...] = (conv * jax.nn.sigmoid(conv)).astype(o_ref.dtype)


def _ssd_core(xbc_ref, dt_ref, dtb_ref, a_ref, exp_ref, h_ref, yoff_ref, y_ref, *, rev):
    q = SSD_CHUNK
    xbc = xbc_ref[...]

    hb = SSD_HEADS if rev else 0
    x = dt_ref[...] + dtb_ref[...]
    dt = jnp.maximum(x, 0.0) + jnp.log(1.0 + jnp.exp(-jnp.abs(x)))
    la = dt * a_ref[...]
    ri = lax.broadcasted_iota(jnp.int32, (q, q), 0)
    ci = lax.broadcasted_iota(jnp.int32, (q, q), 1)
    allowed = (ci >= ri) if rev else (ci <= ri)
    tri = jnp.where(allowed, 1.0, 0.0).astype(F32)
    a_cs = jnp.dot(tri, la, precision=HIGHEST, preferred_element_type=F32)
    last = 0 if rev else q - 1
    a_cs_t = a_cs.T
    e_all = jnp.exp(a_cs)
    src_t = a_cs_t - jnp.log(dt.T)
    wdt_t = jnp.exp(a_cs_t[:, last:last + 1] - src_t)
    neg_mask = jnp.where(allowed, 0.0, NEG_INF)
    blk = (last // 8) * 8
    e_last = jnp.exp(jnp.dot(a_cs[blk:blk + 8, :], exp_ref[...], precision=HIGHEST,
                             preferred_element_type=F32))[last - blk:last - blk + 1, hb * SSD_HEAD_DIM:
                                                          (hb + SSD_HEADS) * SSD_HEAD_DIM]

    lane_lo = lax.broadcasted_iota(jnp.int32, (1, LANE), 1) < SSD_HEAD_DIM
    nt = (((1,), (1,)), ((), ()))
    for g in range(SSD_GROUPS):
        b_g = xbc[:, D_SSD + g * SSD_STATE:D_SSD + (g + 1) * SSD_STATE]
        c_g = xbc[:, D_SSD + BC_DIM + g * SSD_STATE:D_SSD + BC_DIM + (g + 1) * SSD_STATE]
        cols = slice(g * SSD_HPG * SSD_HEAD_DIM, (g + 1) * SSD_HPG * SSD_HEAD_DIM)
        yoff_ref[:, cols] = jnp.dot(c_g, h_ref[:, cols].astype(BF16), preferred_element_type=F32)
        cb = lax.dot_general(c_g, b_g, nt, preferred_element_type=F32)
        b_t = b_g.astype(F32).T
        for pj in range(PAIRS_PER_GROUP):
            pair = g * PAIRS_PER_GROUP + pj
            lanes = slice(pair * LANE, (pair + 1) * LANE)
            lhs, ecol = [], []
            for hh in range(HEAD_PAIR):
                head = hb + pair * HEAD_PAIR + hh
                seg = (a_cs[:, head:head + 1] - src_t[head:head + 1, :]) + neg_mask
                lhs.append((cb * jnp.exp(seg)).astype(BF16))
                ecol.append(jnp.broadcast_to(e_all[:, head:head + 1], (q, LANE)))
            for hh in range(HEAD_PAIR):
                head = hb + pair * HEAD_PAIR + hh
                lhs.append((b_t * wdt_t[head:head + 1, :]).astype(BF16))
            res = jnp.dot(jnp.concatenate(lhs, axis=0), xbc[:, lanes], preferred_element_type=F32)
            y_ref[:, lanes] = (jnp.where(lane_lo, res[0:q], res[q:2 * q])
                               + yoff_ref[:, lanes] * jnp.where(lane_lo, ecol[0], ecol[1]))
            h_ref[:, lanes] = (h_ref[:, lanes] * e_last[:, lanes]
                               + jnp.where(lane_lo, res[2 * q:3 * q], res[3 * q:4 * q]))


def _ssd_scan_kernel(xf_ref, dtf_ref, xb_ref, dtb_ref, bias_ref, a_ref, exp_ref, dsum_ref, of_ref, ob_ref,
                     hf_ref, yofff_ref, hb_ref, yoffb_ref, yf_ref, yb_ref):
    @pl.when(pl.program_id(1) == 0)
    def _():
        hf_ref[...] = jnp.zeros_like(hf_ref)
        hb_ref[...] = jnp.zeros_like(hb_ref)

    _ssd_core(xf_ref, dtf_ref, bias_ref, a_ref, exp_ref, hf_ref, yofff_ref, yf_ref, rev=False)
    _ssd_core(xb_ref, dtb_ref, bias_ref, a_ref, exp_ref, hb_ref, yoffb_ref, yb_ref, rev=True)
    of_ref[...] = (yf_ref[...] + xf_ref[:, 0:D_SSD].astype(F32) * dsum_ref[...]).astype(of_ref.dtype)
    ob_ref[...] = yb_ref[...].astype(ob_ref.dtype)


def _ssd(pm3, pdt3, conv_w, conv_b, dt_bias, a_log, d_skip, clen):
    bsz, t, _ = pm3.shape
    q = SSD_CHUNK
    nc, cc = t // q, clen // q
    nh = 2 * SSD_HEADS
    cw = jnp.pad(conv_w.astype(F32), ((0, 8 - SSD_CONV), (0, 0)))
    cb = conv_b.astype(F32).reshape(1, XBC_DIM)
    dtb = jnp.pad(dt_bias.astype(F32).reshape(1, nh), ((0, 0), (0, DT_PAD - nh)))
    a_neg = jnp.pad(-jnp.exp(a_log.astype(F32)).reshape(1, nh), ((0, 0), (0, DT_PAD - nh)))
    dsum = jnp.repeat((d_skip[0] + d_skip[1]).astype(F32), SSD_HEAD_DIM).reshape(1, D_SSD)
    expand = jnp.repeat(jnp.eye(DT_PAD, dtype=F32)[:, :nh], SSD_HEAD_DIM, axis=1)
    n_halo = t // HALO

    const = lambda b, s: (0, 0)
    xcol = COL_XBC // XBC_DIM
    cq = CONV_ROWS
    hpb = cq // HALO
    xbc = pl.pallas_call(
        functools.partial(_ssd_conv_kernel, cc=clen // cq, nc=t // cq),
        grid=(bsz, t // cq),
        in_specs=[pl.BlockSpec((None, cq, XBC_DIM), lambda b, s: (b, s, xcol)),
                  pl.BlockSpec((None, HALO, XBC_DIM), lambda b, s: (b, jnp.maximum(s * hpb - 1, 0), xcol)),
                  pl.BlockSpec((None, HALO, XBC_DIM),
                               lambda b, s: (b, jnp.minimum((s + 1) * hpb, n_halo - 1), xcol)),
                  pl.BlockSpec((8, XBC_DIM), const), pl.BlockSpec((1, XBC_DIM), const)],
        out_specs=pl.BlockSpec((None, cq, XBC_DIM), lambda b, s: (b, s, 0)),
        out_shape=jax.ShapeDtypeStruct((bsz, t, XBC_DIM), BF16),
        compiler_params=_params(("arbitrary", "arbitrary")),
        name="ssd_conv",
    )(pm3, pm3, pm3, cw, cb)

    tok_f = lambda b, s: (b, s, 0)
    tok_b = lambda b, s: (b, _ssd_chunk_index(s, True, cc, nc), 0)
    state = [pltpu.VMEM((SSD_STATE, D_SSD), F32), pltpu.VMEM((q, D_SSD), F32)]
    return pl.pallas_call(
        _ssd_scan_kernel,
        grid=(bsz, nc),
        in_specs=[pl.BlockSpec((None, q, XBC_DIM), tok_f), pl.BlockSpec((None, q, DT_PAD), tok_f),
                  pl.BlockSpec((None, q, XBC_DIM), tok_b), pl.BlockSpec((None, q, DT_PAD), tok_b),
                  pl.BlockSpec((1, DT_PAD), const), pl.BlockSpec((1, DT_PAD), const),
                  pl.BlockSpec((DT_PAD, nh * SSD_HEAD_DIM), const), pl.BlockSpec((1, D_SSD), const)],
        out_specs=[pl.BlockSpec((None, q, D_SSD), tok_f), pl.BlockSpec((None, q, D_SSD), tok_b)],
        out_shape=[jax.ShapeDtypeStruct((bsz, t, D_SSD), BF16), jax.ShapeDtypeStruct((bsz, t, D_SSD), BF16)],
        scratch_shapes=state + state + [pltpu.VMEM((q, D_SSD), F32), pltpu.VMEM((q, D_SSD), F32)],
        compiler_params=_params(("arbitrary", "arbitrary")),
        name="ssd_scan",
    )(xbc, pdt3, xbc, pdt3, dtb, a_neg, expand, dsum)


MOE_TB = 512
ROUTE_TM = 512
INFO_W = 8
BIG_NEG = -3.0e38
ROW_SEG = D_MODEL // LANE


def _load_tokens(ref, t0, n):
    return jnp.concatenate([ref[pl.ds(t0 * ROW_SEG + c, n, stride=ROW_SEG), :] for c in range(ROW_SEG)], axis=1)


def _store_tokens(ref, t0, val):
    n = val.shape[0]
    for c in range(ROW_SEG):
        ref[pl.ds(t0 * ROW_SEG + c, n, stride=ROW_SEG), :] = val[:, c * LANE:(c + 1) * LANE]


def _route_kernel(f_ref, wr_ref, info_ref, cnt_ref, run_ref):
    tm = f_ref.shape[0] // ROW_SEG

    @pl.when(pl.program_id(0) == 0)
    def _():
        run_ref[...] = jnp.zeros_like(run_ref)

    logits = jnp.dot(_load_tokens(f_ref, 0, tm).astype(BF16), wr_ref[...], preferred_element_type=F32)
    lane = lax.broadcasted_iota(jnp.int32, (tm, LANE), 1).astype(F32)
    lg = jnp.where(lane < N_EXPERTS, logits, BIG_NEG)
    m1 = lg.max(axis=-1, keepdims=True)
    i1 = jnp.where(lg == m1, lane, float(LANE)).min(axis=-1, keepdims=True)
    lg2 = jnp.where(lane == i1, BIG_NEG, lg)
    m2 = lg2.max(axis=-1, keepdims=True)
    i2 = jnp.where(lg2 == m2, lane, float(LANE)).min(axis=-1, keepdims=True)
    ex = jnp.exp(m2 - m1)
    g0 = 1.0 / (1.0 + ex)
    g1 = ex / (1.0 + ex)
    sel0 = lane == i1
    sel1 = lane == i2
    onehot = jnp.where(sel0 | sel1, 1.0, 0.0)
    ri = lax.broadcasted_iota(jnp.int32, (tm, tm), 0)
    ci = lax.broadcasted_iota(jnp.int32, (tm, tm), 1)
    strict_lower = jnp.where(ci < ri, 1.0, 0.0).astype(BF16)
    before = jnp.dot(strict_lower, onehot.astype(BF16), preferred_element_type=F32) + run_ref[...]
    pos0 = jnp.where(sel0, before, 0.0).sum(axis=-1, keepdims=True)
    pos1 = jnp.where(sel1, before, 0.0).sum(axis=-1, keepdims=True)
    run_ref[...] = run_ref[...] + onehot.sum(axis=0, keepdims=True)
    cnt_ref[...] = run_ref[...]
    info = jnp.where(lane == 0, i1, jnp.where(lane == 1, i2, jnp.where(lane == 2, g0, jnp.where(
        lane == 3, g1, jnp.where(lane == 4, pos0, jnp.where(lane == 5, pos1, 0.0))))))
    info_ref[...] = info[:, :INFO_W]


def _route(f8, w_router):
    m, d = f8.shape[0] // ROW_SEG, D_MODEL
    wr = jnp.pad(w_router.astype(BF16), ((0, 0), (0, LANE - N_EXPERTS)))
    return pl.pallas_call(
        _route_kernel,
        grid=(m // ROUTE_TM,),
        in_specs=[pl.BlockSpec((ROUTE_TM * ROW_SEG, LANE), lambda i: (i, 0)),
                  pl.BlockSpec((d, LANE), lambda i: (0, 0))],
        out_specs=[pl.BlockSpec((ROUTE_TM, INFO_W), lambda i: (i, 0)), pl.BlockSpec((1, LANE), lambda i: (0, 0))],
        out_shape=[jax.ShapeDtypeStruct((m, INFO_W), F32), jax.ShapeDtypeStruct((1, LANE), F32)],
        scratch_shapes=[pltpu.VMEM((1, LANE), F32)],
        compiler_params=_params(("arbitrary",)),
        name="moe_route",
    )(f8, wr)


def _row_copy(src_ref, src_row, dst_ref, dst_row, sem):
    src = src_ref.at[pl.ds(pl.multiple_of(src_row * ROW_SEG, ROW_SEG), ROW_SEG)]
    dst = dst_ref.at[pl.ds(pl.multiple_of(dst_row * ROW_SEG, ROW_SEG), ROW_SEG)]
    return pltpu.make_async_copy(src, dst, sem)


def _wait_rows(src_ref, dst_ref, sem, n):
    def wait(t, c):
        _row_copy(src_ref, 0, dst_ref, 0, sem).wait()
        return c

    lax.fori_loop(0, n, wait, 0, unroll=8)


def _dispatch_kernel(pad_ref, dest_hbm, f_ref, rows_hbm, dest_smem, stage_ref, zero_ref, idx_sem, row_sem,
                     pad_sem):
    tm = f_ref.shape[0] // ROW_SEG
    i, n = pl.program_id(0), pl.num_programs(0)
    slot = i % 2
    stage = stage_ref.at[slot]
    sem = row_sem.at[slot]

    @pl.when(i == 0)
    def _():
        zero_ref[...] = jnp.zeros_like(zero_ref)
        for e in range(N_EXPERTS):
            first, count = pad_ref[e], pad_ref[N_EXPERTS + e]

            def start_zero(j, c):
                _row_copy(zero_ref, 0, rows_hbm, first + j, pad_sem).start()
                return c

            lax.fori_loop(0, count, start_zero, 0)
        for e in range(N_EXPERTS):
            def wait_zero(j, c):
                _row_copy(zero_ref, 0, rows_hbm, 0, pad_sem).wait()
                return c

            lax.fori_loop(0, pad_ref[N_EXPERTS + e], wait_zero, 0)

    @pl.when(i >= 2)
    def _():
        _wait_rows(stage, rows_hbm, sem, TOP_K * tm)

    stage[...] = f_ref[...]
    base = pl.multiple_of(i * (TOP_K * tm), TOP_K * tm)
    idx_copy = pltpu.make_async_copy(dest_hbm.at[pl.ds(base, TOP_K * tm)], dest_smem, idx_sem)
    idx_copy.start()
    idx_copy.wait()

    def start(t, c):
        for k in range(TOP_K):
            _row_copy(stage, t, rows_hbm, dest_smem[TOP_K * t + k], sem).start()
        return c

    lax.fori_loop(0, tm, start, 0, unroll=8)

    @pl.when(i == n - 1)
    def _():
        _wait_rows(stage, rows_hbm, sem, TOP_K * tm)

        @pl.when(n >= 2)
        def _():
            _wait_rows(stage_ref.at[1 - slot], rows_hbm, row_sem.at[1 - slot], TOP_K * tm)


def _dispatch(f8, dest_flat, pad_info, n_rows):
    m = f8.shape[0] // ROW_SEG
    grid_spec = pltpu.PrefetchScalarGridSpec(
        num_scalar_prefetch=1,
        grid=(m // ROUTE_TM,),
        in_specs=[pl.BlockSpec(memory_space=pl.ANY),
                  pl.BlockSpec((ROUTE_TM * ROW_SEG, LANE), lambda i, pad: (i, 0))],
        out_specs=pl.BlockSpec(memory_space=pl.ANY),
        scratch_shapes=[pltpu.SMEM((TOP_K * ROUTE_TM,), jnp.int32),
                        pltpu.VMEM((2, ROUTE_TM * ROW_SEG, LANE), f8.dtype),
                        pltpu.VMEM((ROW_SEG, LANE), f8.dtype),
                        pltpu.SemaphoreType.DMA(()), pltpu.SemaphoreType.DMA((2,)), pltpu.SemaphoreType.DMA(())],
    )
    return pl.pallas_call(
        _dispatch_kernel,
        grid_spec=grid_spec,
        out_shape=jax.ShapeDtypeStruct((n_rows * ROW_SEG, LANE), f8.dtype),
        compiler_params=_params(("arbitrary",)),
        name="moe_dispatch",
    )(pad_info, dest_flat, f8)


def _gffn_kernel(be_ref, nu_ref, x_ref, w1_ref, w3_ref, w2_ref, y_ref):
    i = pl.program_id(0)

    @pl.when(i < nu_ref[0])
    def _():
        x = _load_tokens(x_ref, 0, MOE_TB).astype(BF16)
        h1 = jnp.dot(x, w1_ref[...], preferred_element_type=F32)
        h3 = jnp.dot(x, w3_ref[...], preferred_element_type=F32)
        act = (h1 * jax.nn.sigmoid(h1) * h3).astype(BF16)
        _store_tokens(y_ref, 0, jnp.dot(act, w2_ref[...], preferred_element_type=F32))

    @pl.when(i >= nu_ref[0])
    def _():
        y_ref[...] = jnp.zeros_like(y_ref)


def _gffn(rows, block_e, n_used, w1, w3, w2):
    r, d = rows.shape[0] // ROW_SEG, D_MODEL
    ff = w1.shape[2]
    nb = r // MOE_TB
    once = pl.Buffered(1)
    grid_spec = pltpu.PrefetchScalarGridSpec(
        num_scalar_prefetch=2,
        grid=(nb,),
        in_specs=[pl.BlockSpec((MOE_TB * ROW_SEG, LANE), lambda i, be, nu: (i, 0)),
                  pl.BlockSpec((None, d, ff), lambda i, be, nu: (be[i], 0, 0), pipeline_mode=once),
                  pl.BlockSpec((None, d, ff), lambda i, be, nu: (be[i], 0, 0), pipeline_mode=once),
                  pl.BlockSpec((None, ff, d), lambda i, be, nu: (be[i], 0, 0), pipeline_mode=once)],
        out_specs=pl.BlockSpec((MOE_TB * ROW_SEG, LANE), lambda i, be, nu: (i, 0)),
    )
    return pl.pallas_call(
        _gffn_kernel,
        grid_spec=grid_spec,
        out_shape=jax.ShapeDtypeStruct((r * ROW_SEG, LANE), F32),
        compiler_params=_params(("arbitrary",)),
        name="moe_ffn",
    )(block_e, n_used, rows, w1, w3, w2)


def _combine_kernel(dest_hbm, y_hbm, h_ref, info_ref, modt_ref, o_ref, dest_smem, buf_ref, idx_sem, row_sem,
                    *scratch, nsub, tiles_per_batch):
    tm = h_ref.shape[0]
    i, n = pl.program_id(0), pl.num_programs(0)
    latent_only = tiles_per_batch is not None
    if latent_only:
        obuf_ref, out_sem = scratch

        def latent_copies(step, buf_slot, fn):
            for s in range(nsub):
                g = step * nsub + s
                j = lax.rem(g, tiles_per_batch)

                @pl.when(j > 0)
                def _():
                    row0 = (lax.div(g, tiles_per_batch) * (tiles_per_batch - 1) + (j - 1)) * MOD_TILE
                    fn(pltpu.make_async_copy(obuf_ref.at[buf_slot, pl.ds(s * MOD_TILE, MOD_TILE)],
                                             o_ref.at[pl.ds(pl.multiple_of(row0, MOD_TILE), MOD_TILE)],
                                             out_sem.at[buf_slot]))

    def gather(step, slot):
        base = pl.multiple_of(step * (TOP_K * tm), TOP_K * tm)
        idx_copy = pltpu.make_async_copy(dest_hbm.at[pl.ds(base, TOP_K * tm)], dest_smem, idx_sem)
        idx_copy.start()
        idx_copy.wait()

        def start(t, c):
            for k in range(TOP_K):
                _row_copy(y_hbm, dest_smem[TOP_K * t + k], buf_ref.at[slot, k], t, row_sem.at[slot]).start()
            return c

        lax.fori_loop(0, tm, start, 0, unroll=8)

    @pl.when(i == 0)
    def _():
        gather(0, 0)

    @pl.when(i + 1 < n)
    def _():
        gather(i + 1, (i + 1) % 2)

    slot = i % 2
    _wait_rows(y_hbm, buf_ref.at[slot, 0], row_sem.at[slot], TOP_K * tm)
    if latent_only:
        @pl.when(i >= 2)
        def _():
            latent_copies(i - 2, slot, lambda cp: cp.wait())
    for s in range(nsub):
        r0, r1 = s * MOD_TILE, (s + 1) * MOD_TILE
        info = info_ref[r0:r1, :]
        mix = (info[:, 2:3] * _load_tokens(buf_ref.at[slot, 0], r0, MOD_TILE)
               + info[:, 3:4] * _load_tokens(buf_ref.at[slot, 1], r0, MOD_TILE))
        out = h_ref[r0:r1, :] + modt_ref[s, 5:6, :] * mix
        if latent_only:
            obuf_ref[slot, r0:r1, :] = out
        else:
            o_ref[r0:r1, :] = out
    if latent_only:
        latent_copies(i, slot, lambda cp: cp.start())

        @pl.when(i == n - 1)
        def _():
            latent_copies(i, slot, lambda cp: cp.wait())

            @pl.when(n >= 2)
            def _():
                latent_copies(i - 1, 1 - slot, lambda cp: cp.wait())


def _combine(y_rows, dest_flat, h, info, modt, tiles_per_batch):
    m, d = h.shape
    nsub = ROUTE_TM // MOD_TILE
    scratch = [pltpu.SMEM((TOP_K * ROUTE_TM,), jnp.int32), pltpu.VMEM((2, TOP_K, ROUTE_TM * ROW_SEG, LANE), F32),
               pltpu.SemaphoreType.DMA(()), pltpu.SemaphoreType.DMA((2,))]
    if tiles_per_batch is None:
        out_spec, out_rows = pl.BlockSpec((ROUTE_TM, d), lambda i: (i, 0)), m
    else:
        out_spec, out_rows = pl.BlockSpec(memory_space=pl.ANY), m - m // tiles_per_batch
        scratch += [pltpu.VMEM((2, ROUTE_TM, d), F32), pltpu.SemaphoreType.DMA((2,))]
    return pl.pallas_call(
        functools.partial(_combine_kernel, nsub=nsub, tiles_per_batch=tiles_per_batch),
        grid=(m // ROUTE_TM,),
        in_specs=[pl.BlockSpec(memory_space=pl.ANY), pl.BlockSpec(memory_space=pl.ANY),
                  pl.BlockSpec((ROUTE_TM, d), lambda i: (i, 0)),
                  pl.BlockSpec((ROUTE_TM, INFO_W), lambda i: (i, 0)),
                  pl.BlockSpec((nsub, 6, d), lambda i: (i, 0, 0))],
        out_specs=out_spec,
        out_shape=jax.ShapeDtypeStruct((out_rows, d), F32),
        scratch_shapes=scratch,
        compiler_params=_params(("arbitrary",)),
        name="moe_combine",
    )(dest_flat, y_rows, h, info, modt)


def _moe(f8, h, w_router, w1, w3, w2, modt, tiles_per_batch=None):
    m, d = h.shape
    info, cnt = _route(f8, w_router)
    e = info[:, 0:TOP_K].astype(jnp.int32)
    pos = info[:, 4:4 + TOP_K].astype(jnp.int32)
    counts = cnt[0, :N_EXPERTS].astype(jnp.int32)
    padded = (counts + MOE_TB - 1) // MOE_TB * MOE_TB
    pad_end = jnp.cumsum(padded)
    pad_start = pad_end - padded
    dest_flat = (pad_start[e] + pos).reshape(-1)
    n_rows = m * TOP_K + N_EXPERTS * MOE_TB
    nb = n_rows // MOE_TB
    block_e = jnp.minimum(jnp.searchsorted(pad_end, jnp.arange(nb, dtype=jnp.int32) * MOE_TB, side='right'),
                          N_EXPERTS - 1).astype(jnp.int32)
    n_used = (pad_end[-1:] // MOE_TB).astype(jnp.int32)
    pad_info = jnp.concatenate([pad_start + counts, padded - counts]).astype(jnp.int32)
    rows = _dispatch(f8, dest_flat, pad_info, n_rows)
    y_rows = _gffn(rows, block_e, n_used, w1.astype(BF16), w3.astype(BF16), w2.astype(BF16))
    return _combine(y_rows, dest_flat, h, info, modt, tiles_per_batch)


def _tile_mod(mod_l, bsz, tiles_per_batch):
    t = jnp.arange(bsz * tiles_per_batch)
    idx = jnp.where(t % tiles_per_batch == 0, bsz, t // tiles_per_batch)
    return jnp.transpose(mod_l[:, idx, :], (1, 0, 2))


def kernel(x, c, ctx, c_ctx, w_mod, b_mod, norm1_w, norm2_w, w_in, conv_w, conv_b, dt_bias, a_log, d_skip,
           ssd_norm_w, q_norm_w, k_norm_w, rpb, w_br_ssd, w_br_na, w_out, w_ff1, w_ff3, w_ff2, w_router,
           w_e1, w_e3, w_e2):
    bsz, seqlen, d = x.shape
    clen = ctx.shape[1]
    n_layers = w_mod.shape[0]
    assert clen == MOD_TILE and seqlen % MOD_TILE == 0 and bsz < 16
    t = clen + seqlen
    m = bsz * t
    tiles_per_batch = t // MOD_TILE

    cond = jnp.zeros((16, d), F32).at[:bsz].set(c).at[bsz].set(c_ctx)
    mod = _modulation(cond, w_mod, b_mod)
    modt = [_tile_mod(mod[i], bsz, tiles_per_batch) for i in range(n_layers)]

    h = jnp.concatenate([ctx, x], axis=1).reshape(m, d)
    a = _adaln(h, norm1_w[0], modt[0])
    for i in range(n_layers):
        last = i == n_layers - 1
        wi = w_in[i]
        w_main = jnp.concatenate([wi[:, :IN_SPLITS[3]], _qk_reorder(wi[:, IN_SPLITS[4]:IN_SPLITS[5]]),
                                  _qk_reorder(wi[:, IN_SPLITS[5]:IN_SPLITS[6]]), wi[:, IN_SPLITS[6]:]],
                                 axis=1).astype(BF16)
        w_dt = jnp.pad(wi[:, IN_SPLITS[3]:IN_SPLITS[4]], ((0, 0), (0, DT_PAD - 2 * SSD_HEADS))).astype(BF16)
        p_main, p_dt = _inproj(a, w_main, w_dt)
        pm3 = p_main.reshape(bsz, t, MAIN_DIM)
        yf, yb = _ssd(pm3, p_dt.reshape(bsz, t, DT_PAD), conv_w[i], conv_b[i], dt_bias[i], a_log[i],
                      d_skip[i], clen)
        y_na = _na(pm3, q_norm_w[i], k_norm_w[i], rpb[i], clen).reshape(m, d)
        fi = i // 2
        moe_layer = i % 2 == 1
        h, f = _merge(p_main, yf.reshape(m, d), yb.reshape(m, d), y_na, h, w_br_ssd[i].astype(BF16),
                      w_br_na[i].astype(BF16), w_out[i].astype(BF16), ssd_norm_w[i], norm2_w[i], modt[i],
                      row_tile_out=moe_layer)
        nxt = min(i + 1, n_layers - 1)
        if not moe_layer:
            h, a = _ffn(f, h, w_ff1[fi].astype(BF16), w_ff3[fi].astype(BF16), w_ff2[fi].astype(BF16),
                        norm1_w[nxt], modt[i], modt[nxt])
        else:
            if last:
                return _moe(f, h, w_router[fi], w_e1[fi], w_e3[fi], w_e2[fi], modt[i],
                            tiles_per_batch).reshape(bsz, seqlen, d)
            h = _moe(f, h, w_router[fi], w_e1[fi], w_e3[fi], w_e2[fi], modt[i])
            a = _adaln(h, norm1_w[nxt], modt[nxt])
    return h.reshape(bsz, t, d)[:, clen:]
```

```python
import functools
import math

import jax
import jax.numpy as jnp
from jax import lax
from jax.experimental import pallas as pl
from jax.experimental.pallas import tpu as pltpu

D_MODEL = 1024
GRID_W = 64
SSD_HEAD_DIM = 64
SSD_HEADS = D_MODEL // SSD_HEAD_DIM
D_SSD = SSD_HEADS * SSD_HEAD_DIM
SSD_GROUPS = 2
SSD_HPG = SSD_HEADS // SSD_GROUPS
SSD_STATE = 128
SSD_CONV = 5
SSD_CHUNK = 128
XBC_DIM = D_SSD + 2 * SSD_GROUPS * SSD_STATE
NA_HEAD_DIM = 64
NA_HEADS = D_MODEL // NA_HEAD_DIM
D_NA = NA_HEADS * NA_HEAD_DIM
NA_ROWS_MAX = 8
NA_COLS = 16
ROPE_BASE = 10000.0
N_EXPERTS = 8
TOP_K = 2
EPS = 1e-6
NEG_INF = -1e30
IN_SPLITS = (D_MODEL, 2 * D_MODEL, 2 * D_MODEL + D_SSD, 2 * D_MODEL + D_SSD + XBC_DIM,
             2 * D_MODEL + D_SSD + XBC_DIM + 2 * SSD_HEADS,
             2 * D_MODEL + D_SSD + XBC_DIM + 2 * SSD_HEADS + D_NA,
             2 * D_MODEL + D_SSD + XBC_DIM + 2 * SSD_HEADS + 2 * D_NA)

F32 = jnp.float32
BF16 = jnp.bfloat16
HIGHEST = lax.Precision.HIGHEST

MOD_TILE = 256
LANE = 128
VMEM_LIMIT = 48 * 1024 * 1024
COL_GS, COL_GN, COL_Z, COL_XBC, COL_Q, COL_K, COL_V = 0, 1024, 2048, 3072, 4608, 5632, 6656
MAIN_DIM = 7680
DT_PAD = LANE


def _params(sem):
    return pltpu.CompilerParams(dimension_semantics=sem, vmem_limit_bytes=VMEM_LIMIT)


def _rms_mod(x, nw, shift, scale):
    ms = jnp.mean(x * x, axis=-1, keepdims=True)
    return (x * lax.rsqrt(ms + EPS) * nw) * (1.0 + scale) + shift


def _mod_kernel(cond_ref, w_ref, b_ref, o_ref):
    c = cond_ref[...]
    s = c * jax.nn.sigmoid(c)
    o_ref[...] = jnp.dot(s, w_ref[...], precision=HIGHEST, preferred_element_type=F32) + b_ref[...]


def _modulation(cond, w_mod, b_mod):
    n_layers, d, _ = w_mod.shape
    rows = cond.shape[0]
    b4 = b_mod.reshape(n_layers, 6, 1, d)
    return pl.pallas_call(
        _mod_kernel,
        grid=(n_layers, 6),
        in_specs=[pl.BlockSpec((rows, d), lambda l, k: (0, 0)),
                  pl.BlockSpec((None, d, d), lambda l, k: (l, 0, k)),
                  pl.BlockSpec((None, None, 1, d), lambda l, k: (l, k, 0, 0))],
        out_specs=pl.BlockSpec((None, None, rows, d), lambda l, k: (l, k, 0, 0)),
        out_shape=jax.ShapeDtypeStruct((n_layers, 6, rows, d), F32),
        compiler_params=_params(("arbitrary", "arbitrary")),
        name="modulation",
    )(cond, w_mod, b4)


def _adaln_kernel(h_ref, nw_ref, modt_ref, o_ref, *, nsub):
    nw = nw_ref[...]
    for s in range(nsub):
        rows = pl.ds(s * MOD_TILE, MOD_TILE)
        x = h_ref[rows, :]
        o_ref[rows, :] = _rms_mod(x, nw, modt_ref[s, 0:1, :], modt_ref[s, 1:2, :]).astype(o_ref.dtype)


def _adaln(h, nw, modt, tm=1024):
    m, d = h.shape
    nsub = tm // MOD_TILE
    return pl.pallas_call(
        functools.partial(_adaln_kernel, nsub=nsub),
        grid=(m // tm,),
        in_specs=[pl.BlockSpec((tm, d), lambda i: (i, 0)),
                  pl.BlockSpec((1, d), lambda i: (0, 0)),
                  pl.BlockSpec((nsub, 6, d), lambda i: (i, 0, 0))],
        out_specs=pl.BlockSpec((tm, d), lambda i: (i, 0)),
        out_shape=jax.ShapeDtypeStruct((m, d), BF16),
        compiler_params=_params(("arbitrary",)),
        name="adaln",
    )(h, nw.reshape(1, d), modt)


def _matmul_kernel(a_ref, w_ref, o_ref):
    o_ref[...] = jnp.dot(a_ref[...], w_ref[...], preferred_element_type=F32).astype(o_ref.dtype)


def _matmul(a, w, out_dtype, tm, tn, name):
    m, k = a.shape
    n = w.shape[1]
    return pl.pallas_call(
        _matmul_kernel,
        grid=(n // tn, m // tm),
        in_specs=[pl.BlockSpec((tm, k), lambda j, i: (i, 0)),
                  pl.BlockSpec((k, tn), lambda j, i: (0, j))],
        out_specs=pl.BlockSpec((tm, tn), lambda j, i: (i, j)),
        out_shape=jax.ShapeDtypeStruct((m, n), out_dtype),
        compiler_params=_params(("arbitrary", "arbitrary")),
        name=name,
    )(a, w)


def _merge_kernel(gs_ref, gn_ref, z_ref, yf_ref, yb_ref, yn_ref, h_ref, wbs_ref, wbn_ref, wo_ref, snw_ref,
                  nw_ref, modt_ref, ho_ref, f_ref, *, nsub, row_tile_out):
    gw = D_SSD // SSD_GROUPS
    nw = nw_ref[...]
    for s in range(nsub):
        r0, r1 = s * MOD_TILE, (s + 1) * MOD_TILE
        z = z_ref[r0:r1, :].astype(F32)
        u = (yf_ref[r0:r1, :].astype(F32) + yb_ref[r0:r1, :].astype(F32)) * (z * jax.nn.sigmoid(z))
        ys = []
        for g in range(SSD_GROUPS):
            ug = u[:, g * gw:(g + 1) * gw]
            ms = jnp.mean(ug * ug, axis=-1, keepdims=True)
            ys.append((ug * lax.rsqrt(ms + EPS) * snw_ref[:, g * gw:(g + 1) * gw]).astype(BF16))
        a = jnp.dot(jnp.concatenate(ys, axis=1), wbs_ref[...], preferred_element_type=F32)
        b = jnp.dot(yn_ref[r0:r1, :], wbn_ref[...], preferred_element_type=F32)
        mixed = (jax.nn.sigmoid(gs_ref[r0:r1, :].astype(F32)) * a
                 + jax.nn.sigmoid(gn_ref[r0:r1, :].astype(F32)) * b)
        m = jnp.dot(mixed.astype(BF16), wo_ref[...], preferred_element_type=F32)
        h = h_ref[r0:r1, :] + modt_ref[s, 2:3, :] * m
        ho_ref[r0:r1, :] = h
        f = _rms_mod(h, nw, modt_ref[s, 3:4, :], modt_ref[s, 4:5, :])
        if row_tile_out:
            _store_tokens(f_ref, r0, f)
        else:
            f_ref[r0:r1, :] = f.astype(f_ref.dtype)


def _merge(p_main, yf, yb, y_na, h, wbs, wbn, wo, ssd_nw, nw2, modt, row_tile_out, tm=512):
    m, d = h.shape
    nsub = tm // MOD_TILE
    row = lambda i: (i, 0)
    const = lambda i: (0, 0)
    if row_tile_out:
        f_spec, f_shape = pl.BlockSpec((tm * ROW_SEG, LANE), row), jax.ShapeDtypeStruct((m * ROW_SEG, LANE), F32)
    else:
        f_spec, f_shape = pl.BlockSpec((tm, d), row), jax.ShapeDtypeStruct((m, d), BF16)
    return pl.pallas_call(
        functools.partial(_merge_kernel, nsub=nsub, row_tile_out=row_tile_out),
        grid=(m // tm,),
        in_specs=[pl.BlockSpec((tm, d), lambda i: (i, COL_GS // D_MODEL)),
                  pl.BlockSpec((tm, d), lambda i: (i, COL_GN // D_MODEL)),
                  pl.BlockSpec((tm, d), lambda i: (i, COL_Z // D_MODEL)),
                  pl.BlockSpec((tm, d), row), pl.BlockSpec((tm, d), row), pl.BlockSpec((tm, d), row),
                  pl.BlockSpec((tm, d), row),
                  pl.BlockSpec((d, d), const), pl.BlockSpec((d, d), const), pl.BlockSpec((d, d), const),
                  pl.BlockSpec((1, d), const), pl.BlockSpec((1, d), const),
                  pl.BlockSpec((nsub, 6, d), lambda i: (i, 0, 0))],
        out_specs=[pl.BlockSpec((tm, d), row), f_spec],
        out_shape=[jax.ShapeDtypeStruct((m, d), F32), f_shape],
        compiler_params=_params(("arbitrary",)),
        name="merge",
    )(p_main, p_main, p_main, yf, yb, y_na, h, wbs, wbn, wo, ssd_nw.astype(F32).reshape(1, d), nw2.reshape(1, d),
      modt)


def _ffn_kernel(x_ref, h_ref, w1_ref, w3_ref, w2_ref, nw_ref, modt_ref, modn_ref, ho_ref, an_ref, *, nsub):
    x = x_ref[...]
    h1 = jnp.dot(x, w1_ref[...], preferred_element_type=F32)
    h3 = jnp.dot(x, w3_ref[...], preferred_element_type=F32)
    act = (h1 * jax.nn.sigmoid(h1) * h3).astype(BF16)
    out = jnp.dot(act, w2_ref[...], preferred_element_type=F32)
    nw = nw_ref[...]
    for s in range(nsub):
        r0, r1 = s * MOD_TILE, (s + 1) * MOD_TILE
        h = h_ref[r0:r1, :] + modt_ref[s, 5:6, :] * out[r0:r1, :]
        ho_ref[r0:r1, :] = h
        an_ref[r0:r1, :] = _rms_mod(h, nw, modn_ref[s, 0:1, :], modn_ref[s, 1:2, :]).astype(an_ref.dtype)


def _ffn(x, h, w1, w3, w2, nw_next, modt, modt_next, tm=512):
    m, d = h.shape
    ff = w1.shape[1]
    nsub = tm // MOD_TILE
    row = lambda i: (i, 0)
    const = lambda i: (0, 0)
    once = pl.Buffered(1)
    return pl.pallas_call(
        functools.partial(_ffn_kernel, nsub=nsub),
        grid=(m // tm,),
        in_specs=[pl.BlockSpec((tm, d), row), pl.BlockSpec((tm, d), row),
                  pl.BlockSpec((d, ff), const, pipeline_mode=once),
                  pl.BlockSpec((d, ff), const, pipeline_mode=once),
                  pl.BlockSpec((ff, d), const, pipeline_mode=once),
                  pl.BlockSpec((1, d), const),
                  pl.BlockSpec((nsub, 6, d), lambda i: (i, 0, 0)),
                  pl.BlockSpec((nsub, 6, d), lambda i: (i, 0, 0))],
        out_specs=[pl.BlockSpec((tm, d), row), pl.BlockSpec((tm, d), row)],
        out_shape=[jax.ShapeDtypeStruct((m, d), F32), jax.ShapeDtypeStruct((m, d), BF16)],
        compiler_params=_params(("arbitrary",)),
        name="ffn",
    )(x, h, w1, w3, w2, nw_next.reshape(1, d), modt, modt_next)


NA_BAND = NA_ROWS_MAX * GRID_W
HEAD_PAIR = LANE // NA_HEAD_DIM
NA_UNROLL = 8
NA_PREP_UNROLL = 8


def _head_mean_matrix():
    ri = lax.broadcasted_iota(jnp.int32, (2 * LANE, LANE), 0) % LANE
    ci = lax.broadcasted_iota(jnp.int32, (2 * LANE, LANE), 1)
    half = NA_HEAD_DIM // 2
    same_head = (ri % NA_HEAD_DIM) // half == (ci % NA_HEAD_DIM) // half
    return jnp.where(same_head, 1.0 / NA_HEAD_DIM, 0.0).astype(BF16)


def _head_rms(x, w, mean_mat):
    sq = x * x
    hi = sq.astype(BF16)
    lo = (sq - hi.astype(F32)).astype(BF16)
    ms = jnp.dot(jnp.concatenate([hi, lo], axis=1), mean_mat, preferred_element_type=F32)
    return x * lax.rsqrt(ms + EPS) * w


def _qk_lane_order():
    quarter = NA_HEAD_DIM // 4
    order = []
    for half in range(2):
        for head in range(HEAD_PAIR):
            for part in range(2):
                start = head * NA_HEAD_DIM + part * 2 * quarter + half * quarter
                order += list(range(start, start + quarter))
    return order


def _qk_reorder(w):
    k, n = w.shape
    return w.reshape(k, n // LANE, LANE)[:, :, jnp.asarray(_qk_lane_order())].reshape(k, n)


def _rope(x, cos, sin):
    return x * cos + pltpu.roll(x, LANE // 2, axis=1) * sin


def _stack_heads(x, lane_lo):
    zero = jnp.zeros_like(x)
    return jnp.concatenate([jnp.where(lane_lo, x, zero), jnp.where(lane_lo, zero, x)], axis=0)


def _attend(qs, k_list, v_list, bias_list, lane_lo):
    nt = (((1,), (1,)), ((), ()))
    s_list = []
    for k, bias in zip(k_list, bias_list):
        s = lax.dot_general(qs, k, nt, preferred_element_type=F32)
        s_list.append(s if bias is None else s + bias)
    mx = s_list[0].max(axis=-1, keepdims=True)
    for s in s_list[1:]:
        mx = jnp.maximum(mx, s.max(axis=-1, keepdims=True))
    den = None
    acc = None
    for s, v in zip(s_list, v_list):
        p = jnp.exp(s - mx)
        d = p.sum(axis=-1, keepdims=True)
        o = jnp.dot(p.astype(BF16), v, preferred_element_type=F32)
        den = d if den is None else den + d
        acc = o if acc is None else acc + o
    acc = acc / den
    n = acc.shape[0] // 2
    return jnp.where(lane_lo, acc[:n], acc[n:])


def _na_kernel(q_ref, k_ref, v_ref, bias_ref, rowc_ref, rows_ref, colc_ref, cols_ref, qw_ref, kw_ref,
               o_ref, qn_ref, kn_ref, s_ref, m_ref, *, clen, n_rows):
    lane = lax.broadcasted_iota(jnp.int32, (1, LANE), 1)
    lane_lo = lane < NA_HEAD_DIM
    qk_h0 = (lane % NA_HEAD_DIM) < NA_HEAD_DIM // 2
    qw = qw_ref[...] * (NA_HEAD_DIM ** -0.5)
    kw = kw_ref[...]
    kr = NA_ROWS_MAX

    mean_mat = _head_mean_matrix()
    qn_ref[0:clen, :] = _head_rms(q_ref[0:clen, :].astype(F32), qw, mean_mat).astype(BF16)
    kn_ref[0:clen, :] = _head_rms(k_ref[0:clen, :].astype(F32), kw, mean_mat).astype(BF16)
    colc = colc_ref[...]
    cols = cols_ref[...]

    def prep(r, carry):
        rows = pl.ds(pl.multiple_of(clen + r * GRID_W, GRID_W), GRID_W)
        cos = rowc_ref[pl.ds(r, 1), :] + colc
        sin = rows_ref[pl.ds(r, 1), :] + cols
        q = _head_rms(q_ref[rows, :].astype(F32), qw, mean_mat)
        k = _head_rms(k_ref[rows, :].astype(F32), kw, mean_mat)
        qn_ref[rows, :] = _rope(q, cos, sin).astype(BF16)
        kn_ref[rows, :] = _rope(k, cos, sin).astype(BF16)
        return carry

    lax.fori_loop(0, n_rows, prep, 0, unroll=NA_PREP_UNROLL)

    kc = kn_ref[0:clen, :]
    vc = v_ref[0:clen, :]
    for blk in range(clen // GRID_W):
        rows = slice(blk * GRID_W, (blk + 1) * GRID_W)
        qs = _stack_heads(qn_ref[rows, :], qk_h0)
        o_ref[rows, :] = _attend(qs, [kc], [vc], [None], lane_lo).astype(o_ref.dtype)

    nt = (((1,), (1,)), ((), ()))
    n_blk = (NA_BAND + clen) // LANE

    def band_of(r):
        r0 = jnp.clip(r - kr // 2, 0, n_rows - kr)
        return r0, pl.ds(pl.multiple_of(clen + r0 * GRID_W, GRID_W), NA_BAND)

    n_win = NA_BAND // LANE
    sq = HEAD_PAIR * GRID_W

    def scores(rows_r, slots):
        qs = [_stack_heads(qn_ref[pl.ds(pl.multiple_of(clen + r * GRID_W, GRID_W), GRID_W), :], qk_h0)
              for r in rows_r]
        s_ctx = lax.dot_general(jnp.concatenate(qs, axis=0), kn_ref[0:clen, :], nt, preferred_element_type=F32)
        for u, (r, slot) in enumerate(zip(rows_r, slots)):
            r0, band = band_of(r)
            s_win = lax.dot_general(qs[u], kn_ref[band, :], nt, preferred_element_type=F32)
            bias = bias_ref[r0 - r + (NA_ROWS_MAX - 1)]
            blocks = [s_win[:, j * LANE:(j + 1) * LANE] + bias[:, j * LANE:(j + 1) * LANE] for j in range(n_win)]
            blocks += [s_ctx[u * sq:(u + 1) * sq, j * LANE:(j + 1) * LANE] for j in range(n_blk - n_win)]
            mx = blocks[0]
            for j in range(n_blk):
                s_ref[slot, :, j * LANE:(j + 1) * LANE] = blocks[j]
                mx = jnp.maximum(mx, blocks[j])
            m_ref[slot] = jnp.broadcast_to(mx.max(axis=-1, keepdims=True), mx.shape)

    def output(rows_r, slots):
        accs, dens, p_ctx = [], [], []
        for r, slot in zip(rows_r, slots):
            _, band = band_of(r)
            mx = m_ref[slot]
            p = [jnp.exp(s_ref[slot, :, j * LANE:(j + 1) * LANE] - mx) for j in range(n_blk)]
            den = p[0]
            for pj in p[1:]:
                den = den + pj
            dens.append(den.sum(axis=-1, keepdims=True))
            pb = [pj.astype(BF16) for pj in p]
            accs.append(jnp.dot(jnp.concatenate(pb[:n_win], axis=1), v_ref[band, :], preferred_element_type=F32))
            p_ctx.append(jnp.concatenate(pb[n_win:], axis=1))
        acc_ctx = jnp.dot(jnp.concatenate(p_ctx, axis=0), v_ref[0:clen, :], preferred_element_type=F32)
        for u, r in enumerate(rows_r):
            acc = (accs[u] + acc_ctx[u * sq:(u + 1) * sq]) / dens[u]
            rows = pl.ds(pl.multiple_of(clen + r * GRID_W, GRID_W), GRID_W)
            o_ref[rows, :] = jnp.where(lane_lo, acc[:GRID_W], acc[GRID_W:]).astype(o_ref.dtype)

    scores(list(range(NA_UNROLL)), list(range(NA_UNROLL)))

    def body(j, carry):
        r = 2 * NA_UNROLL * j
        for half in range(2):
            output([r + half * NA_UNROLL + u for u in range(NA_UNROLL)],
                   [half * NA_UNROLL + u for u in range(NA_UNROLL)])
            scores([jnp.minimum(r + (half + 1) * NA_UNROLL + u, n_rows - 1) for u in range(NA_UNROLL)],
                   [(1 - half) * NA_UNROLL + u for u in range(NA_UNROLL)])
        return carry

    lax.fori_loop(0, n_rows // (2 * NA_UNROLL), body, 0)


def _na_tables(rpb, n_rows):
    col = jnp.arange(GRID_W)
    col_start = jnp.clip(col - NA_COLS // 2, 0, GRID_W - NA_COLS)
    col_ok = (col[None, :] >= col_start[:, None]) & (col[None, :] < col_start[:, None] + NA_COLS)
    col_idx = jnp.clip(col[None, :] - col[:, None], -(NA_COLS - 1), NA_COLS - 1) + NA_COLS - 1
    rpb_c = jnp.where(col_ok[None, None], rpb.astype(F32)[:, :, col_idx], NEG_INF)
    dr = jnp.arange(NA_ROWS_MAX)[:, None] + jnp.arange(NA_ROWS_MAX)[None, :]
    tab = rpb_c[:, dr]
    tab = jnp.transpose(tab, (0, 1, 3, 2, 4)).reshape(NA_HEADS, NA_ROWS_MAX, GRID_W, NA_BAND)
    tab = tab.reshape(NA_HEADS // HEAD_PAIR, HEAD_PAIR, NA_ROWS_MAX, GRID_W, NA_BAND)
    tab = jnp.transpose(tab, (0, 2, 1, 3, 4)).reshape(NA_HEADS // HEAD_PAIR, NA_ROWS_MAX,
                                                       HEAD_PAIR * GRID_W, NA_BAND)
    quarter = NA_HEAD_DIM // 4
    inv = ROPE_BASE ** (-jnp.arange(quarter, dtype=F32) / quarter)

    def tables(pos, lo):
        ang = pos[:, None] * inv[None, :]
        z = jnp.zeros_like(ang)
        c, s = jnp.cos(ang), jnp.sin(ang)
        ch = jnp.concatenate([c, c, z, z] if lo else [z, z, c, c], axis=-1)
        sh = jnp.concatenate([-s, s, z, z] if lo else [z, z, -s, s], axis=-1)
        order = jnp.asarray(_qk_lane_order())
        return jnp.tile(ch, (1, HEAD_PAIR))[:, order], jnp.tile(sh, (1, HEAD_PAIR))[:, order]

    rowc, rows = tables(jnp.arange(n_rows).astype(F32), True)
    colc, cols = tables(jnp.arange(GRID_W).astype(F32), False)
    return tab, rowc, rows, colc, cols


def _na(pm3, q_norm_w, k_norm_w, rpb, clen):
    bsz, t, _ = pm3.shape
    n_rows = (t - clen) // GRID_W
    tab, rowc, rows, colc, cols = _na_tables(rpb, n_rows)
    order = jnp.asarray(_qk_lane_order())
    qw = jnp.tile(q_norm_w.astype(F32), HEAD_PAIR)[order].reshape(1, LANE)
    kw = jnp.tile(k_norm_w.astype(F32), HEAD_PAIR)[order].reshape(1, LANE)
    n_pairs = NA_HEADS // HEAD_PAIR
    const = lambda p, b: (0, 0)
    tok = lambda c0: pl.BlockSpec((None, t, LANE), lambda p, b: (b, 0, c0 // LANE + p))
    return pl.pallas_call(
        functools.partial(_na_kernel, clen=clen, n_rows=n_rows),
        grid=(n_pairs, bsz),
        in_specs=[tok(COL_Q), tok(COL_K), tok(COL_V),
                  pl.BlockSpec((None, NA_ROWS_MAX, HEAD_PAIR * GRID_W, NA_BAND), lambda p, b: (p, 0, 0, 0)),
                  pl.BlockSpec((n_rows, LANE), const), pl.BlockSpec((n_rows, LANE), const),
                  pl.BlockSpec((GRID_W, LANE), const), pl.BlockSpec((GRID_W, LANE), const),
                  pl.BlockSpec((1, LANE), const), pl.BlockSpec((1, LANE), const)],
        out_specs=pl.BlockSpec((None, t, LANE), lambda p, b: (b, 0, p)),
        out_shape=jax.ShapeDtypeStruct((bsz, t, D_NA), BF16),
        scratch_shapes=[pltpu.VMEM((t, LANE), BF16), pltpu.VMEM((t, LANE), BF16),
                        pltpu.VMEM((2 * NA_UNROLL, HEAD_PAIR * GRID_W, NA_BAND + clen), F32),
                        pltpu.VMEM((2 * NA_UNROLL, HEAD_PAIR * GRID_W, LANE), F32)],
        compiler_params=_params(("arbitrary", "arbitrary")),
        name="natten",
    )(pm3, pm3, pm3, tab, rowc, rows, colc, cols, qw, kw)


HALO = 16
CONV_ROWS = SSD_CHUNK
CONV_PAD = SSD_CONV // 2
BC_DIM = SSD_GROUPS * SSD_STATE
PAIRS_PER_GROUP = SSD_HPG // HEAD_PAIR


def _ssd_chunk_index(s, rev, cc, nc):
    if not rev:
        return s
    return jnp.where(s < cc, cc - 1 - s, nc - 1 - (s - cc))


def _ssd_conv_kernel(main_ref, prev_ref, next_ref, cw_ref, cb_ref, o_ref, *, cc, nc):
    q = main_ref.shape[0]
    cidx = pl.program_id(1)
    first_col = jnp.where((cidx == 0) | (cidx == cc), HALO, 0)
    end_col = jnp.where((cidx == cc - 1) | (cidx == nc - 1), HALO + q, q + 2 * HALO)
    ext = jnp.concatenate([prev_ref[...], main_ref[...], next_ref[...]], axis=0)
    ti = lax.broadcasted_iota(jnp.int32, (q, q + 2 * HALO), 0)
    ji = lax.broadcasted_iota(jnp.int32, (q, q + 2 * HALO), 1)
    in_segment = (ji >= first_col) & (ji < end_col)
    conv = cb_ref[...] + cw_ref[CONV_PAD:CONV_PAD + 1, :] * main_ref[...].astype(F32)
    for k in range(SSD_CONV):
        if k == CONV_PAD:
            continue
        shift = jnp.where((ji == ti + (HALO + k - CONV_PAD)) & in_segment, 1.0, 0.0).astype(BF16)
        conv = conv + cw_ref[k:k + 1, :] * jnp.dot(shift, ext, preferred_element_type=F32)
    o_ref[...] = (conv * jax.nn.sigmoid(conv)).astype(o_ref.dtype)


def _ssd_decay(dt_ref, dtb_ref, a_ref, *, rev):
    q = SSD_CHUNK
    x = dt_ref[...] + dtb_ref[...]
    dt = jnp.maximum(x, 0.0) + jnp.log(1.0 + jnp.exp(-jnp.abs(x)))
    la = dt * a_ref[...]
    ri = lax.broadcasted_iota(jnp.int32, (q, q), 0)
    ci = lax.broadcasted_iota(jnp.int32, (q, q), 1)
    allowed = (ci >= ri) if rev else (ci <= ri)
    tri = jnp.where(allowed, 1.0, 0.0).astype(F32)
    a_cs = jnp.dot(tri, la, precision=HIGHEST, preferred_element_type=F32)
    last = 0 if rev else q - 1
    a_cs_t = a_cs.T
    src_t = a_cs_t - jnp.log(dt.T)
    wdt_t = jnp.exp(a_cs_t[:, last:last + 1] - src_t)
    neg_mask = jnp.where(allowed, 0.0, NEG_INF)
    return a_cs, src_t, wdt_t, neg_mask


def _ssd_group(xbc, h_ref, yoff_ref, g):
    nt = (((1,), (1,)), ((), ()))
    b_g = xbc[:, D_SSD + g * SSD_STATE:D_SSD + (g + 1) * SSD_STATE]
    c_g = xbc[:, D_SSD + BC_DIM + g * SSD_STATE:D_SSD + BC_DIM + (g + 1) * SSD_STATE]
    cols = slice(g * SSD_HPG * SSD_HEAD_DIM, (g + 1) * SSD_HPG * SSD_HEAD_DIM)
    yoff_ref[:, cols] = jnp.dot(c_g, h_ref[:, cols].astype(BF16), preferred_element_type=F32)
    cb = lax.dot_general(c_g, b_g, nt, preferred_element_type=F32)
    return cb, b_g.astype(F32).T


def _ssd_pair(xbc, decay, cb, b_t, h_ref, yoff_ref, y_ref, pair, *, rev):
    q = SSD_CHUNK
    a_cs, src_t, wdt_t, neg_mask = decay
    hb = SSD_HEADS if rev else 0
    last = 0 if rev else q - 1
    lane_lo = lax.broadcasted_iota(jnp.int32, (1, LANE), 1) < SSD_HEAD_DIM
    lanes = slice(pair * LANE, (pair + 1) * LANE)
    lhs, ecol = [], []
    for hh in range(HEAD_PAIR):
        head = hb + pair * HEAD_PAIR + hh
        a_col = jnp.broadcast_to(a_cs[:, head:head + 1], (q, LANE))
        lhs.append((cb * jnp.exp((a_col - src_t[head:head + 1, :]) + neg_mask)).astype(BF16))
        ecol.append(jnp.exp(a_col))
    for hh in range(HEAD_PAIR):
        head = hb + pair * HEAD_PAIR + hh
        lhs.append((b_t * wdt_t[head:head + 1, :]).astype(BF16))
    res = jnp.dot(jnp.concatenate(lhs, axis=0), xbc[:, lanes], preferred_element_type=F32)
    e_pair = jnp.where(lane_lo, ecol[0], ecol[1])
    y_ref[:, lanes] = jnp.where(lane_lo, res[0:q], res[q:2 * q]) + yoff_ref[:, lanes] * e_pair
    h_ref[:, lanes] = (h_ref[:, lanes] * e_pair[last:last + 1, :]
                       + jnp.where(lane_lo, res[2 * q:3 * q], res[3 * q:4 * q]))


def _ssd_scan_kernel(xf_ref, dtf_ref, xb_ref, dtb_ref, bias_ref, a_ref, dsum_ref, of_ref, ob_ref,
                     hf_ref, yofff_ref, hb_ref, yoffb_ref, yf_ref, yb_ref):
    @pl.when(pl.program_id(1) == 0)
    def _():
        hf_ref[...] = jnp.zeros_like(hf_ref)
        hb_ref[...] = jnp.zeros_like(hb_ref)

    decay_f = _ssd_decay(dtf_ref, bias_ref, a_ref, rev=False)
    decay_b = _ssd_decay(dtb_ref, bias_ref, a_ref, rev=True)
    xf, xb = xf_ref[...], xb_ref[...]
    for g in range(SSD_GROUPS):
        cb_f, bt_f = _ssd_group(xf, hf_ref, yofff_ref, g)
        cb_b, bt_b = _ssd_group(xb, hb_ref, yoffb_ref, g)
        for pj in range(PAIRS_PER_GROUP):
            pair = g * PAIRS_PER_GROUP + pj
            _ssd_pair(xf, decay_f, cb_f, bt_f, hf_ref, yofff_ref, yf_ref, pair, rev=False)
            _ssd_pair(xb, decay_b, cb_b, bt_b, hb_ref, yoffb_ref, yb_ref, pair, rev=True)
    of_ref[...] = (yf_ref[...] + xf_ref[:, 0:D_SSD].astype(F32) * dsum_ref[...]).astype(of_ref.dtype)
    ob_ref[...] = yb_ref[...].astype(ob_ref.dtype)


def _ssd(pm3, pdt3, conv_w, conv_b, dt_bias, a_log, d_skip, clen):
    bsz, t, _ = pm3.shape
    q = SSD_CHUNK
    nc, cc = t // q, clen // q
    nh = 2 * SSD_HEADS
    cw = jnp.pad(conv_w.astype(F32), ((0, 8 - SSD_CONV), (0, 0)))
    cb = conv_b.astype(F32).reshape(1, XBC_DIM)
    dtb = jnp.pad(dt_bias.astype(F32).reshape(1, nh), ((0, 0), (0, DT_PAD - nh)))
    a_neg = jnp.pad(-jnp.exp(a_log.astype(F32)).reshape(1, nh), ((0, 0), (0, DT_PAD - nh)))
    dsum = jnp.repeat((d_skip[0] + d_skip[1]).astype(F32), SSD_HEAD_DIM).reshape(1, D_SSD)
    n_halo = t // HALO

    const = lambda b, s: (0, 0)
    xcol = COL_XBC // XBC_DIM
    cq = CONV_ROWS
    hpb = cq // HALO
    xbc = pl.pallas_call(
        functools.partial(_ssd_conv_kernel, cc=clen // cq, nc=t // cq),
        grid=(bsz, t // cq),
        in_specs=[pl.BlockSpec((None, cq, XBC_DIM), lambda b, s: (b, s, xcol)),
                  pl.BlockSpec((None, HALO, XBC_DIM), lambda b, s: (b, jnp.maximum(s * hpb - 1, 0), xcol)),
                  pl.BlockSpec((None, HALO, XBC_DIM),
                               lambda b, s: (b, jnp.minimum((s + 1) * hpb, n_halo - 1), xcol)),
                  pl.BlockSpec((8, XBC_DIM), const), pl.BlockSpec((1, XBC_DIM), const)],
        out_specs=pl.BlockSpec((None, cq, XBC_DIM), lambda b, s: (b, s, 0)),
        out_shape=jax.ShapeDtypeStruct((bsz, t, XBC_DIM), BF16),
        compiler_params=_params(("arbitrary", "arbitrary")),
        name="ssd_conv",
    )(pm3, pm3, pm3, cw, cb)

    tok_f = lambda b, s: (b, s, 0)
    tok_b = lambda b, s: (b, _ssd_chunk_index(s, True, cc, nc), 0)
    state = [pltpu.VMEM((SSD_STATE, D_SSD), F32), pltpu.VMEM((q, D_SSD), F32)]
    return pl.pallas_call(
        _ssd_scan_kernel,
        grid=(bsz, nc),
        in_specs=[pl.BlockSpec((None, q, XBC_DIM), tok_f), pl.BlockSpec((None, q, DT_PAD), tok_f),
                  pl.BlockSpec((None, q, XBC_DIM), tok_b), pl.BlockSpec((None, q, DT_PAD), tok_b),
                  pl.BlockSpec((1, DT_PAD), const), pl.BlockSpec((1, DT_PAD), const),
                  pl.BlockSpec((1, D_SSD), const)],
        out_specs=[pl.BlockSpec((None, q, D_SSD), tok_f), pl.BlockSpec((None, q, D_SSD), tok_b)],
        out_shape=[jax.ShapeDtypeStruct((bsz, t, D_SSD), BF16), jax.ShapeDtypeStruct((bsz, t, D_SSD), BF16)],
        scratch_shapes=state + state + [pltpu.VMEM((q, D_SSD), F32), pltpu.VMEM((q, D_SSD), F32)],
        compiler_params=_params(("arbitrary", "arbitrary")),
        name="ssd_scan",
    )(xbc, pdt3, xbc, pdt3, dtb, a_neg, dsum)


MOE_TB = 512
ROUTE_TM = 512
INFO_W = 8
BIG_NEG = -3.0e38
ROW_SEG = D_MODEL // LANE


def _load_tokens(ref, t0, n):
    return jnp.concatenate([ref[pl.ds(t0 * ROW_SEG + c, n, stride=ROW_SEG), :] for c in range(ROW_SEG)], axis=1)


def _store_tokens(ref, t0, val):
    n = val.shape[0]
    for c in range(ROW_SEG):
        ref[pl.ds(t0 * ROW_SEG + c, n, stride=ROW_SEG), :] = val[:, c * LANE:(c + 1) * LANE]


def _route_kernel(f_ref, wr_ref, info_ref, cnt_ref, run_ref):
    tm = f_ref.shape[0] // ROW_SEG

    @pl.when(pl.program_id(0) == 0)
    def _():
        run_ref[...] = jnp.zeros_like(run_ref)

    logits = jnp.dot(_load_tokens(f_ref, 0, tm).astype(BF16), wr_ref[...], preferred_element_type=F32)
    lane = lax.broadcasted_iota(jnp.int32, (tm, LANE), 1).astype(F32)
    lg = jnp.where(lane < N_EXPERTS, logits, BIG_NEG)
    m1 = lg.max(axis=-1, keepdims=True)
    i1 = jnp.where(lg == m1, lane, float(LANE)).min(axis=-1, keepdims=True)
    lg2 = jnp.where(lane == i1, BIG_NEG, lg)
    m2 = lg2.max(axis=-1, keepdims=True)
    i2 = jnp.where(lg2 == m2, lane, float(LANE)).min(axis=-1, keepdims=True)
    ex = jnp.exp(m2 - m1)
    g0 = 1.0 / (1.0 + ex)
    g1 = ex / (1.0 + ex)
    sel0 = lane == i1
    sel1 = lane == i2
    onehot = jnp.where(sel0 | sel1, 1.0, 0.0)
    ri = lax.broadcasted_iota(jnp.int32, (tm, tm), 0)
    ci = lax.broadcasted_iota(jnp.int32, (tm, tm), 1)
    strict_lower = jnp.where(ci < ri, 1.0, 0.0).astype(BF16)
    before = jnp.dot(strict_lower, onehot.astype(BF16), preferred_element_type=F32) + run_ref[...]
    pos0 = jnp.where(sel0, before, 0.0).sum(axis=-1, keepdims=True)
    pos1 = jnp.where(sel1, before, 0.0).sum(axis=-1, keepdims=True)
    run_ref[...] = run_ref[...] + onehot.sum(axis=0, keepdims=True)
    cnt_ref[...] = run_ref[...]
    info = jnp.where(lane == 0, i1, jnp.where(lane == 1, i2, jnp.where(lane == 2, g0, jnp.where(
        lane == 3, g1, jnp.where(lane == 4, pos0, jnp.where(lane == 5, pos1, 0.0))))))
    info_ref[...] = info[:, :INFO_W]


def _route(f8, w_router):
    m, d = f8.shape[0] // ROW_SEG, D_MODEL
    wr = jnp.pad(w_router.astype(BF16), ((0, 0), (0, LANE - N_EXPERTS)))
    return pl.pallas_call(
        _route_kernel,
        grid=(m // ROUTE_TM,),
        in_specs=[pl.BlockSpec((ROUTE_TM * ROW_SEG, LANE), lambda i: (i, 0)),
                  pl.BlockSpec((d, LANE), lambda i: (0, 0))],
        out_specs=[pl.BlockSpec((ROUTE_TM, INFO_W), lambda i: (i, 0)), pl.BlockSpec((1, LANE), lambda i: (0, 0))],
        out_shape=[jax.ShapeDtypeStruct((m, INFO_W), F32), jax.ShapeDtypeStruct((1, LANE), F32)],
        scratch_shapes=[pltpu.VMEM((1, LANE), F32)],
        compiler_params=_params(("arbitrary",)),
        name="moe_route",
    )(f8, wr)


def _row_copy(src_ref, src_row, dst_ref, dst_row, sem):
    src = src_ref.at[pl.ds(pl.multiple_of(src_row * ROW_SEG, ROW_SEG), ROW_SEG)]
    dst = dst_ref.at[pl.ds(pl.multiple_of(dst_row * ROW_SEG, ROW_SEG), ROW_SEG)]
    return pltpu.make_async_copy(src, dst, sem)


def _wait_rows(src_ref, dst_ref, sem, n):
    def wait(t, c):
        _row_copy(src_ref, 0, dst_ref, 0, sem).wait()
        return c

    lax.fori_loop(0, n, wait, 0, unroll=8)


def _dispatch_kernel(pad_ref, dest_hbm, f_ref, rows_hbm, dest_smem, stage_ref, zero_ref, idx_sem, row_sem,
                     pad_sem):
    tm = f_ref.shape[0] // ROW_SEG
    i, n = pl.program_id(0), pl.num_programs(0)
    slot = i % 2
    stage = stage_ref.at[slot]
    sem = row_sem.at[slot]

    @pl.when(i == 0)
    def _():
        zero_ref[...] = jnp.zeros_like(zero_ref)
        for e in range(N_EXPERTS):
            first, count = pad_ref[e], pad_ref[N_EXPERTS + e]

            def start_zero(j, c):
                _row_copy(zero_ref, 0, rows_hbm, first + j, pad_sem).start()
                return c

            lax.fori_loop(0, count, start_zero, 0)
        for e in range(N_EXPERTS):
            def wait_zero(j, c):
                _row_copy(zero_ref, 0, rows_hbm, 0, pad_sem).wait()
                return c

            lax.fori_loop(0, pad_ref[N_EXPERTS + e], wait_zero, 0)

    @pl.when(i >= 2)
    def _():
        _wait_rows(stage, rows_hbm, sem, TOP_K * tm)

    stage[...] = f_ref[...]
    base = pl.multiple_of(i * (TOP_K * tm), TOP_K * tm)
    idx_copy = pltpu.make_async_copy(dest_hbm.at[pl.ds(base, TOP_K * tm)], dest_smem, idx_sem)
    idx_copy.start()
    idx_copy.wait()

    def start(t, c):
        for k in range(TOP_K):
            _row_copy(stage, t, rows_hbm, dest_smem[TOP_K * t + k], sem).start()
        return c

    lax.fori_loop(0, tm, start, 0, unroll=8)

    @pl.when(i == n - 1)
    def _():
        _wait_rows(stage, rows_hbm, sem, TOP_K * tm)

        @pl.when(n >= 2)
        def _():
            _wait_rows(stage_ref.at[1 - slot], rows_hbm, row_sem.at[1 - slot], TOP_K * tm)


def _dispatch(f8, dest_flat, pad_info, n_rows):
    m = f8.shape[0] // ROW_SEG
    grid_spec = pltpu.PrefetchScalarGridSpec(
        num_scalar_prefetch=1,
        grid=(m // ROUTE_TM,),
        in_specs=[pl.BlockSpec(memory_space=pl.ANY),
                  pl.BlockSpec((ROUTE_TM * ROW_SEG, LANE), lambda i, pad: (i, 0))],
        out_specs=pl.BlockSpec(memory_space=pl.ANY),
        scratch_shapes=[pltpu.SMEM((TOP_K * ROUTE_TM,), jnp.int32),
                        pltpu.VMEM((2, ROUTE_TM * ROW_SEG, LANE), f8.dtype),
                        pltpu.VMEM((ROW_SEG, LANE), f8.dtype),
                        pltpu.SemaphoreType.DMA(()), pltpu.SemaphoreType.DMA((2,)), pltpu.SemaphoreType.DMA(())],
    )
    return pl.pallas_call(
        _dispatch_kernel,
        grid_spec=grid_spec,
        out_shape=jax.ShapeDtypeStruct((n_rows * ROW_SEG, LANE), f8.dtype),
        compiler_params=_params(("arbitrary",)),
        name="moe_dispatch",
    )(pad_info, dest_flat, f8)


def _gffn_kernel(be_ref, nu_ref, x_ref, w1_ref, w3_ref, w2_ref, y_ref):
    i = pl.program_id(0)

    @pl.when(i < nu_ref[0])
    def _():
        x = _load_tokens(x_ref, 0, MOE_TB).astype(BF16)
        h1 = jnp.dot(x, w1_ref[...], preferred_element_type=F32)
        h3 = jnp.dot(x, w3_ref[...], preferred_element_type=F32)
        act = (h1 * jax.nn.sigmoid(h1) * h3).astype(BF16)
        _store_tokens(y_ref, 0, jnp.dot(act, w2_ref[...], preferred_element_type=F32))

    @pl.when(i >= nu_ref[0])
    def _():
        y_ref[...] = jnp.zeros_like(y_ref)


def _gffn(rows, block_e, n_used, w1, w3, w2):
    r, d = rows.shape[0] // ROW_SEG, D_MODEL
    ff = w1.shape[2]
    nb = r // MOE_TB
    once = pl.Buffered(1)
    grid_spec = pltpu.PrefetchScalarGridSpec(
        num_scalar_prefetch=2,
        grid=(nb,),
        in_specs=[pl.BlockSpec((MOE_TB * ROW_SEG, LANE), lambda i, be, nu: (i, 0)),
                  pl.BlockSpec((None, d, ff), lambda i, be, nu: (be[i], 0, 0), pipeline_mode=once),
                  pl.BlockSpec((None, d, ff), lambda i, be, nu: (be[i], 0, 0), pipeline_mode=once),
                  pl.BlockSpec((None, ff, d), lambda i, be, nu: (be[i], 0, 0), pipeline_mode=once)],
        out_specs=pl.BlockSpec((MOE_TB * ROW_SEG, LANE), lambda i, be, nu: (i, 0)),
    )
    return pl.pallas_call(
        _gffn_kernel,
        grid_spec=grid_spec,
        out_shape=jax.ShapeDtypeStruct((r * ROW_SEG, LANE), F32),
        compiler_params=_params(("arbitrary",)),
        name="moe_ffn",
    )(block_e, n_used, rows, w1, w3, w2)


def _combine_kernel(dest_hbm, y_hbm, h_ref, info_ref, modt_ref, o_ref, dest_smem, buf_ref, idx_sem, row_sem,
                    *scratch, nsub, tiles_per_batch):
    tm = h_ref.shape[0]
    i, n = pl.program_id(0), pl.num_programs(0)
    latent_only = tiles_per_batch is not None
    if latent_only:
        obuf_ref, out_sem = scratch

        def latent_copies(step, buf_slot, fn):
            for s in range(nsub):
                g = step * nsub + s
                j = lax.rem(g, tiles_per_batch)

                @pl.when(j > 0)
                def _():
                    row0 = (lax.div(g, tiles_per_batch) * (tiles_per_batch - 1) + (j - 1)) * MOD_TILE
                    fn(pltpu.make_async_copy(obuf_ref.at[buf_slot, pl.ds(s * MOD_TILE, MOD_TILE)],
                                             o_ref.at[pl.ds(pl.multiple_of(row0, MOD_TILE), MOD_TILE)],
                                             out_sem.at[buf_slot]))

    def gather(step, slot):
        base = pl.multiple_of(step * (TOP_K * tm), TOP_K * tm)
        idx_copy = pltpu.make_async_copy(dest_hbm.at[pl.ds(base, TOP_K * tm)], dest_smem, idx_sem)
        idx_copy.start()
        idx_copy.wait()

        def start(t, c):
            for k in range(TOP_K):
                _row_copy(y_hbm, dest_smem[TOP_K * t + k], buf_ref.at[slot, k], t, row_sem.at[slot]).start()
            return c

        lax.fori_loop(0, tm, start, 0, unroll=8)

    @pl.when(i == 0)
    def _():
        gather(0, 0)

    @pl.when(i + 1 < n)
    def _():
        gather(i + 1, (i + 1) % 2)

    slot = i % 2
    _wait_rows(y_hbm, buf_ref.at[slot, 0], row_sem.at[slot], TOP_K * tm)
    if latent_only:
        @pl.when(i >= 2)
        def _():
            latent_copies(i - 2, slot, lambda cp: cp.wait())
    for s in range(nsub):
        r0, r1 = s * MOD_TILE, (s + 1) * MOD_TILE
        info = info_ref[r0:r1, :]
        mix = (info[:, 2:3] * _load_tokens(buf_ref.at[slot, 0], r0, MOD_TILE)
               + info[:, 3:4] * _load_tokens(buf_ref.at[slot, 1], r0, MOD_TILE))
        out = h_ref[r0:r1, :] + modt_ref[s, 5:6, :] * mix
        if latent_only:
            obuf_ref[slot, r0:r1, :] = out
        else:
            o_ref[r0:r1, :] = out
    if latent_only:
        latent_copies(i, slot, lambda cp: cp.start())

        @pl.when(i == n - 1)
        def _():
            latent_copies(i, slot, lambda cp: cp.wait())

            @pl.when(n >= 2)
            def _():
                latent_copies(i - 1, 1 - slot, lambda cp: cp.wait())


def _combine(y_rows, dest_flat, h, info, modt, tiles_per_batch):
    m, d = h.shape
    nsub = ROUTE_TM // MOD_TILE
    scratch = [pltpu.SMEM((TOP_K * ROUTE_TM,), jnp.int32), pltpu.VMEM((2, TOP_K, ROUTE_TM * ROW_SEG, LANE), F32),
               pltpu.SemaphoreType.DMA(()), pltpu.SemaphoreType.DMA((2,))]
    if tiles_per_batch is None:
        out_spec, out_rows = pl.BlockSpec((ROUTE_TM, d), lambda i: (i, 0)), m
    else:
        out_spec, out_rows = pl.BlockSpec(memory_space=pl.ANY), m - m // tiles_per_batch
        scratch += [pltpu.VMEM((2, ROUTE_TM, d), F32), pltpu.SemaphoreType.DMA((2,))]
    return pl.pallas_call(
        functools.partial(_combine_kernel, nsub=nsub, tiles_per_batch=tiles_per_batch),
        grid=(m // ROUTE_TM,),
        in_specs=[pl.BlockSpec(memory_space=pl.ANY), pl.BlockSpec(memory_space=pl.ANY),
                  pl.BlockSpec((ROUTE_TM, d), lambda i: (i, 0)),
                  pl.BlockSpec((ROUTE_TM, INFO_W), lambda i: (i, 0)),
                  pl.BlockSpec((nsub, 6, d), lambda i: (i, 0, 0))],
        out_specs=out_spec,
        out_shape=jax.ShapeDtypeStruct((out_rows, d), F32),
        scratch_shapes=scratch,
        compiler_params=_params(("arbitrary",)),
        name="moe_combine",
    )(dest_flat, y_rows, h, info, modt)


def _moe(f8, h, w_router, w1, w3, w2, modt, tiles_per_batch=None):
    m, d = h.shape
    info, cnt = _route(f8, w_router)
    e = info[:, 0:TOP_K].astype(jnp.int32)
    pos = info[:, 4:4 + TOP_K].astype(jnp.int32)
    counts = cnt[0, :N_EXPERTS].astype(jnp.int32)
    padded = (counts + MOE_TB - 1) // MOE_TB * MOE_TB
    pad_end = jnp.cumsum(padded)
    pad_start = pad_end - padded
    dest_flat = (pad_start[e] + pos).reshape(-1)
    n_rows = m * TOP_K + N_EXPERTS * MOE_TB
    nb = n_rows // MOE_TB
    block_e = jnp.minimum(jnp.searchsorted(pad_end, jnp.arange(nb, dtype=jnp.int32) * MOE_TB, side='right'),
                          N_EXPERTS - 1).astype(jnp.int32)
    n_used = (pad_end[-1:] // MOE_TB).astype(jnp.int32)
    pad_info = jnp.concatenate([pad_start + counts, padded - counts]).astype(jnp.int32)
    rows = _dispatch(f8, dest_flat, pad_info, n_rows)
    y_rows = _gffn(rows, block_e, n_used, w1.astype(BF16), w3.astype(BF16), w2.astype(BF16))
    return _combine(y_rows, dest_flat, h, info, modt, tiles_per_batch)


def _tile_mod(mod_l, bsz, tiles_per_batch):
    t = jnp.arange(bsz * tiles_per_batch)
    idx = jnp.where(t % tiles_per_batch == 0, bsz, t // tiles_per_batch)
    return jnp.transpose(mod_l[:, idx, :], (1, 0, 2))


def kernel(x, c, ctx, c_ctx, w_mod, b_mod, norm1_w, norm2_w, w_in, conv_w, conv_b, dt_bias, a_log, d_skip,
           ssd_norm_w, q_norm_w, k_norm_w, rpb, w_br_ssd, w_br_na, w_out, w_ff1, w_ff3, w_ff2, w_router,
           w_e1, w_e3, w_e2):
    bsz, seqlen, d = x.shape
    clen = ctx.shape[1]
    n_layers = w_mod.shape[0]
    assert clen == MOD_TILE and seqlen % MOD_TILE == 0 and bsz < 16
    t = clen + seqlen
    m = bsz * t
    tiles_per_batch = t // MOD_TILE

    cond = jnp.zeros((16, d), F32).at[:bsz].set(c).at[bsz].set(c_ctx)
    mod = _modulation(cond, w_mod, b_mod)
    modt = [_tile_mod(mod[i], bsz, tiles_per_batch) for i in range(n_layers)]

    h = jnp.concatenate([ctx, x], axis=1).reshape(m, d)
    a = _adaln(h, norm1_w[0], modt[0])
    for i in range(n_layers):
        last = i == n_layers - 1
        wi = w_in[i]
        w_main = jnp.concatenate([wi[:, :IN_SPLITS[3]], _qk_reorder(wi[:, IN_SPLITS[4]:IN_SPLITS[5]]),
                                  _qk_reorder(wi[:, IN_SPLITS[5]:IN_SPLITS[6]]), wi[:, IN_SPLITS[6]:]],
                                 axis=1).astype(BF16)
        w_dt = jnp.pad(wi[:, IN_SPLITS[3]:IN_SPLITS[4]], ((0, 0), (0, DT_PAD - 2 * SSD_HEADS))).astype(BF16)
        p_main = _matmul(a, w_main, BF16, 512, 2560, "inproj")
        p_dt = _matmul(a, w_dt, F32, 1024, DT_PAD, "inproj_dt")
        pm3 = p_main.reshape(bsz, t, MAIN_DIM)
        yf, yb = _ssd(pm3, p_dt.reshape(bsz, t, DT_PAD), conv_w[i], conv_b[i], dt_bias[i], a_log[i],
                      d_skip[i], clen)
        y_na = _na(pm3, q_norm_w[i], k_norm_w[i], rpb[i], clen).reshape(m, d)
        fi = i // 2
        moe_layer = i % 2 == 1
        h, f = _merge(p_main, yf.reshape(m, d), yb.reshape(m, d), y_na, h, w_br_ssd[i].astype(BF16),
                      w_br_na[i].astype(BF16), w_out[i].astype(BF16), ssd_norm_w[i], norm2_w[i], modt[i],
                      row_tile_out=moe_layer)
        nxt = min(i + 1, n_layers - 1)
        if not moe_layer:
            h, a = _ffn(f, h, w_ff1[fi].astype(BF16), w_ff3[fi].astype(BF16), w_ff2[fi].astype(BF16),
                        norm1_w[nxt], modt[i], modt[nxt])
        else:
            if last:
                return _moe(f, h, w_router[fi], w_e1[fi], w_e3[fi], w_e2[fi], modt[i],
                            tiles_per_batch).reshape(bsz, seqlen, d)
            h = _moe(f, h, w_router[fi], w_e1[fi], w_e3[fi], w_e2[fi], modt[i])
            a = _adaln(h, norm1_w[nxt], modt[nxt])
    return h.reshape(bsz, t, d)[:, clen:]
```

```python
import functools
import math

import jax
import jax.numpy as jnp
from jax import lax
from jax.experimental import pallas as pl
from jax.experimental.pallas import tpu as pltpu

D_MODEL = 1024
GRID_W = 64
SSD_HEAD_DIM = 64
SSD_HEADS = D_MODEL // SSD_HEAD_DIM
D_SSD = SSD_HEADS * SSD_HEAD_DIM
SSD_GROUPS = 2
SSD_HPG = SSD_HEADS // SSD_GROUPS
SSD_STATE = 128
SSD_CONV = 5
SSD_CHUNK = 128
XBC_DIM = D_SSD + 2 * SSD_GROUPS * SSD_STATE
NA_HEAD_DIM = 64
NA_HEADS = D_MODEL // NA_HEAD_DIM
D_NA = NA_HEADS * NA_HEAD_DIM
NA_ROWS_MAX = 8
NA_COLS = 16
ROPE_BASE = 10000.0
N_EXPERTS = 8
TOP_K = 2
EPS = 1e-6
NEG_INF = -1e30
IN_SPLITS = (D_MODEL, 2 * D_MODEL, 2 * D_MODEL + D_SSD, 2 * D_MODEL + D_SSD + XBC_DIM,
             2 * D_MODEL + D_SSD + XBC_DIM + 2 * SSD_HEADS,
             2 * D_MODEL + D_SSD + XBC_DIM + 2 * SSD_HEADS + D_NA,
             2 * D_MODEL + D_SSD + XBC_DIM + 2 * SSD_HEADS + 2 * D_NA)

F32 = jnp.float32
BF16 = jnp.bfloat16
HIGHEST = lax.Precision.HIGHEST

MOD_TILE = 256
LANE = 128
VMEM_LIMIT = 48 * 1024 * 1024
COL_GS, COL_GN, COL_Z, COL_XBC, COL_Q, COL_K, COL_V = 0, 1024, 2048, 3072, 4608, 5632, 6656
MAIN_DIM = 7680
DT_PAD = LANE


def _params(sem):
    return pltpu.CompilerParams(dimension_semantics=sem, vmem_limit_bytes=VMEM_LIMIT)


def _rms_mod(x, nw, shift, scale):
    ms = jnp.mean(x * x, axis=-1, keepdims=True)
    return (x * lax.rsqrt(ms + EPS) * nw) * (1.0 + scale) + shift


def _mod_kernel(cond_ref, w_ref, b_ref, o_ref):
    c = cond_ref[...]
    s = c * jax.nn.sigmoid(c)
    o_ref[...] = jnp.dot(s, w_ref[...], precision=HIGHEST, preferred_element_type=F32) + b_ref[...]


def _modulation(cond, w_mod, b_mod):
    n_layers, d, _ = w_mod.shape
    rows = cond.shape[0]
    b4 = b_mod.reshape(n_layers, 6, 1, d)
    return pl.pallas_call(
        _mod_kernel,
        grid=(n_layers, 6),
        in_specs=[pl.BlockSpec((rows, d), lambda l, k: (0, 0)),
                  pl.BlockSpec((None, d, d), lambda l, k: (l, 0, k)),
                  pl.BlockSpec((None, None, 1, d), lambda l, k: (l, k, 0, 0))],
        out_specs=pl.BlockSpec((None, None, rows, d), lambda l, k: (l, k, 0, 0)),
        out_shape=jax.ShapeDtypeStruct((n_layers, 6, rows, d), F32),
        compiler_params=_params(("arbitrary", "arbitrary")),
        name="modulation",
    )(cond, w_mod, b4)


def _adaln_kernel(h_ref, nw_ref, modt_ref, o_ref, *, nsub):
    nw = nw_ref[...]
    for s in range(nsub):
        rows = pl.ds(s * MOD_TILE, MOD_TILE)
        x = h_ref[rows, :]
        o_ref[rows, :] = _rms_mod(x, nw, modt_ref[s, 0:1, :], modt_ref[s, 1:2, :]).astype(o_ref.dtype)


def _adaln(h, nw, modt, tm=1024):
    m, d = h.shape
    nsub = tm // MOD_TILE
    return pl.pallas_call(
        functools.partial(_adaln_kernel, nsub=nsub),
        grid=(m // tm,),
        in_specs=[pl.BlockSpec((tm, d), lambda i: (i, 0)),
                  pl.BlockSpec((1, d), lambda i: (0, 0)),
                  pl.BlockSpec((nsub, 6, d), lambda i: (i, 0, 0))],
        out_specs=pl.BlockSpec((tm, d), lambda i: (i, 0)),
        out_shape=jax.ShapeDtypeStruct((m, d), BF16),
        compiler_params=_params(("arbitrary",)),
        name="adaln",
    )(h, nw.reshape(1, d), modt)


def _matmul_kernel(a_ref, w_ref, o_ref):
    o_ref[...] = jnp.dot(a_ref[...], w_ref[...], preferred_element_type=F32).astype(o_ref.dtype)


def _matmul(a, w, out_dtype, tm, tn, name):
    m, k = a.shape
    n = w.shape[1]
    return pl.pallas_call(
        _matmul_kernel,
        grid=(n // tn, m // tm),
        in_specs=[pl.BlockSpec((tm, k), lambda j, i: (i, 0)),
                  pl.BlockSpec((k, tn), lambda j, i: (0, j))],
        out_specs=pl.BlockSpec((tm, tn), lambda j, i: (i, j)),
        out_shape=jax.ShapeDtypeStruct((m, n), out_dtype),
        compiler_params=_params(("arbitrary", "arbitrary")),
        name=name,
    )(a, w)


def _merge_kernel(gs_ref, gn_ref, z_ref, yf_ref, yb_ref, yn_ref, h_ref, wbs_ref, wbn_ref, wo_ref, snw_ref,
                  nw_ref, modt_ref, ho_ref, f_ref, *, nsub, row_tile_out):
    gw = D_SSD // SSD_GROUPS
    nw = nw_ref[...]
    sub = MOD_TILE
    n_chain = nsub
    ys, mixed = {}, {}

    def gated_norm(c):
        rows = slice(c * sub, (c + 1) * sub)
        z = z_ref[rows, :].astype(F32)
        u = (yf_ref[rows, :].astype(F32) + yb_ref[rows, :].astype(F32)) * (z * jax.nn.sigmoid(z))
        parts = []
        for g in range(SSD_GROUPS):
            ug = u[:, g * gw:(g + 1) * gw]
            ms = jnp.mean(ug * ug, axis=-1, keepdims=True)
            parts.append((ug * lax.rsqrt(ms + EPS) * snw_ref[:, g * gw:(g + 1) * gw]).astype(BF16))
        ys[c] = jnp.concatenate(parts, axis=1)

    def branches(c):
        rows = slice(c * sub, (c + 1) * sub)
        a = jnp.dot(ys.pop(c), wbs_ref[...], preferred_element_type=F32)
        b = jnp.dot(yn_ref[rows, :], wbn_ref[...], preferred_element_type=F32)
        mixed[c] = (jax.nn.sigmoid(gs_ref[rows, :].astype(F32)) * a
                    + jax.nn.sigmoid(gn_ref[rows, :].astype(F32)) * b).astype(BF16)

    def project(c):
        r0, s = c * sub, c
        m = jnp.dot(mixed.pop(c), wo_ref[...], preferred_element_type=F32)
        h = h_ref[r0:r0 + sub, :] + modt_ref[s, 2:3, :] * m
        ho_ref[r0:r0 + sub, :] = h
        f = _rms_mod(h, nw, modt_ref[s, 3:4, :], modt_ref[s, 4:5, :])
        if row_tile_out:
            _store_tokens(f_ref, r0, f)
        else:
            f_ref[r0:r0 + sub, :] = f.astype(f_ref.dtype)

    for k in range(n_chain + 2):
        if k < n_chain:
            gated_norm(k)
        if 0 <= k - 1 < n_chain:
            branches(k - 1)
        if 0 <= k - 2 < n_chain:
            project(k - 2)


def _merge(p_main, yf, yb, y_na, h, wbs, wbn, wo, ssd_nw, nw2, modt, row_tile_out, tm=512):
    m, d = h.shape
    nsub = tm // MOD_TILE
    row = lambda i: (i, 0)
    const = lambda i: (0, 0)
    if row_tile_out:
        f_spec, f_shape = pl.BlockSpec((tm * ROW_SEG, LANE), row), jax.ShapeDtypeStruct((m * ROW_SEG, LANE), F32)
    else:
        f_spec, f_shape = pl.BlockSpec((tm, d), row), jax.ShapeDtypeStruct((m, d), BF16)
    return pl.pallas_call(
        functools.partial(_merge_kernel, nsub=nsub, row_tile_out=row_tile_out),
        grid=(m // tm,),
        in_specs=[pl.BlockSpec((tm, d), lambda i: (i, COL_GS // D_MODEL)),
                  pl.BlockSpec((tm, d), lambda i: (i, COL_GN // D_MODEL)),
                  pl.BlockSpec((tm, d), lambda i: (i, COL_Z // D_MODEL)),
                  pl.BlockSpec((tm, d), row), pl.BlockSpec((tm, d), row), pl.BlockSpec((tm, d), row),
                  pl.BlockSpec((tm, d), row),
                  pl.BlockSpec((d, d), const), pl.BlockSpec((d, d), const), pl.BlockSpec((d, d), const),
                  pl.BlockSpec((1, d), const), pl.BlockSpec((1, d), const),
                  pl.BlockSpec((nsub, 6, d), lambda i: (i, 0, 0))],
        out_specs=[pl.BlockSpec((tm, d), row), f_spec],
        out_shape=[jax.ShapeDtypeStruct((m, d), F32), f_shape],
        compiler_params=_params(("arbitrary",)),
        name="merge",
    )(p_main, p_main, p_main, yf, yb, y_na, h, wbs, wbn, wo, ssd_nw.astype(F32).reshape(1, d), nw2.reshape(1, d),
      modt)


def _ffn_kernel(x_ref, h_ref, w1_ref, w3_ref, w2_ref, nw_ref, modt_ref, modn_ref, ho_ref, an_ref, *, nsub):
    x = x_ref[...]
    h1 = jnp.dot(x, w1_ref[...], preferred_element_type=F32)
    h3 = jnp.dot(x, w3_ref[...], preferred_element_type=F32)
    act = (h1 * jax.nn.sigmoid(h1) * h3).astype(BF16)
    out = jnp.dot(act, w2_ref[...], preferred_element_type=F32)
    nw = nw_ref[...]
    for s in range(nsub):
        r0, r1 = s * MOD_TILE, (s + 1) * MOD_TILE
        h = h_ref[r0:r1, :] + modt_ref[s, 5:6, :] * out[r0:r1, :]
        ho_ref[r0:r1, :] = h
        an_ref[r0:r1, :] = _rms_mod(h, nw, modn_ref[s, 0:1, :], modn_ref[s, 1:2, :]).astype(an_ref.dtype)


def _ffn(x, h, w1, w3, w2, nw_next, modt, modt_next, tm=512):
    m, d = h.shape
    ff = w1.shape[1]
    nsub = tm // MOD_TILE
    row = lambda i: (i, 0)
    const = lambda i: (0, 0)
    once = pl.Buffered(1)
    return pl.pallas_call(
        functools.partial(_ffn_kernel, nsub=nsub),
        grid=(m // tm,),
        in_specs=[pl.BlockSpec((tm, d), row), pl.BlockSpec((tm, d), row),
                  pl.BlockSpec((d, ff), const, pipeline_mode=once),
                  pl.BlockSpec((d, ff), const, pipeline_mode=once),
                  pl.BlockSpec((ff, d), const, pipeline_mode=once),
                  pl.BlockSpec((1, d), const),
                  pl.BlockSpec((nsub, 6, d), lambda i: (i, 0, 0)),
                  pl.BlockSpec((nsub, 6, d), lambda i: (i, 0, 0))],
        out_specs=[pl.BlockSpec((tm, d), row), pl.BlockSpec((tm, d), row)],
        out_shape=[jax.ShapeDtypeStruct((m, d), F32), jax.ShapeDtypeStruct((m, d), BF16)],
        compiler_params=_params(("arbitrary",)),
        name="ffn",
    )(x, h, w1, w3, w2, nw_next.reshape(1, d), modt, modt_next)


NA_BAND = NA_ROWS_MAX * GRID_W
HEAD_PAIR = LANE // NA_HEAD_DIM
NA_UNROLL = 8
NA_PREP_UNROLL = 8


def _head_mean_matrix():
    ri = lax.broadcasted_iota(jnp.int32, (2 * LANE, LANE), 0) % LANE
    ci = lax.broadcasted_iota(jnp.int32, (2 * LANE, LANE), 1)
    half = NA_HEAD_DIM // 2
    same_head = (ri % NA_HEAD_DIM) // half == (ci % NA_HEAD_DIM) // half
    return jnp.where(same_head, 1.0 / NA_HEAD_DIM, 0.0).astype(BF16)


def _head_rms(x, w, mean_mat):
    sq = x * x
    hi = sq.astype(BF16)
    lo = (sq - hi.astype(F32)).astype(BF16)
    ms = jnp.dot(jnp.concatenate([hi, lo], axis=1), mean_mat, preferred_element_type=F32)
    return x * lax.rsqrt(ms + EPS) * w


def _qk_lane_order():
    quarter = NA_HEAD_DIM // 4
    order = []
    for half in range(2):
        for head in range(HEAD_PAIR):
            for part in range(2):
                start = head * NA_HEAD_DIM + part * 2 * quarter + half * quarter
                order += list(range(start, start + quarter))
    return order


def _qk_reorder(w):
    k, n = w.shape
    return w.reshape(k, n // LANE, LANE)[:, :, jnp.asarray(_qk_lane_order())].reshape(k, n)


def _rope(x, cos, sin):
    return x * cos + pltpu.roll(x, LANE // 2, axis=1) * sin


def _stack_heads(x, lane_lo):
    zero = jnp.zeros_like(x)
    return jnp.concatenate([jnp.where(lane_lo, x, zero), jnp.where(lane_lo, zero, x)], axis=0)


def _attend(qs, k_list, v_list, bias_list, lane_lo):
    nt = (((1,), (1,)), ((), ()))
    s_list = []
    for k, bias in zip(k_list, bias_list):
        s = lax.dot_general(qs, k, nt, preferred_element_type=F32)
        s_list.append(s if bias is None else s + bias)
    mx = s_list[0].max(axis=-1, keepdims=True)
    for s in s_list[1:]:
        mx = jnp.maximum(mx, s.max(axis=-1, keepdims=True))
    den = None
    acc = None
    for s, v in zip(s_list, v_list):
        p = jnp.exp(s - mx)
        d = p.sum(axis=-1, keepdims=True)
        o = jnp.dot(p.astype(BF16), v, preferred_element_type=F32)
        den = d if den is None else den + d
        acc = o if acc is None else acc + o
    acc = acc / den
    n = acc.shape[0] // 2
    return jnp.where(lane_lo, acc[:n], acc[n:])


def _na_kernel(q_ref, k_ref, v_ref, bias_ref, rowc_ref, rows_ref, colc_ref, cols_ref, qw_ref, kw_ref,
               o_ref, qn_ref, kn_ref, s_ref, m_ref, *, clen, n_rows):
    lane = lax.broadcasted_iota(jnp.int32, (1, LANE), 1)
    lane_lo = lane < NA_HEAD_DIM
    qk_h0 = (lane % NA_HEAD_DIM) < NA_HEAD_DIM // 2
    qw = qw_ref[...] * (NA_HEAD_DIM ** -0.5)
    kw = kw_ref[...]
    kr = NA_ROWS_MAX

    mean_mat = _head_mean_matrix()
    qn_ref[0:clen, :] = _head_rms(q_ref[0:clen, :].astype(F32), qw, mean_mat).astype(BF16)
    kn_ref[0:clen, :] = _head_rms(k_ref[0:clen, :].astype(F32), kw, mean_mat).astype(BF16)
    colc = colc_ref[...]
    cols = cols_ref[...]

    def prep(r, carry):
        rows = pl.ds(pl.multiple_of(clen + r * GRID_W, GRID_W), GRID_W)
        cos = rowc_ref[pl.ds(r, 1), :] + colc
        sin = rows_ref[pl.ds(r, 1), :] + cols
        q = _head_rms(q_ref[rows, :].astype(F32), qw, mean_mat)
        k = _head_rms(k_ref[rows, :].astype(F32), kw, mean_mat)
        qn_ref[rows, :] = _rope(q, cos, sin).astype(BF16)
        kn_ref[rows, :] = _rope(k, cos, sin).astype(BF16)
        return carry

    lax.fori_loop(0, n_rows, prep, 0, unroll=NA_PREP_UNROLL)

    kc = kn_ref[0:clen, :]
    vc = v_ref[0:clen, :]
    for blk in range(clen // GRID_W):
        rows = slice(blk * GRID_W, (blk + 1) * GRID_W)
        qs = _stack_heads(qn_ref[rows, :], qk_h0)
        o_ref[rows, :] = _attend(qs, [kc], [vc], [None], lane_lo).astype(o_ref.dtype)

    nt = (((1,), (1,)), ((), ()))
    n_blk = (NA_BAND + clen) // LANE

    def band_of(r):
        r0 = jnp.clip(r - kr // 2, 0, n_rows - kr)
        return r0, pl.ds(pl.multiple_of(clen + r0 * GRID_W, GRID_W), NA_BAND)

    n_win = NA_BAND // LANE
    sq = HEAD_PAIR * GRID_W

    def scores(rows_r, slots):
        qs = [_stack_heads(qn_ref[pl.ds(pl.multiple_of(clen + r * GRID_W, GRID_W), GRID_W), :], qk_h0)
              for r in rows_r]
        s_ctx = lax.dot_general(jnp.concatenate(qs, axis=0), kn_ref[0:clen, :], nt, preferred_element_type=F32)
        for u, (r, slot) in enumerate(zip(rows_r, slots)):
            r0, band = band_of(r)
            s_win = lax.dot_general(qs[u], kn_ref[band, :], nt, preferred_element_type=F32)
            bias = bias_ref[r0 - r + (NA_ROWS_MAX - 1)]
            blocks = [s_win[:, j * LANE:(j + 1) * LANE] + bias[:, j * LANE:(j + 1) * LANE] for j in range(n_win)]
            blocks += [s_ctx[u * sq:(u + 1) * sq, j * LANE:(j + 1) * LANE] for j in range(n_blk - n_win)]
            mx = blocks[0]
            for j in range(n_blk):
                s_ref[slot, :, j * LANE:(j + 1) * LANE] = blocks[j]
                mx = jnp.maximum(mx, blocks[j])
            m_ref[slot] = jnp.broadcast_to(mx.max(axis=-1, keepdims=True), mx.shape)

    def output(rows_r, slots):
        accs, dens, p_ctx = [], [], []
        for r, slot in zip(rows_r, slots):
            _, band = band_of(r)
            mx = m_ref[slot]
            p = [jnp.exp(s_ref[slot, :, j * LANE:(j + 1) * LANE] - mx) for j in range(n_blk)]
            den = p[0]
            for pj in p[1:]:
                den = den + pj
            dens.append(den.sum(axis=-1, keepdims=True))
            pb = [pj.astype(BF16) for pj in p]
            accs.append(jnp.dot(jnp.concatenate(pb[:n_win], axis=1), v_ref[band, :], preferred_element_type=F32))
            p_ctx.append(jnp.concatenate(pb[n_win:], axis=1))
        acc_ctx = jnp.dot(jnp.concatenate(p_ctx, axis=0), v_ref[0:clen, :], preferred_element_type=F32)
        for u, r in enumerate(rows_r):
            acc = (accs[u] + acc_ctx[u * sq:(u + 1) * sq]) / dens[u]
            rows = pl.ds(pl.multiple_of(clen + r * GRID_W, GRID_W), GRID_W)
            o_ref[rows, :] = jnp.where(lane_lo, acc[:GRID_W], acc[GRID_W:]).astype(o_ref.dtype)

    scores(list(range(NA_UNROLL)), list(range(NA_UNROLL)))

    def body(j, carry):
        r = 2 * NA_UNROLL * j
        for half in range(2):
            output([r + half * NA_UNROLL + u for u in range(NA_UNROLL)],
                   [half * NA_UNROLL + u for u in range(NA_UNROLL)])
            scores([jnp.minimum(r + (half + 1) * NA_UNROLL + u, n_rows - 1) for u in range(NA_UNROLL)],
                   [(1 - half) * NA_UNROLL + u for u in range(NA_UNROLL)])
        return carry

    lax.fori_loop(0, n_rows // (2 * NA_UNROLL), body, 0)


def _na_tables(rpb, n_rows):
    col = jnp.arange(GRID_W)
    col_start = jnp.clip(col - NA_COLS // 2, 0, GRID_W - NA_COLS)
    col_ok = (col[None, :] >= col_start[:, None]) & (col[None, :] < col_start[:, None] + NA_COLS)
    col_idx = jnp.clip(col[None, :] - col[:, None], -(NA_COLS - 1), NA_COLS - 1) + NA_COLS - 1
    rpb_c = jnp.where(col_ok[None, None], rpb.astype(F32)[:, :, col_idx], NEG_INF)
    dr = jnp.arange(NA_ROWS_MAX)[:, None] + jnp.arange(NA_ROWS_MAX)[None, :]
    tab = rpb_c[:, dr]
    tab = jnp.transpose(tab, (0, 1, 3, 2, 4)).reshape(NA_HEADS, NA_ROWS_MAX, GRID_W, NA_BAND)
    tab = tab.reshape(NA_HEADS // HEAD_PAIR, HEAD_PAIR, NA_ROWS_MAX, GRID_W, NA_BAND)
    tab = jnp.transpose(tab, (0, 2, 1, 3, 4)).reshape(NA_HEADS // HEAD_PAIR, NA_ROWS_MAX,
                                                       HEAD_PAIR * GRID_W, NA_BAND)
    quarter = NA_HEAD_DIM // 4
    inv = ROPE_BASE ** (-jnp.arange(quarter, dtype=F32) / quarter)

    def tables(pos, lo):
        ang = pos[:, None] * inv[None, :]
        z = jnp.zeros_like(ang)
        c, s = jnp.cos(ang), jnp.sin(ang)
        ch = jnp.concatenate([c, c, z, z] if lo else [z, z, c, c], axis=-1)
        sh = jnp.concatenate([-s, s, z, z] if lo else [z, z, -s, s], axis=-1)
        order = jnp.asarray(_qk_lane_order())
        return jnp.tile(ch, (1, HEAD_PAIR))[:, order], jnp.tile(sh, (1, HEAD_PAIR))[:, order]

    rowc, rows = tables(jnp.arange(n_rows).astype(F32), True)
    colc, cols = tables(jnp.arange(GRID_W).astype(F32), False)
    return tab, rowc, rows, colc, cols


def _na(pm3, q_norm_w, k_norm_w, rpb, clen):
    bsz, t, _ = pm3.shape
    n_rows = (t - clen) // GRID_W
    tab, rowc, rows, colc, cols = _na_tables(rpb, n_rows)
    order = jnp.asarray(_qk_lane_order())
    qw = jnp.tile(q_norm_w.astype(F32), HEAD_PAIR)[order].reshape(1, LANE)
    kw = jnp.tile(k_norm_w.astype(F32), HEAD_PAIR)[order].reshape(1, LANE)
    n_pairs = NA_HEADS // HEAD_PAIR
    const = lambda p, b: (0, 0)
    tok = lambda c0: pl.BlockSpec((None, t, LANE), lambda p, b: (b, 0, c0 // LANE + p))
    return pl.pallas_call(
        functools.partial(_na_kernel, clen=clen, n_rows=n_rows),
        grid=(n_pairs, bsz),
        in_specs=[tok(COL_Q), tok(COL_K), tok(COL_V),
                  pl.BlockSpec((None, NA_ROWS_MAX, HEAD_PAIR * GRID_W, NA_BAND), lambda p, b: (p, 0, 0, 0)),
                  pl.BlockSpec((n_rows, LANE), const), pl.BlockSpec((n_rows, LANE), const),
                  pl.BlockSpec((GRID_W, LANE), const), pl.BlockSpec((GRID_W, LANE), const),
                  pl.BlockSpec((1, LANE), const), pl.BlockSpec((1, LANE), const)],
        out_specs=pl.BlockSpec((None, t, LANE), lambda p, b: (b, 0, p)),
        out_shape=jax.ShapeDtypeStruct((bsz, t, D_NA), BF16),
        scratch_shapes=[pltpu.VMEM((t, LANE), BF16), pltpu.VMEM((t, LANE), BF16),
                        pltpu.VMEM((2 * NA_UNROLL, HEAD_PAIR * GRID_W, NA_BAND + clen), F32),
                        pltpu.VMEM((2 * NA_UNROLL, HEAD_PAIR * GRID_W, LANE), F32)],
        compiler_params=_params(("arbitrary", "arbitrary")),
        name="natten",
    )(pm3, pm3, pm3, tab, rowc, rows, colc, cols, qw, kw)


HALO = 16
CONV_ROWS = 2 * SSD_CHUNK
CONV_PAD = SSD_CONV // 2
BC_DIM = SSD_GROUPS * SSD_STATE
PAIRS_PER_GROUP = SSD_HPG // HEAD_PAIR


def _ssd_chunk_index(s, rev, cc, nc):
    if not rev:
        return s
    return jnp.where(s < cc, cc - 1 - s, nc - 1 - (s - cc))


def _ssd_conv_kernel(main_ref, prev_ref, next_ref, cw_ref, cb_ref, o_ref, *, cc, nc):
    q = SSD_CHUNK
    n_piece = main_ref.shape[0] // q
    cidx = pl.program_id(1)
    ti = lax.broadcasted_iota(jnp.int32, (q, q + 2 * HALO), 0)
    ji = lax.broadcasted_iota(jnp.int32, (q, q + 2 * HALO), 1)
    taps = [k for k in range(SSD_CONV) if k != CONV_PAD]
    exts, convs = [], []
    for p in range(n_piece):
        before = prev_ref[...] if p == 0 else main_ref[p * q - HALO:p * q, :]
        after = next_ref[...] if p == n_piece - 1 else main_ref[(p + 1) * q:(p + 1) * q + HALO, :]
        exts.append(jnp.concatenate([before, main_ref[p * q:(p + 1) * q, :], after], axis=0))
        convs.append(cb_ref[...] + cw_ref[CONV_PAD:CONV_PAD + 1, :] * main_ref[p * q:(p + 1) * q, :].astype(F32))
    for k in taps:
        for p in range(n_piece):
            first_col = jnp.where((cidx == 0) | (cidx == cc), HALO, 0) if p == 0 else 0
            end_col = (jnp.where((cidx == cc - 1) | (cidx == nc - 1), HALO + q, q + 2 * HALO)
                       if p == n_piece - 1 else q + 2 * HALO)
            in_segment = (ji >= first_col) & (ji < end_col)
            shift = jnp.where((ji == ti + (HALO + k - CONV_PAD)) & in_segment, 1.0, 0.0).astype(BF16)
            convs[p] = convs[p] + cw_ref[k:k + 1, :] * jnp.dot(shift, exts[p], preferred_element_type=F32)
    for p in range(n_piece):
        o_ref[p * q:(p + 1) * q, :] = (convs[p] * jax.nn.sigmoid(convs[p])).astype(o_ref.dtype)


def _ssd_decay(dt_ref, dtb_ref, a_ref, *, rev):
    q = SSD_CHUNK
    x = dt_ref[...] + dtb_ref[...]
    dt = jnp.maximum(x, 0.0) + jnp.log(1.0 + jnp.exp(-jnp.abs(x)))
    la = dt * a_ref[...]
    ri = lax.broadcasted_iota(jnp.int32, (q, q), 0)
    ci = lax.broadcasted_iota(jnp.int32, (q, q), 1)
    allowed = (ci >= ri) if rev else (ci <= ri)
    tri = jnp.where(allowed, 1.0, 0.0).astype(F32)
    a_cs = jnp.dot(tri, la, precision=HIGHEST, preferred_element_type=F32)
    last = 0 if rev else q - 1
    a_cs_t = a_cs.T
    src_t = a_cs_t - jnp.log(dt.T)
    wdt_t = jnp.exp(a_cs_t[:, last:last + 1] - src_t)
    neg_mask = jnp.where(allowed, 0.0, NEG_INF)
    return a_cs, src_t, wdt_t, neg_mask


def _ssd_group(xbc, h_ref, yoff_ref, g):
    nt = (((1,), (1,)), ((), ()))
    b_g = xbc[:, D_SSD + g * SSD_STATE:D_SSD + (g + 1) * SSD_STATE]
    c_g = xbc[:, D_SSD + BC_DIM + g * SSD_STATE:D_SSD + BC_DIM + (g + 1) * SSD_STATE]
    cols = slice(g * SSD_HPG * SSD_HEAD_DIM, (g + 1) * SSD_HPG * SSD_HEAD_DIM)
    yoff_ref[:, cols] = jnp.dot(c_g, h_ref[:, cols].astype(BF16), preferred_element_type=F32)
    cb = lax.dot_general(c_g, b_g, nt, preferred_element_type=F32)
    return cb, b_g.astype(F32).T


def _ssd_pair(xbc, decay, cb, b_t, h_ref, yoff_ref, y_ref, pair, *, rev):
    q = SSD_CHUNK
    a_cs, src_t, wdt_t, neg_mask = decay
    hb = SSD_HEADS if rev else 0
    last = 0 if rev else q - 1
    lane_lo = lax.broadcasted_iota(jnp.int32, (1, LANE), 1) < SSD_HEAD_DIM
    lanes = slice(pair * LANE, (pair + 1) * LANE)
    lhs, ecol = [], []
    for hh in range(HEAD_PAIR):
        head = hb + pair * HEAD_PAIR + hh
        a_col = jnp.broadcast_to(a_cs[:, head:head + 1], (q, LANE))
        lhs.append((cb * jnp.exp((a_col - src_t[head:head + 1, :]) + neg_mask)).astype(BF16))
        ecol.append(jnp.exp(a_col))
    for hh in range(HEAD_PAIR):
        head = hb + pair * HEAD_PAIR + hh
        lhs.append((b_t * wdt_t[head:head + 1, :]).astype(BF16))
    res = jnp.dot(jnp.concatenate(lhs, axis=0), xbc[:, lanes], preferred_element_type=F32)
    e_pair = jnp.where(lane_lo, ecol[0], ecol[1])
    y_ref[:, lanes] = jnp.where(lane_lo, res[0:q], res[q:2 * q]) + yoff_ref[:, lanes] * e_pair
    h_ref[:, lanes] = (h_ref[:, lanes] * e_pair[last:last + 1, :]
                       + jnp.where(lane_lo, res[2 * q:3 * q], res[3 * q:4 * q]))


def _ssd_scan_kernel(xf_ref, dtf_ref, xb_ref, dtb_ref, bias_ref, a_ref, dsum_ref, of_ref, ob_ref,
                     hf_ref, yofff_ref, hb_ref, yoffb_ref, yf_ref, yb_ref):
    @pl.when(pl.program_id(1) == 0)
    def _():
        hf_ref[...] = jnp.zeros_like(hf_ref)
        hb_ref[...] = jnp.zeros_like(hb_ref)

    decay_f = _ssd_decay(dtf_ref, bias_ref, a_ref, rev=False)
    decay_b = _ssd_decay(dtb_ref, bias_ref, a_ref, rev=True)
    xf, xb = xf_ref[...], xb_ref[...]
    for g in range(SSD_GROUPS):
        cb_f, bt_f = _ssd_group(xf, hf_ref, yofff_ref, g)
        cb_b, bt_b = _ssd_group(xb, hb_ref, yoffb_ref, g)
        for pj in range(PAIRS_PER_GROUP):
            pair = g * PAIRS_PER_GROUP + pj
            _ssd_pair(xf, decay_f, cb_f, bt_f, hf_ref, yofff_ref, yf_ref, pair, rev=False)
            _ssd_pair(xb, decay_b, cb_b, bt_b, hb_ref, yoffb_ref, yb_ref, pair, rev=True)
    of_ref[...] = (yf_ref[...] + xf_ref[:, 0:D_SSD].astype(F32) * dsum_ref[...]).astype(of_ref.dtype)
    ob_ref[...] = yb_ref[...].astype(ob_ref.dtype)


def _ssd(pm3, pdt3, conv_w, conv_b, dt_bias, a_log, d_skip, clen):
    bsz, t, _ = pm3.shape
    q = SSD_CHUNK
    nc, cc = t // q, clen // q
    nh = 2 * SSD_HEADS
    cw = jnp.pad(conv_w.astype(F32), ((0, 8 - SSD_CONV), (0, 0)))
    cb = conv_b.astype(F32).reshape(1, XBC_DIM)
    dtb = jnp.pad(dt_bias.astype(F32).reshape(1, nh), ((0, 0), (0, DT_PAD - nh)))
    a_neg = jnp.pad(-jnp.exp(a_log.astype(F32)).reshape(1, nh), ((0, 0), (0, DT_PAD - nh)))
    dsum = jnp.repeat((d_skip[0] + d_skip[1]).astype(F32), SSD_HEAD_DIM).reshape(1, D_SSD)
    n_halo = t // HALO

    const = lambda b, s: (0, 0)
    xcol = COL_XBC // XBC_DIM
    cq = CONV_ROWS
    hpb = cq // HALO
    xbc = pl.pallas_call(
        functools.partial(_ssd_conv_kernel, cc=clen // cq, nc=t // cq),
        grid=(bsz, t // cq),
        in_specs=[pl.BlockSpec((None, cq, XBC_DIM), lambda b, s: (b, s, xcol)),
                  pl.BlockSpec((None, HALO, XBC_DIM), lambda b, s: (b, jnp.maximum(s * hpb - 1, 0), xcol)),
                  pl.BlockSpec((None, HALO, XBC_DIM),
                               lambda b, s: (b, jnp.minimum((s + 1) * hpb, n_halo - 1), xcol)),
                  pl.BlockSpec((8, XBC_DIM), const), pl.BlockSpec((1, XBC_DIM), const)],
        out_specs=pl.BlockSpec((None, cq, XBC_DIM), lambda b, s: (b, s, 0)),
        out_shape=jax.ShapeDtypeStruct((bsz, t, XBC_DIM), BF16),
        compiler_params=_params(("arbitrary", "arbitrary")),
        name="ssd_conv",
    )(pm3, pm3, pm3, cw, cb)

    tok_f = lambda b, s: (b, s, 0)
    tok_b = lambda b, s: (b, _ssd_chunk_index(s, True, cc, nc), 0)
    state = [pltpu.VMEM((SSD_STATE, D_SSD), F32), pltpu.VMEM((q, D_SSD), F32)]
    return pl.pallas_call(
        _ssd_scan_kernel,
        grid=(bsz, nc),
        in_specs=[pl.BlockSpec((None, q, XBC_DIM), tok_f), pl.BlockSpec((None, q, DT_PAD), tok_f),
                  pl.BlockSpec((None, q, XBC_DIM), tok_b), pl.BlockSpec((None, q, DT_PAD), tok_b),
                  pl.BlockSpec((1, DT_PAD), const), pl.BlockSpec((1, DT_PAD), const),
                  pl.BlockSpec((1, D_SSD), const)],
        out_specs=[pl.BlockSpec((None, q, D_SSD), tok_f), pl.BlockSpec((None, q, D_SSD), tok_b)],
        out_shape=[jax.ShapeDtypeStruct((bsz, t, D_SSD), BF16), jax.ShapeDtypeStruct((bsz, t, D_SSD), BF16)],
        scratch_shapes=state + state + [pltpu.VMEM((q, D_SSD), F32), pltpu.VMEM((q, D_SSD), F32)],
        compiler_params=_params(("arbitrary", "arbitrary")),
        name="ssd_scan",
    )(xbc, pdt3, xbc, pdt3, dtb, a_neg, dsum)


MOE_TB = 512
ROUTE_TM = 512
INFO_W = 8
BIG_NEG = -3.0e38
ROW_SEG = D_MODEL // LANE


def _load_tokens(ref, t0, n):
    return jnp.concatenate([ref[pl.ds(t0 * ROW_SEG + c, n, stride=ROW_SEG), :] for c in range(ROW_SEG)], axis=1)


def _store_tokens(ref, t0, val):
    n = val.shape[0]
    for c in range(ROW_SEG):
        ref[pl.ds(t0 * ROW_SEG + c, n, stride=ROW_SEG), :] = val[:, c * LANE:(c + 1) * LANE]


def _route_kernel(f_ref, wr_ref, info_ref, cnt_ref, run_ref):
    tm = f_ref.shape[0] // ROW_SEG

    @pl.when(pl.program_id(0) == 0)
    def _():
        run_ref[...] = jnp.zeros_like(run_ref)

    logits = jnp.dot(_load_tokens(f_ref, 0, tm).astype(BF16), wr_ref[...], preferred_element_type=F32)
    lane = lax.broadcasted_iota(jnp.int32, (tm, LANE), 1).astype(F32)
    lg = jnp.where(lane < N_EXPERTS, logits, BIG_NEG)
    m1 = lg.max(axis=-1, keepdims=True)
    i1 = jnp.where(lg == m1, lane, float(LANE)).min(axis=-1, keepdims=True)
    lg2 = jnp.where(lane == i1, BIG_NEG, lg)
    m2 = lg2.max(axis=-1, keepdims=True)
    i2 = jnp.where(lg2 == m2, lane, float(LANE)).min(axis=-1, keepdims=True)
    ex = jnp.exp(m2 - m1)
    g0 = 1.0 / (1.0 + ex)
    g1 = ex / (1.0 + ex)
    sel0 = lane == i1
    sel1 = lane == i2
    onehot = jnp.where(sel0 | sel1, 1.0, 0.0)
    ri = lax.broadcasted_iota(jnp.int32, (tm, tm), 0)
    ci = lax.broadcasted_iota(jnp.int32, (tm, tm), 1)
    strict_lower = jnp.where(ci < ri, 1.0, 0.0).astype(BF16)
    before = jnp.dot(strict_lower, onehot.astype(BF16), preferred_element_type=F32) + run_ref[...]
    pos0 = jnp.where(sel0, before, 0.0).sum(axis=-1, keepdims=True)
    pos1 = jnp.where(sel1, before, 0.0).sum(axis=-1, keepdims=True)
    run_ref[...] = run_ref[...] + onehot.sum(axis=0, keepdims=True)
    cnt_ref[...] = run_ref[...]
    info = jnp.where(lane == 0, i1, jnp.where(lane == 1, i2, jnp.where(lane == 2, g0, jnp.where(
        lane == 3, g1, jnp.where(lane == 4, pos0, jnp.where(lane == 5, pos1, 0.0))))))
    info_ref[...] = info[:, :INFO_W]


def _route(f8, w_router):
    m, d = f8.shape[0] // ROW_SEG, D_MODEL
    wr = jnp.pad(w_router.astype(BF16), ((0, 0), (0, LANE - N_EXPERTS)))
    return pl.pallas_call(
        _route_kernel,
        grid=(m // ROUTE_TM,),
        in_specs=[pl.BlockSpec((ROUTE_TM * ROW_SEG, LANE), lambda i: (i, 0)),
                  pl.BlockSpec((d, LANE), lambda i: (0, 0))],
        out_specs=[pl.BlockSpec((ROUTE_TM, INFO_W), lambda i: (i, 0)), pl.BlockSpec((1, LANE), lambda i: (0, 0))],
        out_shape=[jax.ShapeDtypeStruct((m, INFO_W), F32), jax.ShapeDtypeStruct((1, LANE), F32)],
        scratch_shapes=[pltpu.VMEM((1, LANE), F32)],
        compiler_params=_params(("arbitrary",)),
        name="moe_route",
    )(f8, wr)


def _row_copy(src_ref, src_row, dst_ref, dst_row, sem):
    src = src_ref.at[pl.ds(pl.multiple_of(src_row * ROW_SEG, ROW_SEG), ROW_SEG)]
    dst = dst_ref.at[pl.ds(pl.multiple_of(dst_row * ROW_SEG, ROW_SEG), ROW_SEG)]
    return pltpu.make_async_copy(src, dst, sem)


def _wait_rows(src_ref, dst_ref, sem, n):
    def wait(t, c):
        _row_copy(src_ref, 0, dst_ref, 0, sem).wait()
        return c

    lax.fori_loop(0, n, wait, 0, unroll=8)


def _dispatch_kernel(pad_ref, dest_hbm, f_ref, rows_hbm, dest_smem, stage_ref, zero_ref, idx_sem, row_sem,
                     pad_sem):
    tm = f_ref.shape[0] // ROW_SEG
    i, n = pl.program_id(0), pl.num_programs(0)
    slot = i % 2
    stage = stage_ref.at[slot]
    sem = row_sem.at[slot]

    @pl.when(i == 0)
    def _():
        zero_ref[...] = jnp.zeros_like(zero_ref)
        for e in range(N_EXPERTS):
            first, count = pad_ref[e], pad_ref[N_EXPERTS + e]

            def start_zero(j, c):
                _row_copy(zero_ref, 0, rows_hbm, first + j, pad_sem).start()
                return c

            lax.fori_loop(0, count, start_zero, 0)
        for e in range(N_EXPERTS):
            def wait_zero(j, c):
                _row_copy(zero_ref, 0, rows_hbm, 0, pad_sem).wait()
                return c

            lax.fori_loop(0, pad_ref[N_EXPERTS + e], wait_zero, 0)

    @pl.when(i >= 2)
    def _():
        _wait_rows(stage, rows_hbm, sem, TOP_K * tm)

    stage[...] = f_ref[...]
    base = pl.multiple_of(i * (TOP_K * tm), TOP_K * tm)
    idx_copy = pltpu.make_async_copy(dest_hbm.at[pl.ds(base, TOP_K * tm)], dest_smem, idx_sem)
    idx_copy.start()
    idx_copy.wait()

    def start(t, c):
        for k in range(TOP_K):
            _row_copy(stage, t, rows_hbm, dest_smem[TOP_K * t + k], sem).start()
        return c

    lax.fori_loop(0, tm, start, 0, unroll=8)

    @pl.when(i == n - 1)
    def _():
        _wait_rows(stage, rows_hbm, sem, TOP_K * tm)

        @pl.when(n >= 2)
        def _():
            _wait_rows(stage_ref.at[1 - slot], rows_hbm, row_sem.at[1 - slot], TOP_K * tm)


def _dispatch(f8, dest_flat, pad_info, n_rows):
    m = f8.shape[0] // ROW_SEG
    grid_spec = pltpu.PrefetchScalarGridSpec(
        num_scalar_prefetch=1,
        grid=(m // ROUTE_TM,),
        in_specs=[pl.BlockSpec(memory_space=pl.ANY),
                  pl.BlockSpec((ROUTE_TM * ROW_SEG, LANE), lambda i, pad: (i, 0))],
        out_specs=pl.BlockSpec(memory_space=pl.ANY),
        scratch_shapes=[pltpu.SMEM((TOP_K * ROUTE_TM,), jnp.int32),
                        pltpu.VMEM((2, ROUTE_TM * ROW_SEG, LANE), f8.dtype),
                        pltpu.VMEM((ROW_SEG, LANE), f8.dtype),
                        pltpu.SemaphoreType.DMA(()), pltpu.SemaphoreType.DMA((2,)), pltpu.SemaphoreType.DMA(())],
    )
    return pl.pallas_call(
        _dispatch_kernel,
        grid_spec=grid_spec,
        out_shape=jax.ShapeDtypeStruct((n_rows * ROW_SEG, LANE), f8.dtype),
        compiler_params=_params(("arbitrary",)),
        name="moe_dispatch",
    )(pad_info, dest_flat, f8)


def _gffn_kernel(be_ref, nu_ref, x_ref, w1_ref, w3_ref, w2_ref, y_ref):
    i = pl.program_id(0)

    @pl.when(i < nu_ref[0])
    def _():
        x = _load_tokens(x_ref, 0, MOE_TB).astype(BF16)
        h1 = jnp.dot(x, w1_ref[...], preferred_element_type=F32)
        h3 = jnp.dot(x, w3_ref[...], preferred_element_type=F32)
        act = (h1 * jax.nn.sigmoid(h1) * h3).astype(BF16)
        _store_tokens(y_ref, 0, jnp.dot(act, w2_ref[...], preferred_element_type=F32))

    @pl.when(i >= nu_ref[0])
    def _():
        y_ref[...] = jnp.zeros_like(y_ref)


def _gffn(rows, block_e, n_used, w1, w3, w2):
    r, d = rows.shape[0] // ROW_SEG, D_MODEL
    ff = w1.shape[2]
    nb = r // MOE_TB
    once = pl.Buffered(1)
    grid_spec = pltpu.PrefetchScalarGridSpec(
        num_scalar_prefetch=2,
        grid=(nb,),
        in_specs=[pl.BlockSpec((MOE_TB * ROW_SEG, LANE), lambda i, be, nu: (i, 0)),
                  pl.BlockSpec((None, d, ff), lambda i, be, nu: (be[i], 0, 0), pipeline_mode=once),
                  pl.BlockSpec((None, d, ff), lambda i, be, nu: (be[i], 0, 0), pipeline_mode=once),
                  pl.BlockSpec((None, ff, d), lambda i, be, nu: (be[i], 0, 0), pipeline_mode=once)],
        out_specs=pl.BlockSpec((MOE_TB * ROW_SEG, LANE), lambda i, be, nu: (i, 0)),
    )
    return pl.pallas_call(
        _gffn_kernel,
        grid_spec=grid_spec,
        out_shape=jax.ShapeDtypeStruct((r * ROW_SEG, LANE), F32),
        compiler_params=_params(("arbitrary",)),
        name="moe_ffn",
    )(block_e, n_used, rows, w1, w3, w2)


def _combine_kernel(dest_hbm, y_hbm, h_ref, info_ref, modt_ref, o_ref, dest_smem, buf_ref, idx_sem, row_sem,
                    *scratch, nsub, tiles_per_batch):
    tm = h_ref.shape[0]
    i, n = pl.program_id(0), pl.num_programs(0)
    latent_only = tiles_per_batch is not None
    if latent_only:
        obuf_ref, out_sem = scratch

        def latent_copies(step, buf_slot, fn):
            for s in range(nsub):
                g = step * nsub + s
                j = lax.rem(g, tiles_per_batch)

                @pl.when(j > 0)
                def _():
                    row0 = (lax.div(g, tiles_per_batch) * (tiles_per_batch - 1) + (j - 1)) * MOD_TILE
                    fn(pltpu.make_async_copy(obuf_ref.at[buf_slot, pl.ds(s * MOD_TILE, MOD_TILE)],
                                             o_ref.at[pl.ds(pl.multiple_of(row0, MOD_TILE), MOD_TILE)],
                                             out_sem.at[buf_slot]))

    def gather(step, slot):
        base = pl.multiple_of(step * (TOP_K * tm), TOP_K * tm)
        idx_copy = pltpu.make_async_copy(dest_hbm.at[pl.ds(base, TOP_K * tm)], dest_smem, idx_sem)
        idx_copy.start()
        idx_copy.wait()

        def start(t, c):
            for k in range(TOP_K):
                _row_copy(y_hbm, dest_smem[TOP_K * t + k], buf_ref.at[slot, k], t, row_sem.at[slot]).start()
            return c

        lax.fori_loop(0, tm, start, 0, unroll=8)

    @pl.when(i == 0)
    def _():
        gather(0, 0)

    @pl.when(i + 1 < n)
    def _():
        gather(i + 1, (i + 1) % 2)

    slot = i % 2
    _wait_rows(y_hbm, buf_ref.at[slot, 0], row_sem.at[slot], TOP_K * tm)
    if latent_only:
        @pl.when(i >= 2)
        def _():
            latent_copies(i - 2, slot, lambda cp: cp.wait())
    for s in range(nsub):
        r0, r1 = s * MOD_TILE, (s + 1) * MOD_TILE
        info = info_ref[r0:r1, :]
        mix = (info[:, 2:3] * _load_tokens(buf_ref.at[slot, 0], r0, MOD_TILE)
               + info[:, 3:4] * _load_tokens(buf_ref.at[slot, 1], r0, MOD_TILE))
        out = h_ref[r0:r1, :] + modt_ref[s, 5:6, :] * mix
        if latent_only:
            obuf_ref[slot, r0:r1, :] = out
        else:
            o_ref[r0:r1, :] = out
    if latent_only:
        latent_copies(i, slot, lambda cp: cp.start())

        @pl.when(i == n - 1)
        def _():
            latent_copies(i, slot, lambda cp: cp.wait())

            @pl.when(n >= 2)
            def _():
                latent_copies(i - 1, 1 - slot, lambda cp: cp.wait())


def _combine(y_rows, dest_flat, h, info, modt, tiles_per_batch):
    m, d = h.shape
    nsub = ROUTE_TM // MOD_TILE
    scratch = [pltpu.SMEM((TOP_K * ROUTE_TM,), jnp.int32), pltpu.VMEM((2, TOP_K, ROUTE_TM * ROW_SEG, LANE), F32),
               pltpu.SemaphoreType.DMA(()), pltpu.SemaphoreType.DMA((2,))]
    if tiles_per_batch is None:
        out_spec, out_rows = pl.BlockSpec((ROUTE_TM, d), lambda i: (i, 0)), m
    else:
        out_spec, out_rows = pl.BlockSpec(memory_space=pl.ANY), m - m // tiles_per_batch
        scratch += [pltpu.VMEM((2, ROUTE_TM, d), F32), pltpu.SemaphoreType.DMA((2,))]
    return pl.pallas_call(
        functools.partial(_combine_kernel, nsub=nsub, tiles_per_batch=tiles_per_batch),
        grid=(m // ROUTE_TM,),
        in_specs=[pl.BlockSpec(memory_space=pl.ANY), pl.BlockSpec(memory_space=pl.ANY),
                  pl.BlockSpec((ROUTE_TM, d), lambda i: (i, 0)),
                  pl.BlockSpec((ROUTE_TM, INFO_W), lambda i: (i, 0)),
                  pl.BlockSpec((nsub, 6, d), lambda i: (i, 0, 0))],
        out_specs=out_spec,
        out_shape=jax.ShapeDtypeStruct((out_rows, d), F32),
        scratch_shapes=scratch,
        compiler_params=_params(("arbitrary",)),
        name="moe_combine",
    )(dest_flat, y_rows, h, info, modt)


def _moe(f8, h, w_router, w1, w3, w2, modt, tiles_per_batch=None):
    m, d = h.shape
    info, cnt = _route(f8, w_router)
    e = info[:, 0:TOP_K].astype(jnp.int32)
    pos = info[:, 4:4 + TOP_K].astype(jnp.int32)
    counts = cnt[0, :N_EXPERTS].astype(jnp.int32)
    padded = (counts + MOE_TB - 1) // MOE_TB * MOE_TB
    pad_end = jnp.cumsum(padded)
    pad_start = pad_end - padded
    dest_flat = (pad_start[e] + pos).reshape(-1)
    n_rows = m * TOP_K + N_EXPERTS * MOE_TB
    nb = n_rows // MOE_TB
    block_e = jnp.minimum(jnp.searchsorted(pad_end, jnp.arange(nb, dtype=jnp.int32) * MOE_TB, side='right'),
                          N_EXPERTS - 1).astype(jnp.int32)
    n_used = (pad_end[-1:] // MOE_TB).astype(jnp.int32)
    pad_info = jnp.concatenate([pad_start + counts, padded - counts]).astype(jnp.int32)
    rows = _dispatch(f8, dest_flat, pad_info, n_rows)
    y_rows = _gffn(rows, block_e, n_used, w1.astype(BF16), w3.astype(BF16), w2.astype(BF16))
    return _combine(y_rows, dest_flat, h, info, modt, tiles_per_batch)


def _tile_mod(mod_l, bsz, tiles_per_batch):
    t = jnp.arange(bsz * tiles_per_batch)
    idx = jnp.where(t % tiles_per_batch == 0, bsz, t // tiles_per_batch)
    return jnp.transpose(mod_l[:, idx, :], (1, 0, 2))


def kernel(x, c, ctx, c_ctx, w_mod, b_mod, norm1_w, norm2_w, w_in, conv_w, conv_b, dt_bias, a_log, d_skip,
           ssd_norm_w, q_norm_w, k_norm_w, rpb, w_br_ssd, w_br_na, w_out, w_ff1, w_ff3, w_ff2, w_router,
           w_e1, w_e3, w_e2):
    bsz, seqlen, d = x.shape
    clen = ctx.shape[1]
    n_layers = w_mod.shape[0]
    assert clen == MOD_TILE and seqlen % MOD_TILE == 0 and bsz < 16
    t = clen + seqlen
    m = bsz * t
    tiles_per_batch = t // MOD_TILE

    cond = jnp.zeros((16, d), F32).at[:bsz].set(c).at[bsz].set(c_ctx)
    mod = _modulation(cond, w_mod, b_mod)
    modt = [_tile_mod(mod[i], bsz, tiles_per_batch) for i in range(n_layers)]

    h = jnp.concatenate([ctx, x], axis=1).reshape(m, d)
    a = _adaln(h, norm1_w[0], modt[0])
    for i in range(n_layers):
        last = i == n_layers - 1
        wi = w_in[i]
        w_main = jnp.concatenate([wi[:, :IN_SPLITS[3]], _qk_reorder(wi[:, IN_SPLITS[4]:IN_SPLITS[5]]),
                                  _qk_reorder(wi[:, IN_SPLITS[5]:IN_SPLITS[6]]), wi[:, IN_SPLITS[6]:]],
                                 axis=1).astype(BF16)
        w_dt = jnp.pad(wi[:, IN_SPLITS[3]:IN_SPLITS[4]], ((0, 0), (0, DT_PAD - 2 * SSD_HEADS))).astype(BF16)
        p_main = _matmul(a, w_main, BF16, 512, 2560, "inproj")
        p_dt = _matmul(a, w_dt, F32, 1024, DT_PAD, "inproj_dt")
        pm3 = p_main.reshape(bsz, t, MAIN_DIM)
        yf, yb = _ssd(pm3, p_dt.reshape(bsz, t, DT_PAD), conv_w[i], conv_b[i], dt_bias[i], a_log[i],
                      d_skip[i], clen)
        y_na = _na(pm3, q_norm_w[i], k_norm_w[i], rpb[i], clen).reshape(m, d)
        fi = i // 2
        moe_layer = i % 2 == 1
        h, f = _merge(p_main, yf.reshape(m, d), yb.reshape(m, d), y_na, h, w_br_ssd[i].astype(BF16),
                      w_br_na[i].astype(BF16), w_out[i].astype(BF16), ssd_norm_w[i], norm2_w[i], modt[i],
                      row_tile_out=moe_layer)
        nxt = min(i + 1, n_layers - 1)
        if not moe_layer:
            h, a = _ffn(f, h, w_ff1[fi].astype(BF16), w_ff3[fi].astype(BF16), w_ff2[fi].astype(BF16),
                        norm1_w[nxt], modt[i], modt[nxt])
        else:
            if last:
                return _moe(f, h, w_router[fi], w_e1[fi], w_e3[fi], w_e2[fi], modt[i],
                            tiles_per_batch).reshape(bsz, seqlen, d)
            h = _moe(f, h, w_router[fi], w_e1[fi], w_e3[fi], w_e2[fi], modt[i])
            a = _adaln(h, norm1_w[nxt], modt[nxt])
    return h.reshape(bsz, t, d)[:, clen:]
```

```python
import functools
import math

import jax
import jax.numpy as jnp
from jax import lax
from jax.experimental import pallas as pl
from jax.experimental.pallas import tpu as pltpu

D_MODEL = 1024
GRID_W = 64
SSD_HEAD_DIM = 64
SSD_HEADS = D_MODEL // SSD_HEAD_DIM
D_SSD = SSD_HEADS * SSD_HEAD_DIM
SSD_GROUPS = 2
SSD_HPG = SSD_HEADS // SSD_GROUPS
SSD_STATE = 128
SSD_CONV = 5
SSD_CHUNK = 128
XBC_DIM = D_SSD + 2 * SSD_GROUPS * SSD_STATE
NA_HEAD_DIM = 64
NA_HEADS = D_MODEL // NA_HEAD_DIM
D_NA = NA_HEADS * NA_HEAD_DIM
NA_ROWS_MAX = 8
NA_COLS = 16
ROPE_BASE = 10000.0
N_EXPERTS = 8
TOP_K = 2
EPS = 1e-6
NEG_INF = -1e30
IN_SPLITS = (D_MODEL, 2 * D_MODEL, 2 * D_MODEL + D_SSD, 2 * D_MODEL + D_SSD + XBC_DIM,
             2 * D_MODEL + D_SSD + XBC_DIM + 2 * SSD_HEADS,
             2 * D_MODEL + D_SSD + XBC_DIM + 2 * SSD_HEADS + D_NA,
             2 * D_MODEL + D_SSD + XBC_DIM + 2 * SSD_HEADS + 2 * D_NA)

F32 = jnp.float32
BF16 = jnp.bfloat16
HIGHEST = lax.Precision.HIGHEST

MOD_TILE = 256
LANE = 128
VMEM_LIMIT = 48 * 1024 * 1024
COL_GS, COL_GN, COL_Z, COL_XBC, COL_Q, COL_K, COL_V = 0, 1024, 2048, 3072, 4608, 5632, 6656
MAIN_DIM = 7680
DT_PAD = LANE


def _params(sem):
    return pltpu.CompilerParams(dimension_semantics=sem, vmem_limit_bytes=VMEM_LIMIT)


def _rms_mod(x, nw, shift, scale):
    ms = jnp.mean(x * x, axis=-1, keepdims=True)
    return (x * lax.rsqrt(ms + EPS) * nw) * (1.0 + scale) + shift


def _mod_kernel(cond_ref, w_ref, b_ref, o_ref):
    c = cond_ref[...]
    s = c * jax.nn.sigmoid(c)
    o_ref[...] = jnp.dot(s, w_ref[...], precision=HIGHEST, preferred_element_type=F32) + b_ref[...]


def _modulation(cond, w_mod, b_mod):
    n_layers, d, _ = w_mod.shape
    rows = cond.shape[0]
    b4 = b_mod.reshape(n_layers, 6, 1, d)
    return pl.pallas_call(
        _mod_kernel,
        grid=(n_layers, 6),
        in_specs=[pl.BlockSpec((rows, d), lambda l, k: (0, 0)),
                  pl.BlockSpec((None, d, d), lambda l, k: (l, 0, k)),
                  pl.BlockSpec((None, None, 1, d), lambda l, k: (l, k, 0, 0))],
        out_specs=pl.BlockSpec((None, None, rows, d), lambda l, k: (l, k, 0, 0)),
        out_shape=jax.ShapeDtypeStruct((n_layers, 6, rows, d), F32),
        compiler_params=_params(("arbitrary", "arbitrary")),
        name="modulation",
    )(cond, w_mod, b4)


def _adaln_kernel(h_ref, nw_ref, modt_ref, o_ref, *, nsub):
    nw = nw_ref[...]
    for s in range(nsub):
        rows = pl.ds(s * MOD_TILE, MOD_TILE)
        x = h_ref[rows, :]
        o_ref[rows, :] = _rms_mod(x, nw, modt_ref[s, 0:1, :], modt_ref[s, 1:2, :]).astype(o_ref.dtype)


def _adaln(h, nw, modt, tm=1024):
    m, d = h.shape
    nsub = tm // MOD_TILE
    return pl.pallas_call(
        functools.partial(_adaln_kernel, nsub=nsub),
        grid=(m // tm,),
        in_specs=[pl.BlockSpec((tm, d), lambda i: (i, 0)),
                  pl.BlockSpec((1, d), lambda i: (0, 0)),
                  pl.BlockSpec((nsub, 6, d), lambda i: (i, 0, 0))],
        out_specs=pl.BlockSpec((tm, d), lambda i: (i, 0)),
        out_shape=jax.ShapeDtypeStruct((m, d), BF16),
        compiler_params=_params(("arbitrary",)),
        name="adaln",
    )(h, nw.reshape(1, d), modt)


def _matmul_kernel(a_ref, w_ref, o_ref):
    o_ref[...] = jnp.dot(a_ref[...], w_ref[...], preferred_element_type=F32).astype(o_ref.dtype)


def _matmul(a, w, out_dtype, tm, tn, name):
    m, k = a.shape
    n = w.shape[1]
    return pl.pallas_call(
        _matmul_kernel,
        grid=(n // tn, m // tm),
        in_specs=[pl.BlockSpec((tm, k), lambda j, i: (i, 0)),
                  pl.BlockSpec((k, tn), lambda j, i: (0, j))],
        out_specs=pl.BlockSpec((tm, tn), lambda j, i: (i, j)),
        out_shape=jax.ShapeDtypeStruct((m, n), out_dtype),
        compiler_params=_params(("arbitrary", "arbitrary")),
        name=name,
    )(a, w)


def _merge_kernel(gs_ref, gn_ref, z_ref, yf_ref, yb_ref, yn_ref, h_ref, wbs_ref, wbn_ref, wo_ref, snw_ref,
                  nw_ref, modt_ref, ho_ref, f_ref, *, nsub, row_tile_out):
    gw = D_SSD // SSD_GROUPS
    nw = nw_ref[...]
    sub = MOD_TILE
    n_chain = nsub
    ys, mixed = {}, {}

    def gated_norm(c):
        rows = slice(c * sub, (c + 1) * sub)
        z = z_ref[rows, :].astype(F32)
        u = (yf_ref[rows, :].astype(F32) + yb_ref[rows, :].astype(F32)) * (z * jax.nn.sigmoid(z))
        parts = []
        for g in range(SSD_GROUPS):
            ug = u[:, g * gw:(g + 1) * gw]
            ms = jnp.mean(ug * ug, axis=-1, keepdims=True)
            parts.append((ug * lax.rsqrt(ms + EPS) * snw_ref[:, g * gw:(g + 1) * gw]).astype(BF16))
        ys[c] = jnp.concatenate(parts, axis=1)

    def branches(c):
        rows = slice(c * sub, (c + 1) * sub)
        a = jnp.dot(ys.pop(c), wbs_ref[...], preferred_element_type=F32)
        b = jnp.dot(yn_ref[rows, :], wbn_ref[...], preferred_element_type=F32)
        mixed[c] = (jax.nn.sigmoid(gs_ref[rows, :].astype(F32)) * a
                    + jax.nn.sigmoid(gn_ref[rows, :].astype(F32)) * b).astype(BF16)

    def project(c):
        r0, s = c * sub, c
        m = jnp.dot(mixed.pop(c), wo_ref[...], preferred_element_type=F32)
        h = h_ref[r0:r0 + sub, :] + modt_ref[s, 2:3, :] * m
        ho_ref[r0:r0 + sub, :] = h
        f = _rms_mod(h, nw, modt_ref[s, 3:4, :], modt_ref[s, 4:5, :])
        if row_tile_out:
            _store_tokens(f_ref, r0, f)
        else:
            f_ref[r0:r0 + sub, :] = f.astype(f_ref.dtype)

    for k in range(n_chain + 2):
        if k < n_chain:
            gated_norm(k)
        if 0 <= k - 1 < n_chain:
            branches(k - 1)
        if 0 <= k - 2 < n_chain:
            project(k - 2)


def _merge(p_main, yf, yb, y_na, h, wbs, wbn, wo, ssd_nw, nw2, modt, row_tile_out, tm=512):
    m, d = h.shape
    nsub = tm // MOD_TILE
    row = lambda i: (i, 0)
    const = lambda i: (0, 0)
    if row_tile_out:
        f_spec, f_shape = pl.BlockSpec((tm * ROW_SEG, LANE), row), jax.ShapeDtypeStruct((m * ROW_SEG, LANE), F32)
    else:
        f_spec, f_shape = pl.BlockSpec((tm, d), row), jax.ShapeDtypeStruct((m, d), BF16)
    return pl.pallas_call(
        functools.partial(_merge_kernel, nsub=nsub, row_tile_out=row_tile_out),
        grid=(m // tm,),
        in_specs=[pl.BlockSpec((tm, d), lambda i: (i, COL_GS // D_MODEL)),
                  pl.BlockSpec((tm, d), lambda i: (i, COL_GN // D_MODEL)),
                  pl.BlockSpec((tm, d), lambda i: (i, COL_Z // D_MODEL)),
                  pl.BlockSpec((tm, d), row), pl.BlockSpec((tm, d), row), pl.BlockSpec((tm, d), row),
                  pl.BlockSpec((tm, d), row),
                  pl.BlockSpec((d, d), const), pl.BlockSpec((d, d), const), pl.BlockSpec((d, d), const),
                  pl.BlockSpec((1, d), const), pl.BlockSpec((1, d), const),
                  pl.BlockSpec((nsub, 6, d), lambda i: (i, 0, 0))],
        out_specs=[pl.BlockSpec((tm, d), row), f_spec],
        out_shape=[jax.ShapeDtypeStruct((m, d), F32), f_shape],
        compiler_params=_params(("arbitrary",)),
        name="merge",
    )(p_main, p_main, p_main, yf, yb, y_na, h, wbs, wbn, wo, ssd_nw.astype(F32).reshape(1, d), nw2.reshape(1, d),
      modt)


def _ffn_kernel(x_ref, h_ref, w1_ref, w3_ref, w2_ref, nw_ref, modt_ref, modn_ref, ho_ref, an_ref, *, nsub):
    x = x_ref[...]
    h1 = jnp.dot(x, w1_ref[...], preferred_element_type=F32)
    h3 = jnp.dot(x, w3_ref[...], preferred_element_type=F32)
    act = (h1 * jax.nn.sigmoid(h1) * h3).astype(BF16)
    out = jnp.dot(act, w2_ref[...], preferred_element_type=F32)
    nw = nw_ref[...]
    for s in range(nsub):
        r0, r1 = s * MOD_TILE, (s + 1) * MOD_TILE
        h = h_ref[r0:r1, :] + modt_ref[s, 5:6, :] * out[r0:r1, :]
        ho_ref[r0:r1, :] = h
        an_ref[r0:r1, :] = _rms_mod(h, nw, modn_ref[s, 0:1, :], modn_ref[s, 1:2, :]).astype(an_ref.dtype)


def _ffn(x, h, w1, w3, w2, nw_next, modt, modt_next, tm=512):
    m, d = h.shape
    ff = w1.shape[1]
    nsub = tm // MOD_TILE
    row = lambda i: (i, 0)
    const = lambda i: (0, 0)
    once = pl.Buffered(1)
    return pl.pallas_call(
        functools.partial(_ffn_kernel, nsub=nsub),
        grid=(m // tm,),
        in_specs=[pl.BlockSpec((tm, d), row), pl.BlockSpec((tm, d), row),
                  pl.BlockSpec((d, ff), const, pipeline_mode=once),
                  pl.BlockSpec((d, ff), const, pipeline_mode=once),
                  pl.BlockSpec((ff, d), const, pipeline_mode=once),
                  pl.BlockSpec((1, d), const),
                  pl.BlockSpec((nsub, 6, d), lambda i: (i, 0, 0)),
                  pl.BlockSpec((nsub, 6, d), lambda i: (i, 0, 0))],
        out_specs=[pl.BlockSpec((tm, d), row), pl.BlockSpec((tm, d), row)],
        out_shape=[jax.ShapeDtypeStruct((m, d), F32), jax.ShapeDtypeStruct((m, d), BF16)],
        compiler_params=_params(("arbitrary",)),
        name="ffn",
    )(x, h, w1, w3, w2, nw_next.reshape(1, d), modt, modt_next)


NA_BAND = NA_ROWS_MAX * GRID_W
HEAD_PAIR = LANE // NA_HEAD_DIM
NA_UNROLL = 8
NA_PREP_UNROLL = 8


def _head_mean_matrix():
    ri = lax.broadcasted_iota(jnp.int32, (2 * LANE, LANE), 0) % LANE
    ci = lax.broadcasted_iota(jnp.int32, (2 * LANE, LANE), 1)
    half = NA_HEAD_DIM // 2
    same_head = (ri % NA_HEAD_DIM) // half == (ci % NA_HEAD_DIM) // half
    return jnp.where(same_head, 1.0 / NA_HEAD_DIM, 0.0).astype(BF16)


def _head_rms(x, w, mean_mat):
    sq = x * x
    hi = sq.astype(BF16)
    lo = (sq - hi.astype(F32)).astype(BF16)
    ms = jnp.dot(jnp.concatenate([hi, lo], axis=1), mean_mat, preferred_element_type=F32)
    return x * lax.rsqrt(ms + EPS) * w


def _qk_lane_order():
    quarter = NA_HEAD_DIM // 4
    order = []
    for half in range(2):
        for head in range(HEAD_PAIR):
            for part in range(2):
                start = head * NA_HEAD_DIM + part * 2 * quarter + half * quarter
                order += list(range(start, start + quarter))
    return order


def _qk_reorder(w):
    k, n = w.shape
    return w.reshape(k, n // LANE, LANE)[:, :, jnp.asarray(_qk_lane_order())].reshape(k, n)


def _rope(x, cos, sin):
    return x * cos + pltpu.roll(x, LANE // 2, axis=1) * sin


def _stack_heads(x, lane_lo):
    zero = jnp.zeros_like(x)
    return jnp.concatenate([jnp.where(lane_lo, x, zero), jnp.where(lane_lo, zero, x)], axis=0)


def _attend(qs, k_list, v_list, bias_list, lane_lo):
    nt = (((1,), (1,)), ((), ()))
    s_list = []
    for k, bias in zip(k_list, bias_list):
        s = lax.dot_general(qs, k, nt, preferred_element_type=F32)
        s_list.append(s if bias is None else s + bias)
    mx = s_list[0].max(axis=-1, keepdims=True)
    for s in s_list[1:]:
        mx = jnp.maximum(mx, s.max(axis=-1, keepdims=True))
    den = None
    acc = None
    for s, v in zip(s_list, v_list):
        p = jnp.exp(s - mx)
        d = p.sum(axis=-1, keepdims=True)
        o = jnp.dot(p.astype(BF16), v, preferred_element_type=F32)
        den = d if den is None else den + d
        acc = o if acc is None else acc + o
    acc = acc / den
    n = acc.shape[0] // 2
    return jnp.where(lane_lo, acc[:n], acc[n:])


def _na_kernel(q_ref, k_ref, v_ref, bias_ref, rowc_ref, rows_ref, colc_ref, cols_ref, qw_ref, kw_ref,
               o_ref, qn_ref, kn_ref, s_ref, m_ref, *, clen, n_rows):
    lane = lax.broadcasted_iota(jnp.int32, (1, LANE), 1)
    lane_lo = lane < NA_HEAD_DIM
    qk_h0 = (lane % NA_HEAD_DIM) < NA_HEAD_DIM // 2
    qw = qw_ref[...] * (NA_HEAD_DIM ** -0.5)
    kw = kw_ref[...]
    kr = NA_ROWS_MAX

    mean_mat = _head_mean_matrix()
    qn_ref[0:clen, :] = _head_rms(q_ref[0:clen, :].astype(F32), qw, mean_mat).astype(BF16)
    kn_ref[0:clen, :] = _head_rms(k_ref[0:clen, :].astype(F32), kw, mean_mat).astype(BF16)
    colc = colc_ref[...]
    cols = cols_ref[...]

    def prep(r, carry):
        rows = pl.ds(pl.multiple_of(clen + r * GRID_W, GRID_W), GRID_W)
        cos = rowc_ref[pl.ds(r, 1), :] + colc
        sin = rows_ref[pl.ds(r, 1), :] + cols
        q = _head_rms(q_ref[rows, :].astype(F32), qw, mean_mat)
        k = _head_rms(k_ref[rows, :].astype(F32), kw, mean_mat)
        qn_ref[rows, :] = _rope(q, cos, sin).astype(BF16)
        kn_ref[rows, :] = _rope(k, cos, sin).astype(BF16)
        return carry

    lax.fori_loop(0, n_rows, prep, 0, unroll=NA_PREP_UNROLL)

    kc = kn_ref[0:clen, :]
    vc = v_ref[0:clen, :]
    for blk in range(clen // GRID_W):
        rows = slice(blk * GRID_W, (blk + 1) * GRID_W)
        qs = _stack_heads(qn_ref[rows, :], qk_h0)
        o_ref[rows, :] = _attend(qs, [kc], [vc], [None], lane_lo).astype(o_ref.dtype)

    nt = (((1,), (1,)), ((), ()))
    n_blk = (NA_BAND + clen) // LANE

    def band_of(r):
        r0 = jnp.clip(r - kr // 2, 0, n_rows - kr)
        return r0, pl.ds(pl.multiple_of(clen + r0 * GRID_W, GRID_W), NA_BAND)

    n_win = NA_BAND // LANE
    sq = HEAD_PAIR * GRID_W

    def scores(rows_r, slots):
        qs = [_stack_heads(qn_ref[pl.ds(pl.multiple_of(clen + r * GRID_W, GRID_W), GRID_W), :], qk_h0)
              for r in rows_r]
        s_ctx = lax.dot_general(jnp.concatenate(qs, axis=0), kn_ref[0:clen, :], nt, preferred_element_type=F32)
        for u, (r, slot) in enumerate(zip(rows_r, slots)):
            r0, band = band_of(r)
            s_win = lax.dot_general(qs[u], kn_ref[band, :], nt, preferred_element_type=F32)
            bias = bias_ref[r0 - r + (NA_ROWS_MAX - 1)]
            blocks = [s_win[:, j * LANE:(j + 1) * LANE] + bias[:, j * LANE:(j + 1) * LANE] for j in range(n_win)]
            blocks += [s_ctx[u * sq:(u + 1) * sq, j * LANE:(j + 1) * LANE] for j in range(n_blk - n_win)]
            mx = blocks[0]
            for j in range(n_blk):
                s_ref[slot, :, j * LANE:(j + 1) * LANE] = blocks[j]
                mx = jnp.maximum(mx, blocks[j])
            m_ref[slot] = jnp.broadcast_to(mx.max(axis=-1, keepdims=True), mx.shape)

    def output(rows_r, slots):
        accs, dens, p_ctx = [], [], []
        for r, slot in zip(rows_r, slots):
            _, band = band_of(r)
            mx = m_ref[slot]
            p = [jnp.exp(s_ref[slot, :, j * LANE:(j + 1) * LANE] - mx) for j in range(n_blk)]
            den = p[0]
            for pj in p[1:]:
                den = den + pj
            dens.append(den.sum(axis=-1, keepdims=True))
            pb = [pj.astype(BF16) for pj in p]
            accs.append(jnp.dot(jnp.concatenate(pb[:n_win], axis=1), v_ref[band, :], preferred_element_type=F32))
            p_ctx.append(jnp.concatenate(pb[n_win:], axis=1))
        acc_ctx = jnp.dot(jnp.concatenate(p_ctx, axis=0), v_ref[0:clen, :], preferred_element_type=F32)
        for u, r in enumerate(rows_r):
            acc = (accs[u] + acc_ctx[u * sq:(u + 1) * sq]) / dens[u]
            rows = pl.ds(pl.multiple_of(clen + r * GRID_W, GRID_W), GRID_W)
            o_ref[rows, :] = jnp.where(lane_lo, acc[:GRID_W], acc[GRID_W:]).astype(o_ref.dtype)

    scores(list(range(NA_UNROLL)), list(range(NA_UNROLL)))

    def body(j, carry):
        r = 2 * NA_UNROLL * j
        for half in range(2):
            output([r + half * NA_UNROLL + u for u in range(NA_UNROLL)],
                   [half * NA_UNROLL + u for u in range(NA_UNROLL)])
            scores([jnp.minimum(r + (half + 1) * NA_UNROLL + u, n_rows - 1) for u in range(NA_UNROLL)],
                   [(1 - half) * NA_UNROLL + u for u in range(NA_UNROLL)])
        return carry

    lax.fori_loop(0, n_rows // (2 * NA_UNROLL), body, 0)


def _na_tables(rpb, n_rows):
    col = jnp.arange(GRID_W)
    col_start = jnp.clip(col - NA_COLS // 2, 0, GRID_W - NA_COLS)
    col_ok = (col[None, :] >= col_start[:, None]) & (col[None, :] < col_start[:, None] + NA_COLS)
    col_idx = jnp.clip(col[None, :] - col[:, None], -(NA_COLS - 1), NA_COLS - 1) + NA_COLS - 1
    rpb_c = jnp.where(col_ok[None, None], rpb.astype(F32)[:, :, col_idx], NEG_INF)
    dr = jnp.arange(NA_ROWS_MAX)[:, None] + jnp.arange(NA_ROWS_MAX)[None, :]
    tab = rpb_c[:, dr]
    tab = jnp.transpose(tab, (0, 1, 3, 2, 4)).reshape(NA_HEADS, NA_ROWS_MAX, GRID_W, NA_BAND)
    tab = tab.reshape(NA_HEADS // HEAD_PAIR, HEAD_PAIR, NA_ROWS_MAX, GRID_W, NA_BAND)
    tab = jnp.transpose(tab, (0, 2, 1, 3, 4)).reshape(NA_HEADS // HEAD_PAIR, NA_ROWS_MAX,
                                                       HEAD_PAIR * GRID_W, NA_BAND)
    quarter = NA_HEAD_DIM // 4
    inv = ROPE_BASE ** (-jnp.arange(quarter, dtype=F32) / quarter)

    def tables(pos, lo):
        ang = pos[:, None] * inv[None, :]
        z = jnp.zeros_like(ang)
        c, s = jnp.cos(ang), jnp.sin(ang)
        ch = jnp.concatenate([c, c, z, z] if lo else [z, z, c, c], axis=-1)
        sh = jnp.concatenate([-s, s, z, z] if lo else [z, z, -s, s], axis=-1)
        order = jnp.asarray(_qk_lane_order())
        return jnp.tile(ch, (1, HEAD_PAIR))[:, order], jnp.tile(sh, (1, HEAD_PAIR))[:, order]

    rowc, rows = tables(jnp.arange(n_rows).astype(F32), True)
    colc, cols = tables(jnp.arange(GRID_W).astype(F32), False)
    return tab, rowc, rows, colc, cols


def _na(pm3, q_norm_w, k_norm_w, rpb, clen):
    bsz, t, _ = pm3.shape
    n_rows = (t - clen) // GRID_W
    tab, rowc, rows, colc, cols = _na_tables(rpb, n_rows)
    order = jnp.asarray(_qk_lane_order())
    qw = jnp.tile(q_norm_w.astype(F32), HEAD_PAIR)[order].reshape(1, LANE)
    kw = jnp.tile(k_norm_w.astype(F32), HEAD_PAIR)[order].reshape(1, LANE)
    n_pairs = NA_HEADS // HEAD_PAIR
    const = lambda p, b: (0, 0)
    tok = lambda c0: pl.BlockSpec((None, t, LANE), lambda p, b: (b, 0, c0 // LANE + p))
    return pl.pallas_call(
        functools.partial(_na_kernel, clen=clen, n_rows=n_rows),
        grid=(n_pairs, bsz),
        in_specs=[tok(COL_Q), tok(COL_K), tok(COL_V),
                  pl.BlockSpec((None, NA_ROWS_MAX, HEAD_PAIR * GRID_W, NA_BAND), lambda p, b: (p, 0, 0, 0)),
                  pl.BlockSpec((n_rows, LANE), const), pl.BlockSpec((n_rows, LANE), const),
                  pl.BlockSpec((GRID_W, LANE), const), pl.BlockSpec((GRID_W, LANE), const),
                  pl.BlockSpec((1, LANE), const), pl.BlockSpec((1, LANE), const)],
        out_specs=pl.BlockSpec((None, t, LANE), lambda p, b: (b, 0, p)),
        out_shape=jax.ShapeDtypeStruct((bsz, t, D_NA), BF16),
        scratch_shapes=[pltpu.VMEM((t, LANE), BF16), pltpu.VMEM((t, LANE), BF16),
                        pltpu.VMEM((2 * NA_UNROLL, HEAD_PAIR * GRID_W, NA_BAND + clen), F32),
                        pltpu.VMEM((2 * NA_UNROLL, HEAD_PAIR * GRID_W, LANE), F32)],
        compiler_params=_params(("arbitrary", "arbitrary")),
        name="natten",
    )(pm3, pm3, pm3, tab, rowc, rows, colc, cols, qw, kw)


HALO = 16
CONV_ROWS = 2 * SSD_CHUNK
CONV_PAD = SSD_CONV // 2
BC_DIM = SSD_GROUPS * SSD_STATE
PAIRS_PER_GROUP = SSD_HPG // HEAD_PAIR


def _ssd_chunk_index(s, rev, cc, nc):
    if not rev:
        return s
    return jnp.where(s < cc, cc - 1 - s, nc - 1 - (s - cc))


def _ssd_conv_kernel(main_ref, prev_ref, next_ref, cw_ref, cb_ref, o_ref, *, cc, nc):
    q = SSD_CHUNK
    n_piece = main_ref.shape[0] // q
    cidx = pl.program_id(1)
    ti = lax.broadcasted_iota(jnp.int32, (q, q + 2 * HALO), 0)
    ji = lax.broadcasted_iota(jnp.int32, (q, q + 2 * HALO), 1)
    taps = [k for k in range(SSD_CONV) if k != CONV_PAD]
    exts, convs = [], []
    for p in range(n_piece):
        before = prev_ref[...] if p == 0 else main_ref[p * q - HALO:p * q, :]
        after = next_ref[...] if p == n_piece - 1 else main_ref[(p + 1) * q:(p + 1) * q + HALO, :]
        exts.append(jnp.concatenate([before, main_ref[p * q:(p + 1) * q, :], after], axis=0))
        convs.append(cb_ref[...] + cw_ref[CONV_PAD:CONV_PAD + 1, :] * main_ref[p * q:(p + 1) * q, :].astype(F32))
    for k in taps:
        for p in range(n_piece):
            first_col = jnp.where((cidx == 0) | (cidx == cc), HALO, 0) if p == 0 else 0
            end_col = (jnp.where((cidx == cc - 1) | (cidx == nc - 1), HALO + q, q + 2 * HALO)
                       if p == n_piece - 1 else q + 2 * HALO)
            in_segment = (ji >= first_col) & (ji < end_col)
            shift = jnp.where((ji == ti + (HALO + k - CONV_PAD)) & in_segment, 1.0, 0.0).astype(BF16)
            convs[p] = convs[p] + cw_ref[k:k + 1, :] * jnp.dot(shift, exts[p], preferred_element_type=F32)
    for p in range(n_piece):
        o_ref[p * q:(p + 1) * q, :] = (convs[p] * jax.nn.sigmoid(convs[p])).astype(o_ref.dtype)


def _ssd_decay(dt_ref, dtb_ref, a_ref, *, rev):
    q = SSD_CHUNK
    x = dt_ref[...] + dtb_ref[...]
    dt = jnp.maximum(x, 0.0) + jnp.log(1.0 + jnp.exp(-jnp.abs(x)))
    la = dt * a_ref[...]
    ri = lax.broadcasted_iota(jnp.int32, (q, q), 0)
    ci = lax.broadcasted_iota(jnp.int32, (q, q), 1)
    allowed = (ci >= ri) if rev else (ci <= ri)
    tri = jnp.where(allowed, 1.0, 0.0).astype(F32)
    a_cs = jnp.dot(tri, la, precision=HIGHEST, preferred_element_type=F32)
    last = 0 if rev else q - 1
    a_cs_t = a_cs.T
    src_t = a_cs_t - jnp.log(dt.T)
    wdt_t = jnp.exp(a_cs_t[:, last:last + 1] - src_t)
    neg_mask = jnp.where(allowed, 0.0, NEG_INF)
    return a_cs, src_t, wdt_t, neg_mask


def _ssd_group(xbc, h_ref, yoff_ref, g):
    nt = (((1,), (1,)), ((), ()))
    b_g = xbc[:, D_SSD + g * SSD_STATE:D_SSD + (g + 1) * SSD_STATE]
    c_g = xbc[:, D_SSD + BC_DIM + g * SSD_STATE:D_SSD + BC_DIM + (g + 1) * SSD_STATE]
    cols = slice(g * SSD_HPG * SSD_HEAD_DIM, (g + 1) * SSD_HPG * SSD_HEAD_DIM)
    yoff_ref[:, cols] = jnp.dot(c_g, h_ref[:, cols].astype(BF16), preferred_element_type=F32)
    cb = lax.dot_general(c_g, b_g, nt, preferred_element_type=F32)
    return cb, b_g.astype(F32).T


def _ssd_pair(xbc, decay, cb, b_t, h_ref, yoff_ref, y_ref, pair, *, rev):
    q = SSD_CHUNK
    a_cs, src_t, wdt_t, neg_mask = decay
    hb = SSD_HEADS if rev else 0
    last = 0 if rev else q - 1
    lane_lo = lax.broadcasted_iota(jnp.int32, (1, LANE), 1) < SSD_HEAD_DIM
    lanes = slice(pair * LANE, (pair + 1) * LANE)
    lhs, ecol = [], []
    for hh in range(HEAD_PAIR):
        head = hb + pair * HEAD_PAIR + hh
        a_col = jnp.broadcast_to(a_cs[:, head:head + 1], (q, LANE))
        lhs.append((cb * jnp.exp((a_col - src_t[head:head + 1, :]) + neg_mask)).astype(BF16))
        ecol.append(jnp.exp(a_col))
    for hh in range(HEAD_PAIR):
        head = hb + pair * HEAD_PAIR + hh
        lhs.append((b_t * wdt_t[head:head + 1, :]).astype(BF16))
    res = jnp.dot(jnp.concatenate(lhs, axis=0), xbc[:, lanes], preferred_element_type=F32)
    e_pair = jnp.where(lane_lo, ecol[0], ecol[1])
    y_ref[:, lanes] = jnp.where(lane_lo, res[0:q], res[q:2 * q]) + yoff_ref[:, lanes] * e_pair
    h_ref[:, lanes] = (h_ref[:, lanes] * e_pair[last:last + 1, :]
                       + jnp.where(lane_lo, res[2 * q:3 * q], res[3 * q:4 * q]))


def _ssd_scan_kernel(xf_ref, dtf_ref, xb_ref, dtb_ref, bias_ref, a_ref, dsum_ref, of_ref, ob_ref,
                     hf_ref, yofff_ref, hb_ref, yoffb_ref, yf_ref, yb_ref):
    @pl.when(pl.program_id(1) == 0)
    def _():
        hf_ref[...] = jnp.zeros_like(hf_ref)
        hb_ref[...] = jnp.zeros_like(hb_ref)

    decay_f = _ssd_decay(dtf_ref, bias_ref, a_ref, rev=False)
    decay_b = _ssd_decay(dtb_ref, bias_ref, a_ref, rev=True)
    xf, xb = xf_ref[...], xb_ref[...]
    for g in range(SSD_GROUPS):
        cb_f, bt_f = _ssd_group(xf, hf_ref, yofff_ref, g)
        cb_b, bt_b = _ssd_group(xb, hb_ref, yoffb_ref, g)
        for pj in range(PAIRS_PER_GROUP):
            pair = g * PAIRS_PER_GROUP + pj
            _ssd_pair(xf, decay_f, cb_f, bt_f, hf_ref, yofff_ref, yf_ref, pair, rev=False)
            _ssd_pair(xb, decay_b, cb_b, bt_b, hb_ref, yoffb_ref, yb_ref, pair, rev=True)
    of_ref[...] = (yf_ref[...] + xf_ref[:, 0:D_SSD].astype(F32) * dsum_ref[...]).astype(of_ref.dtype)
    ob_ref[...] = yb_ref[...].astype(ob_ref.dtype)


def _ssd(pm3, pdt3, conv_w, conv_b, dt_bias, a_log, d_skip, clen):
    bsz, t, _ = pm3.shape
    q = SSD_CHUNK
    nc, cc = t // q, clen // q
    nh = 2 * SSD_HEADS
    cw = jnp.pad(conv_w.astype(F32), ((0, 8 - SSD_CONV), (0, 0)))
    cb = conv_b.astype(F32).reshape(1, XBC_DIM)
    dtb = jnp.pad(dt_bias.astype(F32).reshape(1, nh), ((0, 0), (0, DT_PAD - nh)))
    a_neg = jnp.pad(-jnp.exp(a_log.astype(F32)).reshape(1, nh), ((0, 0), (0, DT_PAD - nh)))
    dsum = jnp.repeat((d_skip[0] + d_skip[1]).astype(F32), SSD_HEAD_DIM).reshape(1, D_SSD)
    n_halo = t // HALO

    const = lambda b, s: (0, 0)
    xcol = COL_XBC // XBC_DIM
    cq = CONV_ROWS
    hpb = cq // HALO
    xbc = pl.pallas_call(
        functools.partial(_ssd_conv_kernel, cc=clen // cq, nc=t // cq),
        grid=(bsz, t // cq),
        in_specs=[pl.BlockSpec((None, cq, XBC_DIM), lambda b, s: (b, s, xcol)),
                  pl.BlockSpec((None, HALO, XBC_DIM), lambda b, s: (b, jnp.maximum(s * hpb - 1, 0), xcol)),
                  pl.BlockSpec((None, HALO, XBC_DIM),
                               lambda b, s: (b, jnp.minimum((s + 1) * hpb, n_halo - 1), xcol)),
                  pl.BlockSpec((8, XBC_DIM), const), pl.BlockSpec((1, XBC_DIM), const)],
        out_specs=pl.BlockSpec((None, cq, XBC_DIM), lambda b, s: (b, s, 0)),
        out_shape=jax.ShapeDtypeStruct((bsz, t, XBC_DIM), BF16),
        compiler_params=_params(("arbitrary", "arbitrary")),
        name="ssd_conv",
    )(pm3, pm3, pm3, cw, cb)

    tok_f = lambda b, s: (b, s, 0)
    tok_b = lambda b, s: (b, _ssd_chunk_index(s, True, cc, nc), 0)
    state = [pltpu.VMEM((SSD_STATE, D_SSD), F32), pltpu.VMEM((q, D_SSD), F32)]
    return pl.pallas_call(
        _ssd_scan_kernel,
        grid=(bsz, nc),
        in_specs=[pl.BlockSpec((None, q, XBC_DIM), tok_f), pl.BlockSpec((None, q, DT_PAD), tok_f),
                  pl.BlockSpec((None, q, XBC_DIM), tok_b), pl.BlockSpec((None, q, DT_PAD), tok_b),
                  pl.BlockSpec((1, DT_PAD), const), pl.BlockSpec((1, DT_PAD), const),
                  pl.BlockSpec((1, D_SSD), const)],
        out_specs=[pl.BlockSpec((None, q, D_SSD), tok_f), pl.BlockSpec((None, q, D_SSD), tok_b)],
        out_shape=[jax.ShapeDtypeStruct((bsz, t, D_SSD), BF16), jax.ShapeDtypeStruct((bsz, t, D_SSD), BF16)],
        scratch_shapes=state + state + [pltpu.VMEM((q, D_SSD), F32), pltpu.VMEM((q, D_SSD), F32)],
        compiler_params=_params(("arbitrary", "arbitrary")),
        name="ssd_scan",
    )(xbc, pdt3, xbc, pdt3, dtb, a_neg, dsum)


MOE_TB = 512
ROUTE_TM = 512
INFO_W = 8
BIG_NEG = -3.0e38
ROW_SEG = D_MODEL // LANE


def _load_tokens(ref, t0, n):
    return jnp.concatenate([ref[pl.ds(t0 * ROW_SEG + c, n, stride=ROW_SEG), :] for c in range(ROW_SEG)], axis=1)


def _store_tokens(ref, t0, val):
    n = val.shape[0]
    for c in range(ROW_SEG):
        ref[pl.ds(t0 * ROW_SEG + c, n, stride=ROW_SEG), :] = val[:, c * LANE:(c + 1) * LANE]


def _route_kernel(f_ref, wr_ref, info_ref, cnt_ref, run_ref):
    tm = f_ref.shape[0] // ROW_SEG

    @pl.when(pl.program_id(0) == 0)
    def _():
        run_ref[...] = jnp.zeros_like(run_ref)

    logits = jnp.dot(_load_tokens(f_ref, 0, tm).astype(BF16), wr_ref[...], preferred_element_type=F32)
    lane = lax.broadcasted_iota(jnp.int32, (tm, LANE), 1).astype(F32)
    lg = jnp.where(lane < N_EXPERTS, logits, BIG_NEG)
    m1 = lg.max(axis=-1, keepdims=True)
    i1 = jnp.where(lg == m1, lane, float(LANE)).min(axis=-1, keepdims=True)
    lg2 = jnp.where(lane == i1, BIG_NEG, lg)
    m2 = lg2.max(axis=-1, keepdims=True)
    i2 = jnp.where(lg2 == m2, lane, float(LANE)).min(axis=-1, keepdims=True)
    ex = jnp.exp(m2 - m1)
    g0 = 1.0 / (1.0 + ex)
    g1 = ex / (1.0 + ex)
    sel0 = lane == i1
    sel1 = lane == i2
    onehot = jnp.where(sel0 | sel1, 1.0, 0.0)
    ri = lax.broadcasted_iota(jnp.int32, (tm, tm), 0)
    ci = lax.broadcasted_iota(jnp.int32, (tm, tm), 1)
    strict_lower = jnp.where(ci < ri, 1.0, 0.0).astype(BF16)
    before = jnp.dot(strict_lower, onehot.astype(BF16), preferred_element_type=F32) + run_ref[...]
    pos0 = jnp.where(sel0, before, 0.0).sum(axis=-1, keepdims=True)
    pos1 = jnp.where(sel1, before, 0.0).sum(axis=-1, keepdims=True)
    run_ref[...] = run_ref[...] + onehot.sum(axis=0, keepdims=True)
    cnt_ref[...] = run_ref[...]
    info = jnp.where(lane == 0, i1, jnp.where(lane == 1, i2, jnp.where(lane == 2, g0, jnp.where(
        lane == 3, g1, jnp.where(lane == 4, pos0, jnp.where(lane == 5, pos1, 0.0))))))
    info_ref[...] = info[:, :INFO_W]


def _route(f8, w_router):
    m, d = f8.shape[0] // ROW_SEG, D_MODEL
    wr = jnp.pad(w_router.astype(BF16), ((0, 0), (0, LANE - N_EXPERTS)))
    return pl.pallas_call(
        _route_kernel,
        grid=(m // ROUTE_TM,),
        in_specs=[pl.BlockSpec((ROUTE_TM * ROW_SEG, LANE), lambda i: (i, 0)),
                  pl.BlockSpec((d, LANE), lambda i: (0, 0))],
        out_specs=[pl.BlockSpec((ROUTE_TM, INFO_W), lambda i: (i, 0)), pl.BlockSpec((1, LANE), lambda i: (0, 0))],
        out_shape=[jax.ShapeDtypeStruct((m, INFO_W), F32), jax.ShapeDtypeStruct((1, LANE), F32)],
        scratch_shapes=[pltpu.VMEM((1, LANE), F32)],
        compiler_params=_params(("arbitrary",)),
        name="moe_route",
    )(f8, wr)


def _row_copy(src_ref, src_row, dst_ref, dst_row, sem):
    src = src_ref.at[pl.ds(pl.multiple_of(src_row * ROW_SEG, ROW_SEG), ROW_SEG)]
    dst = dst_ref.at[pl.ds(pl.multiple_of(dst_row * ROW_SEG, ROW_SEG), ROW_SEG)]
    return pltpu.make_async_copy(src, dst, sem)


def _index_copy(idx_hbm, idx_smem, sems, step, n):
    slot = step % 2
    src = idx_hbm.at[pl.ds(pl.multiple_of(step * n, n), n)]
    return pltpu.make_async_copy(src, idx_smem.at[pl.ds(pl.multiple_of(slot * n, n), n)], sems.at[slot])


def _wait_rows(src_ref, dst_ref, sem, n):
    def wait(t, c):
        _row_copy(src_ref, 0, dst_ref, 0, sem).wait()
        return c

    lax.fori_loop(0, n, wait, 0, unroll=8)


def _dispatch_kernel(pad_ref, dest_hbm, f_ref, rows_hbm, dest_smem, stage_ref, zero_ref, idx_sem, row_sem,
                     pad_sem):
    tm = f_ref.shape[0] // ROW_SEG
    i, n = pl.program_id(0), pl.num_programs(0)
    slot = i % 2
    stage = stage_ref.at[slot]
    sem = row_sem.at[slot]

    @pl.when(i == 0)
    def _():
        zero_ref[...] = jnp.zeros_like(zero_ref)
        for e in range(N_EXPERTS):
            first, count = pad_ref[e], pad_ref[N_EXPERTS + e]

            def start_zero(j, c):
                _row_copy(zero_ref, 0, rows_hbm, first + j, pad_sem).start()
                return c

            lax.fori_loop(0, count, start_zero, 0)
        for e in range(N_EXPERTS):
            def wait_zero(j, c):
                _row_copy(zero_ref, 0, rows_hbm, 0, pad_sem).wait()
                return c

            lax.fori_loop(0, pad_ref[N_EXPERTS + e], wait_zero, 0)

    @pl.when(i >= 2)
    def _():
        _wait_rows(stage, rows_hbm, sem, TOP_K * tm)

    stage[...] = f_ref[...]

    @pl.when(i == 0)
    def _():
        _index_copy(dest_hbm, dest_smem, idx_sem, 0, TOP_K * tm).start()

    @pl.when(i + 1 < n)
    def _():
        _index_copy(dest_hbm, dest_smem, idx_sem, i + 1, TOP_K * tm).start()

    _index_copy(dest_hbm, dest_smem, idx_sem, i, TOP_K * tm).wait()
    idx0 = pl.multiple_of(slot * (TOP_K * tm), TOP_K * tm)

    def start(t, c):
        for k in range(TOP_K):
            _row_copy(stage, t, rows_hbm, dest_smem[idx0 + TOP_K * t + k], sem).start()
        return c

    lax.fori_loop(0, tm, start, 0, unroll=8)

    @pl.when(i == n - 1)
    def _():
        _wait_rows(stage, rows_hbm, sem, TOP_K * tm)

        @pl.when(n >= 2)
        def _():
            _wait_rows(stage_ref.at[1 - slot], rows_hbm, row_sem.at[1 - slot], TOP_K * tm)


def _dispatch(f8, dest_flat, pad_info, n_rows):
    m = f8.shape[0] // ROW_SEG
    grid_spec = pltpu.PrefetchScalarGridSpec(
        num_scalar_prefetch=1,
        grid=(m // ROUTE_TM,),
        in_specs=[pl.BlockSpec(memory_space=pl.ANY),
                  pl.BlockSpec((ROUTE_TM * ROW_SEG, LANE), lambda i, pad: (i, 0))],
        out_specs=pl.BlockSpec(memory_space=pl.ANY),
        scratch_shapes=[pltpu.SMEM((2 * TOP_K * ROUTE_TM,), jnp.int32),
                        pltpu.VMEM((2, ROUTE_TM * ROW_SEG, LANE), f8.dtype),
                        pltpu.VMEM((ROW_SEG, LANE), f8.dtype),
                        pltpu.SemaphoreType.DMA((2,)), pltpu.SemaphoreType.DMA((2,)), pltpu.SemaphoreType.DMA(())],
    )
    return pl.pallas_call(
        _dispatch_kernel,
        grid_spec=grid_spec,
        out_shape=jax.ShapeDtypeStruct((n_rows * ROW_SEG, LANE), f8.dtype),
        compiler_params=_params(("arbitrary",)),
        name="moe_dispatch",
    )(pad_info, dest_flat, f8)


def _gffn_kernel(be_ref, nu_ref, x_ref, w1_ref, w3_ref, w2_ref, y_ref):
    i = pl.program_id(0)

    @pl.when(i < nu_ref[0])
    def _():
        x = _load_tokens(x_ref, 0, MOE_TB).astype(BF16)
        h1 = jnp.dot(x, w1_ref[...], preferred_element_type=F32)
        h3 = jnp.dot(x, w3_ref[...], preferred_element_type=F32)
        act = (h1 * jax.nn.sigmoid(h1) * h3).astype(BF16)
        _store_tokens(y_ref, 0, jnp.dot(act, w2_ref[...], preferred_element_type=F32))

    @pl.when(i >= nu_ref[0])
    def _():
        y_ref[...] = jnp.zeros_like(y_ref)


def _gffn(rows, block_e, n_used, w1, w3, w2):
    r, d = rows.shape[0] // ROW_SEG, D_MODEL
    ff = w1.shape[2]
    nb = r // MOE_TB
    once = pl.Buffered(1)
    grid_spec = pltpu.PrefetchScalarGridSpec(
        num_scalar_prefetch=2,
        grid=(nb,),
        in_specs=[pl.BlockSpec((MOE_TB * ROW_SEG, LANE), lambda i, be, nu: (i, 0)),
                  pl.BlockSpec((None, d, ff), lambda i, be, nu: (be[i], 0, 0), pipeline_mode=once),
                  pl.BlockSpec((None, d, ff), lambda i, be, nu: (be[i], 0, 0), pipeline_mode=once),
                  pl.BlockSpec((None, ff, d), lambda i, be, nu: (be[i], 0, 0), pipeline_mode=once)],
        out_specs=pl.BlockSpec((MOE_TB * ROW_SEG, LANE), lambda i, be, nu: (i, 0)),
    )
    return pl.pallas_call(
        _gffn_kernel,
        grid_spec=grid_spec,
        out_shape=jax.ShapeDtypeStruct((r * ROW_SEG, LANE), F32),
        compiler_params=_params(("arbitrary",)),
        name="moe_ffn",
    )(block_e, n_used, rows, w1, w3, w2)


def _combine_kernel(dest_hbm, y_hbm, h_ref, info_ref, modt_ref, o_ref, dest_smem, buf_ref, idx_sem, row_sem,
                    *scratch, nsub, tiles_per_batch):
    tm = h_ref.shape[0]
    i, n = pl.program_id(0), pl.num_programs(0)
    latent_only = tiles_per_batch is not None
    if latent_only:
        obuf_ref, out_sem = scratch

        def latent_copies(step, buf_slot, fn):
            for s in range(nsub):
                g = step * nsub + s
                j = lax.rem(g, tiles_per_batch)

                @pl.when(j > 0)
                def _():
                    row0 = (lax.div(g, tiles_per_batch) * (tiles_per_batch - 1) + (j - 1)) * MOD_TILE
                    fn(pltpu.make_async_copy(obuf_ref.at[buf_slot, pl.ds(s * MOD_TILE, MOD_TILE)],
                                             o_ref.at[pl.ds(pl.multiple_of(row0, MOD_TILE), MOD_TILE)],
                                             out_sem.at[buf_slot]))

    n_idx = TOP_K * tm

    def gather(step, slot):
        _index_copy(dest_hbm, dest_smem, idx_sem, step, n_idx).wait()
        idx0 = pl.multiple_of(slot * n_idx, n_idx)

        def start(t, c):
            for k in range(TOP_K):
                _row_copy(y_hbm, dest_smem[idx0 + TOP_K * t + k], buf_ref.at[slot, k], t, row_sem.at[slot]).start()
            return c

        lax.fori_loop(0, tm, start, 0, unroll=8)

        @pl.when(step + 1 < n)
        def _():
            _index_copy(dest_hbm, dest_smem, idx_sem, step + 1, n_idx).start()

    @pl.when(i == 0)
    def _():
        _index_copy(dest_hbm, dest_smem, idx_sem, 0, n_idx).start()
        gather(0, 0)

    @pl.when(i + 1 < n)
    def _():
        gather(i + 1, (i + 1) % 2)

    slot = i % 2
    _wait_rows(y_hbm, buf_ref.at[slot, 0], row_sem.at[slot], TOP_K * tm)
    if latent_only:
        @pl.when(i >= 2)
        def _():
            latent_copies(i - 2, slot, lambda cp: cp.wait())
    for s in range(nsub):
        r0, r1 = s * MOD_TILE, (s + 1) * MOD_TILE
        info = info_ref[r0:r1, :]
        mix = (info[:, 2:3] * _load_tokens(buf_ref.at[slot, 0], r0, MOD_TILE)
               + info[:, 3:4] * _load_tokens(buf_ref.at[slot, 1], r0, MOD_TILE))
        out = h_ref[r0:r1, :] + modt_ref[s, 5:6, :] * mix
        if latent_only:
            obuf_ref[slot, r0:r1, :] = out
        else:
            o_ref[r0:r1, :] = out
    if latent_only:
        latent_copies(i, slot, lambda cp: cp.start())

        @pl.when(i == n - 1)
        def _():
            latent_copies(i, slot, lambda cp: cp.wait())

            @pl.when(n >= 2)
            def _():
                latent_copies(i - 1, 1 - slot, lambda cp: cp.wait())


def _combine(y_rows, dest_flat, h, info, modt, tiles_per_batch):
    m, d = h.shape
    nsub = ROUTE_TM // MOD_TILE
    scratch = [pltpu.SMEM((2 * TOP_K * ROUTE_TM,), jnp.int32),
               pltpu.VMEM((2, TOP_K, ROUTE_TM * ROW_SEG, LANE), F32),
               pltpu.SemaphoreType.DMA((2,)), pltpu.SemaphoreType.DMA((2,))]
    if tiles_per_batch is None:
        out_spec, out_rows = pl.BlockSpec((ROUTE_TM, d), lambda i: (i, 0)), m
    else:
        out_spec, out_rows = pl.BlockSpec(memory_space=pl.ANY), m - m // tiles_per_batch
        scratch += [pltpu.VMEM((2, ROUTE_TM, d), F32), pltpu.SemaphoreType.DMA((2,))]
    return pl.pallas_call(
        functools.partial(_combine_kernel, nsub=nsub, tiles_per_batch=tiles_per_batch),
        grid=(m // ROUTE_TM,),
        in_specs=[pl.BlockSpec(memory_space=pl.ANY), pl.BlockSpec(memory_space=pl.ANY),
                  pl.BlockSpec((ROUTE_TM, d), lambda i: (i, 0)),
                  pl.BlockSpec((ROUTE_TM, INFO_W), lambda i: (i, 0)),
                  pl.BlockSpec((nsub, 6, d), lambda i: (i, 0, 0))],
        out_specs=out_spec,
        out_shape=jax.ShapeDtypeStruct((out_rows, d), F32),
        scratch_shapes=scratch,
        compiler_params=_params(("arbitrary",)),
        name="moe_combine",
    )(dest_flat, y_rows, h, info, modt)


def _moe(f8, h, w_router, w1, w3, w2, modt, tiles_per_batch=None):
    m, d = h.shape
    info, cnt = _route(f8, w_router)
    e = info[:, 0:TOP_K].astype(jnp.int32)
    pos = info[:, 4:4 + TOP_K].astype(jnp.int32)
    counts = cnt[0, :N_EXPERTS].astype(jnp.int32)
    padded = (counts + MOE_TB - 1) // MOE_TB * MOE_TB
    pad_end = jnp.cumsum(padded)
    pad_start = pad_end - padded
    dest_flat = (pad_start[e] + pos).reshape(-1)
    n_rows = m * TOP_K + N_EXPERTS * MOE_TB
    nb = n_rows // MOE_TB
    block_e = jnp.minimum(jnp.searchsorted(pad_end, jnp.arange(nb, dtype=jnp.int32) * MOE_TB, side='right'),
                          N_EXPERTS - 1).astype(jnp.int32)
    n_used = (pad_end[-1:] // MOE_TB).astype(jnp.int32)
    pad_info = jnp.concatenate([pad_start + counts, padded - counts]).astype(jnp.int32)
    rows = _dispatch(f8, dest_flat, pad_info, n_rows)
    y_rows = _gffn(rows, block_e, n_used, w1.astype(BF16), w3.astype(BF16), w2.astype(BF16))
    return _combine(y_rows, dest_flat, h, info, modt, tiles_per_batch)


def _tile_mod(mod_l, bsz, tiles_per_batch):
    t = jnp.arange(bsz * tiles_per_batch)
    idx = jnp.where(t % tiles_per_batch == 0, bsz, t // tiles_per_batch)
    return jnp.transpose(mod_l[:, idx, :], (1, 0, 2))


def kernel(x, c, ctx, c_ctx, w_mod, b_mod, norm1_w, norm2_w, w_in, conv_w, conv_b, dt_bias, a_log, d_skip,
           ssd_norm_w, q_norm_w, k_norm_w, rpb, w_br_ssd, w_br_na, w_out, w_ff1, w_ff3, w_ff2, w_router,
           w_e1, w_e3, w_e2):
    bsz, seqlen, d = x.shape
    clen = ctx.shape[1]
    n_layers = w_mod.shape[0]
    assert clen == MOD_TILE and seqlen % MOD_TILE == 0 and bsz < 16
    t = clen + seqlen
    m = bsz * t
    tiles_per_batch = t // MOD_TILE

    cond = jnp.zeros((16, d), F32).at[:bsz].set(c).at[bsz].set(c_ctx)
    mod = _modulation(cond, w_mod, b_mod)
    modt = [_tile_mod(mod[i], bsz, tiles_per_batch) for i in range(n_layers)]

    h = jnp.concatenate([ctx, x], axis=1).reshape(m, d)
    a = _adaln(h, norm1_w[0], modt[0])
    for i in range(n_layers):
        last = i == n_layers - 1
        wi = w_in[i]
        w_main = jnp.concatenate([wi[:, :IN_SPLITS[3]], _qk_reorder(wi[:, IN_SPLITS[4]:IN_SPLITS[5]]),
                                  _qk_reorder(wi[:, IN_SPLITS[5]:IN_SPLITS[6]]), wi[:, IN_SPLITS[6]:]],
                                 axis=1).astype(BF16)
        w_dt = jnp.pad(wi[:, IN_SPLITS[3]:IN_SPLITS[4]], ((0, 0), (0, DT_PAD - 2 * SSD_HEADS))).astype(BF16)
        p_main = _matmul(a, w_main, BF16, 512, 2560, "inproj")
        p_dt = _matmul(a, w_dt, F32, 1024, DT_PAD, "inproj_dt")
        pm3 = p_main.reshape(bsz, t, MAIN_DIM)
        yf, yb = _ssd(pm3, p_dt.reshape(bsz, t, DT_PAD), conv_w[i], conv_b[i], dt_bias[i], a_log[i],
                      d_skip[i], clen)
        y_na = _na(pm3, q_norm_w[i], k_norm_w[i], rpb[i], clen).reshape(m, d)
        fi = i // 2
        moe_layer = i % 2 == 1
        h, f = _merge(p_main, yf.reshape(m, d), yb.reshape(m, d), y_na, h, w_br_ssd[i].astype(BF16),
                      w_br_na[i].astype(BF16), w_out[i].astype(BF16), ssd_norm_w[i], norm2_w[i], modt[i],
                      row_tile_out=moe_layer)
        nxt = min(i + 1, n_layers - 1)
        if not moe_layer:
            h, a = _ffn(f, h, w_ff1[fi].astype(BF16), w_ff3[fi].astype(BF16), w_ff2[fi].astype(BF16),
                        norm1_w[nxt], modt[i], modt[nxt])
        else:
            if last:
                return _moe(f, h, w_router[fi], w_e1[fi], w_e3[fi], w_e2[fi], modt[i],
                            tiles_per_batch).reshape(bsz, seqlen, d)
            h = _moe(f, h, w_router[fi], w_e1[fi], w_e3[fi], w_e2[fi], modt[i])
            a = _adaln(h, norm1_w[nxt], modt[nxt])
    return h.reshape(bsz, t, d)[:, clen:]
```

```python
import functools
import math

import jax
import jax.numpy as jnp
from jax import lax
from jax.experimental import pallas as pl
from jax.experimental.pallas import tpu as pltpu

D_MODEL = 1024
GRID_W = 64
SSD_HEAD_DIM = 64
SSD_HEADS = D_MODEL // SSD_HEAD_DIM
D_SSD = SSD_HEADS * SSD_HEAD_DIM
SSD_GROUPS = 2
SSD_HPG = SSD_HEADS // SSD_GROUPS
SSD_STATE = 128
SSD_CONV = 5
SSD_CHUNK = 128
XBC_DIM = D_SSD + 2 * SSD_GROUPS * SSD_STATE
NA_HEAD_DIM = 64
NA_HEADS = D_MODEL // NA_HEAD_DIM
D_NA = NA_HEADS * NA_HEAD_DIM
NA_ROWS_MAX = 8
NA_COLS = 16
ROPE_BASE = 10000.0
N_EXPERTS = 8
TOP_K = 2
EPS = 1e-6
NEG_INF = -1e30
IN_SPLITS = (D_MODEL, 2 * D_MODEL, 2 * D_MODEL + D_SSD, 2 * D_MODEL + D_SSD + XBC_DIM,
             2 * D_MODEL + D_SSD + XBC_DIM + 2 * SSD_HEADS,
             2 * D_MODEL + D_SSD + XBC_DIM + 2 * SSD_HEADS + D_NA,
             2 * D_MODEL + D_SSD + XBC_DIM + 2 * SSD_HEADS + 2 * D_NA)

F32 = jnp.float32
BF16 = jnp.bfloat16
HIGHEST = lax.Precision.HIGHEST

MOD_TILE = 256
LANE = 128
VMEM_LIMIT = 48 * 1024 * 1024
COL_GS, COL_GN, COL_Z, COL_XBC, COL_Q, COL_K, COL_V = 0, 1024, 2048, 3072, 4608, 5632, 6656
MAIN_DIM = 7680
DT_PAD = LANE


def _params(sem):
    return pltpu.CompilerParams(dimension_semantics=sem, vmem_limit_bytes=VMEM_LIMIT)


def _rms_mod(x, nw, shift, scale):
    ms = jnp.mean(x * x, axis=-1, keepdims=True)
    return (x * lax.rsqrt(ms + EPS) * nw) * (1.0 + scale) + shift


def _mod_kernel(cond_ref, w_ref, b_ref, o_ref):
    c = cond_ref[...]
    s = c * jax.nn.sigmoid(c)
    o_ref[...] = jnp.dot(s, w_ref[...], precision=HIGHEST, preferred_element_type=F32) + b_ref[...]


def _modulation(cond, w_mod, b_mod):
    n_layers, d, _ = w_mod.shape
    rows = cond.shape[0]
    b4 = b_mod.reshape(n_layers, 6, 1, d)
    return pl.pallas_call(
        _mod_kernel,
        grid=(n_layers, 6),
        in_specs=[pl.BlockSpec((rows, d), lambda l, k: (0, 0)),
                  pl.BlockSpec((None, d, d), lambda l, k: (l, 0, k)),
                  pl.BlockSpec((None, None, 1, d), lambda l, k: (l, k, 0, 0))],
        out_specs=pl.BlockSpec((None, None, rows, d), lambda l, k: (l, k, 0, 0)),
        out_shape=jax.ShapeDtypeStruct((n_layers, 6, rows, d), F32),
        compiler_params=_params(("arbitrary", "arbitrary")),
        name="modulation",
    )(cond, w_mod, b4)


def _adaln_kernel(h_ref, nw_ref, modt_ref, o_ref, *, nsub):
    nw = nw_ref[...]
    for s in range(nsub):
        rows = pl.ds(s * MOD_TILE, MOD_TILE)
        x = h_ref[rows, :]
        o_ref[rows, :] = _rms_mod(x, nw, modt_ref[s, 0:1, :], modt_ref[s, 1:2, :]).astype(o_ref.dtype)


def _adaln(h, nw, modt, tm=1024):
    m, d = h.shape
    nsub = tm // MOD_TILE
    return pl.pallas_call(
        functools.partial(_adaln_kernel, nsub=nsub),
        grid=(m // tm,),
        in_specs=[pl.BlockSpec((tm, d), lambda i: (i, 0)),
                  pl.BlockSpec((1, d), lambda i: (0, 0)),
                  pl.BlockSpec((nsub, 6, d), lambda i: (i, 0, 0))],
        out_specs=pl.BlockSpec((tm, d), lambda i: (i, 0)),
        out_shape=jax.ShapeDtypeStruct((m, d), BF16),
        compiler_params=_params(("arbitrary",)),
        name="adaln",
    )(h, nw.reshape(1, d), modt)


def _matmul_kernel(a_ref, w_ref, o_ref):
    o_ref[...] = jnp.dot(a_ref[...], w_ref[...], preferred_element_type=F32).astype(o_ref.dtype)


def _matmul(a, w, out_dtype, tm, tn, name):
    m, k = a.shape
    n = w.shape[1]
    return pl.pallas_call(
        _matmul_kernel,
        grid=(n // tn, m // tm),
        in_specs=[pl.BlockSpec((tm, k), lambda j, i: (i, 0)),
                  pl.BlockSpec((k, tn), lambda j, i: (0, j))],
        out_specs=pl.BlockSpec((tm, tn), lambda j, i: (i, j)),
        out_shape=jax.ShapeDtypeStruct((m, n), out_dtype),
        compiler_params=_params(("arbitrary", "arbitrary")),
        name=name,
    )(a, w)


def _merge_kernel(gs_ref, gn_ref, z_ref, yf_ref, yb_ref, yn_ref, h_ref, wbs_ref, wbn_ref, wo_ref, snw_ref,
                  nw_ref, modt_ref, ho_ref, f_ref, *, nsub, row_tile_out):
    gw = D_SSD // SSD_GROUPS
    nw = nw_ref[...]
    sub = MOD_TILE
    n_chain = nsub
    ys, mixed = {}, {}

    def gated_norm(c):
        rows = slice(c * sub, (c + 1) * sub)
        z = z_ref[rows, :].astype(F32)
        u = (yf_ref[rows, :].astype(F32) + yb_ref[rows, :].astype(F32)) * (z * jax.nn.sigmoid(z))
        parts = []
        for g in range(SSD_GROUPS):
            ug = u[:, g * gw:(g + 1) * gw]
            ms = jnp.mean(ug * ug, axis=-1, keepdims=True)
            parts.append((ug * lax.rsqrt(ms + EPS) * snw_ref[:, g * gw:(g + 1) * gw]).astype(BF16))
        ys[c] = jnp.concatenate(parts, axis=1)

    def branches(c):
        rows = slice(c * sub, (c + 1) * sub)
        a = jnp.dot(ys.pop(c), wbs_ref[...], preferred_element_type=F32)
        b = jnp.dot(yn_ref[rows, :], wbn_ref[...], preferred_element_type=F32)
        mixed[c] = (jax.nn.sigmoid(gs_ref[rows, :].astype(F32)) * a
                    + jax.nn.sigmoid(gn_ref[rows, :].astype(F32)) * b).astype(BF16)

    def project(c):
        r0, s = c * sub, c
        m = jnp.dot(mixed.pop(c), wo_ref[...], preferred_element_type=F32)
        h = h_ref[r0:r0 + sub, :] + modt_ref[s, 2:3, :] * m
        ho_ref[r0:r0 + sub, :] = h
        f = _rms_mod(h, nw, modt_ref[s, 3:4, :], modt_ref[s, 4:5, :])
        if row_tile_out:
            _store_tokens(f_ref, r0, f)
        else:
            f_ref[r0:r0 + sub, :] = f.astype(f_ref.dtype)

    for k in range(n_chain + 2):
        if k < n_chain:
            gated_norm(k)
        if 0 <= k - 1 < n_chain:
            branches(k - 1)
        if 0 <= k - 2 < n_chain:
            project(k - 2)


def _merge(p_main, yf, yb, y_na, h, wbs, wbn, wo, ssd_nw, nw2, modt, row_tile_out, tm=512):
    m, d = h.shape
    nsub = tm // MOD_TILE
    row = lambda i: (i, 0)
    const = lambda i: (0, 0)
    if row_tile_out:
        f_spec, f_shape = pl.BlockSpec((tm * ROW_SEG, LANE), row), jax.ShapeDtypeStruct((m * ROW_SEG, LANE), F32)
    else:
        f_spec, f_shape = pl.BlockSpec((tm, d), row), jax.ShapeDtypeStruct((m, d), BF16)
    return pl.pallas_call(
        functools.partial(_merge_kernel, nsub=nsub, row_tile_out=row_tile_out),
        grid=(m // tm,),
        in_specs=[pl.BlockSpec((tm, d), lambda i: (i, COL_GS // D_MODEL)),
                  pl.BlockSpec((tm, d), lambda i: (i, COL_GN // D_MODEL)),
                  pl.BlockSpec((tm, d), lambda i: (i, COL_Z // D_MODEL)),
                  pl.BlockSpec((tm, d), row), pl.BlockSpec((tm, d), row), pl.BlockSpec((tm, d), row),
                  pl.BlockSpec((tm, d), row),
                  pl.BlockSpec((d, d), const), pl.BlockSpec((d, d), const), pl.BlockSpec((d, d), const),
                  pl.BlockSpec((1, d), const), pl.BlockSpec((1, d), const),
                  pl.BlockSpec((nsub, 6, d), lambda i: (i, 0, 0))],
        out_specs=[pl.BlockSpec((tm, d), row), f_spec],
        out_shape=[jax.ShapeDtypeStruct((m, d), F32), f_shape],
        compiler_params=_params(("arbitrary",)),
        name="merge",
    )(p_main, p_main, p_main, yf, yb, y_na, h, wbs, wbn, wo, ssd_nw.astype(F32).reshape(1, d), nw2.reshape(1, d),
      modt)


def _ffn_kernel(x_ref, h_ref, w1_ref, w3_ref, w2_ref, nw_ref, modt_ref, modn_ref, ho_ref, an_ref, *, nsub):
    x = x_ref[...]
    h1 = jnp.dot(x, w1_ref[...], preferred_element_type=F32)
    h3 = jnp.dot(x, w3_ref[...], preferred_element_type=F32)
    act = (h1 * jax.nn.sigmoid(h1) * h3).astype(BF16)
    out = jnp.dot(act, w2_ref[...], preferred_element_type=F32)
    nw = nw_ref[...]
    for s in range(nsub):
        r0, r1 = s * MOD_TILE, (s + 1) * MOD_TILE
        h = h_ref[r0:r1, :] + modt_ref[s, 5:6, :] * out[r0:r1, :]
        ho_ref[r0:r1, :] = h
        an_ref[r0:r1, :] = _rms_mod(h, nw, modn_ref[s, 0:1, :], modn_ref[s, 1:2, :]).astype(an_ref.dtype)


def _ffn(x, h, w1, w3, w2, nw_next, modt, modt_next, tm=512):
    m, d = h.shape
    ff = w1.shape[1]
    nsub = tm // MOD_TILE
    row = lambda i: (i, 0)
    const = lambda i: (0, 0)
    once = pl.Buffered(1)
    return pl.pallas_call(
        functools.partial(_ffn_kernel, nsub=nsub),
        grid=(m // tm,),
        in_specs=[pl.BlockSpec((tm, d), row), pl.BlockSpec((tm, d), row),
                  pl.BlockSpec((d, ff), const, pipeline_mode=once),
                  pl.BlockSpec((d, ff), const, pipeline_mode=once),
                  pl.BlockSpec((ff, d), const, pipeline_mode=once),
                  pl.BlockSpec((1, d), const),
                  pl.BlockSpec((nsub, 6, d), lambda i: (i, 0, 0)),
                  pl.BlockSpec((nsub, 6, d), lambda i: (i, 0, 0))],
        out_specs=[pl.BlockSpec((tm, d), row), pl.BlockSpec((tm, d), row)],
        out_shape=[jax.ShapeDtypeStruct((m, d), F32), jax.ShapeDtypeStruct((m, d), BF16)],
        compiler_params=_params(("arbitrary",)),
        name="ffn",
    )(x, h, w1, w3, w2, nw_next.reshape(1, d), modt, modt_next)


NA_BAND = NA_ROWS_MAX * GRID_W
HEAD_PAIR = LANE // NA_HEAD_DIM
NA_UNROLL = 8
NA_PREP_UNROLL = 8


def _head_mean_matrix():
    ri = lax.broadcasted_iota(jnp.int32, (2 * LANE, LANE), 0) % LANE
    ci = lax.broadcasted_iota(jnp.int32, (2 * LANE, LANE), 1)
    half = NA_HEAD_DIM // 2
    same_head = (ri % NA_HEAD_DIM) // half == (ci % NA_HEAD_DIM) // half
    return jnp.where(same_head, 1.0 / NA_HEAD_DIM, 0.0).astype(BF16)


def _head_rms(x, w, mean_mat):
    sq = x * x
    hi = sq.astype(BF16)
    lo = (sq - hi.astype(F32)).astype(BF16)
    ms = jnp.dot(jnp.concatenate([hi, lo], axis=1), mean_mat, preferred_element_type=F32)
    return x * lax.rsqrt(ms + EPS) * w


def _qk_lane_order():
    quarter = NA_HEAD_DIM // 4
    order = []
    for half in range(2):
        for head in range(HEAD_PAIR):
            for part in range(2):
                start = head * NA_HEAD_DIM + part * 2 * quarter + half * quarter
                order += list(range(start, start + quarter))
    return order


def _qk_reorder(w):
    k, n = w.shape
    return w.reshape(k, n // LANE, LANE)[:, :, jnp.asarray(_qk_lane_order())].reshape(k, n)


def _rope(x, cos, sin):
    return x * cos + pltpu.roll(x, LANE // 2, axis=1) * sin


def _stack_heads(x, lane_lo):
    zero = jnp.zeros_like(x)
    return jnp.concatenate([jnp.where(lane_lo, x, zero), jnp.where(lane_lo, zero, x)], axis=0)


def _attend(qs, k_list, v_list, bias_list, lane_lo):
    nt = (((1,), (1,)), ((), ()))
    s_list = []
    for k, bias in zip(k_list, bias_list):
        s = lax.dot_general(qs, k, nt, preferred_element_type=F32)
        s_list.append(s if bias is None else s + bias)
    mx = s_list[0].max(axis=-1, keepdims=True)
    for s in s_list[1:]:
        mx = jnp.maximum(mx, s.max(axis=-1, keepdims=True))
    den = None
    acc = None
    for s, v in zip(s_list, v_list):
        p = jnp.exp(s - mx)
        d = p.sum(axis=-1, keepdims=True)
        o = jnp.dot(p.astype(BF16), v, preferred_element_type=F32)
        den = d if den is None else den + d
        acc = o if acc is None else acc + o
    acc = acc / den
    n = acc.shape[0] // 2
    return jnp.where(lane_lo, acc[:n], acc[n:])


def _na_kernel(q_ref, k_ref, v_ref, bias_ref, rowc_ref, rows_ref, colc_ref, cols_ref, qw_ref, kw_ref,
               o_ref, qn_ref, kn_ref, s_ref, m_ref, *, clen, n_rows):
    lane = lax.broadcasted_iota(jnp.int32, (1, LANE), 1)
    lane_lo = lane < NA_HEAD_DIM
    qk_h0 = (lane % NA_HEAD_DIM) < NA_HEAD_DIM // 2
    qw = qw_ref[...] * (NA_HEAD_DIM ** -0.5)
    kw = kw_ref[...]
    kr = NA_ROWS_MAX

    mean_mat = _head_mean_matrix()
    qn_ref[0:clen, :] = _head_rms(q_ref[0:clen, :].astype(F32), qw, mean_mat).astype(BF16)
    kn_ref[0:clen, :] = _head_rms(k_ref[0:clen, :].astype(F32), kw, mean_mat).astype(BF16)
    colc = colc_ref[...]
    cols = cols_ref[...]

    def prep(r, carry):
        rows = pl.ds(pl.multiple_of(clen + r * GRID_W, GRID_W), GRID_W)
        cos = rowc_ref[pl.ds(r, 1), :] + colc
        sin = rows_ref[pl.ds(r, 1), :] + cols
        q = _head_rms(q_ref[rows, :].astype(F32), qw, mean_mat)
        k = _head_rms(k_ref[rows, :].astype(F32), kw, mean_mat)
        qn_ref[rows, :] = _rope(q, cos, sin).astype(BF16)
        kn_ref[rows, :] = _rope(k, cos, sin).astype(BF16)
        return carry

    lax.fori_loop(0, n_rows, prep, 0, unroll=NA_PREP_UNROLL)

    kc = kn_ref[0:clen, :]
    vc = v_ref[0:clen, :]
    for blk in range(clen // GRID_W):
        rows = slice(blk * GRID_W, (blk + 1) * GRID_W)
        qs = _stack_heads(qn_ref[rows, :], qk_h0)
        o_ref[rows, :] = _attend(qs, [kc], [vc], [None], lane_lo).astype(o_ref.dtype)

    nt = (((1,), (1,)), ((), ()))
    n_blk = (NA_BAND + clen) // LANE

    def band_of(r):
        r0 = jnp.clip(r - kr // 2, 0, n_rows - kr)
        return r0, pl.ds(pl.multiple_of(clen + r0 * GRID_W, GRID_W), NA_BAND)

    n_win = NA_BAND // LANE
    sq = HEAD_PAIR * GRID_W

    def scores(rows_r, slots):
        qs = [_stack_heads(qn_ref[pl.ds(pl.multiple_of(clen + r * GRID_W, GRID_W), GRID_W), :], qk_h0)
              for r in rows_r]
        s_ctx = lax.dot_general(jnp.concatenate(qs, axis=0), kn_ref[0:clen, :], nt, preferred_element_type=F32)
        for u, (r, slot) in enumerate(zip(rows_r, slots)):
            r0, band = band_of(r)
            s_win = lax.dot_general(qs[u], kn_ref[band, :], nt, preferred_element_type=F32)
            bias = bias_ref[r0 - r + (NA_ROWS_MAX - 1)]
            blocks = [s_win[:, j * LANE:(j + 1) * LANE] + bias[:, j * LANE:(j + 1) * LANE] for j in range(n_win)]
            blocks += [s_ctx[u * sq:(u + 1) * sq, j * LANE:(j + 1) * LANE] for j in range(n_blk - n_win)]
            mx = blocks[0]
            for j in range(n_blk):
                s_ref[slot, :, j * LANE:(j + 1) * LANE] = blocks[j]
                mx = jnp.maximum(mx, blocks[j])
            m_ref[slot] = jnp.broadcast_to(mx.max(axis=-1, keepdims=True), mx.shape)

    def output(rows_r, slots):
        accs, dens, p_ctx = [], [], []
        for r, slot in zip(rows_r, slots):
            _, band = band_of(r)
            mx = m_ref[slot]
            p = [jnp.exp(s_ref[slot, :, j * LANE:(j + 1) * LANE] - mx) for j in range(n_blk)]
            den = p[0]
            for pj in p[1:]:
                den = den + pj
            dens.append(den.sum(axis=-1, keepdims=True))
            pb = [pj.astype(BF16) for pj in p]
            accs.append(jnp.dot(jnp.concatenate(pb[:n_win], axis=1), v_ref[band, :], preferred_element_type=F32))
            p_ctx.append(jnp.concatenate(pb[n_win:], axis=1))
        acc_ctx = jnp.dot(jnp.concatenate(p_ctx, axis=0), v_ref[0:clen, :], preferred_element_type=F32)
        for u, r in enumerate(rows_r):
            acc = (accs[u] + acc_ctx[u * sq:(u + 1) * sq]) / dens[u]
            rows = pl.ds(pl.multiple_of(clen + r * GRID_W, GRID_W), GRID_W)
            o_ref[rows, :] = jnp.where(lane_lo, acc[:GRID_W], acc[GRID_W:]).astype(o_ref.dtype)

    scores(list(range(NA_UNROLL)), list(range(NA_UNROLL)))

    def body(j, carry):
        r = 2 * NA_UNROLL * j
        for half in range(2):
            output([r + half * NA_UNROLL + u for u in range(NA_UNROLL)],
                   [half * NA_UNROLL + u for u in range(NA_UNROLL)])
            scores([jnp.minimum(r + (half + 1) * NA_UNROLL + u, n_rows - 1) for u in range(NA_UNROLL)],
                   [(1 - half) * NA_UNROLL + u for u in range(NA_UNROLL)])
        return carry

    lax.fori_loop(0, n_rows // (2 * NA_UNROLL), body, 0)


def _na_tables(rpb, n_rows):
    col = jnp.arange(GRID_W)
    col_start = jnp.clip(col - NA_COLS // 2, 0, GRID_W - NA_COLS)
    col_ok = (col[None, :] >= col_start[:, None]) & (col[None, :] < col_start[:, None] + NA_COLS)
    col_idx = jnp.clip(col[None, :] - col[:, None], -(NA_COLS - 1), NA_COLS - 1) + NA_COLS - 1
    rpb_c = jnp.where(col_ok[None, None], rpb.astype(F32)[:, :, col_idx], NEG_INF)
    dr = jnp.arange(NA_ROWS_MAX)[:, None] + jnp.arange(NA_ROWS_MAX)[None, :]
    tab = rpb_c[:, dr]
    tab = jnp.transpose(tab, (0, 1, 3, 2, 4)).reshape(NA_HEADS, NA_ROWS_MAX, GRID_W, NA_BAND)
    tab = tab.reshape(NA_HEADS // HEAD_PAIR, HEAD_PAIR, NA_ROWS_MAX, GRID_W, NA_BAND)
    tab = jnp.transpose(tab, (0, 2, 1, 3, 4)).reshape(NA_HEADS // HEAD_PAIR, NA_ROWS_MAX,
                                                       HEAD_PAIR * GRID_W, NA_BAND)
    quarter = NA_HEAD_DIM // 4
    inv = ROPE_BASE ** (-jnp.arange(quarter, dtype=F32) / quarter)

    def tables(pos, lo):
        ang = pos[:, None] * inv[None, :]
        z = jnp.zeros_like(ang)
        c, s = jnp.cos(ang), jnp.sin(ang)
        ch = jnp.concatenate([c, c, z, z] if lo else [z, z, c, c], axis=-1)
        sh = jnp.concatenate([-s, s, z, z] if lo else [z, z, -s, s], axis=-1)
        order = jnp.asarray(_qk_lane_order())
        return jnp.tile(ch, (1, HEAD_PAIR))[:, order], jnp.tile(sh, (1, HEAD_PAIR))[:, order]

    rowc, rows = tables(jnp.arange(n_rows).astype(F32), True)
    colc, cols = tables(jnp.arange(GRID_W).astype(F32), False)
    return tab, rowc, rows, colc, cols


def _na(pm3, q_norm_w, k_norm_w, rpb, clen):
    bsz, t, _ = pm3.shape
    n_rows = (t - clen) // GRID_W
    tab, rowc, rows, colc, cols = _na_tables(rpb, n_rows)
    order = jnp.asarray(_qk_lane_order())
    qw = jnp.tile(q_norm_w.astype(F32), HEAD_PAIR)[order].reshape(1, LANE)
    kw = jnp.tile(k_norm_w.astype(F32), HEAD_PAIR)[order].reshape(1, LANE)
    n_pairs = NA_HEADS // HEAD_PAIR
    const = lambda p, b: (0, 0)
    tok = lambda c0: pl.BlockSpec((None, t, LANE), lambda p, b: (b, 0, c0 // LANE + p))
    return pl.pallas_call(
        functools.partial(_na_kernel, clen=clen, n_rows=n_rows),
        grid=(n_pairs, bsz),
        in_specs=[tok(COL_Q), tok(COL_K), tok(COL_V),
                  pl.BlockSpec((None, NA_ROWS_MAX, HEAD_PAIR * GRID_W, NA_BAND), lambda p, b: (p, 0, 0, 0)),
                  pl.BlockSpec((n_rows, LANE), const), pl.BlockSpec((n_rows, LANE), const),
                  pl.BlockSpec((GRID_W, LANE), const), pl.BlockSpec((GRID_W, LANE), const),
                  pl.BlockSpec((1, LANE), const), pl.BlockSpec((1, LANE), const)],
        out_specs=pl.BlockSpec((None, t, LANE), lambda p, b: (b, 0, p)),
        out_shape=jax.ShapeDtypeStruct((bsz, t, D_NA), BF16),
        scratch_shapes=[pltpu.VMEM((t, LANE), BF16), pltpu.VMEM((t, LANE), BF16),
                        pltpu.VMEM((2 * NA_UNROLL, HEAD_PAIR * GRID_W, NA_BAND + clen), F32),
                        pltpu.VMEM((2 * NA_UNROLL, HEAD_PAIR * GRID_W, LANE), F32)],
        compiler_params=_params(("arbitrary", "arbitrary")),
        name="natten",
    )(pm3, pm3, pm3, tab, rowc, rows, colc, cols, qw, kw)


HALO = 16
CONV_ROWS = 2 * SSD_CHUNK
CONV_PAD = SSD_CONV // 2
BC_DIM = SSD_GROUPS * SSD_STATE
PAIRS_PER_GROUP = SSD_HPG // HEAD_PAIR


def _ssd_chunk_index(s, rev, cc, nc):
    if not rev:
        return s
    return jnp.where(s < cc, cc - 1 - s, nc - 1 - (s - cc))


def _ssd_conv_kernel(main_ref, prev_ref, next_ref, cw_ref, cb_ref, o_ref, *, cc, nc):
    q = SSD_CHUNK
    n_piece = main_ref.shape[0] // q
    cidx = pl.program_id(1)
    ti = lax.broadcasted_iota(jnp.int32, (q, q + 2 * HALO), 0)
    ji = lax.broadcasted_iota(jnp.int32, (q, q + 2 * HALO), 1)
    taps = [k for k in range(SSD_CONV) if k != CONV_PAD]
    exts, convs = [], []
    for p in range(n_piece):
        before = prev_ref[...] if p == 0 else main_ref[p * q - HALO:p * q, :]
        after = next_ref[...] if p == n_piece - 1 else main_ref[(p + 1) * q:(p + 1) * q + HALO, :]
        exts.append(jnp.concatenate([before, main_ref[p * q:(p + 1) * q, :], after], axis=0))
        convs.append(cb_ref[...] + cw_ref[CONV_PAD:CONV_PAD + 1, :] * main_ref[p * q:(p + 1) * q, :].astype(F32))
    for k in taps:
        for p in range(n_piece):
            first_col = jnp.where((cidx == 0) | (cidx == cc), HALO, 0) if p == 0 else 0
            end_col = (jnp.where((cidx == cc - 1) | (cidx == nc - 1), HALO + q, q + 2 * HALO)
                       if p == n_piece - 1 else q + 2 * HALO)
            in_segment = (ji >= first_col) & (ji < end_col)
            shift = jnp.where((ji == ti + (HALO + k - CONV_PAD)) & in_segment, 1.0, 0.0).astype(BF16)
            convs[p] = convs[p] + cw_ref[k:k + 1, :] * jnp.dot(shift, exts[p], preferred_element_type=F32)
    for p in range(n_piece):
        o_ref[p * q:(p + 1) * q, :] = (convs[p] * jax.nn.sigmoid(convs[p])).astype(o_ref.dtype)


def _ssd_decay(dt_ref, dtb_ref, a_ref, *, rev):
    q = SSD_CHUNK
    x = dt_ref[...] + dtb_ref[...]
    dt = jnp.maximum(x, 0.0) + jnp.log(1.0 + jnp.exp(-jnp.abs(x)))
    la = dt * a_ref[...]
    ri = lax.broadcasted_iota(jnp.int32, (q, q), 0)
    ci = lax.broadcasted_iota(jnp.int32, (q, q), 1)
    allowed = (ci >= ri) if rev else (ci <= ri)
    tri = jnp.where(allowed, 1.0, 0.0).astype(BF16)
    hi = la.astype(BF16)
    mid = (la - hi.astype(F32)).astype(BF16)
    lo = (la - hi.astype(F32) - mid.astype(F32)).astype(BF16)
    a_cs = jnp.dot(jnp.concatenate([tri, tri, tri], axis=1), jnp.concatenate([hi, mid, lo], axis=0),
                   preferred_element_type=F32)
    last = 0 if rev else q - 1
    a_cs_t = a_cs.T
    src_t = a_cs_t - jnp.log(dt.T)
    wdt_t = jnp.exp(a_cs_t[:, last:last + 1] - src_t)
    neg_mask = jnp.where(allowed, 0.0, NEG_INF)
    return a_cs, src_t, wdt_t, neg_mask


def _ssd_group(xbc, h_ref, yoff_ref, g):
    nt = (((1,), (1,)), ((), ()))
    b_g = xbc[:, D_SSD + g * SSD_STATE:D_SSD + (g + 1) * SSD_STATE]
    c_g = xbc[:, D_SSD + BC_DIM + g * SSD_STATE:D_SSD + BC_DIM + (g + 1) * SSD_STATE]
    cols = slice(g * SSD_HPG * SSD_HEAD_DIM, (g + 1) * SSD_HPG * SSD_HEAD_DIM)
    yoff_ref[:, cols] = jnp.dot(c_g, h_ref[:, cols].astype(BF16), preferred_element_type=F32)
    cb = lax.dot_general(c_g, b_g, nt, preferred_element_type=F32)
    return cb, b_g.astype(F32).T


def _ssd_pair(xbc, decay, cb, b_t, h_ref, yoff_ref, y_ref, pair, *, rev):
    q = SSD_CHUNK
    a_cs, src_t, wdt_t, neg_mask = decay
    hb = SSD_HEADS if rev else 0
    last = 0 if rev else q - 1
    lane_lo = lax.broadcasted_iota(jnp.int32, (1, LANE), 1) < SSD_HEAD_DIM
    lanes = slice(pair * LANE, (pair + 1) * LANE)
    lhs, ecol = [], []
    for hh in range(HEAD_PAIR):
        head = hb + pair * HEAD_PAIR + hh
        a_col = jnp.broadcast_to(a_cs[:, head:head + 1], (q, LANE))
        lhs.append((cb * jnp.exp((a_col - src_t[head:head + 1, :]) + neg_mask)).astype(BF16))
        ecol.append(jnp.exp(a_col))
    for hh in range(HEAD_PAIR):
        head = hb + pair * HEAD_PAIR + hh
        lhs.append((b_t * wdt_t[head:head + 1, :]).astype(BF16))
    res = jnp.dot(jnp.concatenate(lhs, axis=0), xbc[:, lanes], preferred_element_type=F32)
    e_pair = jnp.where(lane_lo, ecol[0], ecol[1])
    y_ref[:, lanes] = jnp.where(lane_lo, res[0:q], res[q:2 * q]) + yoff_ref[:, lanes] * e_pair
    h_ref[:, lanes] = (h_ref[:, lanes] * e_pair[last:last + 1, :]
                       + jnp.where(lane_lo, res[2 * q:3 * q], res[3 * q:4 * q]))


def _ssd_scan_kernel(xf_ref, dtf_ref, xb_ref, dtb_ref, bias_ref, a_ref, dsum_ref, of_ref, ob_ref,
                     hf_ref, yofff_ref, hb_ref, yoffb_ref, yf_ref, yb_ref):
    @pl.when(pl.program_id(1) == 0)
    def _():
        hf_ref[...] = jnp.zeros_like(hf_ref)
        hb_ref[...] = jnp.zeros_like(hb_ref)

    decay_f = _ssd_decay(dtf_ref, bias_ref, a_ref, rev=False)
    decay_b = _ssd_decay(dtb_ref, bias_ref, a_ref, rev=True)
    xf, xb = xf_ref[...], xb_ref[...]
    for g in range(SSD_GROUPS):
        cb_f, bt_f = _ssd_group(xf, hf_ref, yofff_ref, g)
        cb_b, bt_b = _ssd_group(xb, hb_ref, yoffb_ref, g)
        for pj in range(PAIRS_PER_GROUP):
            pair = g * PAIRS_PER_GROUP + pj
            _ssd_pair(xf, decay_f, cb_f, bt_f, hf_ref, yofff_ref, yf_ref, pair, rev=False)
            _ssd_pair(xb, decay_b, cb_b, bt_b, hb_ref, yoffb_ref, yb_ref, pair, rev=True)
    of_ref[...] = (yf_ref[...] + xf_ref[:, 0:D_SSD].astype(F32) * dsum_ref[...]).astype(of_ref.dtype)
    ob_ref[...] = yb_ref[...].astype(ob_ref.dtype)


def _ssd(pm3, pdt3, conv_w, conv_b, dt_bias, a_log, d_skip, clen):
    bsz, t, _ = pm3.shape
    q = SSD_CHUNK
    nc, cc = t // q, clen // q
    nh = 2 * SSD_HEADS
    cw = jnp.pad(conv_w.astype(F32), ((0, 8 - SSD_CONV), (0, 0)))
    cb = conv_b.astype(F32).reshape(1, XBC_DIM)
    dtb = jnp.pad(dt_bias.astype(F32).reshape(1, nh), ((0, 0), (0, DT_PAD - nh)))
    a_neg = jnp.pad(-jnp.exp(a_log.astype(F32)).reshape(1, nh), ((0, 0), (0, DT_PAD - nh)))
    dsum = jnp.repeat((d_skip[0] + d_skip[1]).astype(F32), SSD_HEAD_DIM).reshape(1, D_SSD)
    n_halo = t // HALO

    const = lambda b, s: (0, 0)
    xcol = COL_XBC // XBC_DIM
    cq = CONV_ROWS
    hpb = cq // HALO
    xbc = pl.pallas_call(
        functools.partial(_ssd_conv_kernel, cc=clen // cq, nc=t // cq),
        grid=(bsz, t // cq),
        in_specs=[pl.BlockSpec((None, cq, XBC_DIM), lambda b, s: (b, s, xcol)),
                  pl.BlockSpec((None, HALO, XBC_DIM), lambda b, s: (b, jnp.maximum(s * hpb - 1, 0), xcol)),
                  pl.BlockSpec((None, HALO, XBC_DIM),
                               lambda b, s: (b, jnp.minimum((s + 1) * hpb, n_halo - 1), xcol)),
                  pl.BlockSpec((8, XBC_DIM), const), pl.BlockSpec((1, XBC_DIM), const)],
        out_specs=pl.BlockSpec((None, cq, XBC_DIM), lambda b, s: (b, s, 0)),
        out_shape=jax.ShapeDtypeStruct((bsz, t, XBC_DIM), BF16),
        compiler_params=_params(("arbitrary", "arbitrary")),
        name="ssd_conv",
    )(pm3, pm3, pm3, cw, cb)

    tok_f = lambda b, s: (b, s, 0)
    tok_b = lambda b, s: (b, _ssd_chunk_index(s, True, cc, nc), 0)
    state = [pltpu.VMEM((SSD_STATE, D_SSD), F32), pltpu.VMEM((q, D_SSD), F32)]
    return pl.pallas_call(
        _ssd_scan_kernel,
        grid=(bsz, nc),
        in_specs=[pl.BlockSpec((None, q, XBC_DIM), tok_f), pl.BlockSpec((None, q, DT_PAD), tok_f),
                  pl.BlockSpec((None, q, XBC_DIM), tok_b), pl.BlockSpec((None, q, DT_PAD), tok_b),
                  pl.BlockSpec((1, DT_PAD), const), pl.BlockSpec((1, DT_PAD), const),
                  pl.BlockSpec((1, D_SSD), const)],
        out_specs=[pl.BlockSpec((None, q, D_SSD), tok_f), pl.BlockSpec((None, q, D_SSD), tok_b)],
        out_shape=[jax.ShapeDtypeStruct((bsz, t, D_SSD), BF16), jax.ShapeDtypeStruct((bsz, t, D_SSD), BF16)],
        scratch_shapes=state + state + [pltpu.VMEM((q, D_SSD), F32), pltpu.VMEM((q, D_SSD), F32)],
        compiler_params=_params(("arbitrary", "arbitrary")),
        name="ssd_scan",
    )(xbc, pdt3, xbc, pdt3, dtb, a_neg, dsum)


MOE_TB = 512
ROUTE_TM = 512
INFO_W = 8
BIG_NEG = -3.0e38
ROW_SEG = D_MODEL // LANE


def _load_tokens(ref, t0, n):
    return jnp.concatenate([ref[pl.ds(t0 * ROW_SEG + c, n, stride=ROW_SEG), :] for c in range(ROW_SEG)], axis=1)


def _store_tokens(ref, t0, val):
    n = val.shape[0]
    for c in range(ROW_SEG):
        ref[pl.ds(t0 * ROW_SEG + c, n, stride=ROW_SEG), :] = val[:, c * LANE:(c + 1) * LANE]


def _route_kernel(f_ref, wr_ref, info_ref, cnt_ref, run_ref):
    tm = f_ref.shape[0] // ROW_SEG

    @pl.when(pl.program_id(0) == 0)
    def _():
        run_ref[...] = jnp.zeros_like(run_ref)

    logits = jnp.dot(_load_tokens(f_ref, 0, tm).astype(BF16), wr_ref[...], preferred_element_type=F32)
    lane = lax.broadcasted_iota(jnp.int32, (tm, LANE), 1).astype(F32)
    lg = jnp.where(lane < N_EXPERTS, logits, BIG_NEG)
    m1 = lg.max(axis=-1, keepdims=True)
    i1 = jnp.where(lg == m1, lane, float(LANE)).min(axis=-1, keepdims=True)
    lg2 = jnp.where(lane == i1, BIG_NEG, lg)
    m2 = lg2.max(axis=-1, keepdims=True)
    i2 = jnp.where(lg2 == m2, lane, float(LANE)).min(axis=-1, keepdims=True)
    ex = jnp.exp(m2 - m1)
    g0 = 1.0 / (1.0 + ex)
    g1 = ex / (1.0 + ex)
    sel0 = lane == i1
    sel1 = lane == i2
    onehot = jnp.where(sel0 | sel1, 1.0, 0.0)
    ri = lax.broadcasted_iota(jnp.int32, (tm, tm), 0)
    ci = lax.broadcasted_iota(jnp.int32, (tm, tm), 1)
    strict_lower = jnp.where(ci < ri, 1.0, 0.0).astype(BF16)
    before = jnp.dot(strict_lower, onehot.astype(BF16), preferred_element_type=F32) + run_ref[...]
    pos0 = jnp.where(sel0, before, 0.0).sum(axis=-1, keepdims=True)
    pos1 = jnp.where(sel1, before, 0.0).sum(axis=-1, keepdims=True)
    run_ref[...] = run_ref[...] + onehot.sum(axis=0, keepdims=True)
    cnt_ref[...] = run_ref[...]
    info = jnp.where(lane == 0, i1, jnp.where(lane == 1, i2, jnp.where(lane == 2, g0, jnp.where(
        lane == 3, g1, jnp.where(lane == 4, pos0, jnp.where(lane == 5, pos1, 0.0))))))
    info_ref[...] = info[:, :INFO_W]


def _route(f8, w_router):
    m, d = f8.shape[0] // ROW_SEG, D_MODEL
    wr = jnp.pad(w_router.astype(BF16), ((0, 0), (0, LANE - N_EXPERTS)))
    return pl.pallas_call(
        _route_kernel,
        grid=(m // ROUTE_TM,),
        in_specs=[pl.BlockSpec((ROUTE_TM * ROW_SEG, LANE), lambda i: (i, 0)),
                  pl.BlockSpec((d, LANE), lambda i: (0, 0))],
        out_specs=[pl.BlockSpec((ROUTE_TM, INFO_W), lambda i: (i, 0)), pl.BlockSpec((1, LANE), lambda i: (0, 0))],
        out_shape=[jax.ShapeDtypeStruct((m, INFO_W), F32), jax.ShapeDtypeStruct((1, LANE), F32)],
        scratch_shapes=[pltpu.VMEM((1, LANE), F32)],
        compiler_params=_params(("arbitrary",)),
        name="moe_route",
    )(f8, wr)


def _row_copy(src_ref, src_row, dst_ref, dst_row, sem):
    src = src_ref.at[pl.ds(pl.multiple_of(src_row * ROW_SEG, ROW_SEG), ROW_SEG)]
    dst = dst_ref.at[pl.ds(pl.multiple_of(dst_row * ROW_SEG, ROW_SEG), ROW_SEG)]
    return pltpu.make_async_copy(src, dst, sem)


def _wait_rows(src_ref, dst_ref, sem, n):
    def wait(t, c):
        _row_copy(src_ref, 0, dst_ref, 0, sem).wait()
        return c

    lax.fori_loop(0, n, wait, 0, unroll=8)


def _dispatch_kernel(pad_ref, dest_hbm, f_ref, rows_hbm, dest_smem, stage_ref, zero_ref, idx_sem, row_sem,
                     pad_sem):
    tm = f_ref.shape[0] // ROW_SEG
    i, n = pl.program_id(0), pl.num_programs(0)
    slot = i % 2
    stage = stage_ref.at[slot]
    sem = row_sem.at[slot]

    @pl.when(i == 0)
    def _():
        zero_ref[...] = jnp.zeros_like(zero_ref)
        for e in range(N_EXPERTS):
            first, count = pad_ref[e], pad_ref[N_EXPERTS + e]

            def start_zero(j, c):
                _row_copy(zero_ref, 0, rows_hbm, first + j, pad_sem).start()
                return c

            lax.fori_loop(0, count, start_zero, 0)
        for e in range(N_EXPERTS):
            def wait_zero(j, c):
                _row_copy(zero_ref, 0, rows_hbm, 0, pad_sem).wait()
                return c

            lax.fori_loop(0, pad_ref[N_EXPERTS + e], wait_zero, 0)

    @pl.when(i >= 2)
    def _():
        _wait_rows(stage, rows_hbm, sem, TOP_K * tm)

    stage[...] = f_ref[...]
    base = pl.multiple_of(i * (TOP_K * tm), TOP_K * tm)
    idx_copy = pltpu.make_async_copy(dest_hbm.at[pl.ds(base, TOP_K * tm)], dest_smem, idx_sem)
    idx_copy.start()
    idx_copy.wait()

    def start(t, c):
        for k in range(TOP_K):
            _row_copy(stage, t, rows_hbm, dest_smem[TOP_K * t + k], sem).start()
        return c

    lax.fori_loop(0, tm, start, 0, unroll=8)

    @pl.when(i == n - 1)
    def _():
        _wait_rows(stage, rows_hbm, sem, TOP_K * tm)

        @pl.when(n >= 2)
        def _():
            _wait_rows(stage_ref.at[1 - slot], rows_hbm, row_sem.at[1 - slot], TOP_K * tm)


def _dispatch(f8, dest_flat, pad_info, n_rows):
    m = f8.shape[0] // ROW_SEG
    grid_spec = pltpu.PrefetchScalarGridSpec(
        num_scalar_prefetch=1,
        grid=(m // ROUTE_TM,),
        in_specs=[pl.BlockSpec(memory_space=pl.ANY),
                  pl.BlockSpec((ROUTE_TM * ROW_SEG, LANE), lambda i, pad: (i, 0))],
        out_specs=pl.BlockSpec(memory_space=pl.ANY),
        scratch_shapes=[pltpu.SMEM((TOP_K * ROUTE_TM,), jnp.int32),
                        pltpu.VMEM((2, ROUTE_TM * ROW_SEG, LANE), f8.dtype),
                        pltpu.VMEM((ROW_SEG, LANE), f8.dtype),
                        pltpu.SemaphoreType.DMA(()), pltpu.SemaphoreType.DMA((2,)), pltpu.SemaphoreType.DMA(())],
    )
    return pl.pallas_call(
        _dispatch_kernel,
        grid_spec=grid_spec,
        out_shape=jax.ShapeDtypeStruct((n_rows * ROW_SEG, LANE), f8.dtype),
        compiler_params=_params(("arbitrary",)),
        name="moe_dispatch",
    )(pad_info, dest_flat, f8)


def _gffn_kernel(be_ref, nu_ref, x_ref, w1_ref, w3_ref, w2_ref, y_ref):
    i = pl.program_id(0)

    @pl.when(i < nu_ref[0])
    def _():
        x = _load_tokens(x_ref, 0, MOE_TB).astype(BF16)
        h1 = jnp.dot(x, w1_ref[...], preferred_element_type=F32)
        h3 = jnp.dot(x, w3_ref[...], preferred_element_type=F32)
        act = (h1 * jax.nn.sigmoid(h1) * h3).astype(BF16)
        _store_tokens(y_ref, 0, jnp.dot(act, w2_ref[...], preferred_element_type=F32))

    @pl.when(i >= nu_ref[0])
    def _():
        y_ref[...] = jnp.zeros_like(y_ref)


def _gffn(rows, block_e, n_used, w1, w3, w2):
    r, d = rows.shape[0] // ROW_SEG, D_MODEL
    ff = w1.shape[2]
    nb = r // MOE_TB
    once = pl.Buffered(1)
    grid_spec = pltpu.PrefetchScalarGridSpec(
        num_scalar_prefetch=2,
        grid=(nb,),
        in_specs=[pl.BlockSpec((MOE_TB * ROW_SEG, LANE), lambda i, be, nu: (i, 0)),
                  pl.BlockSpec((None, d, ff), lambda i, be, nu: (be[i], 0, 0), pipeline_mode=once),
                  pl.BlockSpec((None, d, ff), lambda i, be, nu: (be[i], 0, 0), pipeline_mode=once),
                  pl.BlockSpec((None, ff, d), lambda i, be, nu: (be[i], 0, 0), pipeline_mode=once)],
        out_specs=pl.BlockSpec((MOE_TB * ROW_SEG, LANE), lambda i, be, nu: (i, 0)),
    )
    return pl.pallas_call(
        _gffn_kernel,
        grid_spec=grid_spec,
        out_shape=jax.ShapeDtypeStruct((r * ROW_SEG, LANE), F32),
        compiler_params=_params(("arbitrary",)),
        name="moe_ffn",
    )(block_e, n_used, rows, w1, w3, w2)


def _combine_kernel(dest_hbm, y_hbm, h_ref, info_ref, modt_ref, o_ref, dest_smem, buf_ref, idx_sem, row_sem,
                    *scratch, nsub, tiles_per_batch):
    tm = h_ref.shape[0]
    i, n = pl.program_id(0), pl.num_programs(0)
    latent_only = tiles_per_batch is not None
    if latent_only:
        obuf_ref, out_sem = scratch

        def latent_copies(step, buf_slot, fn):
            for s in range(nsub):
                g = step * nsub + s
                j = lax.rem(g, tiles_per_batch)

                @pl.when(j > 0)
                def _():
                    row0 = (lax.div(g, tiles_per_batch) * (tiles_per_batch - 1) + (j - 1)) * MOD_TILE
                    fn(pltpu.make_async_copy(obuf_ref.at[buf_slot, pl.ds(s * MOD_TILE, MOD_TILE)],
                                             o_ref.at[pl.ds(pl.multiple_of(row0, MOD_TILE), MOD_TILE)],
                                             out_sem.at[buf_slot]))

    def gather(step, slot):
        base = pl.multiple_of(step * (TOP_K * tm), TOP_K * tm)
        idx_copy = pltpu.make_async_copy(dest_hbm.at[pl.ds(base, TOP_K * tm)], dest_smem, idx_sem)
        idx_copy.start()
        idx_copy.wait()

        def start(t, c):
            for k in range(TOP_K):
                _row_copy(y_hbm, dest_smem[TOP_K * t + k], buf_ref.at[slot, k], t, row_sem.at[slot]).start()
            return c

        lax.fori_loop(0, tm, start, 0, unroll=8)

    @pl.when(i == 0)
    def _():
        gather(0, 0)

    @pl.when(i + 1 < n)
    def _():
        gather(i + 1, (i + 1) % 2)

    slot = i % 2
    _wait_rows(y_hbm, buf_ref.at[slot, 0], row_sem.at[slot], TOP_K * tm)
    if latent_only:
        @pl.when(i >= 2)
        def _():
            latent_copies(i - 2, slot, lambda cp: cp.wait())
    for s in range(nsub):
        r0, r1 = s * MOD_TILE, (s + 1) * MOD_TILE
        info = info_ref[r0:r1, :]
        mix = (info[:, 2:3] * _load_tokens(buf_ref.at[slot, 0], r0, MOD_TILE)
               + info[:, 3:4] * _load_tokens(buf_ref.at[slot, 1], r0, MOD_TILE))
        out = h_ref[r0:r1, :] + modt_ref[s, 5:6, :] * mix
        if latent_only:
            obuf_ref[slot, r0:r1, :] = out
        else:
            o_ref[r0:r1, :] = out
    if latent_only:
        latent_copies(i, slot, lambda cp: cp.start())

        @pl.when(i == n - 1)
        def _():
            latent_copies(i, slot, lambda cp: cp.wait())

            @pl.when(n >= 2)
            def _():
                latent_copies(i - 1, 1 - slot, lambda cp: cp.wait())


def _combine(y_rows, dest_flat, h, info, modt, tiles_per_batch):
    m, d = h.shape
    nsub = ROUTE_TM // MOD_TILE
    scratch = [pltpu.SMEM((TOP_K * ROUTE_TM,), jnp.int32), pltpu.VMEM((2, TOP_K, ROUTE_TM * ROW_SEG, LANE), F32),
               pltpu.SemaphoreType.DMA(()), pltpu.SemaphoreType.DMA((2,))]
    if tiles_per_batch is None:
        out_spec, out_rows = pl.BlockSpec((ROUTE_TM, d), lambda i: (i, 0)), m
    else:
        out_spec, out_rows = pl.BlockSpec(memory_space=pl.ANY), m - m // tiles_per_batch
        scratch += [pltpu.VMEM((2, ROUTE_TM, d), F32), pltpu.SemaphoreType.DMA((2,))]
    return pl.pallas_call(
        functools.partial(_combine_kernel, nsub=nsub, tiles_per_batch=tiles_per_batch),
        grid=(m // ROUTE_TM,),
        in_specs=[pl.BlockSpec(memory_space=pl.ANY), pl.BlockSpec(memory_space=pl.ANY),
                  pl.BlockSpec((ROUTE_TM, d), lambda i: (i, 0)),
                  pl.BlockSpec((ROUTE_TM, INFO_W), lambda i: (i, 0)),
                  pl.BlockSpec((nsub, 6, d), lambda i: (i, 0, 0))],
        out_specs=out_spec,
        out_shape=jax.ShapeDtypeStruct((out_rows, d), F32),
        scratch_shapes=scratch,
        compiler_params=_params(("arbitrary",)),
        name="moe_combine",
    )(dest_flat, y_rows, h, info, modt)


def _moe(f8, h, w_router, w1, w3, w2, modt, tiles_per_batch=None):
    m, d = h.shape
    info, cnt = _route(f8, w_router)
    e = info[:, 0:TOP_K].astype(jnp.int32)
    pos = info[:, 4:4 + TOP_K].astype(jnp.int32)
    counts = cnt[0, :N_EXPERTS].astype(jnp.int32)
    padded = (counts + MOE_TB - 1) // MOE_TB * MOE_TB
    pad_end = jnp.cumsum(padded)
    pad_start = pad_end - padded
    dest_flat = (pad_start[e] + pos).reshape(-1)
    n_rows = m * TOP_K + N_EXPERTS * MOE_TB
    nb = n_rows // MOE_TB
    block_e = jnp.minimum(jnp.searchsorted(pad_end, jnp.arange(nb, dtype=jnp.int32) * MOE_TB, side='right'),
                          N_EXPERTS - 1).astype(jnp.int32)
    n_used = (pad_end[-1:] // MOE_TB).astype(jnp.int32)
    pad_info = jnp.concatenate([pad_start + counts, padded - counts]).astype(jnp.int32)
    rows = _dispatch(f8, dest_flat, pad_info, n_rows)
    y_rows = _gffn(rows, block_e, n_used, w1.astype(BF16), w3.astype(BF16), w2.astype(BF16))
    return _combine(y_rows, dest_flat, h, info, modt, tiles_per_batch)


def _tile_mod(mod_l, bsz, tiles_per_batch):
    t = jnp.arange(bsz * tiles_per_batch)
    idx = jnp.where(t % tiles_per_batch == 0, bsz, t // tiles_per_batch)
    return jnp.transpose(mod_l[:, idx, :], (1, 0, 2))


def kernel(x, c, ctx, c_ctx, w_mod, b_mod, norm1_w, norm2_w, w_in, conv_w, conv_b, dt_bias, a_log, d_skip,
           ssd_norm_w, q_norm_w, k_norm_w, rpb, w_br_ssd, w_br_na, w_out, w_ff1, w_ff3, w_ff2, w_router,
           w_e1, w_e3, w_e2):
    bsz, seqlen, d = x.shape
    clen = ctx.shape[1]
    n_layers = w_mod.shape[0]
    assert clen == MOD_TILE and seqlen % MOD_TILE == 0 and bsz < 16
    t = clen + seqlen
    m = bsz * t
    tiles_per_batch = t // MOD_TILE

    cond = jnp.zeros((16, d), F32).at[:bsz].set(c).at[bsz].set(c_ctx)
    mod = _modulation(cond, w_mod, b_mod)
    modt = [_tile_mod(mod[i], bsz, tiles_per_batch) for i in range(n_layers)]

    h = jnp.concatenate([ctx, x], axis=1).reshape(m, d)
    a = _adaln(h, norm1_w[0], modt[0])
    for i in range(n_layers):
        last = i == n_layers - 1
        wi = w_in[i]
        w_main = jnp.concatenate([wi[:, :IN_SPLITS[3]], _qk_reorder(wi[:, IN_SPLITS[4]:IN_SPLITS[5]]),
                                  _qk_reorder(wi[:, IN_SPLITS[5]:IN_SPLITS[6]]), wi[:, IN_SPLITS[6]:]],
                                 axis=1).astype(BF16)
        w_dt = jnp.pad(wi[:, IN_SPLITS[3]:IN_SPLITS[4]], ((0, 0), (0, DT_PAD - 2 * SSD_HEADS))).astype(BF16)
        p_main = _matmul(a, w_main, BF16, 512, 2560, "inproj")
        p_dt = _matmul(a, w_dt, F32, 1024, DT_PAD, "inproj_dt")
        pm3 = p_main.reshape(bsz, t, MAIN_DIM)
        yf, yb = _ssd(pm3, p_dt.reshape(bsz, t, DT_PAD), conv_w[i], conv_b[i], dt_bias[i], a_log[i],
                      d_skip[i], clen)
        y_na = _na(pm3, q_norm_w[i], k_norm_w[i], rpb[i], clen).reshape(m, d)
        fi = i // 2
        moe_layer = i % 2 == 1
        h, f = _merge(p_main, yf.reshape(m, d), yb.reshape(m, d), y_na, h, w_br_ssd[i].astype(BF16),
                      w_br_na[i].astype(BF16), w_out[i].astype(BF16), ssd_norm_w[i], norm2_w[i], modt[i],
                      row_tile_out=moe_layer)
        nxt = min(i + 1, n_layers - 1)
        if not moe_layer:
            h, a = _ffn(f, h, w_ff1[fi].astype(BF16), w_ff3[fi].astype(BF16), w_ff2[fi].astype(BF16),
                        norm1_w[nxt], modt[i], modt[nxt])
        else:
            if last:
                return _moe(f, h, w_router[fi], w_e1[fi], w_e3[fi], w_e2[fi], modt[i],
                            tiles_per_batch).reshape(bsz, seqlen, d)
            h = _moe(f, h, w_router[fi], w_e1[fi], w_e3[fi], w_e2[fi], modt[i])
            a = _adaln(h, norm1_w[nxt], modt[nxt])
    return h.reshape(bsz, t, d)[:, clen:]
```

```python
import functools
import math

import jax
import jax.numpy as jnp
from jax import lax
from jax.experimental import pallas as pl
from jax.experimental.pallas import tpu as pltpu

D_MODEL = 1024
GRID_W = 64
SSD_HEAD_DIM = 64
SSD_HEADS = D_MODEL // SSD_HEAD_DIM
D_SSD = SSD_HEADS * SSD_HEAD_DIM
SSD_GROUPS = 2
SSD_HPG = SSD_HEADS // SSD_GROUPS
SSD_STATE = 128
SSD_CONV = 5
SSD_CHUNK = 128
XBC_DIM = D_SSD + 2 * SSD_GROUPS * SSD_STATE
NA_HEAD_DIM = 64
NA_HEADS = D_MODEL // NA_HEAD_DIM
D_NA = NA_HEADS * NA_HEAD_DIM
NA_ROWS_MAX = 8
NA_COLS = 16
ROPE_BASE = 10000.0
N_EXPERTS = 8
TOP_K = 2
EPS = 1e-6
NEG_INF = -1e30
IN_SPLITS = (D_MODEL, 2 * D_MODEL, 2 * D_MODEL + D_SSD, 2 * D_MODEL + D_SSD + XBC_DIM,
             2 * D_MODEL + D_SSD + XBC_DIM + 2 * SSD_HEADS,
             2 * D_MODEL + D_SSD + XBC_DIM + 2 * SSD_HEADS + D_NA,
             2 * D_MODEL + D_SSD + XBC_DIM + 2 * SSD_HEADS + 2 * D_NA)

F32 = jnp.float32
BF16 = jnp.bfloat16
HIGHEST = lax.Precision.HIGHEST

MOD_TILE = 256
LANE = 128
VMEM_LIMIT = 48 * 1024 * 1024
COL_GS, COL_GN, COL_Z, COL_XBC, COL_Q, COL_K, COL_V = 0, 1024, 2048, 3072, 4608, 5632, 6656
MAIN_DIM = 7680
DT_PAD = LANE


def _params(sem):
    return pltpu.CompilerParams(dimension_semantics=sem, vmem_limit_bytes=VMEM_LIMIT)


def _rms_mod(x, nw, shift, scale):
    ms = jnp.mean(x * x, axis=-1, keepdims=True)
    return (x * lax.rsqrt(ms + EPS) * nw) * (1.0 + scale) + shift


def _mod_kernel(cond_ref, w_ref, b_ref, o_ref):
    c = cond_ref[...]
    s = c * jax.nn.sigmoid(c)
    o_ref[...] = jnp.dot(s, w_ref[...], precision=HIGHEST, preferred_element_type=F32) + b_ref[...]


def _modulation(cond, w_mod, b_mod):
    n_layers, d, _ = w_mod.shape
    rows = cond.shape[0]
    b4 = b_mod.reshape(n_layers, 6, 1, d)
    return pl.pallas_call(
        _mod_kernel,
        grid=(n_layers, 6),
        in_specs=[pl.BlockSpec((rows, d), lambda l, k: (0, 0)),
                  pl.BlockSpec((None, d, d), lambda l, k: (l, 0, k)),
                  pl.BlockSpec((None, None, 1, d), lambda l, k: (l, k, 0, 0))],
        out_specs=pl.BlockSpec((None, None, rows, d), lambda l, k: (l, k, 0, 0)),
        out_shape=jax.ShapeDtypeStruct((n_layers, 6, rows, d), F32),
        compiler_params=_params(("arbitrary", "arbitrary")),
        name="modulation",
    )(cond, w_mod, b4)


def _adaln_kernel(h_ref, nw_ref, modt_ref, o_ref, *, nsub):
    nw = nw_ref[...]
    for s in range(nsub):
        rows = pl.ds(s * MOD_TILE, MOD_TILE)
        x = h_ref[rows, :]
        o_ref[rows, :] = _rms_mod(x, nw, modt_ref[s, 0:1, :], modt_ref[s, 1:2, :]).astype(o_ref.dtype)


def _adaln(h, nw, modt, tm=1024):
    m, d = h.shape
    nsub = tm // MOD_TILE
    return pl.pallas_call(
        functools.partial(_adaln_kernel, nsub=nsub),
        grid=(m // tm,),
        in_specs=[pl.BlockSpec((tm, d), lambda i: (i, 0)),
                  pl.BlockSpec((1, d), lambda i: (0, 0)),
                  pl.BlockSpec((nsub, 6, d), lambda i: (i, 0, 0))],
        out_specs=pl.BlockSpec((tm, d), lambda i: (i, 0)),
        out_shape=jax.ShapeDtypeStruct((m, d), BF16),
        compiler_params=_params(("arbitrary",)),
        name="adaln",
    )(h, nw.reshape(1, d), modt)


def _matmul_kernel(a_ref, w_ref, o_ref):
    o_ref[...] = jnp.dot(a_ref[...], w_ref[...], preferred_element_type=F32).astype(o_ref.dtype)


def _matmul(a, w, out_dtype, tm, tn, name):
    m, k = a.shape
    n = w.shape[1]
    return pl.pallas_call(
        _matmul_kernel,
        grid=(n // tn, m // tm),
        in_specs=[pl.BlockSpec((tm, k), lambda j, i: (i, 0)),
                  pl.BlockSpec((k, tn), lambda j, i: (0, j))],
        out_specs=pl.BlockSpec((tm, tn), lambda j, i: (i, j)),
        out_shape=jax.ShapeDtypeStruct((m, n), out_dtype),
        compiler_params=_params(("arbitrary", "arbitrary")),
        name=name,
    )(a, w)


def _merge_kernel(gs_ref, gn_ref, z_ref, yf_ref, yb_ref, yn_ref, h_ref, wbs_ref, wbn_ref, wo_ref, snw_ref,
                  nw_ref, modt_ref, ho_ref, f_ref, *, nsub, row_tile_out):
    gw = D_SSD // SSD_GROUPS
    nw = nw_ref[...]
    sub = MOD_TILE
    n_chain = nsub
    ys, mixed = {}, {}

    def gated_norm(c):
        rows = slice(c * sub, (c + 1) * sub)
        z = z_ref[rows, :].astype(F32)
        u = (yf_ref[rows, :].astype(F32) + yb_ref[rows, :].astype(F32)) * (z * jax.nn.sigmoid(z))
        parts = []
        for g in range(SSD_GROUPS):
            ug = u[:, g * gw:(g + 1) * gw]
            ms = jnp.mean(ug * ug, axis=-1, keepdims=True)
            parts.append((ug * lax.rsqrt(ms + EPS) * snw_ref[:, g * gw:(g + 1) * gw]).astype(BF16))
        ys[c] = jnp.concatenate(parts, axis=1)

    def branches(c):
        rows = slice(c * sub, (c + 1) * sub)
        a = jnp.dot(ys.pop(c), wbs_ref[...], preferred_element_type=F32)
        b = jnp.dot(yn_ref[rows, :], wbn_ref[...], preferred_element_type=F32)
        mixed[c] = (jax.nn.sigmoid(gs_ref[rows, :].astype(F32)) * a
                    + jax.nn.sigmoid(gn_ref[rows, :].astype(F32)) * b).astype(BF16)

    def project(c):
        r0, s = c * sub, c
        m = jnp.dot(mixed.pop(c), wo_ref[...], preferred_element_type=F32)
        h = h_ref[r0:r0 + sub, :] + modt_ref[s, 2:3, :] * m
        ho_ref[r0:r0 + sub, :] = h
        f = _rms_mod(h, nw, modt_ref[s, 3:4, :], modt_ref[s, 4:5, :])
        if row_tile_out:
            _store_tokens(f_ref, r0, f)
        else:
            f_ref[r0:r0 + sub, :] = f.astype(f_ref.dtype)

    for k in range(n_chain + 2):
        if k < n_chain:
            gated_norm(k)
        if 0 <= k - 1 < n_chain:
            branches(k - 1)
        if 0 <= k - 2 < n_chain:
            project(k - 2)


def _merge(p_main, yf, yb, y_na, h, wbs, wbn, wo, ssd_nw, nw2, modt, row_tile_out, tm=512):
    m, d = h.shape
    nsub = tm // MOD_TILE
    row = lambda i: (i, 0)
    const = lambda i: (0, 0)
    if row_tile_out:
        f_spec, f_shape = pl.BlockSpec((tm * ROW_SEG, LANE), row), jax.ShapeDtypeStruct((m * ROW_SEG, LANE), F32)
    else:
        f_spec, f_shape = pl.BlockSpec((tm, d), row), jax.ShapeDtypeStruct((m, d), BF16)
    return pl.pallas_call(
        functools.partial(_merge_kernel, nsub=nsub, row_tile_out=row_tile_out),
        grid=(m // tm,),
        in_specs=[pl.BlockSpec((tm, d), lambda i: (i, COL_GS // D_MODEL)),
                  pl.BlockSpec((tm, d), lambda i: (i, COL_GN // D_MODEL)),
                  pl.BlockSpec((tm, d), lambda i: (i, COL_Z // D_MODEL)),
                  pl.BlockSpec((tm, d), row), pl.BlockSpec((tm, d), row), pl.BlockSpec((tm, d), row),
                  pl.BlockSpec((tm, d), row),
                  pl.BlockSpec((d, d), const), pl.BlockSpec((d, d), const), pl.BlockSpec((d, d), const),
                  pl.BlockSpec((1, d), const), pl.BlockSpec((1, d), const),
                  pl.BlockSpec((nsub, 6, d), lambda i: (i, 0, 0))],
        out_specs=[pl.BlockSpec((tm, d), row), f_spec],
        out_shape=[jax.ShapeDtypeStruct((m, d), F32), f_shape],
        compiler_params=_params(("arbitrary",)),
        name="merge",
    )(p_main, p_main, p_main, yf, yb, y_na, h, wbs, wbn, wo, ssd_nw.astype(F32).reshape(1, d), nw2.reshape(1, d),
      modt)


def _ffn_kernel(x_ref, h_ref, w1_ref, w3_ref, w2_ref, nw_ref, modt_ref, modn_ref, ho_ref, an_ref, *, nsub):
    x = x_ref[...]
    h1 = jnp.dot(x, w1_ref[...], preferred_element_type=F32)
    h3 = jnp.dot(x, w3_ref[...], preferred_element_type=F32)
    act = (h1 * jax.nn.sigmoid(h1) * h3).astype(BF16)
    out = jnp.dot(act, w2_ref[...], preferred_element_type=F32)
    nw = nw_ref[...]
    for s in range(nsub):
        r0, r1 = s * MOD_TILE, (s + 1) * MOD_TILE
        h = h_ref[r0:r1, :] + modt_ref[s, 5:6, :] * out[r0:r1, :]
        ho_ref[r0:r1, :] = h
        an_ref[r0:r1, :] = _rms_mod(h, nw, modn_ref[s, 0:1, :], modn_ref[s, 1:2, :]).astype(an_ref.dtype)


def _ffn(x, h, w1, w3, w2, nw_next, modt, modt_next, tm=512):
    m, d = h.shape
    ff = w1.shape[1]
    nsub = tm // MOD_TILE
    row = lambda i: (i, 0)
    const = lambda i: (0, 0)
    once = pl.Buffered(1)
    return pl.pallas_call(
        functools.partial(_ffn_kernel, nsub=nsub),
        grid=(m // tm,),
        in_specs=[pl.BlockSpec((tm, d), row), pl.BlockSpec((tm, d), row),
                  pl.BlockSpec((d, ff), const, pipeline_mode=once),
                  pl.BlockSpec((d, ff), const, pipeline_mode=once),
                  pl.BlockSpec((ff, d), const, pipeline_mode=once),
                  pl.BlockSpec((1, d), const),
                  pl.BlockSpec((nsub, 6, d), lambda i: (i, 0, 0)),
                  pl.BlockSpec((nsub, 6, d), lambda i: (i, 0, 0))],
        out_specs=[pl.BlockSpec((tm, d), row), pl.BlockSpec((tm, d), row)],
        out_shape=[jax.ShapeDtypeStruct((m, d), F32), jax.ShapeDtypeStruct((m, d), BF16)],
        compiler_params=_params(("arbitrary",)),
        name="ffn",
    )(x, h, w1, w3, w2, nw_next.reshape(1, d), modt, modt_next)


NA_BAND = NA_ROWS_MAX * GRID_W
HEAD_PAIR = LANE // NA_HEAD_DIM
NA_UNROLL = 8
NA_PREP_UNROLL = 8


def _head_mean_matrix():
    ri = lax.broadcasted_iota(jnp.int32, (2 * LANE, LANE), 0) % LANE
    ci = lax.broadcasted_iota(jnp.int32, (2 * LANE, LANE), 1)
    half = NA_HEAD_DIM // 2
    same_head = (ri % NA_HEAD_DIM) // half == (ci % NA_HEAD_DIM) // half
    return jnp.where(same_head, 1.0 / NA_HEAD_DIM, 0.0).astype(BF16)


def _head_rms(x, w, mean_mat):
    sq = x * x
    hi = sq.astype(BF16)
    lo = (sq - hi.astype(F32)).astype(BF16)
    ms = jnp.dot(jnp.concatenate([hi, lo], axis=1), mean_mat, preferred_element_type=F32)
    return x * lax.rsqrt(ms + EPS) * w


def _qk_lane_order():
    quarter = NA_HEAD_DIM // 4
    order = []
    for half in range(2):
        for head in range(HEAD_PAIR):
            for part in range(2):
                start = head * NA_HEAD_DIM + part * 2 * quarter + half * quarter
                order += list(range(start, start + quarter))
    return order


def _qk_reorder(w):
    k, n = w.shape
    return w.reshape(k, n // LANE, LANE)[:, :, jnp.asarray(_qk_lane_order())].reshape(k, n)


def _rope(x, cos, sin):
    return x * cos + pltpu.roll(x, LANE // 2, axis=1) * sin


def _stack_heads(x, lane_lo):
    zero = jnp.zeros_like(x)
    return jnp.concatenate([jnp.where(lane_lo, x, zero), jnp.where(lane_lo, zero, x)], axis=0)


def _attend(qs, k_list, v_list, bias_list, lane_lo):
    nt = (((1,), (1,)), ((), ()))
    s_list = []
    for k, bias in zip(k_list, bias_list):
        s = lax.dot_general(qs, k, nt, preferred_element_type=F32)
        s_list.append(s if bias is None else s + bias)
    mx = s_list[0].max(axis=-1, keepdims=True)
    for s in s_list[1:]:
        mx = jnp.maximum(mx, s.max(axis=-1, keepdims=True))
    den = None
    acc = None
    for s, v in zip(s_list, v_list):
        p = jnp.exp(s - mx)
        d = p.sum(axis=-1, keepdims=True)
        o = jnp.dot(p.astype(BF16), v, preferred_element_type=F32)
        den = d if den is None else den + d
        acc = o if acc is None else acc + o
    acc = acc / den
    n = acc.shape[0] // 2
    return jnp.where(lane_lo, acc[:n], acc[n:])


def _na_kernel(q_ref, k_ref, v_ref, bias_ref, rowc_ref, rows_ref, colc_ref, cols_ref, qw_ref, kw_ref,
               o_ref, qn_ref, kn_ref, s_ref, m_ref, *, clen, n_rows):
    lane = lax.broadcasted_iota(jnp.int32, (1, LANE), 1)
    lane_lo = lane < NA_HEAD_DIM
    qk_h0 = (lane % NA_HEAD_DIM) < NA_HEAD_DIM // 2
    qw = qw_ref[...] * (NA_HEAD_DIM ** -0.5)
    kw = kw_ref[...]
    kr = NA_ROWS_MAX

    mean_mat = _head_mean_matrix()
    qn_ref[0:clen, :] = _head_rms(q_ref[0:clen, :].astype(F32), qw, mean_mat).astype(BF16)
    kn_ref[0:clen, :] = _head_rms(k_ref[0:clen, :].astype(F32), kw, mean_mat).astype(BF16)
    colc = colc_ref[...]
    cols = cols_ref[...]

    def prep(r, carry):
        rows = pl.ds(pl.multiple_of(clen + r * GRID_W, GRID_W), GRID_W)
        cos = rowc_ref[pl.ds(r, 1), :] + colc
        sin = rows_ref[pl.ds(r, 1), :] + cols
        q = _head_rms(q_ref[rows, :].astype(F32), qw, mean_mat)
        k = _head_rms(k_ref[rows, :].astype(F32), kw, mean_mat)
        qn_ref[rows, :] = _rope(q, cos, sin).astype(BF16)
        kn_ref[rows, :] = _rope(k, cos, sin).astype(BF16)
        return carry

    lax.fori_loop(0, n_rows, prep, 0, unroll=NA_PREP_UNROLL)

    kc = kn_ref[0:clen, :]
    vc = v_ref[0:clen, :]
    for blk in range(clen // GRID_W):
        rows = slice(blk * GRID_W, (blk + 1) * GRID_W)
        qs = _stack_heads(qn_ref[rows, :], qk_h0)
        o_ref[rows, :] = _attend(qs, [kc], [vc], [None], lane_lo).astype(o_ref.dtype)

    nt = (((1,), (1,)), ((), ()))
    n_blk = (NA_BAND + clen) // LANE

    def band_of(r):
        r0 = jnp.clip(r - kr // 2, 0, n_rows - kr)
        return r0, pl.ds(pl.multiple_of(clen + r0 * GRID_W, GRID_W), NA_BAND)

    n_win = NA_BAND // LANE
    sq = HEAD_PAIR * GRID_W

    def scores(rows_r, slots):
        qs = [_stack_heads(qn_ref[pl.ds(pl.multiple_of(clen + r * GRID_W, GRID_W), GRID_W), :], qk_h0)
              for r in rows_r]
        s_ctx = lax.dot_general(jnp.concatenate(qs, axis=0), kn_ref[0:clen, :], nt, preferred_element_type=F32)
        for u, (r, slot) in enumerate(zip(rows_r, slots)):
            r0, band = band_of(r)
            s_win = lax.dot_general(qs[u], kn_ref[band, :], nt, preferred_element_type=F32)
            bias = bias_ref[r0 - r + (NA_ROWS_MAX - 1)]
            blocks = [s_win[:, j * LANE:(j + 1) * LANE] + bias[:, j * LANE:(j + 1) * LANE] for j in range(n_win)]
            blocks += [s_ctx[u * sq:(u + 1) * sq, j * LANE:(j + 1) * LANE] for j in range(n_blk - n_win)]
            mx = blocks[0]
            for j in range(n_blk):
                s_ref[slot, :, j * LANE:(j + 1) * LANE] = blocks[j]
                mx = jnp.maximum(mx, blocks[j])
            m_ref[slot] = jnp.broadcast_to(mx.max(axis=-1, keepdims=True), mx.shape)

    def output(rows_r, slots):
        accs, dens, p_ctx = [], [], []
        for r, slot in zip(rows_r, slots):
            _, band = band_of(r)
            mx = m_ref[slot]
            p = [jnp.exp(s_ref[slot, :, j * LANE:(j + 1) * LANE] - mx) for j in range(n_blk)]
            den = p[0]
            for pj in p[1:]:
                den = den + pj
            dens.append(den.sum(axis=-1, keepdims=True))
            pb = [pj.astype(BF16) for pj in p]
            accs.append(jnp.dot(jnp.concatenate(pb[:n_win], axis=1), v_ref[band, :], preferred_element_type=F32))
            p_ctx.append(jnp.concatenate(pb[n_win:], axis=1))
        acc_ctx = jnp.dot(jnp.concatenate(p_ctx, axis=0), v_ref[0:clen, :], preferred_element_type=F32)
        for u, r in enumerate(rows_r):
            acc = (accs[u] + acc_ctx[u * sq:(u + 1) * sq]) / dens[u]
            rows = pl.ds(pl.multiple_of(clen + r * GRID_W, GRID_W), GRID_W)
            o_ref[rows, :] = jnp.where(lane_lo, acc[:GRID_W], acc[GRID_W:]).astype(o_ref.dtype)

    scores(list(range(NA_UNROLL)), list(range(NA_UNROLL)))

    def body(j, carry):
        r = 2 * NA_UNROLL * j
        for half in range(2):
            output([r + half * NA_UNROLL + u for u in range(NA_UNROLL)],
                   [half * NA_UNROLL + u for u in range(NA_UNROLL)])
            scores([jnp.minimum(r + (half + 1) * NA_UNROLL + u, n_rows - 1) for u in range(NA_UNROLL)],
                   [(1 - half) * NA_UNROLL + u for u in range(NA_UNROLL)])
        return carry

    lax.fori_loop(0, n_rows // (2 * NA_UNROLL), body, 0)


def _na_tables(rpb, n_rows):
    col = jnp.arange(GRID_W)
    col_start = jnp.clip(col - NA_COLS // 2, 0, GRID_W - NA_COLS)
    col_ok = (col[None, :] >= col_start[:, None]) & (col[None, :] < col_start[:, None] + NA_COLS)
    col_idx = jnp.clip(col[None, :] - col[:, None], -(NA_COLS - 1), NA_COLS - 1) + NA_COLS - 1
    rpb_c = jnp.where(col_ok[None, None], rpb.astype(F32)[:, :, col_idx], NEG_INF)
    dr = jnp.arange(NA_ROWS_MAX)[:, None] + jnp.arange(NA_ROWS_MAX)[None, :]
    tab = rpb_c[:, dr]
    tab = jnp.transpose(tab, (0, 1, 3, 2, 4)).reshape(NA_HEADS, NA_ROWS_MAX, GRID_W, NA_BAND)
    tab = tab.reshape(NA_HEADS // HEAD_PAIR, HEAD_PAIR, NA_ROWS_MAX, GRID_W, NA_BAND)
    tab = jnp.transpose(tab, (0, 2, 1, 3, 4)).reshape(NA_HEADS // HEAD_PAIR, NA_ROWS_MAX,
                                                       HEAD_PAIR * GRID_W, NA_BAND)
    quarter = NA_HEAD_DIM // 4
    inv = ROPE_BASE ** (-jnp.arange(quarter, dtype=F32) / quarter)

    def tables(pos, lo):
        ang = pos[:, None] * inv[None, :]
        z = jnp.zeros_like(ang)
        c, s = jnp.cos(ang), jnp.sin(ang)
        ch = jnp.concatenate([c, c, z, z] if lo else [z, z, c, c], axis=-1)
        sh = jnp.concatenate([-s, s, z, z] if lo else [z, z, -s, s], axis=-1)
        order = jnp.asarray(_qk_lane_order())
        return jnp.tile(ch, (1, HEAD_PAIR))[:, order], jnp.tile(sh, (1, HEAD_PAIR))[:, order]

    rowc, rows = tables(jnp.arange(n_rows).astype(F32), True)
    colc, cols = tables(jnp.arange(GRID_W).astype(F32), False)
    return tab, rowc, rows, colc, cols


def _na(pm3, q_norm_w, k_norm_w, rpb, clen):
    bsz, t, _ = pm3.shape
    n_rows = (t - clen) // GRID_W
    tab, rowc, rows, colc, cols = _na_tables(rpb, n_rows)
    order = jnp.asarray(_qk_lane_order())
    qw = jnp.tile(q_norm_w.astype(F32), HEAD_PAIR)[order].reshape(1, LANE)
    kw = jnp.tile(k_norm_w.astype(F32), HEAD_PAIR)[order].reshape(1, LANE)
    n_pairs = NA_HEADS // HEAD_PAIR
    const = lambda p, b: (0, 0)
    tok = lambda c0: pl.BlockSpec((None, t, LANE), lambda p, b: (b, 0, c0 // LANE + p))
    return pl.pallas_call(
        functools.partial(_na_kernel, clen=clen, n_rows=n_rows),
        grid=(n_pairs, bsz),
        in_specs=[tok(COL_Q), tok(COL_K), tok(COL_V),
                  pl.BlockSpec((None, NA_ROWS_MAX, HEAD_PAIR * GRID_W, NA_BAND), lambda p, b: (p, 0, 0, 0)),
                  pl.BlockSpec((n_rows, LANE), const), pl.BlockSpec((n_rows, LANE), const),
                  pl.BlockSpec((GRID_W, LANE), const), pl.BlockSpec((GRID_W, LANE), const),
                  pl.BlockSpec((1, LANE), const), pl.BlockSpec((1, LANE), const)],
        out_specs=pl.BlockSpec((None, t, LANE), lambda p, b: (b, 0, p)),
        out_shape=jax.ShapeDtypeStruct((bsz, t, D_NA), BF16),
        scratch_shapes=[pltpu.VMEM((t, LANE), BF16), pltpu.VMEM((t, LANE), BF16),
                        pltpu.VMEM((2 * NA_UNROLL, HEAD_PAIR * GRID_W, NA_BAND + clen), F32),
                        pltpu.VMEM((2 * NA_UNROLL, HEAD_PAIR * GRID_W, LANE), F32)],
        compiler_params=_params(("arbitrary", "arbitrary")),
        name="natten",
    )(pm3, pm3, pm3, tab, rowc, rows, colc, cols, qw, kw)


HALO = 16
CONV_ROWS = 2 * SSD_CHUNK
CONV_PAD = SSD_CONV // 2
BC_DIM = SSD_GROUPS * SSD_STATE
PAIRS_PER_GROUP = SSD_HPG // HEAD_PAIR


def _ssd_chunk_index(s, rev, cc, nc):
    if not rev:
        return s
    return jnp.where(s < cc, cc - 1 - s, nc - 1 - (s - cc))


def _ssd_conv_kernel(main_ref, prev_ref, next_ref, cw_ref, cb_ref, o_ref, *, cc, nc):
    q = SSD_CHUNK
    n_piece = main_ref.shape[0] // q
    cidx = pl.program_id(1)
    ti = lax.broadcasted_iota(jnp.int32, (q, q + 2 * HALO), 0)
    ji = lax.broadcasted_iota(jnp.int32, (q, q + 2 * HALO), 1)
    taps = [k for k in range(SSD_CONV) if k != CONV_PAD]
    exts, convs = [], []
    for p in range(n_piece):
        before = prev_ref[...] if p == 0 else main_ref[p * q - HALO:p * q, :]
        after = next_ref[...] if p == n_piece - 1 else main_ref[(p + 1) * q:(p + 1) * q + HALO, :]
        exts.append(jnp.concatenate([before, main_ref[p * q:(p + 1) * q, :], after], axis=0))
        convs.append(cb_ref[...] + cw_ref[CONV_PAD:CONV_PAD + 1, :] * main_ref[p * q:(p + 1) * q, :].astype(F32))
    for k in taps:
        for p in range(n_piece):
            first_col = jnp.where((cidx == 0) | (cidx == cc), HALO, 0) if p == 0 else 0
            end_col = (jnp.where((cidx == cc - 1) | (cidx == nc - 1), HALO + q, q + 2 * HALO)
                       if p == n_piece - 1 else q + 2 * HALO)
            in_segment = (ji >= first_col) & (ji < end_col)
            shift = jnp.where((ji == ti + (HALO + k - CONV_PAD)) & in_segment, 1.0, 0.0).astype(BF16)
            convs[p] = convs[p] + cw_ref[k:k + 1, :] * jnp.dot(shift, exts[p], preferred_element_type=F32)
    for p in range(n_piece):
        o_ref[p * q:(p + 1) * q, :] = (convs[p] * jax.nn.sigmoid(convs[p])).astype(o_ref.dtype)


def _ssd_decay(dt_ref, dtb_ref, a_ref, *, rev):
    q = SSD_CHUNK
    x = dt_ref[...] + dtb_ref[...]
    dt = jnp.maximum(x, 0.0) + jnp.log(1.0 + jnp.exp(-jnp.abs(x)))
    la = dt * a_ref[...]
    ri = lax.broadcasted_iota(jnp.int32, (q, q), 0)
    ci = lax.broadcasted_iota(jnp.int32, (q, q), 1)
    allowed = (ci >= ri) if rev else (ci <= ri)
    tri = jnp.where(allowed, 1.0, 0.0).astype(BF16)
    hi = la.astype(BF16)
    mid = (la - hi.astype(F32)).astype(BF16)
    lo = (la - hi.astype(F32) - mid.astype(F32)).astype(BF16)
    a_cs = jnp.dot(jnp.concatenate([tri, tri, tri], axis=1), jnp.concatenate([hi, mid, lo], axis=0),
                   preferred_element_type=F32)
    last = 0 if rev else q - 1
    a_cs_t = a_cs.T
    src_t = a_cs_t - jnp.log(dt.T)
    wdt_t = jnp.exp(a_cs_t[:, last:last + 1] - src_t)
    neg_mask = jnp.where(allowed, 0.0, NEG_INF)
    return a_cs, src_t, wdt_t, neg_mask


def _ssd_group(xbc, h_ref, yoff_ref, g):
    nt = (((1,), (1,)), ((), ()))
    b_g = xbc[:, D_SSD + g * SSD_STATE:D_SSD + (g + 1) * SSD_STATE]
    c_g = xbc[:, D_SSD + BC_DIM + g * SSD_STATE:D_SSD + BC_DIM + (g + 1) * SSD_STATE]
    cols = slice(g * SSD_HPG * SSD_HEAD_DIM, (g + 1) * SSD_HPG * SSD_HEAD_DIM)
    yoff_ref[:, cols] = jnp.dot(c_g, h_ref[:, cols].astype(BF16), preferred_element_type=F32)
    cb = lax.dot_general(c_g, b_g, nt, preferred_element_type=F32)
    return cb, b_g.astype(F32).T


def _ssd_pair(xbc, decay, cb, b_t, h_ref, yoff_ref, y_ref, pair, *, rev):
    q = SSD_CHUNK
    a_cs, src_t, wdt_t, neg_mask = decay
    hb = SSD_HEADS if rev else 0
    last = 0 if rev else q - 1
    lane_lo = lax.broadcasted_iota(jnp.int32, (1, LANE), 1) < SSD_HEAD_DIM
    lanes = slice(pair * LANE, (pair + 1) * LANE)
    lhs, ecol = [], []
    for hh in range(HEAD_PAIR):
        head = hb + pair * HEAD_PAIR + hh
        a_col = jnp.broadcast_to(a_cs[:, head:head + 1], (q, LANE))
        lhs.append((cb * jnp.exp((a_col - src_t[head:head + 1, :]) + neg_mask)).astype(BF16))
        ecol.append(a_col)
    for hh in range(HEAD_PAIR):
        head = hb + pair * HEAD_PAIR + hh
        lhs.append((b_t * wdt_t[head:head + 1, :]).astype(BF16))
    res = jnp.dot(jnp.concatenate(lhs, axis=0), xbc[:, lanes], preferred_element_type=F32)
    e_pair = jnp.exp(jnp.where(lane_lo, ecol[0], ecol[1]))
    y_ref[:, lanes] = jnp.where(lane_lo, res[0:q], res[q:2 * q]) + yoff_ref[:, lanes] * e_pair
    h_ref[:, lanes] = (h_ref[:, lanes] * e_pair[last:last + 1, :]
                       + jnp.where(lane_lo, res[2 * q:3 * q], res[3 * q:4 * q]))


def _ssd_scan_kernel(xf_ref, dtf_ref, xb_ref, dtb_ref, bias_ref, a_ref, dsum_ref, of_ref, ob_ref,
                     hf_ref, yofff_ref, hb_ref, yoffb_ref, yf_ref, yb_ref):
    @pl.when(pl.program_id(1) == 0)
    def _():
        hf_ref[...] = jnp.zeros_like(hf_ref)
        hb_ref[...] = jnp.zeros_like(hb_ref)

    decay_f = _ssd_decay(dtf_ref, bias_ref, a_ref, rev=False)
    decay_b = _ssd_decay(dtb_ref, bias_ref, a_ref, rev=True)
    xf, xb = xf_ref[...], xb_ref[...]
    for g in range(SSD_GROUPS):
        cb_f, bt_f = _ssd_group(xf, hf_ref, yofff_ref, g)
        cb_b, bt_b = _ssd_group(xb, hb_ref, yoffb_ref, g)
        for pj in range(PAIRS_PER_GROUP):
            pair = g * PAIRS_PER_GROUP + pj
            _ssd_pair(xf, decay_f, cb_f, bt_f, hf_ref, yofff_ref, yf_ref, pair, rev=False)
            _ssd_pair(xb, decay_b, cb_b, bt_b, hb_ref, yoffb_ref, yb_ref, pair, rev=True)
    of_ref[...] = (yf_ref[...] + xf_ref[:, 0:D_SSD].astype(F32) * dsum_ref[...]).astype(of_ref.dtype)
    ob_ref[...] = yb_ref[...].astype(ob_ref.dtype)


def _ssd(pm3, pdt3, conv_w, conv_b, dt_bias, a_log, d_skip, clen):
    bsz, t, _ = pm3.shape
    q = SSD_CHUNK
    nc, cc = t // q, clen // q
    nh = 2 * SSD_HEADS
    cw = jnp.pad(conv_w.astype(F32), ((0, 8 - SSD_CONV), (0, 0)))
    cb = conv_b.astype(F32).reshape(1, XBC_DIM)
    dtb = jnp.pad(dt_bias.astype(F32).reshape(1, nh), ((0, 0), (0, DT_PAD - nh)))
    a_neg = jnp.pad(-jnp.exp(a_log.astype(F32)).reshape(1, nh), ((0, 0), (0, DT_PAD - nh)))
    dsum = jnp.repeat((d_skip[0] + d_skip[1]).astype(F32), SSD_HEAD_DIM).reshape(1, D_SSD)
    n_halo = t // HALO

    const = lambda b, s: (0, 0)
    xcol = COL_XBC // XBC_DIM
    cq = CONV_ROWS
    hpb = cq // HALO
    xbc = pl.pallas_call(
        functools.partial(_ssd_conv_kernel, cc=clen // cq, nc=t // cq),
        grid=(bsz, t // cq),
        in_specs=[pl.BlockSpec((None, cq, XBC_DIM), lambda b, s: (b, s, xcol)),
                  pl.BlockSpec((None, HALO, XBC_DIM), lambda b, s: (b, jnp.maximum(s * hpb - 1, 0), xcol)),
                  pl.BlockSpec((None, HALO, XBC_DIM),
                               lambda b, s: (b, jnp.minimum((s + 1) * hpb, n_halo - 1), xcol)),
                  pl.BlockSpec((8, XBC_DIM), const), pl.BlockSpec((1, XBC_DIM), const)],
        out_specs=pl.BlockSpec((None, cq, XBC_DIM), lambda b, s: (b, s, 0)),
        out_shape=jax.ShapeDtypeStruct((bsz, t, XBC_DIM), BF16),
        compiler_params=_params(("arbitrary", "arbitrary")),
        name="ssd_conv",
    )(pm3, pm3, pm3, cw, cb)

    tok_f = lambda b, s: (b, s, 0)
    tok_b = lambda b, s: (b, _ssd_chunk_index(s, True, cc, nc), 0)
    state = [pltpu.VMEM((SSD_STATE, D_SSD), F32), pltpu.VMEM((q, D_SSD), F32)]
    return pl.pallas_call(
        _ssd_scan_kernel,
        grid=(bsz, nc),
        in_specs=[pl.BlockSpec((None, q, XBC_DIM), tok_f), pl.BlockSpec((None, q, DT_PAD), tok_f),
                  pl.BlockSpec((None, q, XBC_DIM), tok_b), pl.BlockSpec((None, q, DT_PAD), tok_b),
                  pl.BlockSpec((1, DT_PAD), const), pl.BlockSpec((1, DT_PAD), const),
                  pl.BlockSpec((1, D_SSD), const)],
        out_specs=[pl.BlockSpec((None, q, D_SSD), tok_f), pl.BlockSpec((None, q, D_SSD), tok_b)],
        out_shape=[jax.ShapeDtypeStruct((bsz, t, D_SSD), BF16), jax.ShapeDtypeStruct((bsz, t, D_SSD), BF16)],
        scratch_shapes=state + state + [pltpu.VMEM((q, D_SSD), F32), pltpu.VMEM((q, D_SSD), F32)],
        compiler_params=_params(("arbitrary", "arbitrary")),
        name="ssd_scan",
    )(xbc, pdt3, xbc, pdt3, dtb, a_neg, dsum)


MOE_TB = 512
ROUTE_TM = 512
INFO_W = 8
BIG_NEG = -3.0e38
ROW_SEG = D_MODEL // LANE


def _load_tokens(ref, t0, n):
    return jnp.concatenate([ref[pl.ds(t0 * ROW_SEG + c, n, stride=ROW_SEG), :] for c in range(ROW_SEG)], axis=1)


def _store_tokens(ref, t0, val):
    n = val.shape[0]
    for c in range(ROW_SEG):
        ref[pl.ds(t0 * ROW_SEG + c, n, stride=ROW_SEG), :] = val[:, c * LANE:(c + 1) * LANE]


def _route_kernel(f_ref, wr_ref, info_ref, cnt_ref, run_ref):
    tm = f_ref.shape[0] // ROW_SEG

    @pl.when(pl.program_id(0) == 0)
    def _():
        run_ref[...] = jnp.zeros_like(run_ref)

    logits = jnp.dot(_load_tokens(f_ref, 0, tm).astype(BF16), wr_ref[...], preferred_element_type=F32)
    lane = lax.broadcasted_iota(jnp.int32, (tm, LANE), 1).astype(F32)
    lg = jnp.where(lane < N_EXPERTS, logits, BIG_NEG)
    m1 = lg.max(axis=-1, keepdims=True)
    i1 = jnp.where(lg == m1, lane, float(LANE)).min(axis=-1, keepdims=True)
    lg2 = jnp.where(lane == i1, BIG_NEG, lg)
    m2 = lg2.max(axis=-1, keepdims=True)
    i2 = jnp.where(lg2 == m2, lane, float(LANE)).min(axis=-1, keepdims=True)
    ex = jnp.exp(m2 - m1)
    g0 = 1.0 / (1.0 + ex)
    g1 = ex / (1.0 + ex)
    sel0 = lane == i1
    sel1 = lane == i2
    onehot = jnp.where(sel0 | sel1, 1.0, 0.0)
    ri = lax.broadcasted_iota(jnp.int32, (tm, tm), 0)
    ci = lax.broadcasted_iota(jnp.int32, (tm, tm), 1)
    strict_lower = jnp.where(ci < ri, 1.0, 0.0).astype(BF16)
    before = jnp.dot(strict_lower, onehot.astype(BF16), preferred_element_type=F32) + run_ref[...]
    pos0 = jnp.where(sel0, before, 0.0).sum(axis=-1, keepdims=True)
    pos1 = jnp.where(sel1, before, 0.0).sum(axis=-1, keepdims=True)
    run_ref[...] = run_ref[...] + onehot.sum(axis=0, keepdims=True)
    cnt_ref[...] = run_ref[...]
    info = jnp.where(lane == 0, i1, jnp.where(lane == 1, i2, jnp.where(lane == 2, g0, jnp.where(
        lane == 3, g1, jnp.where(lane == 4, pos0, jnp.where(lane == 5, pos1, 0.0))))))
    info_ref[...] = info[:, :INFO_W]


def _route(f8, w_router):
    m, d = f8.shape[0] // ROW_SEG, D_MODEL
    wr = jnp.pad(w_router.astype(BF16), ((0, 0), (0, LANE - N_EXPERTS)))
    return pl.pallas_call(
        _route_kernel,
        grid=(m // ROUTE_TM,),
        in_specs=[pl.BlockSpec((ROUTE_TM * ROW_SEG, LANE), lambda i: (i, 0)),
                  pl.BlockSpec((d, LANE), lambda i: (0, 0))],
        out_specs=[pl.BlockSpec((ROUTE_TM, INFO_W), lambda i: (i, 0)), pl.BlockSpec((1, LANE), lambda i: (0, 0))],
        out_shape=[jax.ShapeDtypeStruct((m, INFO_W), F32), jax.ShapeDtypeStruct((1, LANE), F32)],
        scratch_shapes=[pltpu.VMEM((1, LANE), F32)],
        compiler_params=_params(("arbitrary",)),
        name="moe_route",
    )(f8, wr)


def _row_copy(src_ref, src_row, dst_ref, dst_row, sem):
    src = src_ref.at[pl.ds(pl.multiple_of(src_row * ROW_SEG, ROW_SEG), ROW_SEG)]
    dst = dst_ref.at[pl.ds(pl.multiple_of(dst_row * ROW_SEG, ROW_SEG), ROW_SEG)]
    return pltpu.make_async_copy(src, dst, sem)


def _wait_rows(src_ref, dst_ref, sem, n):
    def wait(t, c):
        _row_copy(src_ref, 0, dst_ref, 0, sem).wait()
        return c

    lax.fori_loop(0, n, wait, 0, unroll=8)


def _dispatch_kernel(pad_ref, dest_hbm, f_ref, rows_hbm, dest_smem, stage_ref, zero_ref, idx_sem, row_sem,
                     pad_sem):
    tm = f_ref.shape[0] // ROW_SEG
    i, n = pl.program_id(0), pl.num_programs(0)
    slot = i % 2
    stage = stage_ref.at[slot]
    sem = row_sem.at[slot]

    @pl.when(i == 0)
    def _():
        zero_ref[...] = jnp.zeros_like(zero_ref)
        for e in range(N_EXPERTS):
            first, count = pad_ref[e], pad_ref[N_EXPERTS + e]

            def start_zero(j, c):
                _row_copy(zero_ref, 0, rows_hbm, first + j, pad_sem).start()
                return c

            lax.fori_loop(0, count, start_zero, 0)
        for e in range(N_EXPERTS):
            def wait_zero(j, c):
                _row_copy(zero_ref, 0, rows_hbm, 0, pad_sem).wait()
                return c

            lax.fori_loop(0, pad_ref[N_EXPERTS + e], wait_zero, 0)

    @pl.when(i >= 2)
    def _():
        _wait_rows(stage, rows_hbm, sem, TOP_K * tm)

    stage[...] = f_ref[...]
    base = pl.multiple_of(i * (TOP_K * tm), TOP_K * tm)
    idx_copy = pltpu.make_async_copy(dest_hbm.at[pl.ds(base, TOP_K * tm)], dest_smem, idx_sem)
    idx_copy.start()
    idx_copy.wait()

    def start(t, c):
        for k in range(TOP_K):
            _row_copy(stage, t, rows_hbm, dest_smem[TOP_K * t + k], sem).start()
        return c

    lax.fori_loop(0, tm, start, 0, unroll=8)

    @pl.when(i == n - 1)
    def _():
        _wait_rows(stage, rows_hbm, sem, TOP_K * tm)

        @pl.when(n >= 2)
        def _():
            _wait_rows(stage_ref.at[1 - slot], rows_hbm, row_sem.at[1 - slot], TOP_K * tm)


def _dispatch(f8, dest_flat, pad_info, n_rows):
    m = f8.shape[0] // ROW_SEG
    grid_spec = pltpu.PrefetchScalarGridSpec(
        num_scalar_prefetch=1,
        grid=(m // ROUTE_TM,),
        in_specs=[pl.BlockSpec(memory_space=pl.ANY),
                  pl.BlockSpec((ROUTE_TM * ROW_SEG, LANE), lambda i, pad: (i, 0))],
        out_specs=pl.BlockSpec(memory_space=pl.ANY),
        scratch_shapes=[pltpu.SMEM((TOP_K * ROUTE_TM,), jnp.int32),
                        pltpu.VMEM((2, ROUTE_TM * ROW_SEG, LANE), f8.dtype),
                        pltpu.VMEM((ROW_SEG, LANE), f8.dtype),
                        pltpu.SemaphoreType.DMA(()), pltpu.SemaphoreType.DMA((2,)), pltpu.SemaphoreType.DMA(())],
    )
    return pl.pallas_call(
        _dispatch_kernel,
        grid_spec=grid_spec,
        out_shape=jax.ShapeDtypeStruct((n_rows * ROW_SEG, LANE), f8.dtype),
        compiler_params=_params(("arbitrary",)),
        name="moe_dispatch",
    )(pad_info, dest_flat, f8)


def _gffn_kernel(be_ref, nu_ref, x_ref, w1_ref, w3_ref, w2_ref, y_ref):
    i = pl.program_id(0)

    @pl.when(i < nu_ref[0])
    def _():
        x = _load_tokens(x_ref, 0, MOE_TB).astype(BF16)
        h1 = jnp.dot(x, w1_ref[...], preferred_element_type=F32)
        h3 = jnp.dot(x, w3_ref[...], preferred_element_type=F32)
        act = (h1 * jax.nn.sigmoid(h1) * h3).astype(BF16)
        _store_tokens(y_ref, 0, jnp.dot(act, w2_ref[...], preferred_element_type=F32))

    @pl.when(i >= nu_ref[0])
    def _():
        y_ref[...] = jnp.zeros_like(y_ref)


def _gffn(rows, block_e, n_used, w1, w3, w2):
    r, d = rows.shape[0] // ROW_SEG, D_MODEL
    ff = w1.shape[2]
    nb = r // MOE_TB
    once = pl.Buffered(1)
    grid_spec = pltpu.PrefetchScalarGridSpec(
        num_scalar_prefetch=2,
        grid=(nb,),
        in_specs=[pl.BlockSpec((MOE_TB * ROW_SEG, LANE), lambda i, be, nu: (i, 0)),
                  pl.BlockSpec((None, d, ff), lambda i, be, nu: (be[i], 0, 0), pipeline_mode=once),
                  pl.BlockSpec((None, d, ff), lambda i, be, nu: (be[i], 0, 0), pipeline_mode=once),
                  pl.BlockSpec((None, ff, d), lambda i, be, nu: (be[i], 0, 0), pipeline_mode=once)],
        out_specs=pl.BlockSpec((MOE_TB * ROW_SEG, LANE), lambda i, be, nu: (i, 0)),
    )
    return pl.pallas_call(
        _gffn_kernel,
        grid_spec=grid_spec,
        out_shape=jax.ShapeDtypeStruct((r * ROW_SEG, LANE), F32),
        compiler_params=_params(("arbitrary",)),
        name="moe_ffn",
    )(block_e, n_used, rows, w1, w3, w2)


def _combine_kernel(dest_hbm, y_hbm, h_ref, info_ref, modt_ref, o_ref, dest_smem, buf_ref, idx_sem, row_sem,
                    *scratch, nsub, tiles_per_batch):
    tm = h_ref.shape[0]
    i, n = pl.program_id(0), pl.num_programs(0)
    latent_only = tiles_per_batch is not None
    if latent_only:
        obuf_ref, out_sem = scratch

        def latent_copies(step, buf_slot, fn):
            for s in range(nsub):
                g = step * nsub + s
                j = lax.rem(g, tiles_per_batch)

                @pl.when(j > 0)
                def _():
                    row0 = (lax.div(g, tiles_per_batch) * (tiles_per_batch - 1) + (j - 1)) * MOD_TILE
                    fn(pltpu.make_async_copy(obuf_ref.at[buf_slot, pl.ds(s * MOD_TILE, MOD_TILE)],
                                             o_ref.at[pl.ds(pl.multiple_of(row0, MOD_TILE), MOD_TILE)],
                                             out_sem.at[buf_slot]))

    def gather(step, slot):
        base = pl.multiple_of(step * (TOP_K * tm), TOP_K * tm)
        idx_copy = pltpu.make_async_copy(dest_hbm.at[pl.ds(base, TOP_K * tm)], dest_smem, idx_sem)
        idx_copy.start()
        idx_copy.wait()

        def start(t, c):
            for k in range(TOP_K):
                _row_copy(y_hbm, dest_smem[TOP_K * t + k], buf_ref.at[slot, k], t, row_sem.at[slot]).start()
            return c

        lax.fori_loop(0, tm, start, 0, unroll=8)

    @pl.when(i == 0)
    def _():
        gather(0, 0)

    @pl.when(i + 1 < n)
    def _():
        gather(i + 1, (i + 1) % 2)

    slot = i % 2
    _wait_rows(y_hbm, buf_ref.at[slot, 0], row_sem.at[slot], TOP_K * tm)
    if latent_only:
        @pl.when(i >= 2)
        def _():
            latent_copies(i - 2, slot, lambda cp: cp.wait())
    for s in range(nsub):
        r0, r1 = s * MOD_TILE, (s + 1) * MOD_TILE
        info = info_ref[r0:r1, :]
        mix = (info[:, 2:3] * _load_tokens(buf_ref.at[slot, 0], r0, MOD_TILE)
               + info[:, 3:4] * _load_tokens(buf_ref.at[slot, 1], r0, MOD_TILE))
        out = h_ref[r0:r1, :] + modt_ref[s, 5:6, :] * mix
        if latent_only:
            obuf_ref[slot, r0:r1, :] = out
        else:
            o_ref[r0:r1, :] = out
    if latent_only:
        latent_copies(i, slot, lambda cp: cp.start())

        @pl.when(i == n - 1)
        def _():
            latent_copies(i, slot, lambda cp: cp.wait())

            @pl.when(n >= 2)
            def _():
                latent_copies(i - 1, 1 - slot, lambda cp: cp.wait())


def _combine(y_rows, dest_flat, h, info, modt, tiles_per_batch):
    m, d = h.shape
    nsub = ROUTE_TM // MOD_TILE
    scratch = [pltpu.SMEM((TOP_K * ROUTE_TM,), jnp.int32), pltpu.VMEM((2, TOP_K, ROUTE_TM * ROW_SEG, LANE), F32),
               pltpu.SemaphoreType.DMA(()), pltpu.SemaphoreType.DMA((2,))]
    if tiles_per_batch is None:
        out_spec, out_rows = pl.BlockSpec((ROUTE_TM, d), lambda i: (i, 0)), m
    else:
        out_spec, out_rows = pl.BlockSpec(memory_space=pl.ANY), m - m // tiles_per_batch
        scratch += [pltpu.VMEM((2, ROUTE_TM, d), F32), pltpu.SemaphoreType.DMA((2,))]
    return pl.pallas_call(
        functools.partial(_combine_kernel, nsub=nsub, tiles_per_batch=tiles_per_batch),
        grid=(m // ROUTE_TM,),
        in_specs=[pl.BlockSpec(memory_space=pl.ANY), pl.BlockSpec(memory_space=pl.ANY),
                  pl.BlockSpec((ROUTE_TM, d), lambda i: (i, 0)),
                  pl.BlockSpec((ROUTE_TM, INFO_W), lambda i: (i, 0)),
                  pl.BlockSpec((nsub, 6, d), lambda i: (i, 0, 0))],
        out_specs=out_spec,
        out_shape=jax.ShapeDtypeStruct((out_rows, d), F32),
        scratch_shapes=scratch,
        compiler_params=_params(("arbitrary",)),
        name="moe_combine",
    )(dest_flat, y_rows, h, info, modt)


def _moe(f8, h, w_router, w1, w3, w2, modt, tiles_per_batch=None):
    m, d = h.shape
    info, cnt = _route(f8, w_router)
    e = info[:, 0:TOP_K].astype(jnp.int32)
    pos = info[:, 4:4 + TOP_K].astype(jnp.int32)
    counts = cnt[0, :N_EXPERTS].astype(jnp.int32)
    padded = (counts + MOE_TB - 1) // MOE_TB * MOE_TB
    pad_end = jnp.cumsum(padded)
    pad_start = pad_end - padded
    dest_flat = (pad_start[e] + pos).reshape(-1)
    n_rows = m * TOP_K + N_EXPERTS * MOE_TB
    nb = n_rows // MOE_TB
    block_e = jnp.minimum(jnp.searchsorted(pad_end, jnp.arange(nb, dtype=jnp.int32) * MOE_TB, side='right'),
                          N_EXPERTS - 1).astype(jnp.int32)
    n_used = (pad_end[-1:] // MOE_TB).astype(jnp.int32)
    pad_info = jnp.concatenate([pad_start + counts, padded - counts]).astype(jnp.int32)
    rows = _dispatch(f8, dest_flat, pad_info, n_rows)
    y_rows = _gffn(rows, block_e, n_used, w1.astype(BF16), w3.astype(BF16), w2.astype(BF16))
    return _combine(y_rows, dest_flat, h, info, modt, tiles_per_batch)


def _tile_mod(mod_l, bsz, tiles_per_batch):
    t = jnp.arange(bsz * tiles_per_batch)
    idx = jnp.where(t % tiles_per_batch == 0, bsz, t // tiles_per_batch)
    return jnp.transpose(mod_l[:, idx, :], (1, 0, 2))


def kernel(x, c, ctx, c_ctx, w_mod, b_mod, norm1_w, norm2_w, w_in, conv_w, conv_b, dt_bias, a_log, d_skip,
           ssd_norm_w, q_norm_w, k_norm_w, rpb, w_br_ssd, w_br_na, w_out, w_ff1, w_ff3, w_ff2, w_router,
           w_e1, w_e3, w_e2):
    bsz, seqlen, d = x.shape
    clen = ctx.shape[1]
    n_layers = w_mod.shape[0]
    assert clen == MOD_TILE and seqlen % MOD_TILE == 0 and bsz < 16
    t = clen + seqlen
    m = bsz * t
    tiles_per_batch = t // MOD_TILE

    cond = jnp.zeros((16, d), F32).at[:bsz].set(c).at[bsz].set(c_ctx)
    mod = _modulation(cond, w_mod, b_mod)
    modt = [_tile_mod(mod[i], bsz, tiles_per_batch) for i in range(n_layers)]

    h = jnp.concatenate([ctx, x], axis=1).reshape(m, d)
    a = _adaln(h, norm1_w[0], modt[0])
    for i in range(n_layers):
        last = i == n_layers - 1
        wi = w_in[i]
        w_main = jnp.concatenate([wi[:, :IN_SPLITS[3]], _qk_reorder(wi[:, IN_SPLITS[4]:IN_SPLITS[5]]),
                                  _qk_reorder(wi[:, IN_SPLITS[5]:IN_SPLITS[6]]), wi[:, IN_SPLITS[6]:]],
                                 axis=1).astype(BF16)
        w_dt = jnp.pad(wi[:, IN_SPLITS[3]:IN_SPLITS[4]], ((0, 0), (0, DT_PAD - 2 * SSD_HEADS))).astype(BF16)
        p_main = _matmul(a, w_main, BF16, 512, 2560, "inproj")
        p_dt = _matmul(a, w_dt, F32, 1024, DT_PAD, "inproj_dt")
        pm3 = p_main.reshape(bsz, t, MAIN_DIM)
        yf, yb = _ssd(pm3, p_dt.reshape(bsz, t, DT_PAD), conv_w[i], conv_b[i], dt_bias[i], a_log[i],
                      d_skip[i], clen)
        y_na = _na(pm3, q_norm_w[i], k_norm_w[i], rpb[i], clen).reshape(m, d)
        fi = i // 2
        moe_layer = i % 2 == 1
        h, f = _merge(p_main, yf.reshape(m, d), yb.reshape(m, d), y_na, h, w_br_ssd[i].astype(BF16),
                      w_br_na[i].astype(BF16), w_out[i].astype(BF16), ssd_norm_w[i], norm2_w[i], modt[i],
                      row_tile_out=moe_layer)
        nxt = min(i + 1, n_layers - 1)
        if not moe_layer:
            h, a = _ffn(f, h, w_ff1[fi].astype(BF16), w_ff3[fi].astype(BF16), w_ff2[fi].astype(BF16),
                        norm1_w[nxt], modt[i], modt[nxt])
        else:
            if last:
                return _moe(f, h, w_router[fi], w_e1[fi], w_e3[fi], w_e2[fi], modt[i],
                            tiles_per_batch).reshape(bsz, seqlen, d)
            h = _moe(f, h, w_router[fi], w_e1[fi], w_e3[fi], w_e2[fi], modt[i])
            a = _adaln(h, norm1_w[nxt], modt[nxt])
    return h.reshape(bsz, t, d)[:, clen:]
```
